```python
import math
import jax, jax.numpy as jnp
from jax import lax
import numpy as np

D_MODEL = 2048
BATCH = 4
SEQ = 2048
DEPTH = 1
DEC_BATCH = 128
DEC_SEQ = 1
PAST_LEN = 16384
PAGE_SIZE = 128

RET_HEADS = 8
RET_DK = 128
RET_DV = 256
RET_QK = RET_HEADS * RET_DK
RET_V = RET_HEADS * RET_DV
RET_CHUNK = 128
ROPE_BASE = 10000.0
HG_HEADS = 8
HG_DK = 128
HG_DV = 128
HG_K = HG_HEADS * HG_DK
HG_V = HG_HEADS * HG_DV
HG_CHUNK = 16
D_FF = -(-8 * D_MODEL // (3 * 256)) * 256
EPS = 1e-6

IN_SIZES = [RET_QK, RET_QK, RET_V, RET_V, HG_K, HG_K, HG_V, HG_V, D_MODEL, D_MODEL]
IN_TOTAL = int(sum(IN_SIZES))
IN_OFFSETS = [int(o) for o in np.cumsum(IN_SIZES)[:-1]]

kernel_name = 'hybrid_retnet_hgrn2_step'


def rms_norm(x, g):
    xf = x.astype(jnp.float32)
    y = xf * lax.rsqrt(jnp.mean(xf * xf, axis=-1, keepdims=True) + EPS)
    return (y * g.astype(jnp.float32)).astype(x.dtype)


def head_rms(o):
    return o * lax.rsqrt(jnp.mean(o * o, axis=-1, keepdims=True) + EPS)


def to_heads(a, n_heads):
    B, T, _ = a.shape
    return a.reshape(B, T, n_heads, -1).transpose(0, 2, 1, 3).astype(jnp.float32)


def from_heads(a):
    B, H, T, d = a.shape
    return a.transpose(0, 2, 1, 3).reshape(B, T, H * d)


def rotary(x, pos):
    half = x.shape[-1] // 2
    inv_freq = ROPE_BASE ** (-jnp.arange(half, dtype=jnp.float32) / half)
    ang = pos.astype(jnp.float32)[:, None] * inv_freq[None, :]
    cos, sin = jnp.cos(ang), jnp.sin(ang)
    x1, x2 = x[..., :half], x[..., half:]
    return jnp.concatenate([x1 * cos - x2 * sin, x2 * cos + x1 * sin], axis=-1)


def split_chunks(a, n_chunks, c):
    B, H, T, d = a.shape
    return a.reshape(B, H, n_chunks, c, d).transpose(2, 0, 1, 3, 4)


def merge_chunks(a):
    N, B, H, C, d = a.shape
    return a.transpose(1, 2, 0, 3, 4).reshape(B, H, N * C, d)


def retention_chunkwise(q, k, v, s0, log_gamma):
    B, H, T, _ = q.shape
    C = math.gcd(T, RET_CHUNK)
    N = T // C
    idx = jnp.arange(C, dtype=jnp.float32)
    diff = idx[:, None] - idx[None, :]
    lg = log_gamma[:, None, None]
    decay_mask = jnp.where(diff >= 0, jnp.exp(jnp.maximum(diff, 0.0) * lg), 0.0)
    q_decay = jnp.exp((idx + 1.0)[None, :] * log_gamma[:, None])[..., None]
    k_decay = jnp.exp((C - 1.0 - idx)[None, :] * log_gamma[:, None])[..., None]
    chunk_decay = jnp.exp(C * log_gamma)[:, None, None]

    def step(S, blk):
        qc, kc, vc = blk
        scores = jnp.einsum('bhnd,bhmd->bhnm', qc, kc) * decay_mask
        o = (jnp.einsum('bhnm,bhme->bhne', scores, vc)
             + jnp.einsum('bhnd,bhde->bhne', qc * q_decay, S))
        S = chunk_decay * S + jnp.einsum('bhmd,bhme->bhde', kc * k_decay, vc)
        return S, o

    S, o = lax.scan(step, s0, (split_chunks(q, N, C), split_chunks(k, N, C), split_chunks(v, N, C)))
    return merge_chunks(o), S


def hgrn2_chunkwise(q, k, logf, v, s0):
    B, H, T, _ = q.shape
    C = math.gcd(T, HG_CHUNK)
    N = T // C
    pos = jnp.arange(C)
    causal = (pos[:, None] >= pos[None, :])[:, :, None]

    def step(S, blk):
        qc, kc, lfc, vc = blk
        b = jnp.cumsum(lfc, axis=-2)
        rel = jnp.where(causal, b[:, :, :, None, :] - b[:, :, None, :, :], -jnp.inf)
        A = jnp.einsum('bhnd,bhmd,bhnmd->bhnm', qc, kc, jnp.exp(rel))
        o = (jnp.einsum('bhnm,bhme->bhne', A, vc)
             + jnp.einsum('bhnd,bhde->bhne', qc * jnp.exp(b), S))
        b_last = b[:, :, -1:, :]
        S = (jnp.exp(b_last[:, :, 0, :])[..., None] * S
             + jnp.einsum('bhmd,bhme->bhde', kc * jnp.exp(b_last - b), vc))
        return S, o

    blocks = (split_chunks(q, N, C), split_chunks(k, N, C), split_chunks(logf, N, C), split_chunks(v, N, C))
    S, o = lax.scan(step, s0, blocks)
    return merge_chunks(o), S


def decoder_layer(x, pos, s_ret, s_hg, lb, w_in, w_ret_out, w_hgrn_out, w_out,
                  norm_mix, norm_ffn, hgrn_norm, w_ffn_in, w_ffn_out):
    h = rms_norm(x, norm_mix)
    proj = h @ w_in
    rq, rk, rv, rg, hq, hf, hi, hg, ga, gb = jnp.split(proj, IN_OFFSETS, axis=-1)

    log_gamma = jnp.log(1.0 - 2.0 ** (-5.0 - jnp.arange(RET_HEADS, dtype=jnp.float32)))
    q = rotary(to_heads(rq, RET_HEADS), pos)
    k = rotary(to_heads(rk, RET_HEADS), pos) * (RET_DK ** -0.5)
    v = to_heads(rv, RET_HEADS)
    o_r, s_ret_new = retention_chunkwise(q, k, v, s_ret.astype(jnp.float32), log_gamma)
    o_r = from_heads(head_rms(o_r)) * jax.nn.silu(rg.astype(jnp.float32))
    y_a = o_r.astype(x.dtype) @ w_ret_out

    lbh = lb.reshape(HG_HEADS, 1, HG_DK)
    z = to_heads(hf, HG_HEADS)
    logf = jnp.logaddexp(jnp.log(lbh), jnp.log1p(-lbh) + jax.nn.log_sigmoid(z))
    k_in = (1.0 - lbh) * jax.nn.sigmoid(-z)
    qh = jax.nn.silu(to_heads(hq, HG_HEADS))
    o_h, s_hg_new = hgrn2_chunkwise(qh, k_in, logf, to_heads(hi, HG_HEADS), s_hg.astype(jnp.float32))
    o_h = head_rms(o_h) * hgrn_norm.astype(jnp.float32)
    o_h = from_heads(o_h) * jax.nn.silu(hg.astype(jnp.float32))
    y_b = o_h.astype(x.dtype) @ w_hgrn_out

    merged = jax.nn.sigmoid(ga) * y_a + jax.nn.sigmoid(gb) * y_b
    x = x + merged @ w_out

    h2 = rms_norm(x, norm_ffn)
    g, u = jnp.split(h2 @ w_ffn_in, [D_FF], axis=-1)
    x = x + (jax.nn.silu(g) * u) @ w_ffn_out
    return x, s_ret_new, s_hg_new


def setup_inputs(seed: int = 0) -> dict:
    key = jax.random.key(seed)
    ks = jax.random.split(key, 16)
    f32 = jnp.float32
    nrm = lambda k, shape, s: (jax.random.normal(k, shape, f32) * s)
    return {
        'x_prompt': nrm(ks[0], (BATCH, SEQ, D_MODEL), 1.0),
        'x_sample': nrm(ks[1], (DEC_BATCH, DEC_SEQ, D_MODEL), 1.0),
        'state_ret': nrm(ks[2], (DEPTH, DEC_BATCH, RET_HEADS, RET_DK, RET_DV), 0.5),
        'state_hgrn': nrm(ks[3], (DEPTH, DEC_BATCH, HG_HEADS, HG_DK, HG_DV), 0.5),
        'w_in': nrm(ks[4], (DEPTH, D_MODEL, IN_TOTAL), D_MODEL ** -0.5),
        'w_ret_out': nrm(ks[5], (DEPTH, RET_V, D_MODEL), RET_V ** -0.5),
        'w_hgrn_out': nrm(ks[6], (DEPTH, HG_V, D_MODEL), HG_V ** -0.5),
        'w_out': nrm(ks[7], (DEPTH, D_MODEL, D_MODEL), D_MODEL ** -0.5),
        'norm_mix': 1.0 + nrm(ks[8], (DEPTH, D_MODEL), 0.02),
        'norm_ffn': 1.0 + nrm(ks[9], (DEPTH, D_MODEL), 0.02),
        'hgrn_norm': 1.0 + nrm(ks[10], (DEPTH, HG_DV), 0.02),
        'hgrn_lb_logits': nrm(ks[11], (DEPTH + 1, HG_K), 0.5),
        'w_ffn_in': nrm(ks[12], (DEPTH, D_MODEL, 2 * D_FF), D_MODEL ** -0.5),
        'w_ffn_out': nrm(ks[13], (DEPTH, D_FF, D_MODEL), D_FF ** -0.5),
        'norm_final': 1.0 + nrm(ks[14], (D_MODEL,), 0.02),
    }


def reference(x_prompt, x_sample, state_ret, state_hgrn, w_in, w_ret_out, w_hgrn_out, w_out,
              norm_mix, norm_ffn, hgrn_norm, hgrn_lb_logits, w_ffn_in, w_ffn_out, norm_final):
    lower_bounds = jnp.cumsum(jax.nn.softmax(hgrn_lb_logits.astype(jnp.float32), axis=0), axis=0)

    pos_prompt = jnp.arange(x_prompt.shape[1])
    pos_sample = PAST_LEN + jnp.arange(x_sample.shape[1])
    b_prompt = x_prompt.shape[0]

    xp, xs = x_prompt, x_sample
    ret_p, hg_p, ret_s, hg_s = [], [], [], []
    for l in range(DEPTH):
        lw = (w_in[l], w_ret_out[l], w_hgrn_out[l], w_out[l], norm_mix[l], norm_ffn[l],
              hgrn_norm[l], w_ffn_in[l], w_ffn_out[l])
        zr = jnp.zeros((b_prompt, RET_HEADS, RET_DK, RET_DV), jnp.float32)
        zh = jnp.zeros((b_prompt, HG_HEADS, HG_DK, HG_DV), jnp.float32)
        xp, sr, sh = decoder_layer(xp, pos_prompt, zr, zh, lower_bounds[l], *lw)
        ret_p.append(sr.astype(state_ret.dtype))
        hg_p.append(sh.astype(state_hgrn.dtype))
        xs, sr, sh = decoder_layer(xs, pos_sample, state_ret[l], state_hgrn[l], lower_bounds[l], *lw)
        ret_s.append(sr.astype(state_ret.dtype))
        hg_s.append(sh.astype(state_hgrn.dtype))

    y_prompt = rms_norm(xp, norm_final)
    y_sample = rms_norm(xs, norm_final)
    return (y_prompt, y_sample, jnp.stack(ret_p), jnp.stack(hg_p), jnp.stack(ret_s), jnp.stack(hg_s))
```

```python
import functools
import math

import numpy as np
import jax
import jax.numpy as jnp
from jax import lax
from jax.experimental import pallas as pl
from jax.experimental.pallas import tpu as pltpu

D_MODEL = 2048
DEPTH = 1
PAST_LEN = 16384
RET_HEADS = 8
RET_DK = 128
RET_DV = 256
RET_QK = RET_HEADS * RET_DK
RET_V = RET_HEADS * RET_DV
ROPE_BASE = 10000.0
HG_HEADS = 8
HG_DK = 128
HG_DV = 128
HG_K = HG_HEADS * HG_DK
HG_V = HG_HEADS * HG_DV
D_FF = 5632
EPS = 1e-6

OFF_RQ = 0
OFF_RK = OFF_RQ + RET_QK
OFF_RV = OFF_RK + RET_QK
OFF_RG = OFF_RV + RET_V
OFF_HQ = OFF_RG + RET_V
OFF_HF = OFF_HQ + HG_K
OFF_HI = OFF_HF + HG_K
OFF_HG = OFF_HI + HG_V
OFF_GA = OFF_HG + HG_V
OFF_GB = OFF_GA + D_MODEL
IN_TOTAL = OFF_GB + D_MODEL

CHUNK = 128
SUB = 8
V7X_VMEM_LIMIT = 56 * 1024 * 1024

F32 = jnp.float32
BF16 = jnp.bfloat16

LOG_GAMMA = [float(np.log(np.float32(1.0) - np.float32(2.0) ** np.float32(-5.0 - h))) for h in range(RET_HEADS)]


def _params(sem, vmem=V7X_VMEM_LIMIT):
    return pltpu.CompilerParams(dimension_semantics=sem, vmem_limit_bytes=vmem)


def _rms_rows(x, g):
    return (x * lax.rsqrt(jnp.mean(x * x, axis=-1, keepdims=True) + EPS)) * g


def _sigmoid(x):
    return 1.0 / (1.0 + jnp.exp(-x))


def _silu(x):
    return x * _sigmoid(x)


def _dot(a, b):
    return jnp.dot(a, b, preferred_element_type=F32)


def _dot_nt(a, b):
    return lax.dot_general(a, b, (((1,), (1,)), ((), ())), preferred_element_type=F32)


def _dot_tn(a, b):
    return lax.dot_general(a, b, (((0,), (0,)), ((), ())), preferred_element_type=F32)


def _proj_kernel(x_ref, g_ref, w_ref, o_ref, h_ref):
    @pl.when(pl.program_id(1) == 0)
    def _():
        h_ref[...] = _rms_rows(x_ref[...], g_ref[...]).astype(BF16)
    o_ref[...] = _dot(h_ref[...], w_ref[...])


def _proj(x, g, w, tm, tn):
    m, d = x.shape
    n = w.shape[1]
    return pl.pallas_call(
        _proj_kernel,
        grid=(m // tm, n // tn),
        in_specs=[pl.BlockSpec((tm, d), lambda i, j: (i, 0)),
                  pl.BlockSpec((1, d), lambda i, j: (0, 0)),
                  pl.BlockSpec((d, tn), lambda i, j: (0, j))],
        out_specs=pl.BlockSpec((tm, tn), lambda i, j: (i, j)),
        out_shape=jax.ShapeDtypeStruct((m, n), F32),
        scratch_shapes=[pltpu.VMEM((tm, d), BF16)],
        compiler_params=_params(("parallel", "arbitrary")),
        name="proj",
    )(x, g, w)


def _merge_kernel(or_ref, oh_ref, wr_ref, wh_ref, ga_ref, gb_ref, o_ref):
    ya = _dot(or_ref[...], wr_ref[...])
    yb = _dot(oh_ref[...], wh_ref[...])
    o_ref[...] = (_sigmoid(ga_ref[...]) * ya + _sigmoid(gb_ref[...]) * yb).astype(BF16)


def _merge(o_r, o_h, w_r, w_h, proj, tm, tn):
    m = o_r.shape[0]
    n = w_r.shape[1]
    return pl.pallas_call(
        _merge_kernel,
        grid=(m // tm, n // tn),
        in_specs=[pl.BlockSpec((tm, RET_V), lambda i, j: (i, 0)),
                  pl.BlockSpec((tm, HG_V), lambda i, j: (i, 0)),
                  pl.BlockSpec((RET_V, tn), lambda i, j: (0, j)),
                  pl.BlockSpec((HG_V, tn), lambda i, j: (0, j)),
                  pl.BlockSpec((tm, tn), lambda i, j: (i, OFF_GA // tn + j)),
                  pl.BlockSpec((tm, tn), lambda i, j: (i, OFF_GB // tn + j))],
        out_specs=pl.BlockSpec((tm, tn), lambda i, j: (i, j)),
        out_shape=jax.ShapeDtypeStruct((m, n), BF16),
        compiler_params=_params(("parallel", "arbitrary")),
        name="merge",
    )(o_r, o_h, w_r, w_h, proj, proj)


def _resid_kernel(x_ref, m_ref, w_ref, o_ref):
    o_ref[...] = x_ref[...] + _dot(m_ref[...], w_ref[...])


def _resid(x, mrg, w, tm, tn):
    m, d = x.shape
    return pl.pallas_call(
        _resid_kernel,
        grid=(m // tm, d // tn),
        in_specs=[pl.BlockSpec((tm, tn), lambda i, j: (i, j)),
                  pl.BlockSpec((tm, d), lambda i, j: (i, 0)),
                  pl.BlockSpec((d, tn), lambda i, j: (0, j))],
        out_specs=pl.BlockSpec((tm, tn), lambda i, j: (i, j)),
        out_shape=jax.ShapeDtypeStruct((m, d), F32),
        compiler_params=_params(("parallel", "arbitrary")),
        name="resid",
    )(x, mrg, w)


def _ffn_in_kernel(x_ref, g_ref, wg_ref, wu_ref, o_ref, h_ref):
    @pl.when(pl.program_id(1) == 0)
    def _():
        h_ref[...] = _rms_rows(x_ref[...], g_ref[...]).astype(BF16)
    h = h_ref[...]
    gate = _dot(h, wg_ref[...])
    up = _dot(h, wu_ref[...])
    o_ref[...] = (_silu(gate) * up).astype(BF16)


def _ffn_in(x, g, w, tm, tn):
    m, d = x.shape
    return pl.pallas_call(
        _ffn_in_kernel,
        grid=(m // tm, D_FF // tn),
        in_specs=[pl.BlockSpec((tm, d), lambda i, j: (i, 0)),
                  pl.BlockSpec((1, d), lambda i, j: (0, 0)),
                  pl.BlockSpec((d, tn), lambda i, j: (0, j)),
                  pl.BlockSpec((d, tn), lambda i, j: (0, D_FF // tn + j))],
        out_specs=pl.BlockSpec((tm, tn), lambda i, j: (i, j)),
        out_shape=jax.ShapeDtypeStruct((m, D_FF), BF16),
        scratch_shapes=[pltpu.VMEM((tm, d), BF16)],
        compiler_params=_params(("parallel", "arbitrary")),
        name="ffn_in",
    )(x, g, w, w)


def _ffn_out_kernel(a_ref, w_ref, x_ref, g_ref, o_ref, acc_ref):
    k = pl.program_id(1)

    @pl.when(k == 0)
    def _():
        acc_ref[...] = jnp.zeros_like(acc_ref)

    acc_ref[...] += _dot(a_ref[...], w_ref[...])

    @pl.when(k == pl.num_programs(1) - 1)
    def _():
        o_ref[...] = _rms_rows(x_ref[...] + acc_ref[...], g_ref[...])


def _ffn_out(act, w, x, g, tm, tk):
    m, d = x.shape
    return pl.pallas_call(
        _ffn_out_kernel,
        grid=(m // tm, D_FF // tk),
        in_specs=[pl.BlockSpec((tm, tk), lambda i, k: (i, k)),
                  pl.BlockSpec((tk, d), lambda i, k: (k, 0)),
                  pl.BlockSpec((tm, d), lambda i, k: (i, 0)),
                  pl.BlockSpec((1, d), lambda i, k: (0, 0))],
        out_specs=pl.BlockSpec((tm, d), lambda i, k: (i, 0)),
        out_shape=jax.ShapeDtypeStruct((m, d), F32),
        scratch_shapes=[pltpu.VMEM((tm, d), F32)],
        compiler_params=_params(("parallel", "arbitrary")),
        name="ffn_out",
    )(act, w, x, g)


def _rotary(x, cos, sin_signed):
    return x * cos + pltpu.roll(x, RET_DK // 2, 1) * sin_signed


def _ret_kernel(q_ref, k_ref, v_ref, g_ref, cos_ref, sin_ref, o_ref, so_ref, s_ref):
    n = pl.program_id(1)

    @pl.when(n == 0)
    def _():
        s_ref[...] = jnp.zeros_like(s_ref)

    c = CHUNK
    cos = cos_ref[...]
    sin = sin_ref[...]
    row = lax.broadcasted_iota(jnp.int32, (c, c), 0)
    col = lax.broadcasted_iota(jnp.int32, (c, c), 1)
    diff = (row - col).astype(F32)
    idx = lax.broadcasted_iota(jnp.int32, (c, RET_DK), 0).astype(F32)
    for h in range(RET_HEADS):
        lg = LOG_GAMMA[h]
        q = _rotary(q_ref[0, :, h * RET_DK:(h + 1) * RET_DK], cos, sin)
        k = _rotary(k_ref[0, :, h * RET_DK:(h + 1) * RET_DK], cos, sin) * (RET_DK ** -0.5)
        v = v_ref[0, :, h * RET_DV:(h + 1) * RET_DV].astype(BF16)
        decay_mask = jnp.where(diff >= 0, jnp.exp(jnp.maximum(diff, 0.0) * lg), 0.0)
        q_decay = jnp.exp((idx + 1.0) * lg)
        k_decay = jnp.exp((c - 1.0 - idx) * lg)
        chunk_decay = math.exp(c * lg)
        s = s_ref[h]
        scores = _dot_nt(q.astype(BF16), k.astype(BF16)) * decay_mask
        o = _dot(scores.astype(BF16), v) + _dot((q * q_decay).astype(BF16), s.astype(BF16))
        s_ref[h] = chunk_decay * s + _dot_tn((k * k_decay).astype(BF16), v)
        o = o * lax.rsqrt(jnp.mean(o * o, axis=-1, keepdims=True) + EPS)
        gate = g_ref[0, :, h * RET_DV:(h + 1) * RET_DV]
        o_ref[0, :, h * RET_DV:(h + 1) * RET_DV] = (o * _silu(gate)).astype(BF16)

    @pl.when(n == pl.num_programs(1) - 1)
    def _():
        so_ref[0] = s_ref[...]


def _retention_prompt(proj3, cos, sin):
    b, t, _ = proj3.shape
    nc = t // CHUNK
    return pl.pallas_call(
        _ret_kernel,
        grid=(b, nc),
        in_specs=[pl.BlockSpec((1, CHUNK, RET_QK), lambda i, n: (i, n, OFF_RQ // RET_QK)),
                  pl.BlockSpec((1, CHUNK, RET_QK), lambda i, n: (i, n, OFF_RK // RET_QK)),
                  pl.BlockSpec((1, CHUNK, RET_V), lambda i, n: (i, n, OFF_RV // RET_V)),
                  pl.BlockSpec((1, CHUNK, RET_V), lambda i, n: (i, n, OFF_RG // RET_V)),
                  pl.BlockSpec((CHUNK, RET_DK), lambda i, n: (n, 0)),
                  pl.BlockSpec((CHUNK, RET_DK), lambda i, n: (n, 0))],
        out_specs=[pl.BlockSpec((1, CHUNK, RET_V), lambda i, n: (i, n, 0)),
                   pl.BlockSpec((1, RET_HEADS, RET_DK, RET_DV), lambda i, n: (i, 0, 0, 0))],
        out_shape=[jax.ShapeDtypeStruct((b, t, RET_V), BF16),
                   jax.ShapeDtypeStruct((b, RET_HEADS, RET_DK, RET_DV), F32)],
        scratch_shapes=[pltpu.VMEM((RET_HEADS, RET_DK, RET_DV), F32)],
        compiler_params=_params(("parallel", "arbitrary")),
        name="retention_prompt",
    )(proj3, proj3, proj3, proj3, cos, sin)


def _lower_bound(lbl):
    mx = jnp.max(lbl, axis=0, keepdims=True)
    e = jnp.exp(lbl - mx)
    return e[0:1, :] / jnp.sum(e, axis=0, keepdims=True)


def _hgrn_gates(z, lb):
    t = jnp.exp(-jnp.abs(z))
    r = 1.0 / (1.0 + t)
    pos = z >= 0
    sig_p = jnp.where(pos, r, t * r)
    sig_n = jnp.where(pos, t * r, r)
    f = lb + (1.0 - lb) * sig_p
    return f, (1.0 - lb) * sig_n


def _cumsum_rows(tri, x):
    hi = x.astype(BF16)
    r1 = x - hi.astype(F32)
    mid = r1.astype(BF16)
    lo = (r1 - mid.astype(F32)).astype(BF16)
    return _dot(tri, hi) + _dot(tri, mid) + _dot(tri, lo)


def _bcast_rows(x, group, r):
    c, d = x.shape
    x3 = x.reshape(c // group, group, d)
    return jnp.broadcast_to(x3[:, r:r + 1, :], x3.shape).reshape(c, d)


def _hgrn_kernel(q_ref, f_ref, i_ref, g_ref, lbl_ref, nrm_ref, ones_ref, o_ref, so_ref, st_ref):
    n = pl.program_id(1)

    @pl.when(n == 0)
    def _():
        st_ref[...] = jnp.zeros_like(st_ref)

    c = CHUNK
    row = lax.broadcasted_iota(jnp.int32, (c, c), 0)
    col = lax.broadcasted_iota(jnp.int32, (c, c), 1)
    rowd = lax.broadcasted_iota(jnp.int32, (c, HG_DK), 0)
    tri = jnp.where(row >= col, 1.0, 0.0).astype(BF16)
    diag_mask = ((row // SUB) == (col // SUB)) & (col <= row)
    ones = ones_ref[...]
    nrm = nrm_ref[...]

    def head(h, carry):
        off = pl.multiple_of(h * HG_DK, HG_DK)
        lb = _lower_bound(lbl_ref[:, pl.ds(off, HG_DK)])
        f, kin = _hgrn_gates(f_ref[0, :, pl.ds(off, HG_DK)], lb)
        qh = _silu(q_ref[0, :, pl.ds(off, HG_DK)])
        v = i_ref[0, :, pl.ds(off, HG_DV)].astype(BF16)
        b = _cumsum_rows(tri, jnp.log(f))

        parts = []
        for mp in range(SUB):
            bb = _bcast_rows(b, SUB, mp)
            kb = _bcast_rows(kin, SUB, mp)
            parts.append((qh * kb * jnp.exp(jnp.minimum(b - bb, 0.0))).astype(BF16))
        a = jnp.where(diag_mask, _dot(jnp.concatenate(parts, axis=1), ones), 0.0)
        lvl = SUB
        while lvl < c:
            ref = _bcast_rows(b, 2 * lvl, lvl - 1)
            upper = (rowd & (2 * lvl - 1)) >= lvl
            qt = jnp.where(upper, qh * jnp.exp(jnp.minimum(b - ref, 0.0)), 0.0)
            kt = jnp.where(upper, 0.0, kin * jnp.exp(jnp.minimum(ref - b, 0.0)))
            same = (row // (2 * lvl)) == (col // (2 * lvl))
            a = a + jnp.where(same, _dot_nt(qt.astype(BF16), kt.astype(BF16)), 0.0)
            lvl *= 2

        st = st_ref[h]
        b_last = b[c - 1:c, :]
        o = _dot(a.astype(BF16), v) + _dot_nt((qh * jnp.exp(b)).astype(BF16), st.astype(BF16))
        st_ref[h] = st * jnp.exp(b_last) + _dot_tn(v, (kin * jnp.exp(b_last - b)).astype(BF16))
        o = o * lax.rsqrt(jnp.mean(o * o, axis=-1, keepdims=True) + EPS) * nrm
        o_ref[0, :, pl.ds(off, HG_DV)] = (o * _silu(g_ref[0, :, pl.ds(off, HG_DV)])).astype(BF16)
        return carry

    lax.fori_loop(0, HG_HEADS, head, 0)

    @pl.when(n == pl.num_programs(1) - 1)
    def _():
        for h in range(HG_HEADS):
            so_ref[0, h] = st_ref[h].T


def _diag_sum_matrix():
    part = np.arange(SUB * HG_DK) // HG_DK
    return jnp.asarray((part[:, None] == (np.arange(CHUNK) % SUB)[None, :]).astype(np.float32), dtype=BF16)


def _hgrn_prompt(proj3, lbl, nrm):
    b, t, _ = proj3.shape
    nc = t // CHUNK
    return pl.pallas_call(
        _hgrn_kernel,
        grid=(b, nc),
        in_specs=[pl.BlockSpec((1, CHUNK, HG_K), lambda i, n: (i, n, OFF_HQ // HG_K)),
                  pl.BlockSpec((1, CHUNK, HG_K), lambda i, n: (i, n, OFF_HF // HG_K)),
                  pl.BlockSpec((1, CHUNK, HG_V), lambda i, n: (i, n, OFF_HI // HG_V)),
                  pl.BlockSpec((1, CHUNK, HG_V), lambda i, n: (i, n, OFF_HG // HG_V)),
                  pl.BlockSpec((DEPTH + 1, HG_K), lambda i, n: (0, 0)),
                  pl.BlockSpec((1, HG_DV), lambda i, n: (0, 0)),
                  pl.BlockSpec((SUB * HG_DK, CHUNK), lambda i, n: (0, 0))],
        out_specs=[pl.BlockSpec((1, CHUNK, HG_V), lambda i, n: (i, n, 0)),
                   pl.BlockSpec((1, HG_HEADS, HG_DK, HG_DV), lambda i, n: (i, 0, 0, 0))],
        out_shape=[jax.ShapeDtypeStruct((b, t, HG_V), BF16),
                   jax.ShapeDtypeStruct((b, HG_HEADS, HG_DK, HG_DV), F32)],
        scratch_shapes=[pltpu.VMEM((HG_HEADS, HG_DV, HG_DK), F32)],
        compiler_params=_params(("parallel", "arbitrary")),
        name="hgrn_prompt",
    )(proj3, proj3, proj3, proj3, lbl, nrm, _diag_sum_matrix())


def _sample_prep_kernel(p_ref, cos_ref, sin_ref, lbl_ref, qr_ref, kr_ref, qh_ref, f_ref, kin_ref):
    cos = cos_ref[...]
    sin = sin_ref[...]
    for h in range(RET_HEADS):
        sl = slice(h * RET_DK, (h + 1) * RET_DK)
        qr_ref[:, sl] = _rotary(p_ref[:, OFF_RQ + h * RET_DK:OFF_RQ + (h + 1) * RET_DK], cos, sin)
        kr_ref[:, sl] = _rotary(p_ref[:, OFF_RK + h * RET_DK:OFF_RK + (h + 1) * RET_DK], cos, sin) * (RET_DK ** -0.5)
    f, kin = _hgrn_gates(p_ref[:, OFF_HF:OFF_HF + HG_K], _lower_bound(lbl_ref[...]))
    f_ref[...] = f
    kin_ref[...] = kin
    qh_ref[...] = _silu(p_ref[:, OFF_HQ:OFF_HQ + HG_K])


def _sample_prep(proj, cos, sin, lbl):
    m = proj.shape[0]
    shp = jax.ShapeDtypeStruct((m, RET_QK), F32)
    return pl.pallas_call(
        _sample_prep_kernel,
        grid=(1,),
        in_specs=[pl.BlockSpec((m, OFF_GA), lambda i: (0, 0)),
                  pl.BlockSpec((1, RET_DK), lambda i: (0, 0)),
                  pl.BlockSpec((1, RET_DK), lambda i: (0, 0)),
                  pl.BlockSpec((DEPTH + 1, HG_K), lambda i: (0, 0))],
        out_specs=[pl.BlockSpec((m, RET_QK), lambda i: (0, 0))] * 5,
        out_shape=[shp] * 5,
        compiler_params=_params(("arbitrary",)),
        name="sample_prep",
    )(proj, cos, sin, lbl)


COL_QR, COL_KR, COL_QH, COL_F, COL_KIN, N_COLS = 0, 1, 2, 3, 4, 8


def _sample_state_kernel(sr_ref, sh_ref, cols_ref, vr_ref, vh_ref, rg_ref, hg_ref, nrm_ref,
                         sro_ref, sho_ref, or_ref, oh_ref):
    for h in range(RET_HEADS):
        cols = cols_ref[0, h]
        s = math.exp(LOG_GAMMA[h]) * sr_ref[0, h] + cols[:, COL_KR:COL_KR + 1] * vr_ref[0, :, h * RET_DV:(h + 1) * RET_DV]
        sro_ref[0, h] = s
        o = jnp.sum(cols[:, COL_QR:COL_QR + 1] * s, axis=0, keepdims=True)
        o = o * lax.rsqrt(jnp.mean(o * o, axis=-1, keepdims=True) + EPS)
        or_ref[0, :, h * RET_DV:(h + 1) * RET_DV] = o * _silu(rg_ref[0, :, h * RET_DV:(h + 1) * RET_DV])
    for h in range(HG_HEADS):
        cols = cols_ref[0, h]
        s = cols[:, COL_F:COL_F + 1] * sh_ref[0, h] + cols[:, COL_KIN:COL_KIN + 1] * vh_ref[0, :, h * HG_DV:(h + 1) * HG_DV]
        sho_ref[0, h] = s
        o = jnp.sum(cols[:, COL_QH:COL_QH + 1] * s, axis=0, keepdims=True)
        o = o * lax.rsqrt(jnp.mean(o * o, axis=-1, keepdims=True) + EPS) * nrm_ref[...]
        oh_ref[0, :, h * HG_DV:(h + 1) * HG_DV] = o * _silu(hg_ref[0, :, h * HG_DV:(h + 1) * HG_DV])


def _sample_state(s_ret, s_hg, cols, proj3, nrm):
    b = s_ret.shape[0]
    return pl.pallas_call(
        _sample_state_kernel,
        grid=(b,),
        in_specs=[pl.BlockSpec((1, RET_HEADS, RET_DK, RET_DV), lambda i: (i, 0, 0, 0)),
                  pl.BlockSpec((1, HG_HEADS, HG_DK, HG_DV), lambda i: (i, 0, 0, 0)),
                  pl.BlockSpec((1, RET_HEADS, RET_DK, N_COLS), lambda i: (i, 0, 0, 0)),
                  pl.BlockSpec((1, 1, RET_V), lambda i: (i, 0, OFF_RV // RET_V)),
                  pl.BlockSpec((1, 1, HG_V), lambda i: (i, 0, OFF_HI // HG_V)),
                  pl.BlockSpec((1, 1, RET_V), lambda i: (i, 0, OFF_RG // RET_V)),
                  pl.BlockSpec((1, 1, HG_V), lambda i: (i, 0, OFF_HG // HG_V)),
                  pl.BlockSpec((1, HG_DV), lambda i: (0, 0))],
        out_specs=[pl.BlockSpec((1, RET_HEADS, RET_DK, RET_DV), lambda i: (i, 0, 0, 0)),
                   pl.BlockSpec((1, HG_HEADS, HG_DK, HG_DV), lambda i: (i, 0, 0, 0)),
                   pl.BlockSpec((1, 1, RET_V), lambda i: (i, 0, 0)),
                   pl.BlockSpec((1, 1, HG_V), lambda i: (i, 0, 0))],
        out_shape=[jax.ShapeDtypeStruct(s_ret.shape, F32),
                   jax.ShapeDtypeStruct(s_hg.shape, F32),
                   jax.ShapeDtypeStruct((b, 1, RET_V), F32),
                   jax.ShapeDtypeStruct((b, 1, HG_V), F32)],
        compiler_params=_params(("parallel",)),
        name="sample_state",
    )(s_ret, s_hg, cols, proj3, proj3, proj3, proj3, nrm)


def _rope_tables(pos):
    half = RET_DK // 2
    inv_freq = ROPE_BASE ** (-jnp.arange(half, dtype=F32) / half)
    ang = pos.astype(F32)[:, None] * inv_freq[None, :]
    cos, sin = jnp.cos(ang), jnp.sin(ang)
    return jnp.concatenate([cos, cos], axis=-1), jnp.concatenate([-sin, sin], axis=-1)


def _dense_tail(x, proj, o_r, o_h, w, tm):
    mrg = _merge(o_r, o_h, w["ret_out"], w["hgrn_out"], proj, tm, 512)
    x1 = _resid(x, mrg, w["out"], tm, 1024)
    act = _ffn_in(x1, w["norm_ffn"], w["ffn_in"], tm, 512)
    return _ffn_out(act, w["ffn_out"], x1, w["norm_final"], tm, 512)


def kernel(x_prompt, x_sample, state_ret, state_hgrn, w_in, w_ret_out, w_hgrn_out, w_out, norm_mix, norm_ffn,
           hgrn_norm, hgrn_lb_logits, w_ffn_in, w_ffn_out, norm_final):
    assert w_in.shape[0] == DEPTH == 1
    bp, t, d = x_prompt.shape
    bs = x_sample.shape[0]
    w = {"in": w_in[0].astype(BF16), "ret_out": w_ret_out[0].astype(BF16), "hgrn_out": w_hgrn_out[0].astype(BF16),
         "out": w_out[0].astype(BF16), "ffn_in": w_ffn_in[0].astype(BF16), "ffn_out": w_ffn_out[0].astype(BF16),
         "norm_ffn": norm_ffn[0][None, :], "norm_final": norm_final[None, :]}
    g_mix = norm_mix[0][None, :]
    nrm = hgrn_norm[0][None, :]
    lbl = hgrn_lb_logits.astype(F32)

    xp = x_prompt.reshape(bp * t, d)
    proj_p = _proj(xp, g_mix, w["in"], 1024, 1024)
    proj_p3 = proj_p.reshape(bp, t, IN_TOTAL)
    cos_p, sin_p = _rope_tables(jnp.arange(t))
    o_r, s_ret_p = _retention_prompt(proj_p3, cos_p, sin_p)
    o_h, s_hg_p = _hgrn_prompt(proj_p3, lbl, nrm)
    y_p = _dense_tail(xp, proj_p, o_r.reshape(bp * t, RET_V), o_h.reshape(bp * t, HG_V), w, 1024)

    xs = x_sample.reshape(bs, d)
    proj_s = _proj(xs, g_mix, w["in"], bs, 1024)
    cos_s, sin_s = _rope_tables(PAST_LEN + jnp.arange(1))
    qr, kr, qh, f, kin = _sample_prep(proj_s, cos_s, sin_s, lbl)
    zeros = jnp.zeros_like(qr)
    cols = jnp.stack([qr, kr, qh, f, kin, zeros, zeros, zeros], axis=-1).reshape(bs, RET_HEADS, RET_DK, N_COLS)
    s_ret_s, s_hg_s, o_rs, o_hs = _sample_state(state_ret[0], state_hgrn[0], cols, proj_s.reshape(bs, 1, IN_TOTAL), nrm)
    y_s = _dense_tail(xs, proj_s, o_rs.reshape(bs, RET_V).astype(BF16), o_hs.reshape(bs, HG_V).astype(BF16), w, bs)

    return (y_p.reshape(bp, t, d), y_s.reshape(bs, 1, d), s_ret_p[None], s_hg_p[None], s_ret_s[None], s_hg_s[None])
```

```python
import functools
import math

import numpy as np
import jax
import jax.numpy as jnp
from jax import lax
from jax.experimental import pallas as pl
from jax.experimental.pallas import tpu as pltpu

D_MODEL = 2048
DEPTH = 1
PAST_LEN = 16384
RET_HEADS = 8
RET_DK = 128
RET_DV = 256
RET_QK = RET_HEADS * RET_DK
RET_V = RET_HEADS * RET_DV
ROPE_BASE = 10000.0
HG_HEADS = 8
HG_DK = 128
HG_DV = 128
HG_K = HG_HEADS * HG_DK
HG_V = HG_HEADS * HG_DV
D_FF = 5632
EPS = 1e-6

OFF_RQ = 0
OFF_RK = OFF_RQ + RET_QK
OFF_RV = OFF_RK + RET_QK
OFF_RG = OFF_RV + RET_V
OFF_HQ = OFF_RG + RET_V
OFF_HF = OFF_HQ + HG_K
OFF_HI = OFF_HF + HG_K
OFF_HG = OFF_HI + HG_V
OFF_GA = OFF_HG + HG_V
OFF_GB = OFF_GA + D_MODEL
IN_TOTAL = OFF_GB + D_MODEL

CHUNK = 128
SUB = 8
V7X_VMEM_LIMIT = 56 * 1024 * 1024

F32 = jnp.float32
BF16 = jnp.bfloat16

LOG_GAMMA = [float(np.log(np.float32(1.0) - np.float32(2.0) ** np.float32(-5.0 - h))) for h in range(RET_HEADS)]


def _params(sem, vmem=V7X_VMEM_LIMIT):
    return pltpu.CompilerParams(dimension_semantics=sem, vmem_limit_bytes=vmem)


def _rms_rows(x, g):
    return (x * lax.rsqrt(jnp.mean(x * x, axis=-1, keepdims=True) + EPS)) * g


def _sigmoid(x):
    return 1.0 / (1.0 + jnp.exp(-x))


def _silu(x):
    return x * _sigmoid(x)


def _dot(a, b):
    return jnp.dot(a, b, preferred_element_type=F32)


def _dot_nt(a, b):
    return lax.dot_general(a, b, (((1,), (1,)), ((), ())), preferred_element_type=F32)


def _dot_tn(a, b):
    return lax.dot_general(a, b, (((0,), (0,)), ((), ())), preferred_element_type=F32)


def _proj_kernel(x_ref, g_ref, w_ref, o_ref, h_ref):
    @pl.when(pl.program_id(1) == 0)
    def _():
        h_ref[...] = _rms_rows(x_ref[...], g_ref[...]).astype(BF16)
    o_ref[...] = _dot(h_ref[...], w_ref[...])


def _proj(x, g, w, tm, tn):
    m, d = x.shape
    n = w.shape[1]
    return pl.pallas_call(
        _proj_kernel,
        grid=(m // tm, n // tn),
        in_specs=[pl.BlockSpec((tm, d), lambda i, j: (i, 0)),
                  pl.BlockSpec((1, d), lambda i, j: (0, 0)),
                  pl.BlockSpec((d, tn), lambda i, j: (0, j))],
        out_specs=pl.BlockSpec((tm, tn), lambda i, j: (i, j)),
        out_shape=jax.ShapeDtypeStruct((m, n), F32),
        scratch_shapes=[pltpu.VMEM((tm, d), BF16)],
        compiler_params=_params(("parallel", "arbitrary")),
        name="proj",
    )(x, g, w)


def _merge_kernel(or_ref, oh_ref, wr_ref, wh_ref, ga_ref, gb_ref, o_ref):
    ya = _dot(or_ref[...], wr_ref[...])
    yb = _dot(oh_ref[...], wh_ref[...])
    o_ref[...] = (_sigmoid(ga_ref[...]) * ya + _sigmoid(gb_ref[...]) * yb).astype(BF16)


def _merge(o_r, o_h, w_r, w_h, proj, tm, tn):
    m = o_r.shape[0]
    n = w_r.shape[1]
    return pl.pallas_call(
        _merge_kernel,
        grid=(m // tm, n // tn),
        in_specs=[pl.BlockSpec((tm, RET_V), lambda i, j: (i, 0)),
                  pl.BlockSpec((tm, HG_V), lambda i, j: (i, 0)),
                  pl.BlockSpec((RET_V, tn), lambda i, j: (0, j)),
                  pl.BlockSpec((HG_V, tn), lambda i, j: (0, j)),
                  pl.BlockSpec((tm, tn), lambda i, j: (i, OFF_GA // tn + j)),
                  pl.BlockSpec((tm, tn), lambda i, j: (i, OFF_GB // tn + j))],
        out_specs=pl.BlockSpec((tm, tn), lambda i, j: (i, j)),
        out_shape=jax.ShapeDtypeStruct((m, n), BF16),
        compiler_params=_params(("parallel", "arbitrary")),
        name="merge",
    )(o_r, o_h, w_r, w_h, proj, proj)


def _resid_kernel(x_ref, m_ref, w_ref, o_ref):
    o_ref[...] = x_ref[...] + _dot(m_ref[...], w_ref[...])


def _resid(x, mrg, w, tm, tn):
    m, d = x.shape
    return pl.pallas_call(
        _resid_kernel,
        grid=(m // tm, d // tn),
        in_specs=[pl.BlockSpec((tm, tn), lambda i, j: (i, j)),
                  pl.BlockSpec((tm, d), lambda i, j: (i, 0)),
                  pl.BlockSpec((d, tn), lambda i, j: (0, j))],
        out_specs=pl.BlockSpec((tm, tn), lambda i, j: (i, j)),
        out_shape=jax.ShapeDtypeStruct((m, d), F32),
        compiler_params=_params(("parallel", "arbitrary")),
        name="resid",
    )(x, mrg, w)


def _ffn_in_kernel(x_ref, g_ref, wg_ref, wu_ref, o_ref, h_ref):
    @pl.when(pl.program_id(1) == 0)
    def _():
        h_ref[...] = _rms_rows(x_ref[...], g_ref[...]).astype(BF16)
    h = h_ref[...]
    gate = _dot(h, wg_ref[...])
    up = _dot(h, wu_ref[...])
    o_ref[...] = (_silu(gate) * up).astype(BF16)


def _ffn_in(x, g, w, tm, tn):
    m, d = x.shape
    return pl.pallas_call(
        _ffn_in_kernel,
        grid=(m // tm, D_FF // tn),
        in_specs=[pl.BlockSpec((tm, d), lambda i, j: (i, 0)),
                  pl.BlockSpec((1, d), lambda i, j: (0, 0)),
                  pl.BlockSpec((d, tn), lambda i, j: (0, j)),
                  pl.BlockSpec((d, tn), lambda i, j: (0, D_FF // tn + j))],
        out_specs=pl.BlockSpec((tm, tn), lambda i, j: (i, j)),
        out_shape=jax.ShapeDtypeStruct((m, D_FF), BF16),
        scratch_shapes=[pltpu.VMEM((tm, d), BF16)],
        compiler_params=_params(("parallel", "arbitrary")),
        name="ffn_in",
    )(x, g, w, w)


def _ffn_out_kernel(a_ref, w_ref, x_ref, g_ref, o_ref, acc_ref):
    k = pl.program_id(1)

    @pl.when(k == 0)
    def _():
        acc_ref[...] = jnp.zeros_like(acc_ref)

    acc_ref[...] += _dot(a_ref[...], w_ref[...])

    @pl.when(k == pl.num_programs(1) - 1)
    def _():
        o_ref[...] = _rms_rows(x_ref[...] + acc_ref[...], g_ref[...])


def _ffn_out(act, w, x, g, tm, tk):
    m, d = x.shape
    return pl.pallas_call(
        _ffn_out_kernel,
        grid=(m // tm, D_FF // tk),
        in_specs=[pl.BlockSpec((tm, tk), lambda i, k: (i, k)),
                  pl.BlockSpec((tk, d), lambda i, k: (k, 0)),
                  pl.BlockSpec((tm, d), lambda i, k: (i, 0)),
                  pl.BlockSpec((1, d), lambda i, k: (0, 0))],
        out_specs=pl.BlockSpec((tm, d), lambda i, k: (i, 0)),
        out_shape=jax.ShapeDtypeStruct((m, d), F32),
        scratch_shapes=[pltpu.VMEM((tm, d), F32)],
        compiler_params=_params(("parallel", "arbitrary")),
        name="ffn_out",
    )(act, w, x, g)


def _rotary(x, cos, sin_signed):
    return x * cos + pltpu.roll(x, RET_DK // 2, 1) * sin_signed


def _ret_kernel(q_ref, k_ref, v_ref, g_ref, cos_ref, sin_ref, o_ref, so_ref, s_ref):
    n = pl.program_id(1)

    @pl.when(n == 0)
    def _():
        s_ref[...] = jnp.zeros_like(s_ref)

    c = CHUNK
    cos = cos_ref[...]
    sin = sin_ref[...]
    row = lax.broadcasted_iota(jnp.int32, (c, c), 0)
    col = lax.broadcasted_iota(jnp.int32, (c, c), 1)
    diff = (row - col).astype(F32)
    idx = lax.broadcasted_iota(jnp.int32, (c, RET_DK), 0).astype(F32)
    for h in range(RET_HEADS):
        lg = LOG_GAMMA[h]
        q = _rotary(q_ref[0, :, h * RET_DK:(h + 1) * RET_DK], cos, sin)
        k = _rotary(k_ref[0, :, h * RET_DK:(h + 1) * RET_DK], cos, sin) * (RET_DK ** -0.5)
        v = v_ref[0, :, h * RET_DV:(h + 1) * RET_DV].astype(BF16)
        decay_mask = jnp.where(diff >= 0, jnp.exp(jnp.maximum(diff, 0.0) * lg), 0.0)
        q_decay = jnp.exp((idx + 1.0) * lg)
        k_decay = jnp.exp((c - 1.0 - idx) * lg)
        chunk_decay = math.exp(c * lg)
        s = s_ref[h]
        scores = _dot_nt(q.astype(BF16), k.astype(BF16)) * decay_mask
        o = _dot(scores.astype(BF16), v) + _dot((q * q_decay).astype(BF16), s.astype(BF16))
        s_ref[h] = chunk_decay * s + _dot_tn((k * k_decay).astype(BF16), v)
        o = o * lax.rsqrt(jnp.mean(o * o, axis=-1, keepdims=True) + EPS)
        gate = g_ref[0, :, h * RET_DV:(h + 1) * RET_DV]
        o_ref[0, :, h * RET_DV:(h + 1) * RET_DV] = (o * _silu(gate)).astype(BF16)

    @pl.when(n == pl.num_programs(1) - 1)
    def _():
        so_ref[0] = s_ref[...]


def _retention_prompt(proj3, cos, sin):
    b, t, _ = proj3.shape
    nc = t // CHUNK
    return pl.pallas_call(
        _ret_kernel,
        grid=(b, nc),
        in_specs=[pl.BlockSpec((1, CHUNK, RET_QK), lambda i, n: (i, n, OFF_RQ // RET_QK)),
                  pl.BlockSpec((1, CHUNK, RET_QK), lambda i, n: (i, n, OFF_RK // RET_QK)),
                  pl.BlockSpec((1, CHUNK, RET_V), lambda i, n: (i, n, OFF_RV // RET_V)),
                  pl.BlockSpec((1, CHUNK, RET_V), lambda i, n: (i, n, OFF_RG // RET_V)),
                  pl.BlockSpec((CHUNK, RET_DK), lambda i, n: (n, 0)),
                  pl.BlockSpec((CHUNK, RET_DK), lambda i, n: (n, 0))],
        out_specs=[pl.BlockSpec((1, CHUNK, RET_V), lambda i, n: (i, n, 0)),
                   pl.BlockSpec((1, RET_HEADS, RET_DK, RET_DV), lambda i, n: (i, 0, 0, 0))],
        out_shape=[jax.ShapeDtypeStruct((b, t, RET_V), BF16),
                   jax.ShapeDtypeStruct((b, RET_HEADS, RET_DK, RET_DV), F32)],
        scratch_shapes=[pltpu.VMEM((RET_HEADS, RET_DK, RET_DV), F32)],
        compiler_params=_params(("parallel", "arbitrary")),
        name="retention_prompt",
    )(proj3, proj3, proj3, proj3, cos, sin)


def _lower_bound(lbl):
    mx = jnp.max(lbl, axis=0, keepdims=True)
    e = jnp.exp(lbl - mx)
    return e[0:1, :] / jnp.sum(e, axis=0, keepdims=True)


def _hgrn_gates(z, lb):
    t = jnp.exp(-jnp.abs(z))
    r = 1.0 / (1.0 + t)
    pos = z >= 0
    sig_p = jnp.where(pos, r, t * r)
    sig_n = jnp.where(pos, t * r, r)
    f = lb + (1.0 - lb) * sig_p
    return f, (1.0 - lb) * sig_n


def _cumsum_rows(tri, x):
    hi = x.astype(BF16)
    r1 = x - hi.astype(F32)
    mid = r1.astype(BF16)
    lo = (r1 - mid.astype(F32)).astype(BF16)
    return _dot(tri, hi) + _dot(tri, mid) + _dot(tri, lo)


def _bcast_rows(x, group, r):
    c, d = x.shape
    x3 = x.reshape(c // group, group, d)
    return jnp.broadcast_to(x3[:, r:r + 1, :], x3.shape).reshape(c, d)


def _hgrn_kernel(q_ref, f_ref, i_ref, g_ref, lbl_ref, nrm_ref, ones_ref, o_ref, so_ref, st_ref):
    n = pl.program_id(1)

    @pl.when(n == 0)
    def _():
        st_ref[...] = jnp.zeros_like(st_ref)

    c = CHUNK
    row = lax.broadcasted_iota(jnp.int32, (c, c), 0)
    col = lax.broadcasted_iota(jnp.int32, (c, c), 1)
    rowd = lax.broadcasted_iota(jnp.int32, (c, HG_DK), 0)
    tri = jnp.where(row >= col, 1.0, 0.0).astype(BF16)
    diag_mask = ((row // SUB) == (col // SUB)) & (col <= row)
    ones = ones_ref[...]
    nrm = nrm_ref[...]

    def head(h, carry):
        off = pl.multiple_of(h * HG_DK, HG_DK)
        lb = _lower_bound(lbl_ref[:, pl.ds(off, HG_DK)])
        f, kin = _hgrn_gates(f_ref[0, :, pl.ds(off, HG_DK)], lb)
        qh = _silu(q_ref[0, :, pl.ds(off, HG_DK)])
        v = i_ref[0, :, pl.ds(off, HG_DV)].astype(BF16)
        b = _cumsum_rows(tri, jnp.log(f))

        parts = []
        for mp in range(SUB):
            bb = _bcast_rows(b, SUB, mp)
            kb = _bcast_rows(kin, SUB, mp)
            parts.append((qh * kb * jnp.exp(jnp.minimum(b - bb, 0.0))).astype(BF16))
        a = jnp.where(diag_mask, _dot(jnp.concatenate(parts, axis=1), ones), 0.0)
        lvl = SUB
        while lvl < c:
            ref = _bcast_rows(b, 2 * lvl, lvl - 1)
            upper = (rowd & (2 * lvl - 1)) >= lvl
            qt = jnp.where(upper, qh * jnp.exp(jnp.minimum(b - ref, 0.0)), 0.0)
            kt = jnp.where(upper, 0.0, kin * jnp.exp(jnp.minimum(ref - b, 0.0)))
            same = (row // (2 * lvl)) == (col // (2 * lvl))
            a = a + jnp.where(same, _dot_nt(qt.astype(BF16), kt.astype(BF16)), 0.0)
            lvl *= 2

        st = st_ref[h]
        b_last = b[c - 1:c, :]
        o = _dot(a.astype(BF16), v) + _dot_nt((qh * jnp.exp(b)).astype(BF16), st.astype(BF16))
        st_ref[h] = st * jnp.exp(b_last) + _dot_tn(v, (kin * jnp.exp(b_last - b)).astype(BF16))
        o = o * lax.rsqrt(jnp.mean(o * o, axis=-1, keepdims=True) + EPS) * nrm
        o_ref[0, :, pl.ds(off, HG_DV)] = (o * _silu(g_ref[0, :, pl.ds(off, HG_DV)])).astype(BF16)
        return carry

    lax.fori_loop(0, HG_HEADS, head, 0)

    @pl.when(n == pl.num_programs(1) - 1)
    def _():
        for h in range(HG_HEADS):
            so_ref[0, h] = st_ref[h].T


def _diag_sum_matrix():
    part = np.arange(SUB * HG_DK) // HG_DK
    return jnp.asarray((part[:, None] == (np.arange(CHUNK) % SUB)[None, :]).astype(np.float32), dtype=BF16)


def _hgrn_prompt(proj3, lbl, nrm):
    b, t, _ = proj3.shape
    nc = t // CHUNK
    return pl.pallas_call(
        _hgrn_kernel,
        grid=(b, nc),
        in_specs=[pl.BlockSpec((1, CHUNK, HG_K), lambda i, n: (i, n, OFF_HQ // HG_K)),
                  pl.BlockSpec((1, CHUNK, HG_K), lambda i, n: (i, n, OFF_HF // HG_K)),
                  pl.BlockSpec((1, CHUNK, HG_V), lambda i, n: (i, n, OFF_HI // HG_V)),
                  pl.BlockSpec((1, CHUNK, HG_V), lambda i, n: (i, n, OFF_HG // HG_V)),
                  pl.BlockSpec((DEPTH + 1, HG_K), lambda i, n: (0, 0)),
                  pl.BlockSpec((1, HG_DV), lambda i, n: (0, 0)),
                  pl.BlockSpec((SUB * HG_DK, CHUNK), lambda i, n: (0, 0))],
        out_specs=[pl.BlockSpec((1, CHUNK, HG_V), lambda i, n: (i, n, 0)),
                   pl.BlockSpec((1, HG_HEADS, HG_DK, HG_DV), lambda i, n: (i, 0, 0, 0))],
        out_shape=[jax.ShapeDtypeStruct((b, t, HG_V), BF16),
                   jax.ShapeDtypeStruct((b, HG_HEADS, HG_DK, HG_DV), F32)],
        scratch_shapes=[pltpu.VMEM((HG_HEADS, HG_DV, HG_DK), F32)],
        compiler_params=_params(("parallel", "arbitrary")),
        name="hgrn_prompt",
    )(proj3, proj3, proj3, proj3, lbl, nrm, _diag_sum_matrix())


def _sample_prep_kernel(p_ref, cos_ref, sin_ref, lbl_ref, qr_ref, kr_ref, qh_ref, f_ref, kin_ref):
    cos = cos_ref[...]
    sin = sin_ref[...]
    for h in range(RET_HEADS):
        sl = slice(h * RET_DK, (h + 1) * RET_DK)
        qr_ref[:, sl] = _rotary(p_ref[:, OFF_RQ + h * RET_DK:OFF_RQ + (h + 1) * RET_DK], cos, sin)
        kr_ref[:, sl] = _rotary(p_ref[:, OFF_RK + h * RET_DK:OFF_RK + (h + 1) * RET_DK], cos, sin) * (RET_DK ** -0.5)
    f, kin = _hgrn_gates(p_ref[:, OFF_HF:OFF_HF + HG_K], _lower_bound(lbl_ref[...]))
    f_ref[...] = f
    kin_ref[...] = kin
    qh_ref[...] = _silu(p_ref[:, OFF_HQ:OFF_HQ + HG_K])


def _sample_prep(proj, cos, sin, lbl):
    m = proj.shape[0]
    shp = jax.ShapeDtypeStruct((m, RET_QK), F32)
    return pl.pallas_call(
        _sample_prep_kernel,
        grid=(1,),
        in_specs=[pl.BlockSpec((m, OFF_GA), lambda i: (0, 0)),
                  pl.BlockSpec((1, RET_DK), lambda i: (0, 0)),
                  pl.BlockSpec((1, RET_DK), lambda i: (0, 0)),
                  pl.BlockSpec((DEPTH + 1, HG_K), lambda i: (0, 0))],
        out_specs=[pl.BlockSpec((m, RET_QK), lambda i: (0, 0))] * 5,
        out_shape=[shp] * 5,
        compiler_params=_params(("arbitrary",)),
        name="sample_prep",
    )(proj, cos, sin, lbl)


def _sample_state_kernel(sr_ref, sh_ref, qr_ref, kr_ref, qh_ref, f_ref, kin_ref, vr_ref, rg_ref, vh_ref, hg_ref,
                         nrm_ref, sro_ref, sho_ref, or_ref, oh_ref):
    b = pl.program_id(0)
    base = pl.multiple_of((b // SUB) * SUB, SUB)

    def row(ref, sl):
        tile = ref[pl.ds(base, SUB), sl]
        sub = lax.broadcasted_iota(jnp.int32, tile.shape, 0)
        return jnp.sum(jnp.where(sub == b % SUB, tile, 0.0), axis=0, keepdims=True)

    def column(ref, h):
        r = row(ref, slice(h * RET_DK, (h + 1) * RET_DK))
        return jnp.broadcast_to(r, (RET_DK, RET_DK)).T

    lanes = RET_DK
    for h in range(RET_HEADS):
        kc = column(kr_ref, h)
        qc = column(qr_ref, h)
        gamma = math.exp(LOG_GAMMA[h])
        outs = []
        for half in range(RET_DV // lanes):
            hs = slice(half * lanes, (half + 1) * lanes)
            s = gamma * sr_ref[0, h, :, hs] + kc * row(vr_ref, slice(h * RET_DV + hs.start, h * RET_DV + hs.stop))
            sro_ref[0, h, :, hs] = s
            outs.append(jnp.sum(qc * s, axis=0, keepdims=True))
        o = jnp.concatenate(outs, axis=1)
        o = o * lax.rsqrt(jnp.mean(o * o, axis=-1, keepdims=True) + EPS)
        sl = slice(h * RET_DV, (h + 1) * RET_DV)
        or_ref[0, :, sl] = o * _silu(row(rg_ref, sl))
    for h in range(HG_HEADS):
        sl = slice(h * HG_DV, (h + 1) * HG_DV)
        s = column(f_ref, h) * sh_ref[0, h] + column(kin_ref, h) * row(vh_ref, sl)
        sho_ref[0, h] = s
        o = jnp.sum(column(qh_ref, h) * s, axis=0, keepdims=True)
        o = o * lax.rsqrt(jnp.mean(o * o, axis=-1, keepdims=True) + EPS) * nrm_ref[...]
        oh_ref[0, :, sl] = o * _silu(row(hg_ref, sl))


def _sample_state(s_ret, s_hg, rows, proj, nrm):
    b = s_ret.shape[0]
    row_spec = pl.BlockSpec((b, RET_QK), lambda i: (0, 0))
    return pl.pallas_call(
        _sample_state_kernel,
        grid=(b,),
        in_specs=[pl.BlockSpec((1, RET_HEADS, RET_DK, RET_DV), lambda i: (i, 0, 0, 0)),
                  pl.BlockSpec((1, HG_HEADS, HG_DK, HG_DV), lambda i: (i, 0, 0, 0)),
                  row_spec, row_spec, row_spec, row_spec, row_spec,
                  pl.BlockSpec((b, RET_V), lambda i: (0, OFF_RV // RET_V)),
                  pl.BlockSpec((b, RET_V), lambda i: (0, OFF_RG // RET_V)),
                  pl.BlockSpec((b, HG_V), lambda i: (0, OFF_HI // HG_V)),
                  pl.BlockSpec((b, HG_V), lambda i: (0, OFF_HG // HG_V)),
                  pl.BlockSpec((1, HG_DV), lambda i: (0, 0))],
        out_specs=[pl.BlockSpec((1, RET_HEADS, RET_DK, RET_DV), lambda i: (i, 0, 0, 0)),
                   pl.BlockSpec((1, HG_HEADS, HG_DK, HG_DV), lambda i: (i, 0, 0, 0)),
                   pl.BlockSpec((1, 1, RET_V), lambda i: (i, 0, 0)),
                   pl.BlockSpec((1, 1, HG_V), lambda i: (i, 0, 0))],
        out_shape=[jax.ShapeDtypeStruct(s_ret.shape, F32),
                   jax.ShapeDtypeStruct(s_hg.shape, F32),
                   jax.ShapeDtypeStruct((b, 1, RET_V), F32),
                   jax.ShapeDtypeStruct((b, 1, HG_V), F32)],
        compiler_params=_params(("arbitrary",)),
        name="sample_state",
    )(s_ret, s_hg, *rows, proj, proj, proj, proj, nrm)


def _rope_tables(pos):
    half = RET_DK // 2
    inv_freq = ROPE_BASE ** (-jnp.arange(half, dtype=F32) / half)
    ang = pos.astype(F32)[:, None] * inv_freq[None, :]
    cos, sin = jnp.cos(ang), jnp.sin(ang)
    return jnp.concatenate([cos, cos], axis=-1), jnp.concatenate([-sin, sin], axis=-1)


def _dense_tail(x, proj, o_r, o_h, w, tm):
    mrg = _merge(o_r, o_h, w["ret_out"], w["hgrn_out"], proj, tm, 512)
    x1 = _resid(x, mrg, w["out"], tm, 1024)
    act = _ffn_in(x1, w["norm_ffn"], w["ffn_in"], tm, 512)
    return _ffn_out(act, w["ffn_out"], x1, w["norm_final"], tm, 512)


def kernel(x_prompt, x_sample, state_ret, state_hgrn, w_in, w_ret_out, w_hgrn_out, w_out, norm_mix, norm_ffn,
           hgrn_norm, hgrn_lb_logits, w_ffn_in, w_ffn_out, norm_final):
    assert w_in.shape[0] == DEPTH == 1
    bp, t, d = x_prompt.shape
    bs = x_sample.shape[0]
    w = {"in": w_in[0].astype(BF16), "ret_out": w_ret_out[0].astype(BF16), "hgrn_out": w_hgrn_out[0].astype(BF16),
         "out": w_out[0].astype(BF16), "ffn_in": w_ffn_in[0].astype(BF16), "ffn_out": w_ffn_out[0].astype(BF16),
         "norm_ffn": norm_ffn[0][None, :], "norm_final": norm_final[None, :]}
    g_mix = norm_mix[0][None, :]
    nrm = hgrn_norm[0][None, :]
    lbl = hgrn_lb_logits.astype(F32)

    xp = x_prompt.reshape(bp * t, d)
    proj_p = _proj(xp, g_mix, w["in"], 1024, 1024)
    proj_p3 = proj_p.reshape(bp, t, IN_TOTAL)
    cos_p, sin_p = _rope_tables(jnp.arange(t))
    o_r, s_ret_p = _retention_prompt(proj_p3, cos_p, sin_p)
    o_h, s_hg_p = _hgrn_prompt(proj_p3, lbl, nrm)
    y_p = _dense_tail(xp, proj_p, o_r.reshape(bp * t, RET_V), o_h.reshape(bp * t, HG_V), w, 1024)

    xs = x_sample.reshape(bs, d)
    proj_s = _proj(xs, g_mix, w["in"], bs, 1024)
    cos_s, sin_s = _rope_tables(PAST_LEN + jnp.arange(1))
    rows = _sample_prep(proj_s, cos_s, sin_s, lbl)
    s_ret_s, s_hg_s, o_rs, o_hs = _sample_state(state_ret[0], state_hgrn[0], rows, proj_s, nrm)
    y_s = _dense_tail(xs, proj_s, o_rs.reshape(bs, RET_V).astype(BF16), o_hs.reshape(bs, HG_V).astype(BF16), w, bs)

    return (y_p.reshape(bp, t, d), y_s.reshape(bs, 1, d), s_ret_p[None], s_hg_p[None], s_ret_s[None], s_hg_s[None])
```

```python
import functools
import math

import numpy as np
import jax
import jax.numpy as jnp
from jax import lax
from jax.experimental import pallas as pl
from jax.experimental.pallas import tpu as pltpu

D_MODEL = 2048
DEPTH = 1
PAST_LEN = 16384
RET_HEADS = 8
RET_DK = 128
RET_DV = 256
RET_QK = RET_HEADS * RET_DK
RET_V = RET_HEADS * RET_DV
ROPE_BASE = 10000.0
HG_HEADS = 8
HG_DK = 128
HG_DV = 128
HG_K = HG_HEADS * HG_DK
HG_V = HG_HEADS * HG_DV
D_FF = 5632
EPS = 1e-6

OFF_RQ = 0
OFF_RK = OFF_RQ + RET_QK
OFF_RV = OFF_RK + RET_QK
OFF_RG = OFF_RV + RET_V
OFF_HQ = OFF_RG + RET_V
OFF_HF = OFF_HQ + HG_K
OFF_HI = OFF_HF + HG_K
OFF_HG = OFF_HI + HG_V
OFF_GA = OFF_HG + HG_V
OFF_GB = OFF_GA + D_MODEL
IN_TOTAL = OFF_GB + D_MODEL

CHUNK = 128
SUB = 8
V7X_VMEM_LIMIT = 56 * 1024 * 1024

F32 = jnp.float32
BF16 = jnp.bfloat16

LOG_GAMMA = [float(np.log(np.float32(1.0) - np.float32(2.0) ** np.float32(-5.0 - h))) for h in range(RET_HEADS)]


def _params(sem, vmem=V7X_VMEM_LIMIT):
    return pltpu.CompilerParams(dimension_semantics=sem, vmem_limit_bytes=vmem)


def _rms_rows(x, g):
    return (x * lax.rsqrt(jnp.mean(x * x, axis=-1, keepdims=True) + EPS)) * g


def _sigmoid(x):
    return 0.5 + 0.5 * jnp.tanh(0.5 * x)


def _silu(x):
    hx = 0.5 * x
    return hx + hx * jnp.tanh(hx)


def _dot(a, b):
    return jnp.dot(a, b, preferred_element_type=F32)


def _dot_nt(a, b):
    return lax.dot_general(a, b, (((1,), (1,)), ((), ())), preferred_element_type=F32)


def _dot_tn(a, b):
    return lax.dot_general(a, b, (((0,), (0,)), ((), ())), preferred_element_type=F32)


def _proj_kernel(x_ref, g_ref, w_ref, o_ref, h_ref):
    @pl.when(pl.program_id(1) == 0)
    def _():
        h_ref[...] = _rms_rows(x_ref[...], g_ref[...]).astype(BF16)
    o_ref[...] = _dot(h_ref[...], w_ref[...].astype(BF16))


def _proj(x, g, w, tm, tn):
    m, d = x.shape
    n = w.shape[1]
    return pl.pallas_call(
        _proj_kernel,
        grid=(m // tm, n // tn),
        in_specs=[pl.BlockSpec((tm, d), lambda i, j: (i, 0)),
                  pl.BlockSpec((1, d), lambda i, j: (0, 0)),
                  pl.BlockSpec((d, tn), lambda i, j: (0, j))],
        out_specs=pl.BlockSpec((tm, tn), lambda i, j: (i, j)),
        out_shape=jax.ShapeDtypeStruct((m, n), F32),
        scratch_shapes=[pltpu.VMEM((tm, d), BF16)],
        compiler_params=_params(("parallel", "arbitrary")),
        name="proj",
    )(x, g, w)


def _merge_kernel(or_ref, oh_ref, wr_ref, wh_ref, ga_ref, gb_ref, o_ref):
    ya = _dot(or_ref[...], wr_ref[...].astype(BF16))
    yb = _dot(oh_ref[...], wh_ref[...].astype(BF16))
    o_ref[...] = (_sigmoid(ga_ref[...]) * ya + _sigmoid(gb_ref[...]) * yb).astype(BF16)


def _merge(o_r, o_h, w_r, w_h, proj, tm, tn):
    m = o_r.shape[0]
    n = w_r.shape[1]
    return pl.pallas_call(
        _merge_kernel,
        grid=(m // tm, n // tn),
        in_specs=[pl.BlockSpec((tm, RET_V), lambda i, j: (i, 0)),
                  pl.BlockSpec((tm, HG_V), lambda i, j: (i, 0)),
                  pl.BlockSpec((RET_V, tn), lambda i, j: (0, j)),
                  pl.BlockSpec((HG_V, tn), lambda i, j: (0, j)),
                  pl.BlockSpec((tm, tn), lambda i, j: (i, OFF_GA // tn + j)),
                  pl.BlockSpec((tm, tn), lambda i, j: (i, OFF_GB // tn + j))],
        out_specs=pl.BlockSpec((tm, tn), lambda i, j: (i, j)),
        out_shape=jax.ShapeDtypeStruct((m, n), BF16),
        compiler_params=_params(("parallel", "arbitrary")),
        name="merge",
    )(o_r, o_h, w_r, w_h, proj, proj)


def _resid_kernel(x_ref, m_ref, w_ref, o_ref):
    o_ref[...] = x_ref[...] + _dot(m_ref[...], w_ref[...].astype(BF16))


def _resid(x, mrg, w, tm, tn):
    m, d = x.shape
    return pl.pallas_call(
        _resid_kernel,
        grid=(m // tm, d // tn),
        in_specs=[pl.BlockSpec((tm, tn), lambda i, j: (i, j)),
                  pl.BlockSpec((tm, d), lambda i, j: (i, 0)),
                  pl.BlockSpec((d, tn), lambda i, j: (0, j))],
        out_specs=pl.BlockSpec((tm, tn), lambda i, j: (i, j)),
        out_shape=jax.ShapeDtypeStruct((m, d), F32),
        compiler_params=_params(("parallel", "arbitrary")),
        name="resid",
    )(x, mrg, w)


def _ffn_in_kernel(x_ref, g_ref, wg_ref, wu_ref, o_ref, h_ref):
    @pl.when(pl.program_id(1) == 0)
    def _():
        h_ref[...] = _rms_rows(x_ref[...], g_ref[...]).astype(BF16)
    h = h_ref[...]
    gate = _dot(h, wg_ref[...].astype(BF16))
    up = _dot(h, wu_ref[...].astype(BF16))
    o_ref[...] = (_silu(gate) * up).astype(BF16)


def _ffn_in(x, g, w, tm, tn):
    m, d = x.shape
    return pl.pallas_call(
        _ffn_in_kernel,
        grid=(m // tm, D_FF // tn),
        in_specs=[pl.BlockSpec((tm, d), lambda i, j: (i, 0)),
                  pl.BlockSpec((1, d), lambda i, j: (0, 0)),
                  pl.BlockSpec((d, tn), lambda i, j: (0, j)),
                  pl.BlockSpec((d, tn), lambda i, j: (0, D_FF // tn + j))],
        out_specs=pl.BlockSpec((tm, tn), lambda i, j: (i, j)),
        out_shape=jax.ShapeDtypeStruct((m, D_FF), BF16),
        scratch_shapes=[pltpu.VMEM((tm, d), BF16)],
        compiler_params=_params(("parallel", "arbitrary")),
        name="ffn_in",
    )(x, g, w, w)


def _ffn_out_kernel(a_ref, w_ref, x_ref, g_ref, o_ref):
    k = pl.program_id(1)

    @pl.when(k == 0)
    def _():
        o_ref[...] = x_ref[...]

    o_ref[...] += _dot(a_ref[...], w_ref[...].astype(BF16))

    @pl.when(k == pl.num_programs(1) - 1)
    def _():
        o_ref[...] = _rms_rows(o_ref[...], g_ref[...])


def _ffn_out(act, w, x, g, tm, tk):
    m, d = x.shape
    return pl.pallas_call(
        _ffn_out_kernel,
        grid=(m // tm, D_FF // tk),
        in_specs=[pl.BlockSpec((tm, tk), lambda i, k: (i, k)),
                  pl.BlockSpec((tk, d), lambda i, k: (k, 0)),
                  pl.BlockSpec((tm, d), lambda i, k: (i, 0)),
                  pl.BlockSpec((1, d), lambda i, k: (0, 0))],
        out_specs=pl.BlockSpec((tm, d), lambda i, k: (i, 0)),
        out_shape=jax.ShapeDtypeStruct((m, d), F32),
        compiler_params=_params(("parallel", "arbitrary")),
        name="ffn_out",
    )(act, w, x, g)


def _rotary(x, cos, sin_signed):
    return x * cos + pltpu.roll(x, RET_DK // 2, 1) * sin_signed


def _ret_kernel(q_ref, k_ref, v_ref, g_ref, cos_ref, sin_ref, o_ref, so_ref, s_ref):
    n = pl.program_id(1)

    @pl.when(n == 0)
    def _():
        s_ref[...] = jnp.zeros_like(s_ref)

    c = CHUNK
    cos = cos_ref[...]
    sin = sin_ref[...]
    row = lax.broadcasted_iota(jnp.int32, (c, c), 0)
    col = lax.broadcasted_iota(jnp.int32, (c, c), 1)
    diff = (row - col).astype(F32)
    idx = lax.broadcasted_iota(jnp.int32, (c, RET_DK), 0).astype(F32)
    for h in range(RET_HEADS):
        lg = LOG_GAMMA[h]
        q = _rotary(q_ref[0, :, h * RET_DK:(h + 1) * RET_DK], cos, sin)
        k = _rotary(k_ref[0, :, h * RET_DK:(h + 1) * RET_DK], cos, sin) * (RET_DK ** -0.5)
        v = v_ref[0, :, h * RET_DV:(h + 1) * RET_DV].astype(BF16)
        decay_mask = jnp.where(diff >= 0, jnp.exp(jnp.maximum(diff, 0.0) * lg), 0.0)
        q_decay = jnp.exp((idx + 1.0) * lg)
        k_decay = jnp.exp((c - 1.0 - idx) * lg)
        chunk_decay = math.exp(c * lg)
        s = s_ref[h]
        scores = _dot_nt(q.astype(BF16), k.astype(BF16)) * decay_mask
        o = _dot(scores.astype(BF16), v) + _dot((q * q_decay).astype(BF16), s.astype(BF16))
        s_ref[h] = chunk_decay * s + _dot_tn((k * k_decay).astype(BF16), v)
        o = o * lax.rsqrt(jnp.mean(o * o, axis=-1, keepdims=True) + EPS)
        gate = g_ref[0, :, h * RET_DV:(h + 1) * RET_DV]
        o_ref[0, :, h * RET_DV:(h + 1) * RET_DV] = (o * _silu(gate)).astype(BF16)

    @pl.when(n == pl.num_programs(1) - 1)
    def _():
        so_ref[0] = s_ref[...]


def _retention_prompt(proj3, cos, sin):
    b, t, _ = proj3.shape
    nc = t // CHUNK
    return pl.pallas_call(
        _ret_kernel,
        grid=(b, nc),
        in_specs=[pl.BlockSpec((1, CHUNK, RET_QK), lambda i, n: (i, n, OFF_RQ // RET_QK)),
                  pl.BlockSpec((1, CHUNK, RET_QK), lambda i, n: (i, n, OFF_RK // RET_QK)),
                  pl.BlockSpec((1, CHUNK, RET_V), lambda i, n: (i, n, OFF_RV // RET_V)),
                  pl.BlockSpec((1, CHUNK, RET_V), lambda i, n: (i, n, OFF_RG // RET_V)),
                  pl.BlockSpec((CHUNK, RET_DK), lambda i, n: (n, 0)),
                  pl.BlockSpec((CHUNK, RET_DK), lambda i, n: (n, 0))],
        out_specs=[pl.BlockSpec((1, CHUNK, RET_V), lambda i, n: (i, n, 0)),
                   pl.BlockSpec((1, RET_HEADS, RET_DK, RET_DV), lambda i, n: (i, 0, 0, 0))],
        out_shape=[jax.ShapeDtypeStruct((b, t, RET_V), BF16),
                   jax.ShapeDtypeStruct((b, RET_HEADS, RET_DK, RET_DV), F32)],
        scratch_shapes=[pltpu.VMEM((RET_HEADS, RET_DK, RET_DV), F32)],
        compiler_params=_params(("parallel", "arbitrary")),
        name="retention_prompt",
    )(proj3, proj3, proj3, proj3, cos, sin)


def _lower_bound(lbl):
    mx = jnp.max(lbl, axis=0, keepdims=True)
    e = jnp.exp(lbl - mx)
    return e[0:1, :] / jnp.sum(e, axis=0, keepdims=True)


def _hgrn_gates(z, lb):
    half = 0.5 * (1.0 - lb)
    ht = half * jnp.tanh(0.5 * z)
    return (lb + half) + ht, half - ht


def _cumsum_rows(tri, x):
    hi = x.astype(BF16)
    r1 = x - hi.astype(F32)
    mid = r1.astype(BF16)
    lo = (r1 - mid.astype(F32)).astype(BF16)
    return _dot(tri, hi) + _dot(tri, mid) + _dot(tri, lo)


def _bcast_rows(x, group, r):
    c, d = x.shape
    x3 = x.reshape(c // group, group, d)
    return jnp.broadcast_to(x3[:, r:r + 1, :], x3.shape).reshape(c, d)


def _hgrn_kernel(q_ref, f_ref, i_ref, g_ref, lbl_ref, nrm_ref, o_ref, so_ref, st_ref):
    n = pl.program_id(1)

    @pl.when(n == 0)
    def _():
        st_ref[...] = jnp.zeros_like(st_ref)

    c = CHUNK
    row = lax.broadcasted_iota(jnp.int32, (c, c), 0)
    col = lax.broadcasted_iota(jnp.int32, (c, c), 1)
    rowd = lax.broadcasted_iota(jnp.int32, (c, HG_DK), 0)
    tri = jnp.where(row >= col, 1.0, 0.0).astype(BF16)
    r2, r4, r8 = rowd & 1, rowd & 3, rowd & 7
    block_masks = [(row >> s) == (col >> s) for s in range(c.bit_length())]
    nrm = nrm_ref[...]

    def up(x, k):
        return pltpu.roll(x, k, 0)

    def dn(x, k):
        return pltpu.roll(x, c - k, 0)

    def head(h, carry):
        off = pl.multiple_of(h * HG_DK, HG_DK)
        lb = _lower_bound(lbl_ref[:, pl.ds(off, HG_DK)])
        f, kin = _hgrn_gates(f_ref[0, :, pl.ds(off, HG_DK)], lb)
        qh = _silu(q_ref[0, :, pl.ds(off, HG_DK)])
        v = i_ref[0, :, pl.ds(off, HG_DV)].astype(BF16)
        b = _cumsum_rows(tri, jnp.log(f))

        f_up = f * up(f, 1)
        f_dn = f * dn(f, 1)
        pre2 = jnp.where(r2 == 1, f_up, f)
        pre4 = jnp.where(r4 >= 1, f_up, f)
        pre4 = jnp.where(r4 >= 2, pre4 * up(pre4, 2), pre4)
        suf4 = jnp.where(r4 <= 2, f_dn, f)
        suf4 = jnp.where(r4 <= 1, suf4 * dn(suf4, 2), suf4)
        def level(s):
            if s == 0:
                return qh, kin
            if s == 1:
                return jnp.where(r2 == 1, qh * f, 0.0), jnp.where(r2 == 0, kin, 0.0)
            if s == 2:
                return (jnp.where(r4 >= 2, qh * pre2, 0.0),
                        jnp.where(r4 == 0, kin * dn(f, 1), jnp.where(r4 == 1, kin, 0.0)))
            if s == 3:
                return (jnp.where(r8 >= 4, qh * pre4, 0.0),
                        jnp.where(r8 >= 4, 0.0, jnp.where(r4 == 3, kin, kin * dn(suf4, 1))))
            half = 1 << (s - 1)
            ref = _bcast_rows(b, 2 * half, half - 1)
            upper = (rowd & (2 * half - 1)) >= half
            return jnp.where(upper, qh * jnp.exp(b - ref), 0.0), jnp.where(upper, 0.0, kin * jnp.exp(ref - b))

        a = jnp.zeros((c, c), F32)
        for s in reversed(range(len(block_masks))):
            q_s, k_s = level(s)
            a = jnp.where(block_masks[s], _dot_nt(q_s.astype(BF16), k_s.astype(BF16)), a)

        st = st_ref[h]
        b_last = b[c - 1:c, :]
        o = _dot(a.astype(BF16), v) + _dot_nt((qh * jnp.exp(b)).astype(BF16), st.astype(BF16))
        st_ref[h] = st * jnp.exp(b_last) + _dot_tn(v, (kin * jnp.exp(b_last - b)).astype(BF16))
        o = o * lax.rsqrt(jnp.mean(o * o, axis=-1, keepdims=True) + EPS) * nrm
        o_ref[0, :, pl.ds(off, HG_DV)] = (o * _silu(g_ref[0, :, pl.ds(off, HG_DV)])).astype(BF16)
        return carry

    lax.fori_loop(0, HG_HEADS, head, 0, unroll=2)

    @pl.when(n == pl.num_programs(1) - 1)
    def _():
        for h in range(HG_HEADS):
            so_ref[0, h] = st_ref[h].T


def _hgrn_prompt(proj3, lbl, nrm):
    b, t, _ = proj3.shape
    nc = t // CHUNK
    return pl.pallas_call(
        _hgrn_kernel,
        grid=(b, nc),
        in_specs=[pl.BlockSpec((1, CHUNK, HG_K), lambda i, n: (i, n, OFF_HQ // HG_K)),
                  pl.BlockSpec((1, CHUNK, HG_K), lambda i, n: (i, n, OFF_HF // HG_K)),
                  pl.BlockSpec((1, CHUNK, HG_V), lambda i, n: (i, n, OFF_HI // HG_V)),
                  pl.BlockSpec((1, CHUNK, HG_V), lambda i, n: (i, n, OFF_HG // HG_V)),
                  pl.BlockSpec((DEPTH + 1, HG_K), lambda i, n: (0, 0)),
                  pl.BlockSpec((1, HG_DV), lambda i, n: (0, 0))],
        out_specs=[pl.BlockSpec((1, CHUNK, HG_V), lambda i, n: (i, n, 0)),
                   pl.BlockSpec((1, HG_HEADS, HG_DK, HG_DV), lambda i, n: (i, 0, 0, 0))],
        out_shape=[jax.ShapeDtypeStruct((b, t, HG_V), BF16),
                   jax.ShapeDtypeStruct((b, HG_HEADS, HG_DK, HG_DV), F32)],
        scratch_shapes=[pltpu.VMEM((HG_HEADS, HG_DV, HG_DK), F32)],
        compiler_params=_params(("parallel", "arbitrary")),
        name="hgrn_prompt",
    )(proj3, proj3, proj3, proj3, lbl, nrm)


def _sample_prep_kernel(p_ref, cos_ref, sin_ref, lbl_ref, qr_ref, kr_ref, qh_ref, f_ref, kin_ref):
    cos = cos_ref[...]
    sin = sin_ref[...]
    for h in range(RET_HEADS):
        sl = slice(h * RET_DK, (h + 1) * RET_DK)
        qr_ref[:, sl] = _rotary(p_ref[:, OFF_RQ + h * RET_DK:OFF_RQ + (h + 1) * RET_DK], cos, sin)
        kr_ref[:, sl] = _rotary(p_ref[:, OFF_RK + h * RET_DK:OFF_RK + (h + 1) * RET_DK], cos, sin) * (RET_DK ** -0.5)
    f, kin = _hgrn_gates(p_ref[:, OFF_HF:OFF_HF + HG_K], _lower_bound(lbl_ref[...]))
    f_ref[...] = f
    kin_ref[...] = kin
    qh_ref[...] = _silu(p_ref[:, OFF_HQ:OFF_HQ + HG_K])


def _sample_prep(proj, cos, sin, lbl):
    m = proj.shape[0]
    shp = jax.ShapeDtypeStruct((m, RET_QK), F32)
    return pl.pallas_call(
        _sample_prep_kernel,
        grid=(1,),
        in_specs=[pl.BlockSpec((m, OFF_GA), lambda i: (0, 0)),
                  pl.BlockSpec((1, RET_DK), lambda i: (0, 0)),
                  pl.BlockSpec((1, RET_DK), lambda i: (0, 0)),
                  pl.BlockSpec((DEPTH + 1, HG_K), lambda i: (0, 0))],
        out_specs=[pl.BlockSpec((m, RET_QK), lambda i: (0, 0))] * 5,
        out_shape=[shp] * 5,
        compiler_params=_params(("arbitrary",)),
        name="sample_prep",
    )(proj, cos, sin, lbl)


def _sample_state_kernel(sr_ref, sh_ref, qr_ref, kr_ref, qh_ref, f_ref, kin_ref, vr_ref, rg_ref, vh_ref, hg_ref,
                         nrm_ref, sro_ref, sho_ref, or_ref, oh_ref):
    b = pl.program_id(0)
    base = pl.multiple_of((b // SUB) * SUB, SUB)

    def row(ref, sl):
        tile = ref[pl.ds(base, SUB), sl]
        sub = lax.broadcasted_iota(jnp.int32, tile.shape, 0)
        return jnp.sum(jnp.where(sub == b % SUB, tile, 0.0), axis=0, keepdims=True)

    def column(ref, h):
        r = row(ref, slice(h * RET_DK, (h + 1) * RET_DK))
        return jnp.broadcast_to(r, (RET_DK, RET_DK)).T

    lanes = RET_DK
    for h in range(RET_HEADS):
        kc = column(kr_ref, h)
        qc = column(qr_ref, h)
        gamma = math.exp(LOG_GAMMA[h])
        outs = []
        for half in range(RET_DV // lanes):
            hs = slice(half * lanes, (half + 1) * lanes)
            s = gamma * sr_ref[0, h, :, hs] + kc * row(vr_ref, slice(h * RET_DV + hs.start, h * RET_DV + hs.stop))
            sro_ref[0, h, :, hs] = s
            outs.append(jnp.sum(qc * s, axis=0, keepdims=True))
        o = jnp.concatenate(outs, axis=1)
        o = o * lax.rsqrt(jnp.mean(o * o, axis=-1, keepdims=True) + EPS)
        sl = slice(h * RET_DV, (h + 1) * RET_DV)
        or_ref[0, :, sl] = o * _silu(row(rg_ref, sl))
    for h in range(HG_HEADS):
        sl = slice(h * HG_DV, (h + 1) * HG_DV)
        s = column(f_ref, h) * sh_ref[0, h] + column(kin_ref, h) * row(vh_ref, sl)
        sho_ref[0, h] = s
        o = jnp.sum(column(qh_ref, h) * s, axis=0, keepdims=True)
        o = o * lax.rsqrt(jnp.mean(o * o, axis=-1, keepdims=True) + EPS) * nrm_ref[...]
        oh_ref[0, :, sl] = o * _silu(row(hg_ref, sl))


def _sample_state(s_ret, s_hg, rows, proj, nrm):
    b = s_ret.shape[0]
    row_spec = pl.BlockSpec((b, RET_QK), lambda i: (0, 0))
    return pl.pallas_call(
        _sample_state_kernel,
        grid=(b,),
        in_specs=[pl.BlockSpec((1, RET_HEADS, RET_DK, RET_DV), lambda i: (i, 0, 0, 0)),
                  pl.BlockSpec((1, HG_HEADS, HG_DK, HG_DV), lambda i: (i, 0, 0, 0)),
                  row_spec, row_spec, row_spec, row_spec, row_spec,
                  pl.BlockSpec((b, RET_V), lambda i: (0, OFF_RV // RET_V)),
                  pl.BlockSpec((b, RET_V), lambda i: (0, OFF_RG // RET_V)),
                  pl.BlockSpec((b, HG_V), lambda i: (0, OFF_HI // HG_V)),
                  pl.BlockSpec((b, HG_V), lambda i: (0, OFF_HG // HG_V)),
                  pl.BlockSpec((1, HG_DV), lambda i: (0, 0))],
        out_specs=[pl.BlockSpec((1, RET_HEADS, RET_DK, RET_DV), lambda i: (i, 0, 0, 0)),
                   pl.BlockSpec((1, HG_HEADS, HG_DK, HG_DV), lambda i: (i, 0, 0, 0)),
                   pl.BlockSpec((1, 1, RET_V), lambda i: (i, 0, 0)),
                   pl.BlockSpec((1, 1, HG_V), lambda i: (i, 0, 0))],
        out_shape=[jax.ShapeDtypeStruct(s_ret.shape, F32),
                   jax.ShapeDtypeStruct(s_hg.shape, F32),
                   jax.ShapeDtypeStruct((b, 1, RET_V), F32),
                   jax.ShapeDtypeStruct((b, 1, HG_V), F32)],
        compiler_params=_params(("arbitrary",)),
        name="sample_state",
    )(s_ret, s_hg, *rows, proj, proj, proj, proj, nrm)


def _rope_tables(pos):
    half = RET_DK // 2
    inv_freq = ROPE_BASE ** (-jnp.arange(half, dtype=F32) / half)
    ang = pos.astype(F32)[:, None] * inv_freq[None, :]
    cos, sin = jnp.cos(ang), jnp.sin(ang)
    return jnp.concatenate([cos, cos], axis=-1), jnp.concatenate([-sin, sin], axis=-1)


def _dense_tail(x, proj, o_r, o_h, w, tm):
    mrg = _merge(o_r, o_h, w["ret_out"], w["hgrn_out"], proj, tm, 512)
    x1 = _resid(x, mrg, w["out"], tm, 1024)
    act = _ffn_in(x1, w["norm_ffn"], w["ffn_in"], tm, 512)
    return _ffn_out(act, w["ffn_out"], x1, w["norm_final"], tm, 512)


def kernel(x_prompt, x_sample, state_ret, state_hgrn, w_in, w_ret_out, w_hgrn_out, w_out, norm_mix, norm_ffn,
           hgrn_norm, hgrn_lb_logits, w_ffn_in, w_ffn_out, norm_final):
    assert w_in.shape[0] == DEPTH == 1
    bp, t, d = x_prompt.shape
    bs = x_sample.shape[0]
    w = {"in": w_in[0], "ret_out": w_ret_out[0], "hgrn_out": w_hgrn_out[0], "out": w_out[0], "ffn_in": w_ffn_in[0],
         "ffn_out": w_ffn_out[0], "norm_ffn": norm_ffn[0][None, :], "norm_final": norm_final[None, :]}
    g_mix = norm_mix[0][None, :]
    nrm = hgrn_norm[0][None, :]
    lbl = hgrn_lb_logits.astype(F32)

    xp = x_prompt.reshape(bp * t, d)
    proj_p = _proj(xp, g_mix, w["in"], 1024, 1024)
    proj_p3 = proj_p.reshape(bp, t, IN_TOTAL)
    cos_p, sin_p = _rope_tables(jnp.arange(t))
    o_r, s_ret_p = _retention_prompt(proj_p3, cos_p, sin_p)
    o_h, s_hg_p = _hgrn_prompt(proj_p3, lbl, nrm)
    y_p = _dense_tail(xp, proj_p, o_r.reshape(bp * t, RET_V), o_h.reshape(bp * t, HG_V), w, 1024)

    xs = x_sample.reshape(bs, d)
    proj_s = _proj(xs, g_mix, w["in"], bs, 1024)
    cos_s, sin_s = _rope_tables(PAST_LEN + jnp.arange(1))
    rows = _sample_prep(proj_s, cos_s, sin_s, lbl)
    s_ret_s, s_hg_s, o_rs, o_hs = _sample_state(state_ret[0], state_hgrn[0], rows, proj_s, nrm)
    y_s = _dense_tail(xs, proj_s, o_rs.reshape(bs, RET_V).astype(BF16), o_hs.reshape(bs, HG_V).astype(BF16), w, bs)

    return (y_p.reshape(bp, t, d), y_s.reshape(bs, 1, d), s_ret_p[None], s_hg_p[None], s_ret_s[None], s_hg_s[None])
```

```python
import math
from typing import Any, Callable, NamedTuple

import numpy as np
import jax
import jax.numpy as jnp
from jax import lax
from jax.experimental import pallas as pl
from jax.experimental.pallas import tpu as pltpu

D_MODEL = 2048
DEPTH = 1
PAST_LEN = 16384
RET_HEADS = 8
RET_DK = 128
RET_DV = 256
RET_QK = RET_HEADS * RET_DK
RET_V = RET_HEADS * RET_DV
ROPE_BASE = 10000.0
HG_HEADS = 8
HG_DK = 128
HG_DV = 128
HG_K = HG_HEADS * HG_DK
HG_V = HG_HEADS * HG_DV
D_FF = 5632
EPS = 1e-6

OFF_RQ = 0
OFF_RK = OFF_RQ + RET_QK
OFF_RV = OFF_RK + RET_QK
OFF_RG = OFF_RV + RET_V
OFF_HQ = OFF_RG + RET_V
OFF_HF = OFF_HQ + HG_K
OFF_HI = OFF_HF + HG_K
OFF_HG = OFF_HI + HG_V
OFF_GA = OFF_HG + HG_V
OFF_GB = OFF_GA + D_MODEL
IN_TOTAL = OFF_GB + D_MODEL

CHUNK = 128
SUB = 8
V7X_VMEM_LIMIT = 56 * 1024 * 1024

F32 = jnp.float32
BF16 = jnp.bfloat16

LOG_GAMMA = [float(np.log(np.float32(1.0) - np.float32(2.0) ** np.float32(-5.0 - h))) for h in range(RET_HEADS)]


def _params(sem, vmem=V7X_VMEM_LIMIT):
    return pltpu.CompilerParams(dimension_semantics=sem, vmem_limit_bytes=vmem)


def _rms_rows(x, g):
    return (x * lax.rsqrt(jnp.mean(x * x, axis=-1, keepdims=True) + EPS)) * g


def _sigmoid(x):
    return 0.5 + 0.5 * jnp.tanh(0.5 * x)


def _silu(x):
    hx = 0.5 * x
    return hx + hx * jnp.tanh(hx)


def _dot(a, b):
    return jnp.dot(a, b, preferred_element_type=F32)


def _dot_nt(a, b):
    return lax.dot_general(a, b, (((1,), (1,)), ((), ())), preferred_element_type=F32)


def _dot_tn(a, b):
    return lax.dot_general(a, b, (((0,), (0,)), ((), ())), preferred_element_type=F32)


TILE_M = 1024
TILE_N_PROJ = 1024
TILE_N_MERGE = 512
TILE_N_RESID = 1024
TILE_N_FFN = 512
TILE_K_FFN = 512


class _Rows(NamedTuple):
    prompt: Any
    sample: Any
    cols: int
    col: Callable


class _Shared(NamedTuple):
    array: Any
    block: tuple
    index: Callable


class _Out(NamedTuple):
    total_cols: int
    cols: int
    col: Callable
    dtype: Any


def _dense_call(body, name, operands, outs, scratch, inner):
    rows = [op for op in operands if isinstance(op, _Rows)]
    m_p, m_s = rows[0].prompt.shape[0], rows[0].sample.shape[0]
    last = m_p // TILE_M - 1

    def frozen(i, j):
        return jnp.where(i == last, j, 0)

    specs_p, specs_s, args_p, args_s = [], [], [], []
    for op in operands:
        if isinstance(op, _Rows):
            specs_p.append(pl.BlockSpec((TILE_M, op.cols), lambda i, j, op=op: (i, op.col(j))))
            specs_s.append(pl.BlockSpec((m_s, op.cols), lambda i, j, op=op: (0, op.col(frozen(i, j)))))
            args_p.append(op.prompt)
            args_s.append(op.sample)
        else:
            specs_p.append(pl.BlockSpec(op.block, lambda i, j, op=op: op.index(j)))
            args_p.append(op.array)
    out_specs = ([pl.BlockSpec((TILE_M, o.cols), lambda i, j, o=o: (i, o.col(j))) for o in outs]
                 + [pl.BlockSpec((m_s, o.cols), lambda i, j, o=o: (0, o.col(frozen(i, j)))) for o in outs])
    out_shape = ([jax.ShapeDtypeStruct((m_p, o.total_cols), o.dtype) for o in outs]
                 + [jax.ShapeDtypeStruct((m_s, o.total_cols), o.dtype) for o in outs])
    scratch_shapes = ([pltpu.VMEM((TILE_M, c), dt) for c, dt in scratch]
                      + [pltpu.VMEM((m_s, c), dt) for c, dt in scratch])
    n_in, n_out, n_scr = len(operands), len(outs), len(scratch)
    row_pos = [k for k, op in enumerate(operands) if isinstance(op, _Rows)]

    def kernel(*refs):
        ins_p = list(refs[:n_in])
        ins_s = list(ins_p)
        for k, ref in zip(row_pos, refs[n_in:n_in + len(row_pos)]):
            ins_s[k] = ref
        r = n_in + len(row_pos)
        outs_p, outs_s = refs[r:r + n_out], refs[r + n_out:r + 2 * n_out]
        r += 2 * n_out
        body(*ins_p, *outs_p, *refs[r:r + n_scr])

        @pl.when(pl.program_id(0) == last)
        def _():
            body(*ins_s, *outs_s, *refs[r + n_scr:])

    res = pl.pallas_call(
        kernel,
        grid=(m_p // TILE_M, inner),
        in_specs=specs_p + specs_s,
        out_specs=out_specs,
        out_shape=out_shape,
        scratch_shapes=scratch_shapes,
        compiler_params=_params(("arbitrary", "arbitrary")),
        name=name,
    )(*args_p, *args_s)
    return res[:n_out], res[n_out:]


def _proj_body(x_ref, g_ref, w_ref, o_ref, h_ref):
    @pl.when(pl.program_id(1) == 0)
    def _():
        h_ref[...] = _rms_rows(x_ref[...], g_ref[...]).astype(BF16)
    o_ref[...] = _dot(h_ref[...], w_ref[...].astype(BF16))


def _proj(x_p, x_s, g, w):
    d, n = w.shape
    tn = TILE_N_PROJ
    (p,), (s,) = _dense_call(
        _proj_body, "proj",
        [_Rows(x_p, x_s, d, lambda j: 0),
         _Shared(g, (1, d), lambda j: (0, 0)),
         _Shared(w, (d, tn), lambda j: (0, j))],
        [_Out(n, tn, lambda j: j, F32)], [(d, BF16)], n // tn)
    return p, s


def _merge_body(or_ref, oh_ref, wr_ref, wh_ref, ga_ref, gb_ref, o_ref):
    ya = _dot(or_ref[...], wr_ref[...].astype(BF16))
    yb = _dot(oh_ref[...], wh_ref[...].astype(BF16))
    o_ref[...] = (_sigmoid(ga_ref[...]) * ya + _sigmoid(gb_ref[...]) * yb).astype(BF16)


def _merge(o_r, o_h, w_r, w_h, proj):
    n = w_r.shape[1]
    tn = TILE_N_MERGE
    (p,), (s,) = _dense_call(
        _merge_body, "merge",
        [_Rows(*o_r, RET_V, lambda j: 0),
         _Rows(*o_h, HG_V, lambda j: 0),
         _Shared(w_r, (RET_V, tn), lambda j: (0, j)),
         _Shared(w_h, (HG_V, tn), lambda j: (0, j)),
         _Rows(*proj, tn, lambda j: OFF_GA // tn + j),
         _Rows(*proj, tn, lambda j: OFF_GB // tn + j)],
        [_Out(n, tn, lambda j: j, BF16)], [], n // tn)
    return p, s


def _resid_body(x_ref, m_ref, w_ref, o_ref):
    o_ref[...] = x_ref[...] + _dot(m_ref[...], w_ref[...].astype(BF16))


def _resid(x, mrg, w):
    d = w.shape[1]
    tn = TILE_N_RESID
    (p,), (s,) = _dense_call(
        _resid_body, "resid",
        [_Rows(*x, tn, lambda j: j),
         _Rows(*mrg, d, lambda j: 0),
         _Shared(w, (d, tn), lambda j: (0, j))],
        [_Out(d, tn, lambda j: j, F32)], [], d // tn)
    return p, s


def _ffn_in_body(x_ref, g_ref, wg_ref, wu_ref, o_ref, h_ref):
    @pl.when(pl.program_id(1) == 0)
    def _():
        h_ref[...] = _rms_rows(x_ref[...], g_ref[...]).astype(BF16)
    h = h_ref[...]
    gate = _dot(h, wg_ref[...].astype(BF16))
    up = _dot(h, wu_ref[...].astype(BF16))
    o_ref[...] = (_silu(gate) * up).astype(BF16)


def _ffn_in(x, g, w):
    d = w.shape[0]
    tn = TILE_N_FFN
    (p,), (s,) = _dense_call(
        _ffn_in_body, "ffn_in",
        [_Rows(*x, d, lambda j: 0),
         _Shared(g, (1, d), lambda j: (0, 0)),
         _Shared(w, (d, tn), lambda j: (0, j)),
         _Shared(w, (d, tn), lambda j: (0, D_FF // tn + j))],
        [_Out(D_FF, tn, lambda j: j, BF16)], [(d, BF16)], D_FF // tn)
    return p, s


def _ffn_out_body(a_ref, w_ref, x_ref, g_ref, o_ref):
    k = pl.program_id(1)

    @pl.when(k == 0)
    def _():
        o_ref[...] = x_ref[...]

    o_ref[...] += _dot(a_ref[...], w_ref[...].astype(BF16))

    @pl.when(k == pl.num_programs(1) - 1)
    def _():
        o_ref[...] = _rms_rows(o_ref[...], g_ref[...])


def _ffn_out(act, w, x, g):
    d = w.shape[1]
    tk = TILE_K_FFN
    (p,), (s,) = _dense_call(
        _ffn_out_body, "ffn_out",
        [_Rows(*act, tk, lambda k: k),
         _Shared(w, (tk, d), lambda k: (k, 0)),
         _Rows(*x, d, lambda k: 0),
         _Shared(g, (1, d), lambda k: (0, 0))],
        [_Out(d, d, lambda k: 0, F32)], [], D_FF // tk)
    return p, s


def _rotary(x, cos, sin_signed):
    return x * cos + pltpu.roll(x, RET_DK // 2, 1) * sin_signed


def _ret_kernel(q_ref, k_ref, v_ref, g_ref, cos_ref, sin_ref, o_ref, so_ref, s_ref):
    n = pl.program_id(1)

    @pl.when(n == 0)
    def _():
        s_ref[...] = jnp.zeros_like(s_ref)

    c = CHUNK
    cos = cos_ref[...]
    sin = sin_ref[...]
    row = lax.broadcasted_iota(jnp.int32, (c, c), 0)
    col = lax.broadcasted_iota(jnp.int32, (c, c), 1)
    diff = (row - col).astype(F32)
    idx = lax.broadcasted_iota(jnp.int32, (c, RET_DK), 0).astype(F32)
    for h in range(RET_HEADS):
        lg = LOG_GAMMA[h]
        q = _rotary(q_ref[0, :, h * RET_DK:(h + 1) * RET_DK], cos, sin)
        k = _rotary(k_ref[0, :, h * RET_DK:(h + 1) * RET_DK], cos, sin) * (RET_DK ** -0.5)
        v = v_ref[0, :, h * RET_DV:(h + 1) * RET_DV].astype(BF16)
        decay_mask = jnp.where(diff >= 0, jnp.exp(jnp.maximum(diff, 0.0) * lg), 0.0)
        q_decay = jnp.exp((idx + 1.0) * lg)
        k_decay = jnp.exp((c - 1.0 - idx) * lg)
        chunk_decay = math.exp(c * lg)
        s = s_ref[h]
        scores = _dot_nt(q.astype(BF16), k.astype(BF16)) * decay_mask
        o = _dot(scores.astype(BF16), v) + _dot((q * q_decay).astype(BF16), s.astype(BF16))
        s_ref[h] = chunk_decay * s + _dot_tn((k * k_decay).astype(BF16), v)
        o = o * lax.rsqrt(jnp.mean(o * o, axis=-1, keepdims=True) + EPS)
        gate = g_ref[0, :, h * RET_DV:(h + 1) * RET_DV]
        o_ref[0, :, h * RET_DV:(h + 1) * RET_DV] = (o * _silu(gate)).astype(BF16)

    @pl.when(n == pl.num_programs(1) - 1)
    def _():
        so_ref[0] = s_ref[...]


def _retention_prompt(proj3, cos, sin):
    b, t, _ = proj3.shape
    nc = t // CHUNK
    return pl.pallas_call(
        _ret_kernel,
        grid=(b, nc),
        in_specs=[pl.BlockSpec((1, CHUNK, RET_QK), lambda i, n: (i, n, OFF_RQ // RET_QK)),
                  pl.BlockSpec((1, CHUNK, RET_QK), lambda i, n: (i, n, OFF_RK // RET_QK)),
                  pl.BlockSpec((1, CHUNK, RET_V), lambda i, n: (i, n, OFF_RV // RET_V)),
                  pl.BlockSpec((1, CHUNK, RET_V), lambda i, n: (i, n, OFF_RG // RET_V)),
                  pl.BlockSpec((CHUNK, RET_DK), lambda i, n: (n, 0)),
                  pl.BlockSpec((CHUNK, RET_DK), lambda i, n: (n, 0))],
        out_specs=[pl.BlockSpec((1, CHUNK, RET_V), lambda i, n: (i, n, 0)),
                   pl.BlockSpec((1, RET_HEADS, RET_DK, RET_DV), lambda i, n: (i, 0, 0, 0))],
        out_shape=[jax.ShapeDtypeStruct((b, t, RET_V), BF16),
                   jax.ShapeDtypeStruct((b, RET_HEADS, RET_DK, RET_DV), F32)],
        scratch_shapes=[pltpu.VMEM((RET_HEADS, RET_DK, RET_DV), F32)],
        compiler_params=_params(("parallel", "arbitrary")),
        name="retention_prompt",
    )(proj3, proj3, proj3, proj3, cos, sin)


def _lower_bound(lbl):
    mx = jnp.max(lbl, axis=0, keepdims=True)
    e = jnp.exp(lbl - mx)
    return e[0:1, :] / jnp.sum(e, axis=0, keepdims=True)


def _hgrn_gates(z, lb):
    half = 0.5 * (1.0 - lb)
    ht = half * jnp.tanh(0.5 * z)
    return (lb + half) + ht, half - ht


def _cumsum_rows(tri, x):
    hi = x.astype(BF16)
    r1 = x - hi.astype(F32)
    mid = r1.astype(BF16)
    lo = (r1 - mid.astype(F32)).astype(BF16)
    return _dot(tri, hi) + _dot(tri, mid) + _dot(tri, lo)


def _bcast_rows(x, group, r):
    c, d = x.shape
    x3 = x.reshape(c // group, group, d)
    return jnp.broadcast_to(x3[:, r:r + 1, :], x3.shape).reshape(c, d)


def _hgrn_kernel(q_ref, f_ref, i_ref, g_ref, lbl_ref, nrm_ref, o_ref, so_ref, st_ref):
    n = pl.program_id(1)

    @pl.when(n == 0)
    def _():
        st_ref[...] = jnp.zeros_like(st_ref)

    c = CHUNK
    row = lax.broadcasted_iota(jnp.int32, (c, c), 0)
    col = lax.broadcasted_iota(jnp.int32, (c, c), 1)
    rowd = lax.broadcasted_iota(jnp.int32, (c, HG_DK), 0)
    tri = jnp.where(row >= col, 1.0, 0.0).astype(BF16)
    r2, r4, r8 = rowd & 1, rowd & 3, rowd & 7
    block_masks = [(row >> s) == (col >> s) for s in range(c.bit_length())]
    nrm = nrm_ref[...]

    def up(x, k):
        return pltpu.roll(x, k, 0)

    def dn(x, k):
        return pltpu.roll(x, c - k, 0)

    def head(h, carry):
        off = pl.multiple_of(h * HG_DK, HG_DK)
        lb = _lower_bound(lbl_ref[:, pl.ds(off, HG_DK)])
        f, kin = _hgrn_gates(f_ref[0, :, pl.ds(off, HG_DK)], lb)
        qh = _silu(q_ref[0, :, pl.ds(off, HG_DK)])
        v = i_ref[0, :, pl.ds(off, HG_DV)].astype(BF16)
        b = _cumsum_rows(tri, jnp.log(f))

        f_up = f * up(f, 1)
        f_dn = f * dn(f, 1)
        pre2 = jnp.where(r2 == 1, f_up, f)
        pre4 = jnp.where(r4 >= 1, f_up, f)
        pre4 = jnp.where(r4 >= 2, pre4 * up(pre4, 2), pre4)
        suf4 = jnp.where(r4 <= 2, f_dn, f)
        suf4 = jnp.where(r4 <= 1, suf4 * dn(suf4, 2), suf4)

        def level(s):
            if s == 0:
                return qh, kin
            if s == 1:
                return jnp.where(r2 == 1, qh * f, 0.0), jnp.where(r2 == 0, kin, 0.0)
            if s == 2:
                return (jnp.where(r4 >= 2, qh * pre2, 0.0),
                        jnp.where(r4 == 0, kin * dn(f, 1), jnp.where(r4 == 1, kin, 0.0)))
            if s == 3:
                return (jnp.where(r8 >= 4, qh * pre4, 0.0),
                        jnp.where(r8 >= 4, 0.0, jnp.where(r4 == 3, kin, kin * dn(suf4, 1))))
            half = 1 << (s - 1)
            ref = _bcast_rows(b, 2 * half, half - 1)
            upper = (rowd & (2 * half - 1)) >= half
            return jnp.where(upper, qh * jnp.exp(b - ref), 0.0), jnp.where(upper, 0.0, kin * jnp.exp(ref - b))

        a = jnp.zeros((c, c), F32)
        for s in reversed(range(len(block_masks))):
            q_s, k_s = level(s)
            a = jnp.where(block_masks[s], _dot_nt(q_s.astype(BF16), k_s.astype(BF16)), a)

        st = st_ref[h]
        b_last = b[c - 1:c, :]
        o = _dot(a.astype(BF16), v) + _dot_nt((qh * jnp.exp(b)).astype(BF16), st.astype(BF16))
        st_ref[h] = st * jnp.exp(b_last) + _dot_tn(v, (kin * jnp.exp(b_last - b)).astype(BF16))
        o = o * lax.rsqrt(jnp.mean(o * o, axis=-1, keepdims=True) + EPS) * nrm
        o_ref[0, :, pl.ds(off, HG_DV)] = (o * _silu(g_ref[0, :, pl.ds(off, HG_DV)])).astype(BF16)
        return carry

    lax.fori_loop(0, HG_HEADS, head, 0, unroll=2)

    @pl.when(n == pl.num_programs(1) - 1)
    def _():
        for h in range(HG_HEADS):
            so_ref[0, h] = st_ref[h].T


def _hgrn_prompt(proj3, lbl, nrm):
    b, t, _ = proj3.shape
    nc = t // CHUNK
    return pl.pallas_call(
        _hgrn_kernel,
        grid=(b, nc),
        in_specs=[pl.BlockSpec((1, CHUNK, HG_K), lambda i, n: (i, n, OFF_HQ // HG_K)),
                  pl.BlockSpec((1, CHUNK, HG_K), lambda i, n: (i, n, OFF_HF // HG_K)),
                  pl.BlockSpec((1, CHUNK, HG_V), lambda i, n: (i, n, OFF_HI // HG_V)),
                  pl.BlockSpec((1, CHUNK, HG_V), lambda i, n: (i, n, OFF_HG // HG_V)),
                  pl.BlockSpec((DEPTH + 1, HG_K), lambda i, n: (0, 0)),
                  pl.BlockSpec((1, HG_DV), lambda i, n: (0, 0))],
        out_specs=[pl.BlockSpec((1, CHUNK, HG_V), lambda i, n: (i, n, 0)),
                   pl.BlockSpec((1, HG_HEADS, HG_DK, HG_DV), lambda i, n: (i, 0, 0, 0))],
        out_shape=[jax.ShapeDtypeStruct((b, t, HG_V), BF16),
                   jax.ShapeDtypeStruct((b, HG_HEADS, HG_DK, HG_DV), F32)],
        scratch_shapes=[pltpu.VMEM((HG_HEADS, HG_DV, HG_DK), F32)],
        compiler_params=_params(("parallel", "arbitrary")),
        name="hgrn_prompt",
    )(proj3, proj3, proj3, proj3, lbl, nrm)


def _sample_prep_kernel(p_ref, cos_ref, sin_ref, lbl_ref, qr_ref, kr_ref, qh_ref, f_ref, kin_ref):
    cos = cos_ref[...]
    sin = sin_ref[...]
    for h in range(RET_HEADS):
        sl = slice(h * RET_DK, (h + 1) * RET_DK)
        qr_ref[:, sl] = _rotary(p_ref[:, OFF_RQ + h * RET_DK:OFF_RQ + (h + 1) * RET_DK], cos, sin)
        kr_ref[:, sl] = _rotary(p_ref[:, OFF_RK + h * RET_DK:OFF_RK + (h + 1) * RET_DK], cos, sin) * (RET_DK ** -0.5)
    f, kin = _hgrn_gates(p_ref[:, OFF_HF:OFF_HF + HG_K], _lower_bound(lbl_ref[...]))
    f_ref[...] = f
    kin_ref[...] = kin
    qh_ref[...] = _silu(p_ref[:, OFF_HQ:OFF_HQ + HG_K])


def _sample_prep(proj, cos, sin, lbl):
    m = proj.shape[0]
    shp = jax.ShapeDtypeStruct((m, RET_QK), F32)
    return pl.pallas_call(
        _sample_prep_kernel,
        grid=(1,),
        in_specs=[pl.BlockSpec((m, OFF_GA), lambda i: (0, 0)),
                  pl.BlockSpec((1, RET_DK), lambda i: (0, 0)),
                  pl.BlockSpec((1, RET_DK), lambda i: (0, 0)),
                  pl.BlockSpec((DEPTH + 1, HG_K), lambda i: (0, 0))],
        out_specs=[pl.BlockSpec((m, RET_QK), lambda i: (0, 0))] * 5,
        out_shape=[shp] * 5,
        compiler_params=_params(("arbitrary",)),
        name="sample_prep",
    )(proj, cos, sin, lbl)


def _sample_state_kernel(sr_ref, sh_ref, qr_ref, kr_ref, qh_ref, f_ref, kin_ref, vr_ref, rg_ref, vh_ref, hg_ref,
                         nrm_ref, sro_ref, sho_ref, or_ref, oh_ref):
    b = pl.program_id(0)
    base = pl.multiple_of((b // SUB) * SUB, SUB)

    def row(ref, sl):
        tile = ref[pl.ds(base, SUB), sl]
        sub = lax.broadcasted_iota(jnp.int32, tile.shape, 0)
        return jnp.sum(jnp.where(sub == b % SUB, tile, 0.0), axis=0, keepdims=True)

    def column(ref, h):
        r = row(ref, slice(h * RET_DK, (h + 1) * RET_DK))
        return jnp.broadcast_to(r, (RET_DK, RET_DK)).T

    lanes = RET_DK
    for h in range(RET_HEADS):
        kc = column(kr_ref, h)
        qc = column(qr_ref, h)
        gamma = math.exp(LOG_GAMMA[h])
        outs = []
        for half in range(RET_DV // lanes):
            hs = slice(half * lanes, (half + 1) * lanes)
            s = gamma * sr_ref[0, h, :, hs] + kc * row(vr_ref, slice(h * RET_DV + hs.start, h * RET_DV + hs.stop))
            sro_ref[0, h, :, hs] = s
            outs.append(jnp.sum(qc * s, axis=0, keepdims=True))
        o = jnp.concatenate(outs, axis=1)
        o = o * lax.rsqrt(jnp.mean(o * o, axis=-1, keepdims=True) + EPS)
        sl = slice(h * RET_DV, (h + 1) * RET_DV)
        or_ref[0, :, sl] = o * _silu(row(rg_ref, sl))
    for h in range(HG_HEADS):
        sl = slice(h * HG_DV, (h + 1) * HG_DV)
        s = column(f_ref, h) * sh_ref[0, h] + column(kin_ref, h) * row(vh_ref, sl)
        sho_ref[0, h] = s
        o = jnp.sum(column(qh_ref, h) * s, axis=0, keepdims=True)
        o = o * lax.rsqrt(jnp.mean(o * o, axis=-1, keepdims=True) + EPS) * nrm_ref[...]
        oh_ref[0, :, sl] = o * _silu(row(hg_ref, sl))


def _sample_state(s_ret, s_hg, rows, proj, nrm):
    b = s_ret.shape[0]
    row_spec = pl.BlockSpec((b, RET_QK), lambda i: (0, 0))
    return pl.pallas_call(
        _sample_state_kernel,
        grid=(b,),
        in_specs=[pl.BlockSpec((1, RET_HEADS, RET_DK, RET_DV), lambda i: (i, 0, 0, 0)),
                  pl.BlockSpec((1, HG_HEADS, HG_DK, HG_DV), lambda i: (i, 0, 0, 0)),
                  row_spec, row_spec, row_spec, row_spec, row_spec,
                  pl.BlockSpec((b, RET_V), lambda i: (0, OFF_RV // RET_V)),
                  pl.BlockSpec((b, RET_V), lambda i: (0, OFF_RG // RET_V)),
                  pl.BlockSpec((b, HG_V), lambda i: (0, OFF_HI // HG_V)),
                  pl.BlockSpec((b, HG_V), lambda i: (0, OFF_HG // HG_V)),
                  pl.BlockSpec((1, HG_DV), lambda i: (0, 0))],
        out_specs=[pl.BlockSpec((1, RET_HEADS, RET_DK, RET_DV), lambda i: (i, 0, 0, 0)),
                   pl.BlockSpec((1, HG_HEADS, HG_DK, HG_DV), lambda i: (i, 0, 0, 0)),
                   pl.BlockSpec((1, 1, RET_V), lambda i: (i, 0, 0)),
                   pl.BlockSpec((1, 1, HG_V), lambda i: (i, 0, 0))],
        out_shape=[jax.ShapeDtypeStruct(s_ret.shape, F32),
                   jax.ShapeDtypeStruct(s_hg.shape, F32),
                   jax.ShapeDtypeStruct((b, 1, RET_V), F32),
                   jax.ShapeDtypeStruct((b, 1, HG_V), F32)],
        compiler_params=_params(("arbitrary",)),
        name="sample_state",
    )(s_ret, s_hg, *rows, proj, proj, proj, proj, nrm)


def _rope_tables(pos):
    half = RET_DK // 2
    inv_freq = ROPE_BASE ** (-jnp.arange(half, dtype=F32) / half)
    ang = pos.astype(F32)[:, None] * inv_freq[None, :]
    cos, sin = jnp.cos(ang), jnp.sin(ang)
    return jnp.concatenate([cos, cos], axis=-1), jnp.concatenate([-sin, sin], axis=-1)


def kernel(x_prompt, x_sample, state_ret, state_hgrn, w_in, w_ret_out, w_hgrn_out, w_out, norm_mix, norm_ffn,
           hgrn_norm, hgrn_lb_logits, w_ffn_in, w_ffn_out, norm_final):
    assert w_in.shape[0] == DEPTH == 1
    bp, t, d = x_prompt.shape
    bs = x_sample.shape[0]
    nrm = hgrn_norm[0][None, :]
    lbl = hgrn_lb_logits.astype(F32)
    x = (x_prompt.reshape(bp * t, d), x_sample.reshape(bs, d))

    proj = _proj(*x, norm_mix[0][None, :], w_in[0])

    proj_p3 = proj[0].reshape(bp, t, IN_TOTAL)
    cos_p, sin_p = _rope_tables(jnp.arange(t))
    o_r, s_ret_p = _retention_prompt(proj_p3, cos_p, sin_p)
    o_h, s_hg_p = _hgrn_prompt(proj_p3, lbl, nrm)

    cos_s, sin_s = _rope_tables(PAST_LEN + jnp.arange(1))
    rows = _sample_prep(proj[1], cos_s, sin_s, lbl)
    s_ret_s, s_hg_s, o_rs, o_hs = _sample_state(state_ret[0], state_hgrn[0], rows, proj[1], nrm)

    o_r = (o_r.reshape(bp * t, RET_V), o_rs.reshape(bs, RET_V).astype(BF16))
    o_h = (o_h.reshape(bp * t, HG_V), o_hs.reshape(bs, HG_V).astype(BF16))
    mrg = _merge(o_r, o_h, w_ret_out[0], w_hgrn_out[0], proj)
    x1 = _resid(x, mrg, w_out[0])
    act = _ffn_in(x1, norm_ffn[0][None, :], w_ffn_in[0])
    y_p, y_s = _ffn_out(act, w_ffn_out[0], x1, norm_final[None, :])

    return (y_p.reshape(bp, t, d), y_s.reshape(bs, 1, d), s_ret_p[None], s_hg_p[None], s_ret_s[None], s_hg_s[None])
```

```python
import math
from typing import Any, Callable, NamedTuple

import numpy as np
import jax
import jax.numpy as jnp
from jax import lax
from jax.experimental import pallas as pl
from jax.experimental.pallas import tpu as pltpu

D_MODEL = 2048
DEPTH = 1
PAST_LEN = 16384
RET_HEADS = 8
RET_DK = 128
RET_DV = 256
RET_QK = RET_HEADS * RET_DK
RET_V = RET_HEADS * RET_DV
ROPE_BASE = 10000.0
HG_HEADS = 8
HG_DK = 128
HG_DV = 128
HG_K = HG_HEADS * HG_DK
HG_V = HG_HEADS * HG_DV
D_FF = 5632
EPS = 1e-6

OFF_RQ = 0
OFF_RK = OFF_RQ + RET_QK
OFF_RV = OFF_RK + RET_QK
OFF_RG = OFF_RV + RET_V
OFF_HQ = OFF_RG + RET_V
OFF_HF = OFF_HQ + HG_K
OFF_HI = OFF_HF + HG_K
OFF_HG = OFF_HI + HG_V
OFF_GA = OFF_HG + HG_V
OFF_GB = OFF_GA + D_MODEL
IN_TOTAL = OFF_GB + D_MODEL

CHUNK = 128
SUB = 8
V7X_VMEM_LIMIT = 56 * 1024 * 1024

F32 = jnp.float32
BF16 = jnp.bfloat16

LOG_GAMMA = [float(np.log(np.float32(1.0) - np.float32(2.0) ** np.float32(-5.0 - h))) for h in range(RET_HEADS)]


def _params(sem, vmem=V7X_VMEM_LIMIT):
    return pltpu.CompilerParams(dimension_semantics=sem, vmem_limit_bytes=vmem)


def _rms_rows(x, g):
    return (x * lax.rsqrt(jnp.mean(x * x, axis=-1, keepdims=True) + EPS)) * g


def _sigmoid(x):
    return 0.5 + 0.5 * jnp.tanh(0.5 * x)


def _silu(x):
    hx = 0.5 * x
    return hx + hx * jnp.tanh(hx)


def _dot(a, b):
    return jnp.dot(a, b, preferred_element_type=F32)


def _dot_nt(a, b):
    return lax.dot_general(a, b, (((1,), (1,)), ((), ())), preferred_element_type=F32)


def _dot_tn(a, b):
    return lax.dot_general(a, b, (((0,), (0,)), ((), ())), preferred_element_type=F32)


TILE_M_PROJ, TILE_N_PROJ = 1024, 1024
TILE_M_MERGE = 512
TILE_M_RESID = 512
TILE_M_FFN_IN, TILE_N_FFN_IN = 2048, 512
TILE_M_FFN_OUT = 256


class _Rows(NamedTuple):
    prompt: Any
    sample: Any
    cols: int
    col: Callable


class _Shared(NamedTuple):
    array: Any
    block: tuple
    index: Callable
    resident: bool = False


class _Out(NamedTuple):
    total_cols: int
    cols: int
    col: Callable
    dtype: Any


def _dense_call(body, name, tile_m, operands, outs, scratch, inner):
    rows = [op for op in operands if isinstance(op, _Rows)]
    m_p, m_s = rows[0].prompt.shape[0], rows[0].sample.shape[0]
    last = m_p // tile_m - 1

    def frozen(i, j):
        return jnp.where(i == last, j, 0)

    specs_p, specs_s, args_p, args_s = [], [], [], []
    for op in operands:
        if isinstance(op, _Rows):
            specs_p.append(pl.BlockSpec((tile_m, op.cols), lambda i, j, op=op: (i, op.col(j))))
            specs_s.append(pl.BlockSpec((m_s, op.cols), lambda i, j, op=op: (0, op.col(frozen(i, j)))))
            args_p.append(op.prompt)
            args_s.append(op.sample)
        else:
            mode = dict(pipeline_mode=pl.Buffered(1)) if op.resident else {}
            specs_p.append(pl.BlockSpec(op.block, lambda i, j, op=op: op.index(j), **mode))
            args_p.append(op.array)
    out_specs = ([pl.BlockSpec((tile_m, o.cols), lambda i, j, o=o: (i, o.col(j))) for o in outs]
                 + [pl.BlockSpec((m_s, o.cols), lambda i, j, o=o: (0, o.col(frozen(i, j)))) for o in outs])
    out_shape = ([jax.ShapeDtypeStruct((m_p, o.total_cols), o.dtype) for o in outs]
                 + [jax.ShapeDtypeStruct((m_s, o.total_cols), o.dtype) for o in outs])
    scratch_shapes = ([pltpu.VMEM((tile_m, c), dt) for c, dt in scratch]
                      + [pltpu.VMEM((m_s, c), dt) for c, dt in scratch])
    n_in, n_out, n_scr = len(operands), len(outs), len(scratch)
    row_pos = [k for k, op in enumerate(operands) if isinstance(op, _Rows)]

    def kernel(*refs):
        ins_p = list(refs[:n_in])
        ins_s = list(ins_p)
        for k, ref in zip(row_pos, refs[n_in:n_in + len(row_pos)]):
            ins_s[k] = ref
        r = n_in + len(row_pos)
        outs_p, outs_s = refs[r:r + n_out], refs[r + n_out:r + 2 * n_out]
        r += 2 * n_out
        body(*ins_p, *outs_p, *refs[r:r + n_scr])

        @pl.when(pl.program_id(0) == last)
        def _():
            body(*ins_s, *outs_s, *refs[r + n_scr:])

    res = pl.pallas_call(
        kernel,
        grid=(m_p // tile_m, inner),
        in_specs=specs_p + specs_s,
        out_specs=out_specs,
        out_shape=out_shape,
        scratch_shapes=scratch_shapes,
        compiler_params=_params(("arbitrary", "arbitrary")),
        name=name,
    )(*args_p, *args_s)
    return list(zip(res[:n_out], res[n_out:]))


def _proj_body(x_ref, g_ref, w_ref, o_ref, h_ref):
    @pl.when(pl.program_id(1) == 0)
    def _():
        h_ref[...] = _rms_rows(x_ref[...], g_ref[...]).astype(BF16)
    o_ref[...] = _dot(h_ref[...], w_ref[...].astype(BF16))


def _proj(x, g, w):
    d, n = w.shape
    tn = TILE_N_PROJ
    return _dense_call(
        _proj_body, "proj", TILE_M_PROJ,
        [_Rows(*x, d, lambda j: 0),
         _Shared(g, (1, d), lambda j: (0, 0)),
         _Shared(w, (d, tn), lambda j: (0, j))],
        [_Out(n, tn, lambda j: j, F32)], [(d, BF16)], n // tn)[0]


def _merge_body(or_ref, oh_ref, wr_ref, wh_ref, ga_ref, gb_ref, o_ref):
    ya = _dot(or_ref[...], wr_ref[...])
    yb = _dot(oh_ref[...], wh_ref[...])
    o_ref[...] = (_sigmoid(ga_ref[...]) * ya + _sigmoid(gb_ref[...]) * yb).astype(BF16)


def _merge(o_r, o_h, w_r, w_h, proj):
    n = w_r.shape[1]
    return _dense_call(
        _merge_body, "merge", TILE_M_MERGE,
        [_Rows(*o_r, RET_V, lambda j: 0),
         _Rows(*o_h, HG_V, lambda j: 0),
         _Shared(w_r, (RET_V, n), lambda j: (0, 0), resident=True),
         _Shared(w_h, (HG_V, n), lambda j: (0, 0), resident=True),
         _Rows(*proj, n, lambda j: OFF_GA // n),
         _Rows(*proj, n, lambda j: OFF_GB // n)],
        [_Out(n, n, lambda j: 0, BF16)], [], 1)[0]


def _resid_body(x_ref, m_ref, w_ref, g_ref, o_ref, h_ref):
    x1 = x_ref[...] + _dot(m_ref[...], w_ref[...])
    o_ref[...] = x1
    h_ref[...] = _rms_rows(x1, g_ref[...]).astype(BF16)


def _resid(x, mrg, w, g):
    d = w.shape[1]
    return _dense_call(
        _resid_body, "resid", TILE_M_RESID,
        [_Rows(*x, d, lambda j: 0),
         _Rows(*mrg, d, lambda j: 0),
         _Shared(w, (d, d), lambda j: (0, 0), resident=True),
         _Shared(g, (1, d), lambda j: (0, 0))],
        [_Out(d, d, lambda j: 0, F32), _Out(d, d, lambda j: 0, BF16)], [], 1)


def _ffn_in_body(h_ref, wg_ref, wu_ref, o_ref):
    h = h_ref[...]
    gate = _dot(h, wg_ref[...].astype(BF16))
    up = _dot(h, wu_ref[...].astype(BF16))
    o_ref[...] = (_silu(gate) * up).astype(BF16)


def _ffn_in(h, w):
    d = w.shape[0]
    tn = TILE_N_FFN_IN
    return _dense_call(
        _ffn_in_body, "ffn_in", TILE_M_FFN_IN,
        [_Rows(*h, d, lambda j: 0),
         _Shared(w, (d, tn), lambda j: (0, j)),
         _Shared(w, (d, tn), lambda j: (0, D_FF // tn + j))],
        [_Out(D_FF, tn, lambda j: j, BF16)], [], D_FF // tn)[0]


def _ffn_out_body(a_ref, w_ref, x_ref, g_ref, o_ref):
    o_ref[...] = _rms_rows(x_ref[...] + _dot(a_ref[...], w_ref[...]), g_ref[...])


def _ffn_out(act, w, x, g):
    k, d = w.shape
    return _dense_call(
        _ffn_out_body, "ffn_out", TILE_M_FFN_OUT,
        [_Rows(*act, k, lambda j: 0),
         _Shared(w, (k, d), lambda j: (0, 0), resident=True),
         _Rows(*x, d, lambda j: 0),
         _Shared(g, (1, d), lambda j: (0, 0))],
        [_Out(d, d, lambda j: 0, F32)], [], 1)[0]


def _rotary(x, cos, sin_signed):
    return x * cos + pltpu.roll(x, RET_DK // 2, 1) * sin_signed


def _ret_kernel(q_ref, k_ref, v_ref, g_ref, cos_ref, sin_ref, o_ref, so_ref, s_ref):
    n = pl.program_id(1)

    @pl.when(n == 0)
    def _():
        s_ref[...] = jnp.zeros_like(s_ref)

    c = CHUNK
    cos = cos_ref[...]
    sin = sin_ref[...]
    row = lax.broadcasted_iota(jnp.int32, (c, c), 0)
    col = lax.broadcasted_iota(jnp.int32, (c, c), 1)
    diff = (row - col).astype(F32)
    idx = lax.broadcasted_iota(jnp.int32, (c, RET_DK), 0).astype(F32)
    for h in range(RET_HEADS):
        lg = LOG_GAMMA[h]
        q = _rotary(q_ref[0, :, h * RET_DK:(h + 1) * RET_DK], cos, sin)
        k = _rotary(k_ref[0, :, h * RET_DK:(h + 1) * RET_DK], cos, sin) * (RET_DK ** -0.5)
        v = v_ref[0, :, h * RET_DV:(h + 1) * RET_DV].astype(BF16)
        decay_mask = jnp.where(diff >= 0, jnp.exp(jnp.maximum(diff, 0.0) * lg), 0.0)
        q_decay = jnp.exp((idx + 1.0) * lg)
        k_decay = jnp.exp((c - 1.0 - idx) * lg)
        chunk_decay = math.exp(c * lg)
        s = s_ref[h]
        scores = _dot_nt(q.astype(BF16), k.astype(BF16)) * decay_mask
        o = _dot(scores.astype(BF16), v) + _dot((q * q_decay).astype(BF16), s.astype(BF16))
        s_ref[h] = chunk_decay * s + _dot_tn((k * k_decay).astype(BF16), v)
        o = o * lax.rsqrt(jnp.mean(o * o, axis=-1, keepdims=True) + EPS)
        gate = g_ref[0, :, h * RET_DV:(h + 1) * RET_DV]
        o_ref[0, :, h * RET_DV:(h + 1) * RET_DV] = (o * _silu(gate)).astype(BF16)

    @pl.when(n == pl.num_programs(1) - 1)
    def _():
        so_ref[0] = s_ref[...]


def _retention_prompt(proj3, cos, sin):
    b, t, _ = proj3.shape
    nc = t // CHUNK
    return pl.pallas_call(
        _ret_kernel,
        grid=(b, nc),
        in_specs=[pl.BlockSpec((1, CHUNK, RET_QK), lambda i, n: (i, n, OFF_RQ // RET_QK)),
                  pl.BlockSpec((1, CHUNK, RET_QK), lambda i, n: (i, n, OFF_RK // RET_QK)),
                  pl.BlockSpec((1, CHUNK, RET_V), lambda i, n: (i, n, OFF_RV // RET_V)),
                  pl.BlockSpec((1, CHUNK, RET_V), lambda i, n: (i, n, OFF_RG // RET_V)),
                  pl.BlockSpec((CHUNK, RET_DK), lambda i, n: (n, 0)),
                  pl.BlockSpec((CHUNK, RET_DK), lambda i, n: (n, 0))],
        out_specs=[pl.BlockSpec((1, CHUNK, RET_V), lambda i, n: (i, n, 0)),
                   pl.BlockSpec((1, RET_HEADS, RET_DK, RET_DV), lambda i, n: (i, 0, 0, 0))],
        out_shape=[jax.ShapeDtypeStruct((b, t, RET_V), BF16),
                   jax.ShapeDtypeStruct((b, RET_HEADS, RET_DK, RET_DV), F32)],
        scratch_shapes=[pltpu.VMEM((RET_HEADS, RET_DK, RET_DV), F32)],
        compiler_params=_params(("parallel", "arbitrary")),
        name="retention_prompt",
    )(proj3, proj3, proj3, proj3, cos, sin)


def _lower_bound(lbl):
    mx = jnp.max(lbl, axis=0, keepdims=True)
    e = jnp.exp(lbl - mx)
    return e[0:1, :] / jnp.sum(e, axis=0, keepdims=True)


def _hgrn_gates(z, lb):
    half = 0.5 * (1.0 - lb)
    ht = half * jnp.tanh(0.5 * z)
    return (lb + half) + ht, half - ht


def _cumsum_rows(tri, x):
    hi = x.astype(BF16)
    r1 = x - hi.astype(F32)
    mid = r1.astype(BF16)
    lo = (r1 - mid.astype(F32)).astype(BF16)
    return _dot(tri, hi) + _dot(tri, mid) + _dot(tri, lo)


def _bcast_rows(x, group, r):
    c, d = x.shape
    x3 = x.reshape(c // group, group, d)
    return jnp.broadcast_to(x3[:, r:r + 1, :], x3.shape).reshape(c, d)


def _hgrn_kernel(q_ref, f_ref, i_ref, g_ref, lbl_ref, nrm_ref, o_ref, so_ref, st_ref):
    n = pl.program_id(1)

    @pl.when(n == 0)
    def _():
        st_ref[...] = jnp.zeros_like(st_ref)

    c = CHUNK
    row = lax.broadcasted_iota(jnp.int32, (c, c), 0)
    col = lax.broadcasted_iota(jnp.int32, (c, c), 1)
    rowd = lax.broadcasted_iota(jnp.int32, (c, HG_DK), 0)
    tri = jnp.where(row >= col, 1.0, 0.0).astype(BF16)
    r2, r4, r8 = rowd & 1, rowd & 3, rowd & 7
    block_masks = [(row >> s) == (col >> s) for s in range(c.bit_length())]
    nrm = nrm_ref[...]

    def up(x, k):
        return pltpu.roll(x, k, 0)

    def dn(x, k):
        return pltpu.roll(x, c - k, 0)

    def head(h, carry):
        off = pl.multiple_of(h * HG_DK, HG_DK)
        lb = _lower_bound(lbl_ref[:, pl.ds(off, HG_DK)])
        f, kin = _hgrn_gates(f_ref[0, :, pl.ds(off, HG_DK)], lb)
        qh = _silu(q_ref[0, :, pl.ds(off, HG_DK)])
        v = i_ref[0, :, pl.ds(off, HG_DV)].astype(BF16)
        b = _cumsum_rows(tri, jnp.log(f))

        f_up = f * up(f, 1)
        f_dn = f * dn(f, 1)
        pre2 = jnp.where(r2 == 1, f_up, f)
        pre4 = jnp.where(r4 >= 1, f_up, f)
        pre4 = jnp.where(r4 >= 2, pre4 * up(pre4, 2), pre4)
        suf4 = jnp.where(r4 <= 2, f_dn, f)
        suf4 = jnp.where(r4 <= 1, suf4 * dn(suf4, 2), suf4)

        def level(s):
            if s == 0:
                return qh, kin
            if s == 1:
                return jnp.where(r2 == 1, qh * f, 0.0), jnp.where(r2 == 0, kin, 0.0)
            if s == 2:
                return (jnp.where(r4 >= 2, qh * pre2, 0.0),
                        jnp.where(r4 == 0, kin * dn(f, 1), jnp.where(r4 == 1, kin, 0.0)))
            if s == 3:
                return (jnp.where(r8 >= 4, qh * pre4, 0.0),
                        jnp.where(r8 >= 4, 0.0, jnp.where(r4 == 3, kin, kin * dn(suf4, 1))))
            half = 1 << (s - 1)
            ref = _bcast_rows(b, 2 * half, half - 1)
            upper = (rowd & (2 * half - 1)) >= half
            return jnp.where(upper, qh * jnp.exp(b - ref), 0.0), jnp.where(upper, 0.0, kin * jnp.exp(ref - b))

        a = jnp.zeros((c, c), F32)
        for s in reversed(range(len(block_masks))):
            q_s, k_s = level(s)
            a = jnp.where(block_masks[s], _dot_nt(q_s.astype(BF16), k_s.astype(BF16)), a)

        st = st_ref[h]
        b_last = b[c - 1:c, :]
        o = _dot(a.astype(BF16), v) + _dot_nt((qh * jnp.exp(b)).astype(BF16), st.astype(BF16))
        st_ref[h] = st * jnp.exp(b_last) + _dot_tn(v, (kin * jnp.exp(b_last - b)).astype(BF16))
        o = o * lax.rsqrt(jnp.mean(o * o, axis=-1, keepdims=True) + EPS) * nrm
        o_ref[0, :, pl.ds(off, HG_DV)] = (o * _silu(g_ref[0, :, pl.ds(off, HG_DV)])).astype(BF16)
        return carry

    lax.fori_loop(0, HG_HEADS, head, 0, unroll=2)

    @pl.when(n == pl.num_programs(1) - 1)
    def _():
        for h in range(HG_HEADS):
            so_ref[0, h] = st_ref[h].T


def _hgrn_prompt(proj3, lbl, nrm):
    b, t, _ = proj3.shape
    nc = t // CHUNK
    return pl.pallas_call(
        _hgrn_kernel,
        grid=(b, nc),
        in_specs=[pl.BlockSpec((1, CHUNK, HG_K), lambda i, n: (i, n, OFF_HQ // HG_K)),
                  pl.BlockSpec((1, CHUNK, HG_K), lambda i, n: (i, n, OFF_HF // HG_K)),
                  pl.BlockSpec((1, CHUNK, HG_V), lambda i, n: (i, n, OFF_HI // HG_V)),
                  pl.BlockSpec((1, CHUNK, HG_V), lambda i, n: (i, n, OFF_HG // HG_V)),
                  pl.BlockSpec((DEPTH + 1, HG_K), lambda i, n: (0, 0)),
                  pl.BlockSpec((1, HG_DV), lambda i, n: (0, 0))],
        out_specs=[pl.BlockSpec((1, CHUNK, HG_V), lambda i, n: (i, n, 0)),
                   pl.BlockSpec((1, HG_HEADS, HG_DK, HG_DV), lambda i, n: (i, 0, 0, 0))],
        out_shape=[jax.ShapeDtypeStruct((b, t, HG_V), BF16),
                   jax.ShapeDtypeStruct((b, HG_HEADS, HG_DK, HG_DV), F32)],
        scratch_shapes=[pltpu.VMEM((HG_HEADS, HG_DV, HG_DK), F32)],
        compiler_params=_params(("parallel", "arbitrary")),
        name="hgrn_prompt",
    )(proj3, proj3, proj3, proj3, lbl, nrm)


def _sample_prep_kernel(p_ref, cos_ref, sin_ref, lbl_ref, qr_ref, kr_ref, qh_ref, f_ref, kin_ref):
    cos = cos_ref[...]
    sin = sin_ref[...]
    for h in range(RET_HEADS):
        sl = slice(h * RET_DK, (h + 1) * RET_DK)
        qr_ref[:, sl] = _rotary(p_ref[:, OFF_RQ + h * RET_DK:OFF_RQ + (h + 1) * RET_DK], cos, sin)
        kr_ref[:, sl] = _rotary(p_ref[:, OFF_RK + h * RET_DK:OFF_RK + (h + 1) * RET_DK], cos, sin) * (RET_DK ** -0.5)
    f, kin = _hgrn_gates(p_ref[:, OFF_HF:OFF_HF + HG_K], _lower_bound(lbl_ref[...]))
    f_ref[...] = f
    kin_ref[...] = kin
    qh_ref[...] = _silu(p_ref[:, OFF_HQ:OFF_HQ + HG_K])


def _sample_prep(proj, cos, sin, lbl):
    m = proj.shape[0]
    shp = jax.ShapeDtypeStruct((m, RET_QK), F32)
    return pl.pallas_call(
        _sample_prep_kernel,
        grid=(1,),
        in_specs=[pl.BlockSpec((m, OFF_GA), lambda i: (0, 0)),
                  pl.BlockSpec((1, RET_DK), lambda i: (0, 0)),
                  pl.BlockSpec((1, RET_DK), lambda i: (0, 0)),
                  pl.BlockSpec((DEPTH + 1, HG_K), lambda i: (0, 0))],
        out_specs=[pl.BlockSpec((m, RET_QK), lambda i: (0, 0))] * 5,
        out_shape=[shp] * 5,
        compiler_params=_params(("arbitrary",)),
        name="sample_prep",
    )(proj, cos, sin, lbl)


def _sample_state_kernel(sr_ref, sh_ref, qr_ref, kr_ref, qh_ref, f_ref, kin_ref, vr_ref, rg_ref, vh_ref, hg_ref,
                         nrm_ref, sro_ref, sho_ref, or_ref, oh_ref):
    b = pl.program_id(0)
    base = pl.multiple_of((b // SUB) * SUB, SUB)

    def row(ref, sl):
        tile = ref[pl.ds(base, SUB), sl]
        sub = lax.broadcasted_iota(jnp.int32, tile.shape, 0)
        return jnp.sum(jnp.where(sub == b % SUB, tile, 0.0), axis=0, keepdims=True)

    def column(ref, h):
        r = row(ref, slice(h * RET_DK, (h + 1) * RET_DK))
        return jnp.broadcast_to(r, (RET_DK, RET_DK)).T

    lanes = RET_DK
    for h in range(RET_HEADS):
        kc = column(kr_ref, h)
        qc = column(qr_ref, h)
        gamma = math.exp(LOG_GAMMA[h])
        outs = []
        for half in range(RET_DV // lanes):
            hs = slice(half * lanes, (half + 1) * lanes)
            s = gamma * sr_ref[0, h, :, hs] + kc * row(vr_ref, slice(h * RET_DV + hs.start, h * RET_DV + hs.stop))
            sro_ref[0, h, :, hs] = s
            outs.append(jnp.sum(qc * s, axis=0, keepdims=True))
        o = jnp.concatenate(outs, axis=1)
        o = o * lax.rsqrt(jnp.mean(o * o, axis=-1, keepdims=True) + EPS)
        sl = slice(h * RET_DV, (h + 1) * RET_DV)
        or_ref[0, :, sl] = o * _silu(row(rg_ref, sl))
    for h in range(HG_HEADS):
        sl = slice(h * HG_DV, (h + 1) * HG_DV)
        s = column(f_ref, h) * sh_ref[0, h] + column(kin_ref, h) * row(vh_ref, sl)
        sho_ref[0, h] = s
        o = jnp.sum(column(qh_ref, h) * s, axis=0, keepdims=True)
        o = o * lax.rsqrt(jnp.mean(o * o, axis=-1, keepdims=True) + EPS) * nrm_ref[...]
        oh_ref[0, :, sl] = o * _silu(row(hg_ref, sl))


def _sample_state(s_ret, s_hg, rows, proj, nrm):
    b = s_ret.shape[0]
    row_spec = pl.BlockSpec((b, RET_QK), lambda i: (0, 0))
    return pl.pallas_call(
        _sample_state_kernel,
        grid=(b,),
        in_specs=[pl.BlockSpec((1, RET_HEADS, RET_DK, RET_DV), lambda i: (i, 0, 0, 0)),
                  pl.BlockSpec((1, HG_HEADS, HG_DK, HG_DV), lambda i: (i, 0, 0, 0)),
                  row_spec, row_spec, row_spec, row_spec, row_spec,
                  pl.BlockSpec((b, RET_V), lambda i: (0, OFF_RV // RET_V)),
                  pl.BlockSpec((b, RET_V), lambda i: (0, OFF_RG // RET_V)),
                  pl.BlockSpec((b, HG_V), lambda i: (0, OFF_HI // HG_V)),
                  pl.BlockSpec((b, HG_V), lambda i: (0, OFF_HG // HG_V)),
                  pl.BlockSpec((1, HG_DV), lambda i: (0, 0))],
        out_specs=[pl.BlockSpec((1, RET_HEADS, RET_DK, RET_DV), lambda i: (i, 0, 0, 0)),
                   pl.BlockSpec((1, HG_HEADS, HG_DK, HG_DV), lambda i: (i, 0, 0, 0)),
                   pl.BlockSpec((1, 1, RET_V), lambda i: (i, 0, 0)),
                   pl.BlockSpec((1, 1, HG_V), lambda i: (i, 0, 0))],
        out_shape=[jax.ShapeDtypeStruct(s_ret.shape, F32),
                   jax.ShapeDtypeStruct(s_hg.shape, F32),
                   jax.ShapeDtypeStruct((b, 1, RET_V), F32),
                   jax.ShapeDtypeStruct((b, 1, HG_V), F32)],
        compiler_params=_params(("arbitrary",)),
        name="sample_state",
    )(s_ret, s_hg, *rows, proj, proj, proj, proj, nrm)


def _rope_tables(pos):
    half = RET_DK // 2
    inv_freq = ROPE_BASE ** (-jnp.arange(half, dtype=F32) / half)
    ang = pos.astype(F32)[:, None] * inv_freq[None, :]
    cos, sin = jnp.cos(ang), jnp.sin(ang)
    return jnp.concatenate([cos, cos], axis=-1), jnp.concatenate([-sin, sin], axis=-1)


def kernel(x_prompt, x_sample, state_ret, state_hgrn, w_in, w_ret_out, w_hgrn_out, w_out, norm_mix, norm_ffn,
           hgrn_norm, hgrn_lb_logits, w_ffn_in, w_ffn_out, norm_final):
    assert w_in.shape[0] == DEPTH == 1
    bp, t, d = x_prompt.shape
    bs = x_sample.shape[0]
    nrm = hgrn_norm[0][None, :]
    lbl = hgrn_lb_logits.astype(F32)
    x = (x_prompt.reshape(bp * t, d), x_sample.reshape(bs, d))

    proj = _proj(x, norm_mix[0][None, :], w_in[0])

    proj_p3 = proj[0].reshape(bp, t, IN_TOTAL)
    cos_p, sin_p = _rope_tables(jnp.arange(t))
    o_r, s_ret_p = _retention_prompt(proj_p3, cos_p, sin_p)
    o_h, s_hg_p = _hgrn_prompt(proj_p3, lbl, nrm)

    cos_s, sin_s = _rope_tables(PAST_LEN + jnp.arange(1))
    rows = _sample_prep(proj[1], cos_s, sin_s, lbl)
    s_ret_s, s_hg_s, o_rs, o_hs = _sample_state(state_ret[0], state_hgrn[0], rows, proj[1], nrm)

    o_r = (o_r.reshape(bp * t, RET_V), o_rs.reshape(bs, RET_V).astype(BF16))
    o_h = (o_h.reshape(bp * t, HG_V), o_hs.reshape(bs, HG_V).astype(BF16))
    mrg = _merge(o_r, o_h, w_ret_out[0].astype(BF16), w_hgrn_out[0].astype(BF16), proj)
    x1, h2 = _resid(x, mrg, w_out[0].astype(BF16), norm_ffn[0][None, :])
    act = _ffn_in(h2, w_ffn_in[0])
    y_p, y_s = _ffn_out(act, w_ffn_out[0].astype(BF16), x1, norm_final[None, :])

    return (y_p.reshape(bp, t, d), y_s.reshape(bs, 1, d), s_ret_p[None], s_hg_p[None], s_ret_s[None], s_hg_s[None])
```

```python
import math
from typing import Any, Callable, NamedTuple

import numpy as np
import jax
import jax.numpy as jnp
from jax import lax
from jax.experimental import pallas as pl
from jax.experimental.pallas import tpu as pltpu

D_MODEL = 2048
DEPTH = 1
PAST_LEN = 16384
RET_HEADS = 8
RET_DK = 128
RET_DV = 256
RET_QK = RET_HEADS * RET_DK
RET_V = RET_HEADS * RET_DV
ROPE_BASE = 10000.0
HG_HEADS = 8
HG_DK = 128
HG_DV = 128
HG_K = HG_HEADS * HG_DK
HG_V = HG_HEADS * HG_DV
D_FF = 5632
EPS = 1e-6

OFF_RQ = 0
OFF_RK = OFF_RQ + RET_QK
OFF_RV = OFF_RK + RET_QK
OFF_RG = OFF_RV + RET_V
OFF_HQ = OFF_RG + RET_V
OFF_HF = OFF_HQ + HG_K
OFF_HI = OFF_HF + HG_K
OFF_HG = OFF_HI + HG_V
OFF_GA = OFF_HG + HG_V
OFF_GB = OFF_GA + D_MODEL
IN_TOTAL = OFF_GB + D_MODEL

CHUNK = 128
SUB = 8
SAMPLE_SEQS_PER_STEP = 2
V7X_VMEM_LIMIT = 56 * 1024 * 1024

F32 = jnp.float32
BF16 = jnp.bfloat16

LOG_GAMMA = [float(np.log(np.float32(1.0) - np.float32(2.0) ** np.float32(-5.0 - h))) for h in range(RET_HEADS)]


def _params(sem, vmem=V7X_VMEM_LIMIT):
    return pltpu.CompilerParams(dimension_semantics=sem, vmem_limit_bytes=vmem)


def _rms_rows(x, g):
    return (x * lax.rsqrt(jnp.mean(x * x, axis=-1, keepdims=True) + EPS)) * g


def _sigmoid(x):
    return 0.5 + 0.5 * jnp.tanh(0.5 * x)


def _silu(x):
    hx = 0.5 * x
    return hx + hx * jnp.tanh(hx)


def _dot(a, b):
    return jnp.dot(a, b, preferred_element_type=F32)


def _dot_nt(a, b):
    return lax.dot_general(a, b, (((1,), (1,)), ((), ())), preferred_element_type=F32)


def _dot_tn(a, b):
    return lax.dot_general(a, b, (((0,), (0,)), ((), ())), preferred_element_type=F32)


TILE_M_PROJ, TILE_N_PROJ = 2048, 512
TILE_M_MERGE = 512
TILE_M_RESID = 512
TILE_M_FFN_IN, TILE_N_FFN_IN = 2048, 512
TILE_M_FFN_OUT = 256


class _Rows(NamedTuple):
    prompt: Any
    sample: Any
    cols: int
    col: Callable
    single: bool = False


class _Shared(NamedTuple):
    array: Any
    block: tuple
    index: Callable
    resident: bool = False


class _Out(NamedTuple):
    total_cols: int
    cols: int
    col: Callable
    dtype: Any


def _dense_call(body, name, tile_m, operands, outs, scratch, inner):
    rows = [op for op in operands if isinstance(op, _Rows)]
    m_p, m_s = rows[0].prompt.shape[0], rows[0].sample.shape[0]
    last = m_p // tile_m - 1

    def frozen(i, j):
        return jnp.where(i == last, j, 0)

    specs_p, specs_s, args_p, args_s = [], [], [], []
    for op in operands:
        if isinstance(op, _Rows):
            mode = dict(pipeline_mode=pl.Buffered(1)) if op.single else {}
            specs_p.append(pl.BlockSpec((tile_m, op.cols), lambda i, j, op=op: (i, op.col(j)), **mode))
            specs_s.append(pl.BlockSpec((m_s, op.cols), lambda i, j, op=op: (0, op.col(frozen(i, j)))))
            args_p.append(op.prompt)
            args_s.append(op.sample)
        else:
            mode = dict(pipeline_mode=pl.Buffered(1)) if op.resident else {}
            specs_p.append(pl.BlockSpec(op.block, lambda i, j, op=op: op.index(j), **mode))
            args_p.append(op.array)
    out_specs = ([pl.BlockSpec((tile_m, o.cols), lambda i, j, o=o: (i, o.col(j))) for o in outs]
                 + [pl.BlockSpec((m_s, o.cols), lambda i, j, o=o: (0, o.col(frozen(i, j)))) for o in outs])
    out_shape = ([jax.ShapeDtypeStruct((m_p, o.total_cols), o.dtype) for o in outs]
                 + [jax.ShapeDtypeStruct((m_s, o.total_cols), o.dtype) for o in outs])
    scratch_shapes = ([pltpu.VMEM((tile_m, c), dt) for c, dt in scratch]
                      + [pltpu.VMEM((m_s, c), dt) for c, dt in scratch])
    n_in, n_out, n_scr = len(operands), len(outs), len(scratch)
    row_pos = [k for k, op in enumerate(operands) if isinstance(op, _Rows)]

    def kernel(*refs):
        ins_p = list(refs[:n_in])
        ins_s = list(ins_p)
        for k, ref in zip(row_pos, refs[n_in:n_in + len(row_pos)]):
            ins_s[k] = ref
        r = n_in + len(row_pos)
        outs_p, outs_s = refs[r:r + n_out], refs[r + n_out:r + 2 * n_out]
        r += 2 * n_out
        body(*ins_p, *outs_p, *refs[r:r + n_scr])

        @pl.when(pl.program_id(0) == last)
        def _():
            body(*ins_s, *outs_s, *refs[r + n_scr:])

    res = pl.pallas_call(
        kernel,
        grid=(m_p // tile_m, inner),
        in_specs=specs_p + specs_s,
        out_specs=out_specs,
        out_shape=out_shape,
        scratch_shapes=scratch_shapes,
        compiler_params=_params(("arbitrary", "arbitrary")),
        name=name,
    )(*args_p, *args_s)
    return list(zip(res[:n_out], res[n_out:]))


def _proj_body(x_ref, g_ref, w_ref, o_ref, h_ref):
    @pl.when(pl.program_id(1) == 0)
    def _():
        h_ref[...] = _rms_rows(x_ref[...], g_ref[...]).astype(BF16)
    o_ref[...] = _dot(h_ref[...], w_ref[...].astype(BF16))


def _proj(x, g, w):
    d, n = w.shape
    tn = TILE_N_PROJ
    return _dense_call(
        _proj_body, "proj", TILE_M_PROJ,
        [_Rows(*x, d, lambda j: 0, single=True),
         _Shared(g, (1, d), lambda j: (0, 0)),
         _Shared(w, (d, tn), lambda j: (0, j))],
        [_Out(n, tn, lambda j: j, F32)], [(d, BF16)], n // tn)[0]


def _merge_body(or_ref, oh_ref, wr_ref, wh_ref, ga_ref, gb_ref, o_ref):
    ya = _dot(or_ref[...], wr_ref[...])
    yb = _dot(oh_ref[...], wh_ref[...])
    o_ref[...] = (_sigmoid(ga_ref[...]) * ya + _sigmoid(gb_ref[...]) * yb).astype(BF16)


def _merge(o_r, o_h, w_r, w_h, proj):
    n = w_r.shape[1]
    return _dense_call(
        _merge_body, "merge", TILE_M_MERGE,
        [_Rows(*o_r, RET_V, lambda j: 0),
         _Rows(*o_h, HG_V, lambda j: 0),
         _Shared(w_r, (RET_V, n), lambda j: (0, 0), resident=True),
         _Shared(w_h, (HG_V, n), lambda j: (0, 0), resident=True),
         _Rows(*proj, n, lambda j: OFF_GA // n),
         _Rows(*proj, n, lambda j: OFF_GB // n)],
        [_Out(n, n, lambda j: 0, BF16)], [], 1)[0]


def _resid_body(x_ref, m_ref, w_ref, g_ref, o_ref, h_ref):
    x1 = x_ref[...] + _dot(m_ref[...], w_ref[...])
    o_ref[...] = x1
    h_ref[...] = _rms_rows(x1, g_ref[...]).astype(BF16)


def _resid(x, mrg, w, g):
    d = w.shape[1]
    return _dense_call(
        _resid_body, "resid", TILE_M_RESID,
        [_Rows(*x, d, lambda j: 0),
         _Rows(*mrg, d, lambda j: 0),
         _Shared(w, (d, d), lambda j: (0, 0), resident=True),
         _Shared(g, (1, d), lambda j: (0, 0))],
        [_Out(d, d, lambda j: 0, F32), _Out(d, d, lambda j: 0, BF16)], [], 1)


def _ffn_in_body(h_ref, wg_ref, wu_ref, o_ref):
    h = h_ref[...]
    gate = _dot(h, wg_ref[...].astype(BF16))
    up = _dot(h, wu_ref[...].astype(BF16))
    o_ref[...] = (_silu(gate) * up).astype(BF16)


def _ffn_in(h, w):
    d = w.shape[0]
    tn = TILE_N_FFN_IN
    return _dense_call(
        _ffn_in_body, "ffn_in", TILE_M_FFN_IN,
        [_Rows(*h, d, lambda j: 0),
         _Shared(w, (d, tn), lambda j: (0, j)),
         _Shared(w, (d, tn), lambda j: (0, D_FF // tn + j))],
        [_Out(D_FF, tn, lambda j: j, BF16)], [], D_FF // tn)[0]


def _ffn_out_body(a_ref, w_ref, x_ref, g_ref, o_ref):
    o_ref[...] = _rms_rows(x_ref[...] + _dot(a_ref[...], w_ref[...]), g_ref[...])


def _ffn_out(act, w, x, g):
    k, d = w.shape
    return _dense_call(
        _ffn_out_body, "ffn_out", TILE_M_FFN_OUT,
        [_Rows(*act, k, lambda j: 0),
         _Shared(w, (k, d), lambda j: (0, 0), resident=True),
         _Rows(*x, d, lambda j: 0),
         _Shared(g, (1, d), lambda j: (0, 0))],
        [_Out(d, d, lambda j: 0, F32)], [], 1)[0]


def _rotary(x, cos, sin_signed):
    return x * cos + pltpu.roll(x, RET_DK // 2, 1) * sin_signed


def _ret_kernel(q_ref, k_ref, v_ref, g_ref, cos_ref, sin_ref, o_ref, so_ref, s_ref):
    n = pl.program_id(1)

    @pl.when(n == 0)
    def _():
        s_ref[...] = jnp.zeros_like(s_ref)

    c = CHUNK
    cos = cos_ref[...]
    sin = sin_ref[...]
    row = lax.broadcasted_iota(jnp.int32, (c, c), 0)
    col = lax.broadcasted_iota(jnp.int32, (c, c), 1)
    diff = (row - col).astype(F32)
    idx = lax.broadcasted_iota(jnp.int32, (c, RET_DK), 0).astype(F32)
    for h in range(RET_HEADS):
        lg = LOG_GAMMA[h]
        q = _rotary(q_ref[0, :, h * RET_DK:(h + 1) * RET_DK], cos, sin)
        k = _rotary(k_ref[0, :, h * RET_DK:(h + 1) * RET_DK], cos, sin) * (RET_DK ** -0.5)
        v = v_ref[0, :, h * RET_DV:(h + 1) * RET_DV].astype(BF16)
        decay_mask = jnp.where(diff >= 0, jnp.exp(jnp.maximum(diff, 0.0) * lg), 0.0)
        q_decay = jnp.exp((idx + 1.0) * lg)
        k_decay = jnp.exp((c - 1.0 - idx) * lg)
        chunk_decay = math.exp(c * lg)
        s = s_ref[h]
        scores = _dot_nt(q.astype(BF16), k.astype(BF16)) * decay_mask
        o = _dot(scores.astype(BF16), v) + _dot((q * q_decay).astype(BF16), s.astype(BF16))
        s_ref[h] = chunk_decay * s + _dot_tn((k * k_decay).astype(BF16), v)
        o = o * lax.rsqrt(jnp.mean(o * o, axis=-1, keepdims=True) + EPS)
        gate = g_ref[0, :, h * RET_DV:(h + 1) * RET_DV]
        o_ref[0, :, h * RET_DV:(h + 1) * RET_DV] = (o * _silu(gate)).astype(BF16)

    @pl.when(n == pl.num_programs(1) - 1)
    def _():
        so_ref[0] = s_ref[...]


def _retention_prompt(proj3, cos, sin):
    b, t, _ = proj3.shape
    nc = t // CHUNK
    return pl.pallas_call(
        _ret_kernel,
        grid=(b, nc),
        in_specs=[pl.BlockSpec((1, CHUNK, RET_QK), lambda i, n: (i, n, OFF_RQ // RET_QK)),
                  pl.BlockSpec((1, CHUNK, RET_QK), lambda i, n: (i, n, OFF_RK // RET_QK)),
                  pl.BlockSpec((1, CHUNK, RET_V), lambda i, n: (i, n, OFF_RV // RET_V)),
                  pl.BlockSpec((1, CHUNK, RET_V), lambda i, n: (i, n, OFF_RG // RET_V)),
                  pl.BlockSpec((CHUNK, RET_DK), lambda i, n: (n, 0)),
                  pl.BlockSpec((CHUNK, RET_DK), lambda i, n: (n, 0))],
        out_specs=[pl.BlockSpec((1, CHUNK, RET_V), lambda i, n: (i, n, 0)),
                   pl.BlockSpec((1, RET_HEADS, RET_DK, RET_DV), lambda i, n: (i, 0, 0, 0))],
        out_shape=[jax.ShapeDtypeStruct((b, t, RET_V), BF16),
                   jax.ShapeDtypeStruct((b, RET_HEADS, RET_DK, RET_DV), F32)],
        scratch_shapes=[pltpu.VMEM((RET_HEADS, RET_DK, RET_DV), F32)],
        compiler_params=_params(("parallel", "arbitrary")),
        name="retention_prompt",
    )(proj3, proj3, proj3, proj3, cos, sin)


def _lower_bound(lbl):
    mx = jnp.max(lbl, axis=0, keepdims=True)
    e = jnp.exp(lbl - mx)
    return e[0:1, :] / jnp.sum(e, axis=0, keepdims=True)


def _hgrn_gates(z, lb):
    half = 0.5 * (1.0 - lb)
    ht = half * jnp.tanh(0.5 * z)
    return (lb + half) + ht, half - ht


def _cumsum_rows(tri, x):
    hi = x.astype(BF16)
    r1 = x - hi.astype(F32)
    mid = r1.astype(BF16)
    lo = (r1 - mid.astype(F32)).astype(BF16)
    return _dot(tri, hi) + _dot(tri, mid) + _dot(tri, lo)


def _bcast_rows(x, group, r):
    c, d = x.shape
    x3 = x.reshape(c // group, group, d)
    return jnp.broadcast_to(x3[:, r:r + 1, :], x3.shape).reshape(c, d)


def _hgrn_kernel(q_ref, f_ref, i_ref, g_ref, lbl_ref, nrm_ref, o_ref, so_ref, st_ref):
    n = pl.program_id(1)

    @pl.when(n == 0)
    def _():
        st_ref[...] = jnp.zeros_like(st_ref)

    c = CHUNK
    row = lax.broadcasted_iota(jnp.int32, (c, c), 0)
    col = lax.broadcasted_iota(jnp.int32, (c, c), 1)
    rowd = lax.broadcasted_iota(jnp.int32, (c, HG_DK), 0)
    tri = jnp.where(row >= col, 1.0, 0.0).astype(BF16)
    r2, r4, r8 = rowd & 1, rowd & 3, rowd & 7
    block_masks = [(row >> s) == (col >> s) for s in range(c.bit_length())]
    nrm = nrm_ref[...]

    def up(x, k):
        return pltpu.roll(x, k, 0)

    def dn(x, k):
        return pltpu.roll(x, c - k, 0)

    def head(h, carry):
        off = pl.multiple_of(h * HG_DK, HG_DK)
        lb = _lower_bound(lbl_ref[:, pl.ds(off, HG_DK)])
        f, kin = _hgrn_gates(f_ref[0, :, pl.ds(off, HG_DK)], lb)
        qh = _silu(q_ref[0, :, pl.ds(off, HG_DK)])
        v = i_ref[0, :, pl.ds(off, HG_DV)].astype(BF16)
        b = _cumsum_rows(tri, jnp.log(f))

        f_up = f * up(f, 1)
        f_dn = f * dn(f, 1)
        pre2 = jnp.where(r2 == 1, f_up, f)
        pre4 = jnp.where(r4 >= 1, f_up, f)
        pre4 = jnp.where(r4 >= 2, pre4 * up(pre4, 2), pre4)
        suf4 = jnp.where(r4 <= 2, f_dn, f)
        suf4 = jnp.where(r4 <= 1, suf4 * dn(suf4, 2), suf4)

        def level(s):
            if s == 0:
                return qh, kin
            if s == 1:
                return jnp.where(r2 == 1, qh * f, 0.0), jnp.where(r2 == 0, kin, 0.0)
            if s == 2:
                return (jnp.where(r4 >= 2, qh * pre2, 0.0),
                        jnp.where(r4 == 0, kin * dn(f, 1), jnp.where(r4 == 1, kin, 0.0)))
            if s == 3:
                return (jnp.where(r8 >= 4, qh * pre4, 0.0),
                        jnp.where(r8 >= 4, 0.0, jnp.where(r4 == 3, kin, kin * dn(suf4, 1))))
            half = 1 << (s - 1)
            ref = _bcast_rows(b, 2 * half, half - 1)
            upper = (rowd & (2 * half - 1)) >= half
            return jnp.where(upper, qh * jnp.exp(b - ref), 0.0), jnp.where(upper, 0.0, kin * jnp.exp(ref - b))

        a = jnp.zeros((c, c), F32)
        for s in reversed(range(len(block_masks))):
            q_s, k_s = level(s)
            a = jnp.where(block_masks[s], _dot_nt(q_s.astype(BF16), k_s.astype(BF16)), a)

        st = st_ref[h]
        b_last = b[c - 1:c, :]
        o = _dot(a.astype(BF16), v) + _dot_nt((qh * jnp.exp(b)).astype(BF16), st.astype(BF16))
        st_ref[h] = st * jnp.exp(b_last) + _dot_tn(v, (kin * jnp.exp(b_last - b)).astype(BF16))
        o = o * lax.rsqrt(jnp.mean(o * o, axis=-1, keepdims=True) + EPS) * nrm
        o_ref[0, :, pl.ds(off, HG_DV)] = (o * _silu(g_ref[0, :, pl.ds(off, HG_DV)])).astype(BF16)
        return carry

    lax.fori_loop(0, HG_HEADS, head, 0, unroll=2)

    @pl.when(n == pl.num_programs(1) - 1)
    def _():
        for h in range(HG_HEADS):
            so_ref[0, h] = st_ref[h].T


def _hgrn_prompt(proj3, lbl, nrm):
    b, t, _ = proj3.shape
    nc = t // CHUNK
    return pl.pallas_call(
        _hgrn_kernel,
        grid=(b, nc),
        in_specs=[pl.BlockSpec((1, CHUNK, HG_K), lambda i, n: (i, n, OFF_HQ // HG_K)),
                  pl.BlockSpec((1, CHUNK, HG_K), lambda i, n: (i, n, OFF_HF // HG_K)),
                  pl.BlockSpec((1, CHUNK, HG_V), lambda i, n: (i, n, OFF_HI // HG_V)),
                  pl.BlockSpec((1, CHUNK, HG_V), lambda i, n: (i, n, OFF_HG // HG_V)),
                  pl.BlockSpec((DEPTH + 1, HG_K), lambda i, n: (0, 0)),
                  pl.BlockSpec((1, HG_DV), lambda i, n: (0, 0))],
        out_specs=[pl.BlockSpec((1, CHUNK, HG_V), lambda i, n: (i, n, 0)),
                   pl.BlockSpec((1, HG_HEADS, HG_DK, HG_DV), lambda i, n: (i, 0, 0, 0))],
        out_shape=[jax.ShapeDtypeStruct((b, t, HG_V), BF16),
                   jax.ShapeDtypeStruct((b, HG_HEADS, HG_DK, HG_DV), F32)],
        scratch_shapes=[pltpu.VMEM((HG_HEADS, HG_DV, HG_DK), F32)],
        compiler_params=_params(("parallel", "arbitrary")),
        name="hgrn_prompt",
    )(proj3, proj3, proj3, proj3, lbl, nrm)


def _sample_prep_kernel(p_ref, cos_ref, sin_ref, lbl_ref, qr_ref, kr_ref, qh_ref, f_ref, kin_ref):
    cos = cos_ref[...]
    sin = sin_ref[...]
    for h in range(RET_HEADS):
        sl = slice(h * RET_DK, (h + 1) * RET_DK)
        qr_ref[:, sl] = _rotary(p_ref[:, OFF_RQ + h * RET_DK:OFF_RQ + (h + 1) * RET_DK], cos, sin)
        kr_ref[:, sl] = _rotary(p_ref[:, OFF_RK + h * RET_DK:OFF_RK + (h + 1) * RET_DK], cos, sin) * (RET_DK ** -0.5)
    f, kin = _hgrn_gates(p_ref[:, OFF_HF:OFF_HF + HG_K], _lower_bound(lbl_ref[...]))
    f_ref[...] = f
    kin_ref[...] = kin
    qh_ref[...] = _silu(p_ref[:, OFF_HQ:OFF_HQ + HG_K])


def _sample_prep(proj, cos, sin, lbl):
    m = proj.shape[0]
    shp = jax.ShapeDtypeStruct((m, RET_QK), F32)
    return pl.pallas_call(
        _sample_prep_kernel,
        grid=(1,),
        in_specs=[pl.BlockSpec((m, OFF_GA), lambda i: (0, 0)),
                  pl.BlockSpec((1, RET_DK), lambda i: (0, 0)),
                  pl.BlockSpec((1, RET_DK), lambda i: (0, 0)),
                  pl.BlockSpec((DEPTH + 1, HG_K), lambda i: (0, 0))],
        out_specs=[pl.BlockSpec((m, RET_QK), lambda i: (0, 0))] * 5,
        out_shape=[shp] * 5,
        compiler_params=_params(("arbitrary",)),
        name="sample_prep",
    )(proj, cos, sin, lbl)


def _sample_state_kernel(sr_ref, sh_ref, qr_ref, kr_ref, qh_ref, f_ref, kin_ref, vr_ref, rg_ref, vh_ref, hg_ref,
                         nrm_ref, sro_ref, sho_ref, or_ref, oh_ref):
    lanes = RET_DK
    for t in range(SAMPLE_SEQS_PER_STEP):
        b = pl.program_id(0) * SAMPLE_SEQS_PER_STEP + t
        base = pl.multiple_of((b // SUB) * SUB, SUB)

        def row(ref, sl, b=b, base=base):
            tile = ref[pl.ds(base, SUB), sl]
            sub = lax.broadcasted_iota(jnp.int32, tile.shape, 0)
            return jnp.sum(jnp.where(sub == b % SUB, tile, 0.0), axis=0, keepdims=True)

        def column(ref, h, row=row):
            r = row(ref, slice(h * RET_DK, (h + 1) * RET_DK))
            return jnp.broadcast_to(r, (RET_DK, RET_DK)).T

        for h in range(RET_HEADS):
            kc = column(kr_ref, h)
            qc = column(qr_ref, h)
            gamma = math.exp(LOG_GAMMA[h])
            outs = []
            for half in range(RET_DV // lanes):
                hs = slice(half * lanes, (half + 1) * lanes)
                s = gamma * sr_ref[t, h, :, hs] + kc * row(vr_ref, slice(h * RET_DV + hs.start, h * RET_DV + hs.stop))
                sro_ref[t, h, :, hs] = s
                outs.append(jnp.sum(qc * s, axis=0, keepdims=True))
            o = jnp.concatenate(outs, axis=1)
            o = o * lax.rsqrt(jnp.mean(o * o, axis=-1, keepdims=True) + EPS)
            sl = slice(h * RET_DV, (h + 1) * RET_DV)
            or_ref[t, :, sl] = o * _silu(row(rg_ref, sl))
        for h in range(HG_HEADS):
            sl = slice(h * HG_DV, (h + 1) * HG_DV)
            s = column(f_ref, h) * sh_ref[t, h] + column(kin_ref, h) * row(vh_ref, sl)
            sho_ref[t, h] = s
            o = jnp.sum(column(qh_ref, h) * s, axis=0, keepdims=True)
            o = o * lax.rsqrt(jnp.mean(o * o, axis=-1, keepdims=True) + EPS) * nrm_ref[...]
            oh_ref[t, :, sl] = o * _silu(row(hg_ref, sl))


def _sample_state(s_ret, s_hg, rows, proj, nrm):
    b = s_ret.shape[0]
    nb = SAMPLE_SEQS_PER_STEP
    row_spec = pl.BlockSpec((b, RET_QK), lambda i: (0, 0))
    return pl.pallas_call(
        _sample_state_kernel,
        grid=(b // nb,),
        in_specs=[pl.BlockSpec((nb, RET_HEADS, RET_DK, RET_DV), lambda i: (i, 0, 0, 0)),
                  pl.BlockSpec((nb, HG_HEADS, HG_DK, HG_DV), lambda i: (i, 0, 0, 0)),
                  row_spec, row_spec, row_spec, row_spec, row_spec,
                  pl.BlockSpec((b, RET_V), lambda i: (0, OFF_RV // RET_V)),
                  pl.BlockSpec((b, RET_V), lambda i: (0, OFF_RG // RET_V)),
                  pl.BlockSpec((b, HG_V), lambda i: (0, OFF_HI // HG_V)),
                  pl.BlockSpec((b, HG_V), lambda i: (0, OFF_HG // HG_V)),
                  pl.BlockSpec((1, HG_DV), lambda i: (0, 0))],
        out_specs=[pl.BlockSpec((nb, RET_HEADS, RET_DK, RET_DV), lambda i: (i, 0, 0, 0)),
                   pl.BlockSpec((nb, HG_HEADS, HG_DK, HG_DV), lambda i: (i, 0, 0, 0)),
                   pl.BlockSpec((nb, 1, RET_V), lambda i: (i, 0, 0)),
                   pl.BlockSpec((nb, 1, HG_V), lambda i: (i, 0, 0))],
        out_shape=[jax.ShapeDtypeStruct(s_ret.shape, F32),
                   jax.ShapeDtypeStruct(s_hg.shape, F32),
                   jax.ShapeDtypeStruct((b, 1, RET_V), F32),
                   jax.ShapeDtypeStruct((b, 1, HG_V), F32)],
        compiler_params=_params(("arbitrary",)),
        name="sample_state",
    )(s_ret, s_hg, *rows, proj, proj, proj, proj, nrm)


def _rope_tables(pos):
    half = RET_DK // 2
    inv_freq = ROPE_BASE ** (-jnp.arange(half, dtype=F32) / half)
    ang = pos.astype(F32)[:, None] * inv_freq[None, :]
    cos, sin = jnp.cos(ang), jnp.sin(ang)
    return jnp.concatenate([cos, cos], axis=-1), jnp.concatenate([-sin, sin], axis=-1)


def kernel(x_prompt, x_sample, state_ret, state_hgrn, w_in, w_ret_out, w_hgrn_out, w_out, norm_mix, norm_ffn,
           hgrn_norm, hgrn_lb_logits, w_ffn_in, w_ffn_out, norm_final):
    assert w_in.shape[0] == DEPTH == 1
    bp, t, d = x_prompt.shape
    bs = x_sample.shape[0]
    nrm = hgrn_norm[0][None, :]
    lbl = hgrn_lb_logits.astype(F32)
    x = (x_prompt.reshape(bp * t, d), x_sample.reshape(bs, d))

    proj = _proj(x, norm_mix[0][None, :], w_in[0])

    proj_p3 = proj[0].reshape(bp, t, IN_TOTAL)
    cos_p, sin_p = _rope_tables(jnp.arange(t))
    o_r, s_ret_p = _retention_prompt(proj_p3, cos_p, sin_p)
    o_h, s_hg_p = _hgrn_prompt(proj_p3, lbl, nrm)

    cos_s, sin_s = _rope_tables(PAST_LEN + jnp.arange(1))
    rows = _sample_prep(proj[1], cos_s, sin_s, lbl)
    s_ret_s, s_hg_s, o_rs, o_hs = _sample_state(state_ret[0], state_hgrn[0], rows, proj[1], nrm)

    o_r = (o_r.reshape(bp * t, RET_V), o_rs.reshape(bs, RET_V).astype(BF16))
    o_h = (o_h.reshape(bp * t, HG_V), o_hs.reshape(bs, HG_V).astype(BF16))
    mrg = _merge(o_r, o_h, w_ret_out[0].astype(BF16), w_hgrn_out[0].astype(BF16), proj)
    x1, h2 = _resid(x, mrg, w_out[0].astype(BF16), norm_ffn[0][None, :])
    act = _ffn_in(h2, w_ffn_in[0])
    y_p, y_s = _ffn_out(act, w_ffn_out[0].astype(BF16), x1, norm_final[None, :])

    return (y_p.reshape(bp, t, d), y_s.reshape(bs, 1, d), s_ret_p[None], s_hg_p[None], s_ret_s[None], s_hg_s[None])
```

```python
import math
from typing import Any, Callable, NamedTuple

import numpy as np
import jax
import jax.numpy as jnp
from jax import lax
from jax.experimental import pallas as pl
from jax.experimental.pallas import tpu as pltpu

D_MODEL = 2048
DEPTH = 1
PAST_LEN = 16384
RET_HEADS = 8
RET_DK = 128
RET_DV = 256
RET_QK = RET_HEADS * RET_DK
RET_V = RET_HEADS * RET_DV
ROPE_BASE = 10000.0
HG_HEADS = 8
HG_DK = 128
HG_DV = 128
HG_K = HG_HEADS * HG_DK
HG_V = HG_HEADS * HG_DV
D_FF = 5632
EPS = 1e-6

OFF_RQ = 0
OFF_RK = OFF_RQ + RET_QK
OFF_RV = OFF_RK + RET_QK
OFF_RG = OFF_RV + RET_V
OFF_HQ = OFF_RG + RET_V
OFF_HF = OFF_HQ + HG_K
OFF_HI = OFF_HF + HG_K
OFF_HG = OFF_HI + HG_V
OFF_GA = OFF_HG + HG_V
OFF_GB = OFF_GA + D_MODEL
IN_TOTAL = OFF_GB + D_MODEL

CHUNK = 128
SUB = 8
SAMPLE_SEQS_PER_STEP = 2
V7X_VMEM_LIMIT = 56 * 1024 * 1024

F32 = jnp.float32
BF16 = jnp.bfloat16

LOG_GAMMA = [float(np.log(np.float32(1.0) - np.float32(2.0) ** np.float32(-5.0 - h))) for h in range(RET_HEADS)]


def _params(sem, vmem=V7X_VMEM_LIMIT):
    return pltpu.CompilerParams(dimension_semantics=sem, vmem_limit_bytes=vmem)


def _rms_rows(x, g):
    return (x * lax.rsqrt(jnp.mean(x * x, axis=-1, keepdims=True) + EPS)) * g


def _sigmoid(x):
    return 0.5 + 0.5 * jnp.tanh(0.5 * x)


def _silu(x):
    hx = 0.5 * x
    return hx + hx * jnp.tanh(hx)


def _dot(a, b):
    return jnp.dot(a, b, preferred_element_type=F32)


def _dot_nt(a, b):
    return lax.dot_general(a, b, (((1,), (1,)), ((), ())), preferred_element_type=F32)


def _dot_tn(a, b):
    return lax.dot_general(a, b, (((0,), (0,)), ((), ())), preferred_element_type=F32)


TILE_M_PROJ, TILE_N_PROJ = 2048, 512
TILE_M_MERGE = 512
TILE_M_RESID = 512
TILE_M_FFN_IN, TILE_N_FFN_IN = 2048, 512
TILE_M_FFN_OUT = 256


class _Rows(NamedTuple):
    prompt: Any
    sample: Any
    cols: int
    col: Callable
    single: bool = False


class _Shared(NamedTuple):
    array: Any
    block: tuple
    index: Callable
    resident: bool = False


class _Out(NamedTuple):
    total_cols: int
    cols: int
    col: Callable
    dtype: Any


def _dense_call(body, name, tile_m, operands, outs, scratch, inner):
    rows = [op for op in operands if isinstance(op, _Rows)]
    m_p, m_s = rows[0].prompt.shape[0], rows[0].sample.shape[0]
    last = m_p // tile_m - 1

    def frozen(i, j):
        return jnp.where(i == last, j, 0)

    specs_p, specs_s, args_p, args_s = [], [], [], []
    for op in operands:
        if isinstance(op, _Rows):
            mode = dict(pipeline_mode=pl.Buffered(1)) if op.single else {}
            specs_p.append(pl.BlockSpec((tile_m, op.cols), lambda i, j, op=op: (i, op.col(j)), **mode))
            specs_s.append(pl.BlockSpec((m_s, op.cols), lambda i, j, op=op: (0, op.col(frozen(i, j)))))
            args_p.append(op.prompt)
            args_s.append(op.sample)
        else:
            mode = dict(pipeline_mode=pl.Buffered(1)) if op.resident else {}
            specs_p.append(pl.BlockSpec(op.block, lambda i, j, op=op: op.index(j), **mode))
            args_p.append(op.array)
    out_specs = ([pl.BlockSpec((tile_m, o.cols), lambda i, j, o=o: (i, o.col(j))) for o in outs]
                 + [pl.BlockSpec((m_s, o.cols), lambda i, j, o=o: (0, o.col(frozen(i, j)))) for o in outs])
    out_shape = ([jax.ShapeDtypeStruct((m_p, o.total_cols), o.dtype) for o in outs]
                 + [jax.ShapeDtypeStruct((m_s, o.total_cols), o.dtype) for o in outs])
    scratch_shapes = ([pltpu.VMEM((tile_m, c), dt) for c, dt in scratch]
                      + [pltpu.VMEM((m_s, c), dt) for c, dt in scratch])
    n_in, n_out, n_scr = len(operands), len(outs), len(scratch)
    row_pos = [k for k, op in enumerate(operands) if isinstance(op, _Rows)]

    def kernel(*refs):
        ins_p = list(refs[:n_in])
        ins_s = list(ins_p)
        for k, ref in zip(row_pos, refs[n_in:n_in + len(row_pos)]):
            ins_s[k] = ref
        r = n_in + len(row_pos)
        outs_p, outs_s = refs[r:r + n_out], refs[r + n_out:r + 2 * n_out]
        r += 2 * n_out
        body(*ins_p, *outs_p, *refs[r:r + n_scr])

        @pl.when(pl.program_id(0) == last)
        def _():
            body(*ins_s, *outs_s, *refs[r + n_scr:])

    res = pl.pallas_call(
        kernel,
        grid=(m_p // tile_m, inner),
        in_specs=specs_p + specs_s,
        out_specs=out_specs,
        out_shape=out_shape,
        scratch_shapes=scratch_shapes,
        compiler_params=_params(("arbitrary", "arbitrary")),
        name=name,
    )(*args_p, *args_s)
    return list(zip(res[:n_out], res[n_out:]))


def _proj_body(x_ref, g_ref, w_ref, o_ref, h_ref):
    @pl.when(pl.program_id(1) == 0)
    def _():
        h_ref[...] = _rms_rows(x_ref[...], g_ref[...]).astype(BF16)
    o_ref[...] = _dot(h_ref[...], w_ref[...].astype(BF16))


def _proj(x, g, w):
    d, n = w.shape
    tn = TILE_N_PROJ
    return _dense_call(
        _proj_body, "proj", TILE_M_PROJ,
        [_Rows(*x, d, lambda j: 0, single=True),
         _Shared(g, (1, d), lambda j: (0, 0)),
         _Shared(w, (d, tn), lambda j: (0, j))],
        [_Out(n, tn, lambda j: j, F32)], [(d, BF16)], n // tn)[0]


def _merge_body(or_ref, oh_ref, wr_ref, wh_ref, ga_ref, gb_ref, o_ref):
    ya = _dot(or_ref[...], wr_ref[...])
    yb = _dot(oh_ref[...], wh_ref[...])
    o_ref[...] = (_sigmoid(ga_ref[...]) * ya + _sigmoid(gb_ref[...]) * yb).astype(BF16)


def _merge(o_r, o_h, w_r, w_h, proj):
    n = w_r.shape[1]
    return _dense_call(
        _merge_body, "merge", TILE_M_MERGE,
        [_Rows(*o_r, RET_V, lambda j: 0),
         _Rows(*o_h, HG_V, lambda j: 0),
         _Shared(w_r, (RET_V, n), lambda j: (0, 0), resident=True),
         _Shared(w_h, (HG_V, n), lambda j: (0, 0), resident=True),
         _Rows(*proj, n, lambda j: OFF_GA // n),
         _Rows(*proj, n, lambda j: OFF_GB // n)],
        [_Out(n, n, lambda j: 0, BF16)], [], 1)[0]


def _resid_body(x_ref, m_ref, w_ref, g_ref, o_ref, h_ref):
    x1 = x_ref[...] + _dot(m_ref[...], w_ref[...])
    o_ref[...] = x1
    h_ref[...] = _rms_rows(x1, g_ref[...]).astype(BF16)


def _resid(x, mrg, w, g):
    d = w.shape[1]
    return _dense_call(
        _resid_body, "resid", TILE_M_RESID,
        [_Rows(*x, d, lambda j: 0),
         _Rows(*mrg, d, lambda j: 0),
         _Shared(w, (d, d), lambda j: (0, 0), resident=True),
         _Shared(g, (1, d), lambda j: (0, 0))],
        [_Out(d, d, lambda j: 0, F32), _Out(d, d, lambda j: 0, BF16)], [], 1)


def _ffn_in_body(h_ref, wg_ref, wu_ref, o_ref):
    h = h_ref[...]
    gate = _dot(h, wg_ref[...].astype(BF16))
    up = _dot(h, wu_ref[...].astype(BF16))
    o_ref[...] = (_silu(gate) * up).astype(BF16)


def _ffn_in(h, w):
    d = w.shape[0]
    tn = TILE_N_FFN_IN
    return _dense_call(
        _ffn_in_body, "ffn_in", TILE_M_FFN_IN,
        [_Rows(*h, d, lambda j: 0),
         _Shared(w, (d, tn), lambda j: (0, j)),
         _Shared(w, (d, tn), lambda j: (0, D_FF // tn + j))],
        [_Out(D_FF, tn, lambda j: j, BF16)], [], D_FF // tn)[0]


def _ffn_out_body(a_ref, w_ref, x_ref, g_ref, o_ref):
    o_ref[...] = _rms_rows(x_ref[...] + _dot(a_ref[...], w_ref[...]), g_ref[...])


def _ffn_out(act, w, x, g):
    k, d = w.shape
    return _dense_call(
        _ffn_out_body, "ffn_out", TILE_M_FFN_OUT,
        [_Rows(*act, k, lambda j: 0),
         _Shared(w, (k, d), lambda j: (0, 0), resident=True),
         _Rows(*x, d, lambda j: 0),
         _Shared(g, (1, d), lambda j: (0, 0))],
        [_Out(d, d, lambda j: 0, F32)], [], 1)[0]


def _rotary(x, cos, sin_signed):
    return x * cos + pltpu.roll(x, RET_DK // 2, 1) * sin_signed


def _ret_kernel(q_ref, k_ref, v_ref, g_ref, cos_ref, sin_ref, o_ref, so_ref, s_ref, dec_ref):
    n = pl.program_id(1)
    c = CHUNK

    @pl.when(n == 0)
    def _():
        s_ref[...] = jnp.zeros_like(s_ref)
        row = lax.broadcasted_iota(jnp.int32, (c, c), 0)
        col = lax.broadcasted_iota(jnp.int32, (c, c), 1)
        diff = (row - col).astype(F32)
        idx = lax.broadcasted_iota(jnp.int32, (c, RET_DK), 0).astype(F32)
        for h in range(RET_HEADS):
            lg = LOG_GAMMA[h]
            dec_ref[h, 0] = jnp.where(diff >= 0, jnp.exp(jnp.maximum(diff, 0.0) * lg), 0.0)
            dec_ref[h, 1] = jnp.exp((idx + 1.0) * lg)
            dec_ref[h, 2] = jnp.exp((c - 1.0 - idx) * lg)

    cos = cos_ref[...]
    sin = sin_ref[...]
    for h in range(RET_HEADS):
        q = _rotary(q_ref[0, :, h * RET_DK:(h + 1) * RET_DK], cos, sin)
        k = _rotary(k_ref[0, :, h * RET_DK:(h + 1) * RET_DK], cos, sin) * (RET_DK ** -0.5)
        v = v_ref[0, :, h * RET_DV:(h + 1) * RET_DV].astype(BF16)
        chunk_decay = math.exp(c * LOG_GAMMA[h])
        s = s_ref[h]
        scores = _dot_nt(q.astype(BF16), k.astype(BF16)) * dec_ref[h, 0]
        o = _dot(scores.astype(BF16), v) + _dot((q * dec_ref[h, 1]).astype(BF16), s.astype(BF16))
        s_ref[h] = chunk_decay * s + _dot_tn((k * dec_ref[h, 2]).astype(BF16), v)
        o = o * lax.rsqrt(jnp.mean(o * o, axis=-1, keepdims=True) + EPS)
        gate = g_ref[0, :, h * RET_DV:(h + 1) * RET_DV]
        o_ref[0, :, h * RET_DV:(h + 1) * RET_DV] = (o * _silu(gate)).astype(BF16)

    @pl.when(n == pl.num_programs(1) - 1)
    def _():
        so_ref[0] = s_ref[...]


def _retention_prompt(proj3, cos, sin):
    b, t, _ = proj3.shape
    nc = t // CHUNK
    return pl.pallas_call(
        _ret_kernel,
        grid=(b, nc),
        in_specs=[pl.BlockSpec((1, CHUNK, RET_QK), lambda i, n: (i, n, OFF_RQ // RET_QK)),
                  pl.BlockSpec((1, CHUNK, RET_QK), lambda i, n: (i, n, OFF_RK // RET_QK)),
                  pl.BlockSpec((1, CHUNK, RET_V), lambda i, n: (i, n, OFF_RV // RET_V)),
                  pl.BlockSpec((1, CHUNK, RET_V), lambda i, n: (i, n, OFF_RG // RET_V)),
                  pl.BlockSpec((CHUNK, RET_DK), lambda i, n: (n, 0)),
                  pl.BlockSpec((CHUNK, RET_DK), lambda i, n: (n, 0))],
        out_specs=[pl.BlockSpec((1, CHUNK, RET_V), lambda i, n: (i, n, 0)),
                   pl.BlockSpec((1, RET_HEADS, RET_DK, RET_DV), lambda i, n: (i, 0, 0, 0))],
        out_shape=[jax.ShapeDtypeStruct((b, t, RET_V), BF16),
                   jax.ShapeDtypeStruct((b, RET_HEADS, RET_DK, RET_DV), F32)],
        scratch_shapes=[pltpu.VMEM((RET_HEADS, RET_DK, RET_DV), F32),
                        pltpu.VMEM((RET_HEADS, 3, CHUNK, RET_DK), F32)],
        compiler_params=_params(("parallel", "arbitrary")),
        name="retention_prompt",
    )(proj3, proj3, proj3, proj3, cos, sin)


def _lower_bound(lbl):
    mx = jnp.max(lbl, axis=0, keepdims=True)
    e = jnp.exp(lbl - mx)
    return e[0:1, :] / jnp.sum(e, axis=0, keepdims=True)


def _hgrn_gates(z, lb):
    half = 0.5 * (1.0 - lb)
    ht = half * jnp.tanh(0.5 * z)
    return (lb + half) + ht, half - ht


def _cumsum_rows(tri, x):
    hi = x.astype(BF16)
    r1 = x - hi.astype(F32)
    mid = r1.astype(BF16)
    lo = (r1 - mid.astype(F32)).astype(BF16)
    return _dot(tri, hi) + _dot(tri, mid) + _dot(tri, lo)


def _bcast_rows(x, group, r):
    c, d = x.shape
    x3 = x.reshape(c // group, group, d)
    return jnp.broadcast_to(x3[:, r:r + 1, :], x3.shape).reshape(c, d)


def _hgrn_kernel(q_ref, f_ref, i_ref, g_ref, lbl_ref, nrm_ref, o_ref, so_ref, st_ref):
    n = pl.program_id(1)

    @pl.when(n == 0)
    def _():
        st_ref[...] = jnp.zeros_like(st_ref)

    c = CHUNK
    row = lax.broadcasted_iota(jnp.int32, (c, c), 0)
    col = lax.broadcasted_iota(jnp.int32, (c, c), 1)
    rowd = lax.broadcasted_iota(jnp.int32, (c, HG_DK), 0)
    tri = jnp.where(row >= col, 1.0, 0.0).astype(BF16)
    r2, r4 = rowd & 1, rowd & 3
    block_masks = [(row >> s) == (col >> s) for s in range(c.bit_length())]
    nrm = nrm_ref[...]

    def up(x, k):
        return pltpu.roll(x, k, 0)

    def dn(x, k):
        return pltpu.roll(x, c - k, 0)

    def head(h):
        off = h * HG_DK
        lb = _lower_bound(lbl_ref[:, pl.ds(off, HG_DK)])
        f, kin = _hgrn_gates(f_ref[0, :, pl.ds(off, HG_DK)], lb)
        qh = _silu(q_ref[0, :, pl.ds(off, HG_DK)])
        v = i_ref[0, :, pl.ds(off, HG_DV)].astype(BF16)
        b = _cumsum_rows(tri, jnp.log2(f))

        f_up = f * up(f, 1)
        f_dn = f * dn(f, 1)
        pre2 = jnp.where(r2 == 1, f_up, f)
        pre4 = jnp.where(r4 >= 1, f_up, f)
        pre4 = jnp.where(r4 >= 2, pre4 * up(pre4, 2), pre4)
        suf4 = jnp.where(r4 <= 2, f_dn, f)
        suf4 = jnp.where(r4 <= 1, suf4 * dn(suf4, 2), suf4)

        def level(s):
            if s == 0:
                return qh, kin
            if s == 1:
                return qh * f, kin
            if s == 2:
                return qh * pre2, kin * jnp.where(r4 == 0, dn(f, 1), 1.0)
            if s == 3:
                return qh * pre4, kin * jnp.where(r4 == 3, 1.0, dn(suf4, 1))
            half = 1 << (s - 1)
            ref = _bcast_rows(b, 2 * half, half - 1)
            upper = (rowd & (2 * half - 1)) >= half
            return jnp.where(upper, qh * jnp.exp2(b - ref), 0.0), jnp.where(upper, 0.0, kin * jnp.exp2(ref - b))

        a = jnp.zeros((c, c), F32)
        for s in reversed(range(len(block_masks))):
            q_s, k_s = level(s)
            a = jnp.where(block_masks[s], _dot_nt(q_s.astype(BF16), k_s.astype(BF16)), a)
        a = jnp.where(col <= row, a, 0.0)

        st = st_ref[h]
        b_last = b[c - 1:c, :]
        o = _dot(a.astype(BF16), v) + _dot_nt((qh * jnp.exp2(b)).astype(BF16), st.astype(BF16))
        st_ref[h] = st * jnp.exp2(b_last) + _dot_tn(v, (kin * jnp.exp2(b_last - b)).astype(BF16))
        o = o * lax.rsqrt(jnp.mean(o * o, axis=-1, keepdims=True) + EPS) * nrm
        o_ref[0, :, pl.ds(off, HG_DV)] = (o * _silu(g_ref[0, :, pl.ds(off, HG_DV)])).astype(BF16)

    for h in range(HG_HEADS):
        head(h)

    @pl.when(n == pl.num_programs(1) - 1)
    def _():
        for h in range(HG_HEADS):
            so_ref[0, h] = st_ref[h].T


def _hgrn_prompt(proj3, lbl, nrm):
    b, t, _ = proj3.shape
    nc = t // CHUNK
    return pl.pallas_call(
        _hgrn_kernel,
        grid=(b, nc),
        in_specs=[pl.BlockSpec((1, CHUNK, HG_K), lambda i, n: (i, n, OFF_HQ // HG_K)),
                  pl.BlockSpec((1, CHUNK, HG_K), lambda i, n: (i, n, OFF_HF // HG_K)),
                  pl.BlockSpec((1, CHUNK, HG_V), lambda i, n: (i, n, OFF_HI // HG_V)),
                  pl.BlockSpec((1, CHUNK, HG_V), lambda i, n: (i, n, OFF_HG // HG_V)),
                  pl.BlockSpec((DEPTH + 1, HG_K), lambda i, n: (0, 0)),
                  pl.BlockSpec((1, HG_DV), lambda i, n: (0, 0))],
        out_specs=[pl.BlockSpec((1, CHUNK, HG_V), lambda i, n: (i, n, 0)),
                   pl.BlockSpec((1, HG_HEADS, HG_DK, HG_DV), lambda i, n: (i, 0, 0, 0))],
        out_shape=[jax.ShapeDtypeStruct((b, t, HG_V), BF16),
                   jax.ShapeDtypeStruct((b, HG_HEADS, HG_DK, HG_DV), F32)],
        scratch_shapes=[pltpu.VMEM((HG_HEADS, HG_DV, HG_DK), F32)],
        compiler_params=_params(("parallel", "arbitrary")),
        name="hgrn_prompt",
    )(proj3, proj3, proj3, proj3, lbl, nrm)


def _sample_prep_kernel(p_ref, cos_ref, sin_ref, lbl_ref, qr_ref, kr_ref, qh_ref, f_ref, kin_ref):
    cos = cos_ref[...]
    sin = sin_ref[...]
    for h in range(RET_HEADS):
        sl = slice(h * RET_DK, (h + 1) * RET_DK)
        qr_ref[:, sl] = _rotary(p_ref[:, OFF_RQ + h * RET_DK:OFF_RQ + (h + 1) * RET_DK], cos, sin)
        kr_ref[:, sl] = _rotary(p_ref[:, OFF_RK + h * RET_DK:OFF_RK + (h + 1) * RET_DK], cos, sin) * (RET_DK ** -0.5)
    f, kin = _hgrn_gates(p_ref[:, OFF_HF:OFF_HF + HG_K], _lower_bound(lbl_ref[...]))
    f_ref[...] = f
    kin_ref[...] = kin
    qh_ref[...] = _silu(p_ref[:, OFF_HQ:OFF_HQ + HG_K])


def _sample_prep(proj, cos, sin, lbl):
    m = proj.shape[0]
    shp = jax.ShapeDtypeStruct((m, RET_QK), F32)
    return pl.pallas_call(
        _sample_prep_kernel,
        grid=(1,),
        in_specs=[pl.BlockSpec((m, OFF_GA), lambda i: (0, 0)),
                  pl.BlockSpec((1, RET_DK), lambda i: (0, 0)),
                  pl.BlockSpec((1, RET_DK), lambda i: (0, 0)),
                  pl.BlockSpec((DEPTH + 1, HG_K), lambda i: (0, 0))],
        out_specs=[pl.BlockSpec((m, RET_QK), lambda i: (0, 0))] * 5,
        out_shape=[shp] * 5,
        compiler_params=_params(("arbitrary",)),
        name="sample_prep",
    )(proj, cos, sin, lbl)


def _sample_state_kernel(sr_ref, sh_ref, qr_ref, kr_ref, qh_ref, f_ref, kin_ref, vr_ref, rg_ref, vh_ref, hg_ref,
                         nrm_ref, sro_ref, sho_ref, or_ref, oh_ref):
    lanes = RET_DK
    for t in range(SAMPLE_SEQS_PER_STEP):
        b = pl.program_id(0) * SAMPLE_SEQS_PER_STEP + t
        base = pl.multiple_of((b // SUB) * SUB, SUB)

        def row(ref, sl, b=b, base=base):
            tile = ref[pl.ds(base, SUB), sl]
            sub = lax.broadcasted_iota(jnp.int32, tile.shape, 0)
            return jnp.sum(jnp.where(sub == b % SUB, tile, 0.0), axis=0, keepdims=True)

        def column(ref, h, row=row):
            r = row(ref, slice(h * RET_DK, (h + 1) * RET_DK))
            return jnp.broadcast_to(r, (RET_DK, RET_DK)).T

        for h in range(RET_HEADS):
            kc = column(kr_ref, h)
            qc = column(qr_ref, h)
            gamma = math.exp(LOG_GAMMA[h])
            outs = []
            for half in range(RET_DV // lanes):
                hs = slice(half * lanes, (half + 1) * lanes)
                s = gamma * sr_ref[t, h, :, hs] + kc * row(vr_ref, slice(h * RET_DV + hs.start, h * RET_DV + hs.stop))
                sro_ref[t, h, :, hs] = s
                outs.append(jnp.sum(qc * s, axis=0, keepdims=True))
            o = jnp.concatenate(outs, axis=1)
            o = o * lax.rsqrt(jnp.mean(o * o, axis=-1, keepdims=True) + EPS)
            sl = slice(h * RET_DV, (h + 1) * RET_DV)
            or_ref[t, :, sl] = o * _silu(row(rg_ref, sl))
        for h in range(HG_HEADS):
            sl = slice(h * HG_DV, (h + 1) * HG_DV)
            s = column(f_ref, h) * sh_ref[t, h] + column(kin_ref, h) * row(vh_ref, sl)
            sho_ref[t, h] = s
            o = jnp.sum(column(qh_ref, h) * s, axis=0, keepdims=True)
            o = o * lax.rsqrt(jnp.mean(o * o, axis=-1, keepdims=True) + EPS) * nrm_ref[...]
            oh_ref[t, :, sl] = o * _silu(row(hg_ref, sl))


def _sample_state(s_ret, s_hg, rows, proj, nrm):
    b = s_ret.shape[0]
    nb = SAMPLE_SEQS_PER_STEP
    row_spec = pl.BlockSpec((b, RET_QK), lambda i: (0, 0))
    return pl.pallas_call(
        _sample_state_kernel,
        grid=(b // nb,),
        in_specs=[pl.BlockSpec((nb, RET_HEADS, RET_DK, RET_DV), lambda i: (i, 0, 0, 0)),
                  pl.BlockSpec((nb, HG_HEADS, HG_DK, HG_DV), lambda i: (i, 0, 0, 0)),
                  row_spec, row_spec, row_spec, row_spec, row_spec,
                  pl.BlockSpec((b, RET_V), lambda i: (0, OFF_RV // RET_V)),
                  pl.BlockSpec((b, RET_V), lambda i: (0, OFF_RG // RET_V)),
                  pl.BlockSpec((b, HG_V), lambda i: (0, OFF_HI // HG_V)),
                  pl.BlockSpec((b, HG_V), lambda i: (0, OFF_HG // HG_V)),
                  pl.BlockSpec((1, HG_DV), lambda i: (0, 0))],
        out_specs=[pl.BlockSpec((nb, RET_HEADS, RET_DK, RET_DV), lambda i: (i, 0, 0, 0)),
                   pl.BlockSpec((nb, HG_HEADS, HG_DK, HG_DV), lambda i: (i, 0, 0, 0)),
                   pl.BlockSpec((nb, 1, RET_V), lambda i: (i, 0, 0)),
                   pl.BlockSpec((nb, 1, HG_V), lambda i: (i, 0, 0))],
        out_shape=[jax.ShapeDtypeStruct(s_ret.shape, F32),
                   jax.ShapeDtypeStruct(s_hg.shape, F32),
                   jax.ShapeDtypeStruct((b, 1, RET_V), F32),
                   jax.ShapeDtypeStruct((b, 1, HG_V), F32)],
        compiler_params=_params(("arbitrary",)),
        name="sample_state",
    )(s_ret, s_hg, *rows, proj, proj, proj, proj, nrm)


def _rope_tables(pos):
    half = RET_DK // 2
    inv_freq = ROPE_BASE ** (-jnp.arange(half, dtype=F32) / half)
    ang = pos.astype(F32)[:, None] * inv_freq[None, :]
    cos, sin = jnp.cos(ang), jnp.sin(ang)
    return jnp.concatenate([cos, cos], axis=-1), jnp.concatenate([-sin, sin], axis=-1)


def kernel(x_prompt, x_sample, state_ret, state_hgrn, w_in, w_ret_out, w_hgrn_out, w_out, norm_mix, norm_ffn,
           hgrn_norm, hgrn_lb_logits, w_ffn_in, w_ffn_out, norm_final):
    assert w_in.shape[0] == DEPTH == 1
    bp, t, d = x_prompt.shape
    bs = x_sample.shape[0]
    nrm = hgrn_norm[0][None, :]
    lbl = hgrn_lb_logits.astype(F32)
    x = (x_prompt.reshape(bp * t, d), x_sample.reshape(bs, d))

    proj = _proj(x, norm_mix[0][None, :], w_in[0])

    proj_p3 = proj[0].reshape(bp, t, IN_TOTAL)
    cos_p, sin_p = _rope_tables(jnp.arange(t))
    o_r, s_ret_p = _retention_prompt(proj_p3, cos_p, sin_p)
    o_h, s_hg_p = _hgrn_prompt(proj_p3, lbl, nrm)

    cos_s, sin_s = _rope_tables(PAST_LEN + jnp.arange(1))
    rows = _sample_prep(proj[1], cos_s, sin_s, lbl)
    s_ret_s, s_hg_s, o_rs, o_hs = _sample_state(state_ret[0], state_hgrn[0], rows, proj[1], nrm)

    o_r = (o_r.reshape(bp * t, RET_V), o_rs.reshape(bs, RET_V).astype(BF16))
    o_h = (o_h.reshape(bp * t, HG_V), o_hs.reshape(bs, HG_V).astype(BF16))
    mrg = _merge(o_r, o_h, w_ret_out[0].astype(BF16), w_hgrn_out[0].astype(BF16), proj)
    x1, h2 = _resid(x, mrg, w_out[0].astype(BF16), norm_ffn[0][None, :])
    act = _ffn_in(h2, w_ffn_in[0])
    y_p, y_s = _ffn_out(act, w_ffn_out[0].astype(BF16), x1, norm_final[None, :])

    return (y_p.reshape(bp, t, d), y_s.reshape(bs, 1, d), s_ret_p[None], s_hg_p[None], s_ret_s[None], s_hg_s[None])
```

```python
import math
from typing import Any, Callable, NamedTuple

import numpy as np
import jax
import jax.numpy as jnp
from jax import lax
from jax.experimental import pallas as pl
from jax.experimental.pallas import tpu as pltpu

D_MODEL = 2048
DEPTH = 1
PAST_LEN = 16384
RET_HEADS = 8
RET_DK = 128
RET_DV = 256
RET_QK = RET_HEADS * RET_DK
RET_V = RET_HEADS * RET_DV
ROPE_BASE = 10000.0
HG_HEADS = 8
HG_DK = 128
HG_DV = 128
HG_K = HG_HEADS * HG_DK
HG_V = HG_HEADS * HG_DV
D_FF = 5632
EPS = 1e-6

OFF_RQ = 0
OFF_RK = OFF_RQ + RET_QK
OFF_RV = OFF_RK + RET_QK
OFF_RG = OFF_RV + RET_V
OFF_HQ = OFF_RG + RET_V
OFF_HF = OFF_HQ + HG_K
OFF_HI = OFF_HF + HG_K
OFF_HG = OFF_HI + HG_V
OFF_GA = OFF_HG + HG_V
OFF_GB = OFF_GA + D_MODEL
IN_TOTAL = OFF_GB + D_MODEL

CHUNK = 128
SUB = 8
SAMPLE_SEQS_PER_STEP = 4
V7X_VMEM_LIMIT = 56 * 1024 * 1024

F32 = jnp.float32
BF16 = jnp.bfloat16

LOG_GAMMA = [float(np.log(np.float32(1.0) - np.float32(2.0) ** np.float32(-5.0 - h))) for h in range(RET_HEADS)]


def _params(sem, vmem=V7X_VMEM_LIMIT):
    return pltpu.CompilerParams(dimension_semantics=sem, vmem_limit_bytes=vmem)


def _rms_rows(x, g):
    return (x * lax.rsqrt(jnp.mean(x * x, axis=-1, keepdims=True) + EPS)) * g


def _sigmoid(x):
    return 0.5 + 0.5 * jnp.tanh(0.5 * x)


def _silu(x):
    hx = 0.5 * x
    return hx + hx * jnp.tanh(hx)


def _dot(a, b):
    return jnp.dot(a, b, preferred_element_type=F32)


def _dot_nt(a, b):
    return lax.dot_general(a, b, (((1,), (1,)), ((), ())), preferred_element_type=F32)


def _dot_tn(a, b):
    return lax.dot_general(a, b, (((0,), (0,)), ((), ())), preferred_element_type=F32)


TILE_M_PROJ, TILE_N_PROJ = 2048, 512
TILE_M_MERGE = 512
TILE_M_RESID = 512
TILE_M_FFN_IN, TILE_N_FFN_IN = 2048, 512
TILE_M_FFN_OUT = 256


class _Rows(NamedTuple):
    prompt: Any
    sample: Any
    cols: int
    col: Callable
    single: bool = False


class _Shared(NamedTuple):
    array: Any
    block: tuple
    index: Callable
    resident: bool = False


class _Out(NamedTuple):
    total_cols: int
    cols: int
    col: Callable
    dtype: Any


def _dense_call(body, name, tile_m, operands, outs, scratch, inner):
    rows = [op for op in operands if isinstance(op, _Rows)]
    m_p, m_s = rows[0].prompt.shape[0], rows[0].sample.shape[0]
    last = m_p // tile_m - 1

    def frozen(i, j):
        return jnp.where(i == last, j, 0)

    specs_p, specs_s, args_p, args_s = [], [], [], []
    for op in operands:
        if isinstance(op, _Rows):
            mode = dict(pipeline_mode=pl.Buffered(1)) if op.single else {}
            specs_p.append(pl.BlockSpec((tile_m, op.cols), lambda i, j, op=op: (i, op.col(j)), **mode))
            specs_s.append(pl.BlockSpec((m_s, op.cols), lambda i, j, op=op: (0, op.col(frozen(i, j)))))
            args_p.append(op.prompt)
            args_s.append(op.sample)
        else:
            mode = dict(pipeline_mode=pl.Buffered(1)) if op.resident else {}
            specs_p.append(pl.BlockSpec(op.block, lambda i, j, op=op: op.index(j), **mode))
            args_p.append(op.array)
    out_specs = ([pl.BlockSpec((tile_m, o.cols), lambda i, j, o=o: (i, o.col(j))) for o in outs]
                 + [pl.BlockSpec((m_s, o.cols), lambda i, j, o=o: (0, o.col(frozen(i, j)))) for o in outs])
    out_shape = ([jax.ShapeDtypeStruct((m_p, o.total_cols), o.dtype) for o in outs]
                 + [jax.ShapeDtypeStruct((m_s, o.total_cols), o.dtype) for o in outs])
    scratch_shapes = ([pltpu.VMEM((tile_m, c), dt) for c, dt in scratch]
                      + [pltpu.VMEM((m_s, c), dt) for c, dt in scratch])
    n_in, n_out, n_scr = len(operands), len(outs), len(scratch)
    row_pos = [k for k, op in enumerate(operands) if isinstance(op, _Rows)]

    def kernel(*refs):
        ins_p = list(refs[:n_in])
        ins_s = list(ins_p)
        for k, ref in zip(row_pos, refs[n_in:n_in + len(row_pos)]):
            ins_s[k] = ref
        r = n_in + len(row_pos)
        outs_p, outs_s = refs[r:r + n_out], refs[r + n_out:r + 2 * n_out]
        r += 2 * n_out
        body(*ins_p, *outs_p, *refs[r:r + n_scr])

        @pl.when(pl.program_id(0) == last)
        def _():
            body(*ins_s, *outs_s, *refs[r + n_scr:])

    res = pl.pallas_call(
        kernel,
        grid=(m_p // tile_m, inner),
        in_specs=specs_p + specs_s,
        out_specs=out_specs,
        out_shape=out_shape,
        scratch_shapes=scratch_shapes,
        compiler_params=_params(("arbitrary", "arbitrary")),
        name=name,
    )(*args_p, *args_s)
    return list(zip(res[:n_out], res[n_out:]))


def _proj_body(x_ref, g_ref, w_ref, o_ref, h_ref):
    @pl.when(pl.program_id(1) == 0)
    def _():
        h_ref[...] = _rms_rows(x_ref[...], g_ref[...]).astype(BF16)
    o_ref[...] = _dot(h_ref[...], w_ref[...].astype(BF16))


def _proj(x, g, w):
    d, n = w.shape
    tn = TILE_N_PROJ
    return _dense_call(
        _proj_body, "proj", TILE_M_PROJ,
        [_Rows(*x, d, lambda j: 0, single=True),
         _Shared(g, (1, d), lambda j: (0, 0)),
         _Shared(w, (d, tn), lambda j: (0, j))],
        [_Out(n, tn, lambda j: j, F32)], [(d, BF16)], n // tn)[0]


def _merge_body(or_ref, oh_ref, wr_ref, wh_ref, ga_ref, gb_ref, o_ref):
    ya = _dot(or_ref[...], wr_ref[...])
    yb = _dot(oh_ref[...], wh_ref[...])
    o_ref[...] = (_sigmoid(ga_ref[...]) * ya + _sigmoid(gb_ref[...]) * yb).astype(BF16)


def _merge(o_r, o_h, w_r, w_h, proj):
    n = w_r.shape[1]
    return _dense_call(
        _merge_body, "merge", TILE_M_MERGE,
        [_Rows(*o_r, RET_V, lambda j: 0),
         _Rows(*o_h, HG_V, lambda j: 0),
         _Shared(w_r, (RET_V, n), lambda j: (0, 0), resident=True),
         _Shared(w_h, (HG_V, n), lambda j: (0, 0), resident=True),
         _Rows(*proj, n, lambda j: OFF_GA // n),
         _Rows(*proj, n, lambda j: OFF_GB // n)],
        [_Out(n, n, lambda j: 0, BF16)], [], 1)[0]


def _resid_body(x_ref, m_ref, w_ref, g_ref, o_ref, h_ref):
    x1 = x_ref[...] + _dot(m_ref[...], w_ref[...])
    o_ref[...] = x1
    h_ref[...] = _rms_rows(x1, g_ref[...]).astype(BF16)


def _resid(x, mrg, w, g):
    d = w.shape[1]
    return _dense_call(
        _resid_body, "resid", TILE_M_RESID,
        [_Rows(*x, d, lambda j: 0),
         _Rows(*mrg, d, lambda j: 0),
         _Shared(w, (d, d), lambda j: (0, 0), resident=True),
         _Shared(g, (1, d), lambda j: (0, 0))],
        [_Out(d, d, lambda j: 0, F32), _Out(d, d, lambda j: 0, BF16)], [], 1)


def _ffn_in_body(h_ref, wg_ref, wu_ref, o_ref):
    h = h_ref[...]
    gate = _dot(h, wg_ref[...].astype(BF16))
    up = _dot(h, wu_ref[...].astype(BF16))
    o_ref[...] = (_silu(gate) * up).astype(BF16)


def _ffn_in(h, w):
    d = w.shape[0]
    tn = TILE_N_FFN_IN
    return _dense_call(
        _ffn_in_body, "ffn_in", TILE_M_FFN_IN,
        [_Rows(*h, d, lambda j: 0),
         _Shared(w, (d, tn), lambda j: (0, j)),
         _Shared(w, (d, tn), lambda j: (0, D_FF // tn + j))],
        [_Out(D_FF, tn, lambda j: j, BF16)], [], D_FF // tn)[0]


def _ffn_out_body(a_ref, w_ref, x_ref, g_ref, o_ref):
    o_ref[...] = _rms_rows(x_ref[...] + _dot(a_ref[...], w_ref[...]), g_ref[...])


def _ffn_out(act, w, x, g):
    k, d = w.shape
    return _dense_call(
        _ffn_out_body, "ffn_out", TILE_M_FFN_OUT,
        [_Rows(*act, k, lambda j: 0),
         _Shared(w, (k, d), lambda j: (0, 0), resident=True),
         _Rows(*x, d, lambda j: 0),
         _Shared(g, (1, d), lambda j: (0, 0))],
        [_Out(d, d, lambda j: 0, F32)], [], 1)[0]


def _rotary(x, cos, sin_signed):
    return x * cos + pltpu.roll(x, RET_DK // 2, 1) * sin_signed


def _ret_kernel(p_ref, cos_ref, sin_ref, o_ref, so_ref, s_ref, dec_ref):
    n = pl.program_id(1)
    c = CHUNK

    @pl.when(n == 0)
    def _():
        s_ref[...] = jnp.zeros_like(s_ref)
        row = lax.broadcasted_iota(jnp.int32, (c, c), 0)
        col = lax.broadcasted_iota(jnp.int32, (c, c), 1)
        diff = (row - col).astype(F32)
        idx = lax.broadcasted_iota(jnp.int32, (c, RET_DK), 0).astype(F32)
        for h in range(RET_HEADS):
            lg = LOG_GAMMA[h]
            dec_ref[h, 0] = jnp.where(diff >= 0, jnp.exp(jnp.maximum(diff, 0.0) * lg), 0.0)
            dec_ref[h, 1] = jnp.exp((idx + 1.0) * lg)
            dec_ref[h, 2] = jnp.exp((c - 1.0 - idx) * lg)

    cos = cos_ref[...]
    sin = sin_ref[...]
    for h in range(RET_HEADS):
        q = _rotary(p_ref[0, :, OFF_RQ + h * RET_DK:OFF_RQ + (h + 1) * RET_DK], cos, sin)
        k = _rotary(p_ref[0, :, OFF_RK + h * RET_DK:OFF_RK + (h + 1) * RET_DK], cos, sin) * (RET_DK ** -0.5)
        v = p_ref[0, :, OFF_RV + h * RET_DV:OFF_RV + (h + 1) * RET_DV].astype(BF16)
        chunk_decay = math.exp(c * LOG_GAMMA[h])
        s = s_ref[h]
        scores = _dot_nt(q.astype(BF16), k.astype(BF16)) * dec_ref[h, 0]
        o = _dot(scores.astype(BF16), v) + _dot((q * dec_ref[h, 1]).astype(BF16), s.astype(BF16))
        s_ref[h] = chunk_decay * s + _dot_tn((k * dec_ref[h, 2]).astype(BF16), v)
        o = o * lax.rsqrt(jnp.mean(o * o, axis=-1, keepdims=True) + EPS)
        gate = p_ref[0, :, OFF_RG + h * RET_DV:OFF_RG + (h + 1) * RET_DV]
        o_ref[0, :, h * RET_DV:(h + 1) * RET_DV] = (o * _silu(gate)).astype(BF16)

    @pl.when(n == pl.num_programs(1) - 1)
    def _():
        so_ref[0] = s_ref[...]


def _retention_prompt(proj3, cos, sin):
    b, t, _ = proj3.shape
    nc = t // CHUNK
    return pl.pallas_call(
        _ret_kernel,
        grid=(b, nc),
        in_specs=[pl.BlockSpec((1, CHUNK, OFF_HQ), lambda i, n: (i, n, 0)),
                  pl.BlockSpec((CHUNK, RET_DK), lambda i, n: (n, 0)),
                  pl.BlockSpec((CHUNK, RET_DK), lambda i, n: (n, 0))],
        out_specs=[pl.BlockSpec((1, CHUNK, RET_V), lambda i, n: (i, n, 0)),
                   pl.BlockSpec((1, RET_HEADS, RET_DK, RET_DV), lambda i, n: (i, 0, 0, 0))],
        out_shape=[jax.ShapeDtypeStruct((b, t, RET_V), BF16),
                   jax.ShapeDtypeStruct((b, RET_HEADS, RET_DK, RET_DV), F32)],
        scratch_shapes=[pltpu.VMEM((RET_HEADS, RET_DK, RET_DV), F32),
                        pltpu.VMEM((RET_HEADS, 3, CHUNK, RET_DK), F32)],
        compiler_params=_params(("parallel", "arbitrary")),
        name="retention_prompt",
    )(proj3, cos, sin)


def _lower_bound(lbl):
    mx = jnp.max(lbl, axis=0, keepdims=True)
    e = jnp.exp(lbl - mx)
    return e[0:1, :] / jnp.sum(e, axis=0, keepdims=True)


def _hgrn_gates(z, lb):
    half = 0.5 * (1.0 - lb)
    ht = half * jnp.tanh(0.5 * z)
    return (lb + half) + ht, half - ht


def _cumsum_rows(tri, x):
    hi = x.astype(BF16)
    r1 = x - hi.astype(F32)
    mid = r1.astype(BF16)
    lo = (r1 - mid.astype(F32)).astype(BF16)
    return _dot(tri, hi) + _dot(tri, mid) + _dot(tri, lo)


def _bcast_rows(x, group, r):
    c, d = x.shape
    x3 = x.reshape(c // group, group, d)
    return jnp.broadcast_to(x3[:, r:r + 1, :], x3.shape).reshape(c, d)


def _hgrn_kernel(qf_ref, ig_ref, lbl_ref, nrm_ref, o_ref, so_ref, st_ref):
    n = pl.program_id(1)

    @pl.when(n == 0)
    def _():
        st_ref[...] = jnp.zeros_like(st_ref)

    c = CHUNK
    row = lax.broadcasted_iota(jnp.int32, (c, c), 0)
    col = lax.broadcasted_iota(jnp.int32, (c, c), 1)
    rowd = lax.broadcasted_iota(jnp.int32, (c, HG_DK), 0)
    tri = jnp.where(row >= col, 1.0, 0.0).astype(BF16)
    r2, r4 = rowd & 1, rowd & 3
    block_masks = [(row >> s) == (col >> s) for s in range(c.bit_length())]
    nrm = nrm_ref[...]

    def up(x, k):
        return pltpu.roll(x, k, 0)

    def dn(x, k):
        return pltpu.roll(x, c - k, 0)

    def head(h):
        off = h * HG_DK
        lb = _lower_bound(lbl_ref[:, pl.ds(off, HG_DK)])
        f, kin = _hgrn_gates(qf_ref[0, :, pl.ds(HG_K + off, HG_DK)], lb)
        qh = _silu(qf_ref[0, :, pl.ds(off, HG_DK)])
        v = ig_ref[0, :, pl.ds(off, HG_DV)].astype(BF16)
        b = _cumsum_rows(tri, jnp.log2(f))

        f_up = f * up(f, 1)
        f_dn = f * dn(f, 1)
        pre2 = jnp.where(r2 == 1, f_up, f)
        pre4 = jnp.where(r4 >= 1, f_up, f)
        pre4 = jnp.where(r4 >= 2, pre4 * up(pre4, 2), pre4)
        suf4 = jnp.where(r4 <= 2, f_dn, f)
        suf4 = jnp.where(r4 <= 1, suf4 * dn(suf4, 2), suf4)

        def level(s):
            if s == 0:
                return qh, kin
            if s == 1:
                return qh * f, kin
            if s == 2:
                return qh * pre2, kin * jnp.where(r4 == 0, dn(f, 1), 1.0)
            if s == 3:
                return qh * pre4, kin * jnp.where(r4 == 3, 1.0, dn(suf4, 1))
            half = 1 << (s - 1)
            ref = _bcast_rows(b, 2 * half, half - 1)
            upper = (rowd & (2 * half - 1)) >= half
            return jnp.where(upper, qh * jnp.exp2(b - ref), 0.0), jnp.where(upper, 0.0, kin * jnp.exp2(ref - b))

        a = jnp.zeros((c, c), F32)
        for s in reversed(range(len(block_masks))):
            q_s, k_s = level(s)
            a = jnp.where(block_masks[s], _dot_nt(q_s.astype(BF16), k_s.astype(BF16)), a)
        a = jnp.where(col <= row, a, 0.0)

        st = st_ref[h]
        b_last = b[c - 1:c, :]
        o = _dot(a.astype(BF16), v) + _dot_nt((qh * jnp.exp2(b)).astype(BF16), st.astype(BF16))
        st_ref[h] = st * jnp.exp2(b_last) + _dot_tn(v, (kin * jnp.exp2(b_last - b)).astype(BF16))
        o = o * lax.rsqrt(jnp.mean(o * o, axis=-1, keepdims=True) + EPS) * nrm
        o_ref[0, :, pl.ds(off, HG_DV)] = (o * _silu(ig_ref[0, :, pl.ds(HG_V + off, HG_DV)])).astype(BF16)

    for h in range(HG_HEADS):
        head(h)

    @pl.when(n == pl.num_programs(1) - 1)
    def _():
        for h in range(HG_HEADS):
            so_ref[0, h] = st_ref[h].T


def _hgrn_prompt(proj3, lbl, nrm):
    b, t, _ = proj3.shape
    nc = t // CHUNK
    return pl.pallas_call(
        _hgrn_kernel,
        grid=(b, nc),
        in_specs=[pl.BlockSpec((1, CHUNK, 2 * HG_K), lambda i, n: (i, n, OFF_HQ // (2 * HG_K))),
                  pl.BlockSpec((1, CHUNK, 2 * HG_V), lambda i, n: (i, n, OFF_HI // (2 * HG_V))),
                  pl.BlockSpec((DEPTH + 1, HG_K), lambda i, n: (0, 0)),
                  pl.BlockSpec((1, HG_DV), lambda i, n: (0, 0))],
        out_specs=[pl.BlockSpec((1, CHUNK, HG_V), lambda i, n: (i, n, 0)),
                   pl.BlockSpec((1, HG_HEADS, HG_DK, HG_DV), lambda i, n: (i, 0, 0, 0))],
        out_shape=[jax.ShapeDtypeStruct((b, t, HG_V), BF16),
                   jax.ShapeDtypeStruct((b, HG_HEADS, HG_DK, HG_DV), F32)],
        scratch_shapes=[pltpu.VMEM((HG_HEADS, HG_DV, HG_DK), F32)],
        compiler_params=_params(("parallel", "arbitrary")),
        name="hgrn_prompt",
    )(proj3, proj3, lbl, nrm)


def _sample_prep_kernel(p_ref, cos_ref, sin_ref, lbl_ref, qr_ref, kr_ref, qh_ref, f_ref, kin_ref):
    cos = cos_ref[...]
    sin = sin_ref[...]
    for h in range(RET_HEADS):
        sl = slice(h * RET_DK, (h + 1) * RET_DK)
        qr_ref[:, sl] = _rotary(p_ref[:, OFF_RQ + h * RET_DK:OFF_RQ + (h + 1) * RET_DK], cos, sin)
        kr_ref[:, sl] = _rotary(p_ref[:, OFF_RK + h * RET_DK:OFF_RK + (h + 1) * RET_DK], cos, sin) * (RET_DK ** -0.5)
    f, kin = _hgrn_gates(p_ref[:, OFF_HF:OFF_HF + HG_K], _lower_bound(lbl_ref[...]))
    f_ref[...] = f
    kin_ref[...] = kin
    qh_ref[...] = _silu(p_ref[:, OFF_HQ:OFF_HQ + HG_K])


def _sample_prep(proj, cos, sin, lbl):
    m = proj.shape[0]
    shp = jax.ShapeDtypeStruct((m, RET_QK), F32)
    return pl.pallas_call(
        _sample_prep_kernel,
        grid=(1,),
        in_specs=[pl.BlockSpec((m, OFF_GA), lambda i: (0, 0)),
                  pl.BlockSpec((1, RET_DK), lambda i: (0, 0)),
                  pl.BlockSpec((1, RET_DK), lambda i: (0, 0)),
                  pl.BlockSpec((DEPTH + 1, HG_K), lambda i: (0, 0))],
        out_specs=[pl.BlockSpec((m, RET_QK), lambda i: (0, 0))] * 5,
        out_shape=[shp] * 5,
        compiler_params=_params(("arbitrary",)),
        name="sample_prep",
    )(proj, cos, sin, lbl)


def _sample_state_kernel(sr_ref, sh_ref, qr_ref, kr_ref, qh_ref, f_ref, kin_ref, vr_ref, rg_ref, vh_ref, hg_ref,
                         nrm_ref, sro_ref, sho_ref, or_ref, oh_ref):
    lanes = RET_DK
    for t in range(SAMPLE_SEQS_PER_STEP):
        b = pl.program_id(0) * SAMPLE_SEQS_PER_STEP + t
        base = pl.multiple_of((b // SUB) * SUB, SUB)

        def row(ref, sl, b=b, base=base):
            tile = ref[pl.ds(base, SUB), sl]
            sub = lax.broadcasted_iota(jnp.int32, tile.shape, 0)
            return jnp.sum(jnp.where(sub == b % SUB, tile, 0.0), axis=0, keepdims=True)

        def column(ref, h, row=row):
            r = row(ref, slice(h * RET_DK, (h + 1) * RET_DK))
            return jnp.broadcast_to(r, (RET_DK, RET_DK)).T

        for h in range(RET_HEADS):
            kc = column(kr_ref, h)
            qc = column(qr_ref, h)
            gamma = math.exp(LOG_GAMMA[h])
            outs = []
            for half in range(RET_DV // lanes):
                hs = slice(half * lanes, (half + 1) * lanes)
                s = gamma * sr_ref[t, h, :, hs] + kc * row(vr_ref, slice(h * RET_DV + hs.start, h * RET_DV + hs.stop))
                sro_ref[t, h, :, hs] = s
                outs.append(jnp.sum(qc * s, axis=0, keepdims=True))
            o = jnp.concatenate(outs, axis=1)
            o = o * lax.rsqrt(jnp.mean(o * o, axis=-1, keepdims=True) + EPS)
            sl = slice(h * RET_DV, (h + 1) * RET_DV)
            or_ref[t, :, sl] = o * _silu(row(rg_ref, sl))
        for h in range(HG_HEADS):
            sl = slice(h * HG_DV, (h + 1) * HG_DV)
            s = column(f_ref, h) * sh_ref[t, h] + column(kin_ref, h) * row(vh_ref, sl)
            sho_ref[t, h] = s
            o = jnp.sum(column(qh_ref, h) * s, axis=0, keepdims=True)
            o = o * lax.rsqrt(jnp.mean(o * o, axis=-1, keepdims=True) + EPS) * nrm_ref[...]
            oh_ref[t, :, sl] = o * _silu(row(hg_ref, sl))


def _sample_state(s_ret, s_hg, rows, proj, nrm):
    b = s_ret.shape[0]
    nb = SAMPLE_SEQS_PER_STEP
    row_spec = pl.BlockSpec((b, RET_QK), lambda i: (0, 0))
    return pl.pallas_call(
        _sample_state_kernel,
        grid=(b // nb,),
        in_specs=[pl.BlockSpec((nb, RET_HEADS, RET_DK, RET_DV), lambda i: (i, 0, 0, 0)),
                  pl.BlockSpec((nb, HG_HEADS, HG_DK, HG_DV), lambda i: (i, 0, 0, 0)),
                  row_spec, row_spec, row_spec, row_spec, row_spec,
                  pl.BlockSpec((b, RET_V), lambda i: (0, OFF_RV // RET_V)),
                  pl.BlockSpec((b, RET_V), lambda i: (0, OFF_RG // RET_V)),
                  pl.BlockSpec((b, HG_V), lambda i: (0, OFF_HI // HG_V)),
                  pl.BlockSpec((b, HG_V), lambda i: (0, OFF_HG // HG_V)),
                  pl.BlockSpec((1, HG_DV), lambda i: (0, 0))],
        out_specs=[pl.BlockSpec((nb, RET_HEADS, RET_DK, RET_DV), lambda i: (i, 0, 0, 0)),
                   pl.BlockSpec((nb, HG_HEADS, HG_DK, HG_DV), lambda i: (i, 0, 0, 0)),
                   pl.BlockSpec((nb, 1, RET_V), lambda i: (i, 0, 0)),
                   pl.BlockSpec((nb, 1, HG_V), lambda i: (i, 0, 0))],
        out_shape=[jax.ShapeDtypeStruct(s_ret.shape, F32),
                   jax.ShapeDtypeStruct(s_hg.shape, F32),
                   jax.ShapeDtypeStruct((b, 1, RET_V), F32),
                   jax.ShapeDtypeStruct((b, 1, HG_V), F32)],
        compiler_params=_params(("arbitrary",)),
        name="sample_state",
    )(s_ret, s_hg, *rows, proj, proj, proj, proj, nrm)


def _rope_tables(pos):
    half = RET_DK // 2
    inv_freq = ROPE_BASE ** (-jnp.arange(half, dtype=F32) / half)
    ang = pos.astype(F32)[:, None] * inv_freq[None, :]
    cos, sin = jnp.cos(ang), jnp.sin(ang)
    return jnp.concatenate([cos, cos], axis=-1), jnp.concatenate([-sin, sin], axis=-1)


def kernel(x_prompt, x_sample, state_ret, state_hgrn, w_in, w_ret_out, w_hgrn_out, w_out, norm_mix, norm_ffn,
           hgrn_norm, hgrn_lb_logits, w_ffn_in, w_ffn_out, norm_final):
    assert w_in.shape[0] == DEPTH == 1
    bp, t, d = x_prompt.shape
    bs = x_sample.shape[0]
    nrm = hgrn_norm[0][None, :]
    lbl = hgrn_lb_logits.astype(F32)
    x = (x_prompt.reshape(bp * t, d), x_sample.reshape(bs, d))

    proj = _proj(x, norm_mix[0][None, :], w_in[0])

    proj_p3 = proj[0].reshape(bp, t, IN_TOTAL)
    cos_p, sin_p = _rope_tables(jnp.arange(t))
    o_r, s_ret_p = _retention_prompt(proj_p3, cos_p, sin_p)
    o_h, s_hg_p = _hgrn_prompt(proj_p3, lbl, nrm)

    cos_s, sin_s = _rope_tables(PAST_LEN + jnp.arange(1))
    rows = _sample_prep(proj[1], cos_s, sin_s, lbl)
    s_ret_s, s_hg_s, o_rs, o_hs = _sample_state(state_ret[0], state_hgrn[0], rows, proj[1], nrm)

    o_r = (o_r.reshape(bp * t, RET_V), o_rs.reshape(bs, RET_V).astype(BF16))
    o_h = (o_h.reshape(bp * t, HG_V), o_hs.reshape(bs, HG_V).astype(BF16))
    mrg = _merge(o_r, o_h, w_ret_out[0].astype(BF16), w_hgrn_out[0].astype(BF16), proj)
    x1, h2 = _resid(x, mrg, w_out[0].astype(BF16), norm_ffn[0][None, :])
    act = _ffn_in(h2, w_ffn_in[0])
    y_p, y_s = _ffn_out(act, w_ffn_out[0].astype(BF16), x1, norm_final[None, :])

    return (y_p.reshape(bp, t, d), y_s.reshape(bs, 1, d), s_ret_p[None], s_hg_p[None], s_ret_s[None], s_hg_s[None])
```

```python
import math
from typing import Any, Callable, NamedTuple

import numpy as np
import jax
import jax.numpy as jnp
from jax import lax
from jax.experimental import pallas as pl
from jax.experimental.pallas import tpu as pltpu

D_MODEL = 2048
DEPTH = 1
PAST_LEN = 16384
RET_HEADS = 8
RET_DK = 128
RET_DV = 256
RET_QK = RET_HEADS * RET_DK
RET_V = RET_HEADS * RET_DV
ROPE_BASE = 10000.0
HG_HEADS = 8
HG_DK = 128
HG_DV = 128
HG_K = HG_HEADS * HG_DK
HG_V = HG_HEADS * HG_DV
D_FF = 5632
EPS = 1e-6

OFF_RQ = 0
OFF_RK = OFF_RQ + RET_QK
OFF_RV = OFF_RK + RET_QK
OFF_RG = OFF_RV + RET_V
OFF_HQ = OFF_RG + RET_V
OFF_HF = OFF_HQ + HG_K
OFF_HI = OFF_HF + HG_K
OFF_HG = OFF_HI + HG_V
OFF_GA = OFF_HG + HG_V
OFF_GB = OFF_GA + D_MODEL
IN_TOTAL = OFF_GB + D_MODEL

CHUNK = 128
SUB = 8
SAMPLE_SEQS_PER_STEP = 4
V7X_VMEM_LIMIT = 56 * 1024 * 1024

F32 = jnp.float32
BF16 = jnp.bfloat16

LOG_GAMMA = [float(np.log(np.float32(1.0) - np.float32(2.0) ** np.float32(-5.0 - h))) for h in range(RET_HEADS)]


def _params(sem, vmem=V7X_VMEM_LIMIT):
    return pltpu.CompilerParams(dimension_semantics=sem, vmem_limit_bytes=vmem)


def _rms_rows(x, g):
    return (x * lax.rsqrt(jnp.mean(x * x, axis=-1, keepdims=True) + EPS)) * g


def _sigmoid(x):
    return 0.5 + 0.5 * jnp.tanh(0.5 * x)


def _silu(x):
    hx = 0.5 * x
    return hx + hx * jnp.tanh(hx)


def _dot(a, b):
    return jnp.dot(a, b, preferred_element_type=F32)


def _dot_nt(a, b):
    return lax.dot_general(a, b, (((1,), (1,)), ((), ())), preferred_element_type=F32)


def _dot_tn(a, b):
    return lax.dot_general(a, b, (((0,), (0,)), ((), ())), preferred_element_type=F32)


TILE_M_PROJ, TILE_N_PROJ = 2048, 512
TILE_M_MERGE = 512
TILE_M_RESID = 512
TILE_M_FFN_IN, TILE_N_FFN_IN = 2048, 512
TILE_M_FFN_OUT = 256
CAST_ROWS_PROJ = 32
CAST_ROWS_FFN = 128


class _Rows(NamedTuple):
    prompt: Any
    sample: Any
    cols: int
    col: Callable
    single: bool = False


class _Shared(NamedTuple):
    array: Any
    block: tuple
    index: Callable
    resident: bool = False


class _Out(NamedTuple):
    total_cols: int
    cols: int
    col: Callable
    dtype: Any


def _dense_call(body, name, tile_m, operands, outs, scratch, inner, casts=()):
    rows = [op for op in operands if isinstance(op, _Rows)]
    m_p, m_s = rows[0].prompt.shape[0], rows[0].sample.shape[0]
    last = m_p // tile_m - 1

    def frozen(i, j):
        return jnp.where(i == last, j, 0)

    specs_p, specs_s, args_p, args_s = [], [], [], []
    for op in operands:
        if isinstance(op, _Rows):
            mode = dict(pipeline_mode=pl.Buffered(1)) if op.single else {}
            specs_p.append(pl.BlockSpec((tile_m, op.cols), lambda i, j, op=op: (i, op.col(j)), **mode))
            specs_s.append(pl.BlockSpec((m_s, op.cols), lambda i, j, op=op: (0, op.col(frozen(i, j)))))
            args_p.append(op.prompt)
            args_s.append(op.sample)
        else:
            mode = dict(pipeline_mode=pl.Buffered(1)) if op.resident else {}
            specs_p.append(pl.BlockSpec(op.block, lambda i, j, op=op: op.index(j), **mode))
            args_p.append(op.array)
    out_specs = ([pl.BlockSpec((tile_m, o.cols), lambda i, j, o=o: (i, o.col(j))) for o in outs]
                 + [pl.BlockSpec((m_s, o.cols), lambda i, j, o=o: (0, o.col(frozen(i, j)))) for o in outs])
    out_shape = ([jax.ShapeDtypeStruct((m_p, o.total_cols), o.dtype) for o in outs]
                 + [jax.ShapeDtypeStruct((m_s, o.total_cols), o.dtype) for o in outs])
    scratch_shapes = ([pltpu.VMEM((tile_m, c), dt) for c, dt in scratch]
                      + [pltpu.VMEM((m_s, c), dt) for c, dt in scratch])
    n_in, n_out, n_scr = len(operands), len(outs), len(scratch)
    row_pos = [k for k, op in enumerate(operands) if isinstance(op, _Rows)]
    n_rows, n_cast = len(row_pos), len(casts)
    cast_blocks = [w.shape[0] // rb for w, rb in casts]
    assert all(nb <= (m_p // tile_m) * inner for nb in cast_blocks)
    cast_specs = [pl.BlockSpec((rb, w.shape[1]), lambda i, j, nb=nb: (jnp.minimum(i * inner + j, nb - 1), 0))
                  for (w, rb), nb in zip(casts, cast_blocks)]

    def kernel(*refs):
        ins_p = list(refs[:n_in])
        ins_s = list(ins_p)
        for k, ref in zip(row_pos, refs[n_in:n_in + len(row_pos)]):
            ins_s[k] = ref
        r = n_in + n_rows
        cast_in = refs[r:r + n_cast]
        r += n_cast
        outs_p, outs_s = refs[r:r + n_out], refs[r + n_out:r + 2 * n_out]
        r += 2 * n_out
        cast_out = refs[r:r + n_cast]
        r += n_cast
        body(*ins_p, *outs_p, *refs[r:r + n_scr])

        @pl.when(pl.program_id(0) == last)
        def _():
            body(*ins_s, *outs_s, *refs[r + n_scr:])

        step = pl.program_id(0) * inner + pl.program_id(1)
        for src, dst, nb in zip(cast_in, cast_out, cast_blocks):
            @pl.when(step < nb)
            def _(src=src, dst=dst):
                dst[...] = src[...].astype(BF16)

    res = pl.pallas_call(
        kernel,
        grid=(m_p // tile_m, inner),
        in_specs=specs_p + specs_s + cast_specs,
        out_specs=out_specs + cast_specs,
        out_shape=out_shape + [jax.ShapeDtypeStruct(w.shape, BF16) for w, _ in casts],
        scratch_shapes=scratch_shapes,
        compiler_params=_params(("arbitrary", "arbitrary")),
        name=name,
    )(*args_p, *args_s, *[w for w, _ in casts])
    return list(zip(res[:n_out], res[n_out:2 * n_out])) + list(res[2 * n_out:])


def _proj_body(x_ref, g_ref, w_ref, o_ref, h_ref):
    @pl.when(pl.program_id(1) == 0)
    def _():
        h_ref[...] = _rms_rows(x_ref[...], g_ref[...]).astype(BF16)
    o_ref[...] = _dot(h_ref[...], w_ref[...].astype(BF16))


def _proj(x, g, w, later_weights):
    d, n = w.shape
    tn = TILE_N_PROJ
    return _dense_call(
        _proj_body, "proj", TILE_M_PROJ,
        [_Rows(*x, d, lambda j: 0, single=True),
         _Shared(g, (1, d), lambda j: (0, 0)),
         _Shared(w, (d, tn), lambda j: (0, j))],
        [_Out(n, tn, lambda j: j, F32)], [(d, BF16)], n // tn,
        casts=[(lw, CAST_ROWS_PROJ) for lw in later_weights])


def _merge_body(or_ref, oh_ref, wr_ref, wh_ref, ga_ref, gb_ref, o_ref):
    ya = _dot(or_ref[...], wr_ref[...])
    yb = _dot(oh_ref[...], wh_ref[...])
    o_ref[...] = (_sigmoid(ga_ref[...]) * ya + _sigmoid(gb_ref[...]) * yb).astype(BF16)


def _merge(o_r, o_h, w_r, w_h, proj):
    n = w_r.shape[1]
    return _dense_call(
        _merge_body, "merge", TILE_M_MERGE,
        [_Rows(*o_r, RET_V, lambda j: 0),
         _Rows(*o_h, HG_V, lambda j: 0),
         _Shared(w_r, (RET_V, n), lambda j: (0, 0), resident=True),
         _Shared(w_h, (HG_V, n), lambda j: (0, 0), resident=True),
         _Rows(*proj, n, lambda j: OFF_GA // n),
         _Rows(*proj, n, lambda j: OFF_GB // n)],
        [_Out(n, n, lambda j: 0, BF16)], [], 1)[0]


def _resid_body(x_ref, m_ref, w_ref, g_ref, o_ref, h_ref):
    x1 = x_ref[...] + _dot(m_ref[...], w_ref[...])
    o_ref[...] = x1
    h_ref[...] = _rms_rows(x1, g_ref[...]).astype(BF16)


def _resid(x, mrg, w, g):
    d = w.shape[1]
    return _dense_call(
        _resid_body, "resid", TILE_M_RESID,
        [_Rows(*x, d, lambda j: 0),
         _Rows(*mrg, d, lambda j: 0),
         _Shared(w, (d, d), lambda j: (0, 0), resident=True),
         _Shared(g, (1, d), lambda j: (0, 0))],
        [_Out(d, d, lambda j: 0, F32), _Out(d, d, lambda j: 0, BF16)], [], 1)


def _ffn_in_body(h_ref, wg_ref, wu_ref, o_ref):
    h = h_ref[...]
    gate = _dot(h, wg_ref[...].astype(BF16))
    up = _dot(h, wu_ref[...].astype(BF16))
    o_ref[...] = (_silu(gate) * up).astype(BF16)


def _ffn_in(h, w, w_down):
    d = w.shape[0]
    tn = TILE_N_FFN_IN
    return _dense_call(
        _ffn_in_body, "ffn_in", TILE_M_FFN_IN,
        [_Rows(*h, d, lambda j: 0),
         _Shared(w, (d, tn), lambda j: (0, j)),
         _Shared(w, (d, tn), lambda j: (0, D_FF // tn + j))],
        [_Out(D_FF, tn, lambda j: j, BF16)], [], D_FF // tn, casts=[(w_down, CAST_ROWS_FFN)])


def _ffn_out_body(a_ref, w_ref, x_ref, g_ref, o_ref):
    o_ref[...] = _rms_rows(x_ref[...] + _dot(a_ref[...], w_ref[...]), g_ref[...])


def _ffn_out(act, w, x, g):
    k, d = w.shape
    return _dense_call(
        _ffn_out_body, "ffn_out", TILE_M_FFN_OUT,
        [_Rows(*act, k, lambda j: 0),
         _Shared(w, (k, d), lambda j: (0, 0), resident=True),
         _Rows(*x, d, lambda j: 0),
         _Shared(g, (1, d), lambda j: (0, 0))],
        [_Out(d, d, lambda j: 0, F32)], [], 1)[0]


def _rotary(x, cos, sin_signed):
    return x * cos + pltpu.roll(x, RET_DK // 2, 1) * sin_signed


def _ret_kernel(p_ref, cos_ref, sin_ref, o_ref, so_ref, s_ref, dec_ref):
    n = pl.program_id(1)
    c = CHUNK

    @pl.when(n == 0)
    def _():
        s_ref[...] = jnp.zeros_like(s_ref)
        row = lax.broadcasted_iota(jnp.int32, (c, c), 0)
        col = lax.broadcasted_iota(jnp.int32, (c, c), 1)
        diff = (row - col).astype(F32)
        idx = lax.broadcasted_iota(jnp.int32, (c, RET_DK), 0).astype(F32)
        for h in range(RET_HEADS):
            lg = LOG_GAMMA[h]
            dec_ref[h, 0] = jnp.where(diff >= 0, jnp.exp(jnp.maximum(diff, 0.0) * lg), 0.0)
            dec_ref[h, 1] = jnp.exp((idx + 1.0) * lg)
            dec_ref[h, 2] = jnp.exp((c - 1.0 - idx) * lg)

    cos = cos_ref[...]
    sin = sin_ref[...]
    for h in range(RET_HEADS):
        q = _rotary(p_ref[0, :, OFF_RQ + h * RET_DK:OFF_RQ + (h + 1) * RET_DK], cos, sin)
        k = _rotary(p_ref[0, :, OFF_RK + h * RET_DK:OFF_RK + (h + 1) * RET_DK], cos, sin) * (RET_DK ** -0.5)
        v = p_ref[0, :, OFF_RV + h * RET_DV:OFF_RV + (h + 1) * RET_DV].astype(BF16)
        chunk_decay = math.exp(c * LOG_GAMMA[h])
        s = s_ref[h]
        scores = _dot_nt(q.astype(BF16), k.astype(BF16)) * dec_ref[h, 0]
        o = _dot(scores.astype(BF16), v) + _dot((q * dec_ref[h, 1]).astype(BF16), s.astype(BF16))
        s_ref[h] = chunk_decay * s + _dot_tn((k * dec_ref[h, 2]).astype(BF16), v)
        o = o * lax.rsqrt(jnp.mean(o * o, axis=-1, keepdims=True) + EPS)
        gate = p_ref[0, :, OFF_RG + h * RET_DV:OFF_RG + (h + 1) * RET_DV]
        o_ref[0, :, h * RET_DV:(h + 1) * RET_DV] = (o * _silu(gate)).astype(BF16)

    @pl.when(n == pl.num_programs(1) - 1)
    def _():
        so_ref[0] = s_ref[...]


def _retention_prompt(proj3, cos, sin):
    b, t, _ = proj3.shape
    nc = t // CHUNK
    return pl.pallas_call(
        _ret_kernel,
        grid=(b, nc),
        in_specs=[pl.BlockSpec((1, CHUNK, OFF_HQ), lambda i, n: (i, n, 0)),
                  pl.BlockSpec((CHUNK, RET_DK), lambda i, n: (n, 0)),
                  pl.BlockSpec((CHUNK, RET_DK), lambda i, n: (n, 0))],
        out_specs=[pl.BlockSpec((1, CHUNK, RET_V), lambda i, n: (i, n, 0)),
                   pl.BlockSpec((1, RET_HEADS, RET_DK, RET_DV), lambda i, n: (i, 0, 0, 0))],
        out_shape=[jax.ShapeDtypeStruct((b, t, RET_V), BF16),
                   jax.ShapeDtypeStruct((b, RET_HEADS, RET_DK, RET_DV), F32)],
        scratch_shapes=[pltpu.VMEM((RET_HEADS, RET_DK, RET_DV), F32),
                        pltpu.VMEM((RET_HEADS, 3, CHUNK, RET_DK), F32)],
        compiler_params=_params(("parallel", "arbitrary")),
        name="retention_prompt",
    )(proj3, cos, sin)


def _lower_bound(lbl):
    mx = jnp.max(lbl, axis=0, keepdims=True)
    e = jnp.exp(lbl - mx)
    return e[0:1, :] / jnp.sum(e, axis=0, keepdims=True)


def _hgrn_gates(z, lb):
    half = 0.5 * (1.0 - lb)
    ht = half * jnp.tanh(0.5 * z)
    return (lb + half) + ht, half - ht


def _cumsum_rows(tri, x):
    hi = x.astype(BF16)
    r1 = x - hi.astype(F32)
    mid = r1.astype(BF16)
    lo = (r1 - mid.astype(F32)).astype(BF16)
    return _dot(tri, hi) + _dot(tri, mid) + _dot(tri, lo)


def _bcast_rows(x, group, r):
    c, d = x.shape
    x3 = x.reshape(c // group, group, d)
    return jnp.broadcast_to(x3[:, r:r + 1, :], x3.shape).reshape(c, d)


def _hgrn_kernel(qf_ref, ig_ref, lbl_ref, nrm_ref, o_ref, so_ref, st_ref):
    n = pl.program_id(1)

    @pl.when(n == 0)
    def _():
        st_ref[...] = jnp.zeros_like(st_ref)

    c = CHUNK
    row = lax.broadcasted_iota(jnp.int32, (c, c), 0)
    col = lax.broadcasted_iota(jnp.int32, (c, c), 1)
    rowd = lax.broadcasted_iota(jnp.int32, (c, HG_DK), 0)
    tri = jnp.where(row >= col, 1.0, 0.0).astype(BF16)
    r2, r4 = rowd & 1, rowd & 3
    block_masks = [(row >> s) == (col >> s) for s in range(c.bit_length())]
    nrm = nrm_ref[...]

    def up(x, k):
        return pltpu.roll(x, k, 0)

    def dn(x, k):
        return pltpu.roll(x, c - k, 0)

    def head(h):
        off = h * HG_DK
        lb = _lower_bound(lbl_ref[:, pl.ds(off, HG_DK)])
        f, kin = _hgrn_gates(qf_ref[0, :, pl.ds(HG_K + off, HG_DK)], lb)
        qh = _silu(qf_ref[0, :, pl.ds(off, HG_DK)])
        v = ig_ref[0, :, pl.ds(off, HG_DV)].astype(BF16)
        b = _cumsum_rows(tri, jnp.log2(f))

        f_up = f * up(f, 1)
        f_dn = f * dn(f, 1)
        pre2 = jnp.where(r2 == 1, f_up, f)
        pre4 = jnp.where(r4 >= 1, f_up, f)
        pre4 = jnp.where(r4 >= 2, pre4 * up(pre4, 2), pre4)
        suf4 = jnp.where(r4 <= 2, f_dn, f)
        suf4 = jnp.where(r4 <= 1, suf4 * dn(suf4, 2), suf4)

        def level(s):
            if s == 0:
                return qh, kin
            if s == 1:
                return qh * f, kin
            if s == 2:
                return qh * pre2, kin * jnp.where(r4 == 0, dn(f, 1), 1.0)
            if s == 3:
                return qh * pre4, kin * jnp.where(r4 == 3, 1.0, dn(suf4, 1))
            half = 1 << (s - 1)
            ref = _bcast_rows(b, 2 * half, half - 1)
            upper = (rowd & (2 * half - 1)) >= half
            return jnp.where(upper, qh * jnp.exp2(b - ref), 0.0), jnp.where(upper, 0.0, kin * jnp.exp2(ref - b))

        a = jnp.zeros((c, c), F32)
        for s in reversed(range(len(block_masks))):
            q_s, k_s = level(s)
            a = jnp.where(block_masks[s], _dot_nt(q_s.astype(BF16), k_s.astype(BF16)), a)
        a = jnp.where(col <= row, a, 0.0)

        st = st_ref[h]
        b_last = b[c - 1:c, :]
        o = _dot(a.astype(BF16), v) + _dot_nt((qh * jnp.exp2(b)).astype(BF16), st.astype(BF16))
        st_ref[h] = st * jnp.exp2(b_last) + _dot_tn(v, (kin * jnp.exp2(b_last - b)).astype(BF16))
        o = o * lax.rsqrt(jnp.mean(o * o, axis=-1, keepdims=True) + EPS) * nrm
        o_ref[0, :, pl.ds(off, HG_DV)] = (o * _silu(ig_ref[0, :, pl.ds(HG_V + off, HG_DV)])).astype(BF16)

    for h in range(HG_HEADS):
        head(h)

    @pl.when(n == pl.num_programs(1) - 1)
    def _():
        for h in range(HG_HEADS):
            so_ref[0, h] = st_ref[h].T


def _hgrn_prompt(proj3, lbl, nrm):
    b, t, _ = proj3.shape
    nc = t // CHUNK
    return pl.pallas_call(
        _hgrn_kernel,
        grid=(b, nc),
        in_specs=[pl.BlockSpec((1, CHUNK, 2 * HG_K), lambda i, n: (i, n, OFF_HQ // (2 * HG_K))),
                  pl.BlockSpec((1, CHUNK, 2 * HG_V), lambda i, n: (i, n, OFF_HI // (2 * HG_V))),
                  pl.BlockSpec((DEPTH + 1, HG_K), lambda i, n: (0, 0)),
                  pl.BlockSpec((1, HG_DV), lambda i, n: (0, 0))],
        out_specs=[pl.BlockSpec((1, CHUNK, HG_V), lambda i, n: (i, n, 0)),
                   pl.BlockSpec((1, HG_HEADS, HG_DK, HG_DV), lambda i, n: (i, 0, 0, 0))],
        out_shape=[jax.ShapeDtypeStruct((b, t, HG_V), BF16),
                   jax.ShapeDtypeStruct((b, HG_HEADS, HG_DK, HG_DV), F32)],
        scratch_shapes=[pltpu.VMEM((HG_HEADS, HG_DV, HG_DK), F32)],
        compiler_params=_params(("parallel", "arbitrary")),
        name="hgrn_prompt",
    )(proj3, proj3, lbl, nrm)


def _sample_prep_kernel(p_ref, cos_ref, sin_ref, lbl_ref, qr_ref, kr_ref, qh_ref, f_ref, kin_ref):
    cos = cos_ref[...]
    sin = sin_ref[...]
    for h in range(RET_HEADS):
        sl = slice(h * RET_DK, (h + 1) * RET_DK)
        qr_ref[:, sl] = _rotary(p_ref[:, OFF_RQ + h * RET_DK:OFF_RQ + (h + 1) * RET_DK], cos, sin)
        kr_ref[:, sl] = _rotary(p_ref[:, OFF_RK + h * RET_DK:OFF_RK + (h + 1) * RET_DK], cos, sin) * (RET_DK ** -0.5)
    f, kin = _hgrn_gates(p_ref[:, OFF_HF:OFF_HF + HG_K], _lower_bound(lbl_ref[...]))
    f_ref[...] = f
    kin_ref[...] = kin
    qh_ref[...] = _silu(p_ref[:, OFF_HQ:OFF_HQ + HG_K])


def _sample_prep(proj, cos, sin, lbl):
    m = proj.shape[0]
    shp = jax.ShapeDtypeStruct((m, RET_QK), F32)
    return pl.pallas_call(
        _sample_prep_kernel,
        grid=(1,),
        in_specs=[pl.BlockSpec((m, OFF_GA), lambda i: (0, 0)),
                  pl.BlockSpec((1, RET_DK), lambda i: (0, 0)),
                  pl.BlockSpec((1, RET_DK), lambda i: (0, 0)),
                  pl.BlockSpec((DEPTH + 1, HG_K), lambda i: (0, 0))],
        out_specs=[pl.BlockSpec((m, RET_QK), lambda i: (0, 0))] * 5,
        out_shape=[shp] * 5,
        compiler_params=_params(("arbitrary",)),
        name="sample_prep",
    )(proj, cos, sin, lbl)


def _sample_state_kernel(sr_ref, sh_ref, qr_ref, kr_ref, qh_ref, f_ref, kin_ref, vr_ref, rg_ref, vh_ref, hg_ref,
                         nrm_ref, sro_ref, sho_ref, or_ref, oh_ref):
    lanes = RET_DK
    for t in range(SAMPLE_SEQS_PER_STEP):
        b = pl.program_id(0) * SAMPLE_SEQS_PER_STEP + t
        base = pl.multiple_of((b // SUB) * SUB, SUB)

        def row(ref, sl, b=b, base=base):
            tile = ref[pl.ds(base, SUB), sl]
            sub = lax.broadcasted_iota(jnp.int32, tile.shape, 0)
            return jnp.sum(jnp.where(sub == b % SUB, tile, 0.0), axis=0, keepdims=True)

        def column(ref, h, row=row):
            r = row(ref, slice(h * RET_DK, (h + 1) * RET_DK))
            return jnp.broadcast_to(r, (RET_DK, RET_DK)).T

        for h in range(RET_HEADS):
            kc = column(kr_ref, h)
            qc = column(qr_ref, h)
            gamma = math.exp(LOG_GAMMA[h])
            outs = []
            for half in range(RET_DV // lanes):
                hs = slice(half * lanes, (half + 1) * lanes)
                s = gamma * sr_ref[t, h, :, hs] + kc * row(vr_ref, slice(h * RET_DV + hs.start, h * RET_DV + hs.stop))
                sro_ref[t, h, :, hs] = s
                outs.append(jnp.sum(qc * s, axis=0, keepdims=True))
            o = jnp.concatenate(outs, axis=1)
            o = o * lax.rsqrt(jnp.mean(o * o, axis=-1, keepdims=True) + EPS)
            sl = slice(h * RET_DV, (h + 1) * RET_DV)
            or_ref[t, :, sl] = o * _silu(row(rg_ref, sl))
        for h in range(HG_HEADS):
            sl = slice(h * HG_DV, (h + 1) * HG_DV)
            s = column(f_ref, h) * sh_ref[t, h] + column(kin_ref, h) * row(vh_ref, sl)
            sho_ref[t, h] = s
            o = jnp.sum(column(qh_ref, h) * s, axis=0, keepdims=True)
            o = o * lax.rsqrt(jnp.mean(o * o, axis=-1, keepdims=True) + EPS) * nrm_ref[...]
            oh_ref[t, :, sl] = o * _silu(row(hg_ref, sl))


def _sample_state(s_ret, s_hg, rows, proj, nrm):
    b = s_ret.shape[0]
    nb = SAMPLE_SEQS_PER_STEP
    row_spec = pl.BlockSpec((b, RET_QK), lambda i: (0, 0))
    return pl.pallas_call(
        _sample_state_kernel,
        grid=(b // nb,),
        in_specs=[pl.BlockSpec((nb, RET_HEADS, RET_DK, RET_DV), lambda i: (i, 0, 0, 0)),
                  pl.BlockSpec((nb, HG_HEADS, HG_DK, HG_DV), lambda i: (i, 0, 0, 0)),
                  row_spec, row_spec, row_spec, row_spec, row_spec,
                  pl.BlockSpec((b, RET_V), lambda i: (0, OFF_RV // RET_V)),
                  pl.BlockSpec((b, RET_V), lambda i: (0, OFF_RG // RET_V)),
                  pl.BlockSpec((b, HG_V), lambda i: (0, OFF_HI // HG_V)),
                  pl.BlockSpec((b, HG_V), lambda i: (0, OFF_HG // HG_V)),
                  pl.BlockSpec((1, HG_DV), lambda i: (0, 0))],
        out_specs=[pl.BlockSpec((nb, RET_HEADS, RET_DK, RET_DV), lambda i: (i, 0, 0, 0)),
                   pl.BlockSpec((nb, HG_HEADS, HG_DK, HG_DV), lambda i: (i, 0, 0, 0)),
                   pl.BlockSpec((nb, 1, RET_V), lambda i: (i, 0, 0)),
                   pl.BlockSpec((nb, 1, HG_V), lambda i: (i, 0, 0))],
        out_shape=[jax.ShapeDtypeStruct(s_ret.shape, F32),
                   jax.ShapeDtypeStruct(s_hg.shape, F32),
                   jax.ShapeDtypeStruct((b, 1, RET_V), F32),
                   jax.ShapeDtypeStruct((b, 1, HG_V), F32)],
        compiler_params=_params(("arbitrary",)),
        name="sample_state",
    )(s_ret, s_hg, *rows, proj, proj, proj, proj, nrm)


def _rope_tables(pos):
    half = RET_DK // 2
    inv_freq = ROPE_BASE ** (-jnp.arange(half, dtype=F32) / half)
    ang = pos.astype(F32)[:, None] * inv_freq[None, :]
    cos, sin = jnp.cos(ang), jnp.sin(ang)
    return jnp.concatenate([cos, cos], axis=-1), jnp.concatenate([-sin, sin], axis=-1)


def kernel(x_prompt, x_sample, state_ret, state_hgrn, w_in, w_ret_out, w_hgrn_out, w_out, norm_mix, norm_ffn,
           hgrn_norm, hgrn_lb_logits, w_ffn_in, w_ffn_out, norm_final):
    assert w_in.shape[0] == DEPTH == 1
    bp, t, d = x_prompt.shape
    bs = x_sample.shape[0]
    nrm = hgrn_norm[0][None, :]
    lbl = hgrn_lb_logits.astype(F32)
    x = (x_prompt.reshape(bp * t, d), x_sample.reshape(bs, d))

    proj, wb_ret, wb_hgrn, wb_out = _proj(x, norm_mix[0][None, :], w_in[0], [w_ret_out[0], w_hgrn_out[0], w_out[0]])

    proj_p3 = proj[0].reshape(bp, t, IN_TOTAL)
    cos_p, sin_p = _rope_tables(jnp.arange(t))
    o_r, s_ret_p = _retention_prompt(proj_p3, cos_p, sin_p)
    o_h, s_hg_p = _hgrn_prompt(proj_p3, lbl, nrm)

    cos_s, sin_s = _rope_tables(PAST_LEN + jnp.arange(1))
    rows = _sample_prep(proj[1], cos_s, sin_s, lbl)
    s_ret_s, s_hg_s, o_rs, o_hs = _sample_state(state_ret[0], state_hgrn[0], rows, proj[1], nrm)

    o_r = (o_r.reshape(bp * t, RET_V), o_rs.reshape(bs, RET_V).astype(BF16))
    o_h = (o_h.reshape(bp * t, HG_V), o_hs.reshape(bs, HG_V).astype(BF16))
    mrg = _merge(o_r, o_h, wb_ret, wb_hgrn, proj)
    x1, h2 = _resid(x, mrg, wb_out, norm_ffn[0][None, :])
    act, wb_down = _ffn_in(h2, w_ffn_in[0], w_ffn_out[0])
    y_p, y_s = _ffn_out(act, wb_down, x1, norm_final[None, :])

    return (y_p.reshape(bp, t, d), y_s.reshape(bs, 1, d), s_ret_p[None], s_hg_p[None], s_ret_s[None], s_hg_s[None])
```

```python
import functools
import math
from typing import Any, Callable, NamedTuple

import numpy as np
import jax
import jax.numpy as jnp
from jax import lax
from jax.experimental import pallas as pl
from jax.experimental.pallas import tpu as pltpu

D_MODEL = 2048
DEPTH = 1
PAST_LEN = 16384
RET_HEADS = 8
RET_DK = 128
RET_DV = 256
RET_QK = RET_HEADS * RET_DK
RET_V = RET_HEADS * RET_DV
ROPE_BASE = 10000.0
HG_HEADS = 8
HG_DK = 128
HG_DV = 128
HG_K = HG_HEADS * HG_DK
HG_V = HG_HEADS * HG_DV
D_FF = 5632
EPS = 1e-6

OFF_RQ = 0
OFF_RK = OFF_RQ + RET_QK
OFF_RV = OFF_RK + RET_QK
OFF_RG = OFF_RV + RET_V
OFF_HQ = OFF_RG + RET_V
OFF_HF = OFF_HQ + HG_K
OFF_HI = OFF_HF + HG_K
OFF_HG = OFF_HI + HG_V
OFF_GA = OFF_HG + HG_V
OFF_GB = OFF_GA + D_MODEL
IN_TOTAL = OFF_GB + D_MODEL

CHUNK = 128
SUB = 8
SAMPLE_SEQS_PER_STEP = 4
V7X_VMEM_LIMIT = 56 * 1024 * 1024

F32 = jnp.float32
BF16 = jnp.bfloat16

LOG_GAMMA = [float(np.log(np.float32(1.0) - np.float32(2.0) ** np.float32(-5.0 - h))) for h in range(RET_HEADS)]


def _params(sem, vmem=V7X_VMEM_LIMIT):
    return pltpu.CompilerParams(dimension_semantics=sem, vmem_limit_bytes=vmem)


def _rms_rows(x, g):
    return (x * lax.rsqrt(jnp.mean(x * x, axis=-1, keepdims=True) + EPS)) * g


def _sigmoid(x):
    return 0.5 + 0.5 * jnp.tanh(0.5 * x)


def _silu(x):
    hx = 0.5 * x
    return hx + hx * jnp.tanh(hx)


def _dot(a, b):
    return jnp.dot(a, b, preferred_element_type=F32)


def _dot_nt(a, b):
    return lax.dot_general(a, b, (((1,), (1,)), ((), ())), preferred_element_type=F32)


def _dot_tn(a, b):
    return lax.dot_general(a, b, (((0,), (0,)), ((), ())), preferred_element_type=F32)


TILE_M_PROJ, TILE_N_PROJ = 2048, 512
TILE_M_MERGE = 512
TILE_M_RESID = 512
TILE_M_FFN_IN, TILE_N_FFN_IN = 2048, 512
TILE_M_FFN_OUT = 256
FFN_IN_ROW_CHUNK = 512
CAST_ROWS_PROJ = 32
CAST_ROWS_FFN = 128


class _Rows(NamedTuple):
    prompt: Any
    sample: Any
    cols: int
    col: Callable
    single: bool = False


class _Shared(NamedTuple):
    array: Any
    block: tuple
    index: Callable
    resident: bool = False


class _Out(NamedTuple):
    total_cols: int
    cols: int
    col: Callable
    dtype: Any


def _dense_call(body, name, tile_m, operands, outs, scratch, inner, casts=()):
    rows = [op for op in operands if isinstance(op, _Rows)]
    m_p, m_s = rows[0].prompt.shape[0], rows[0].sample.shape[0]
    last = m_p // tile_m - 1

    def frozen(i, j):
        return jnp.where(i == last, j, 0)

    specs_p, specs_s, args_p, args_s = [], [], [], []
    for op in operands:
        if isinstance(op, _Rows):
            mode = dict(pipeline_mode=pl.Buffered(1)) if op.single else {}
            specs_p.append(pl.BlockSpec((tile_m, op.cols), lambda i, j, op=op: (i, op.col(j)), **mode))
            specs_s.append(pl.BlockSpec((m_s, op.cols), lambda i, j, op=op: (0, op.col(frozen(i, j)))))
            args_p.append(op.prompt)
            args_s.append(op.sample)
        else:
            mode = dict(pipeline_mode=pl.Buffered(1)) if op.resident else {}
            specs_p.append(pl.BlockSpec(op.block, lambda i, j, op=op: op.index(j), **mode))
            args_p.append(op.array)
    out_specs = ([pl.BlockSpec((tile_m, o.cols), lambda i, j, o=o: (i, o.col(j))) for o in outs]
                 + [pl.BlockSpec((m_s, o.cols), lambda i, j, o=o: (0, o.col(frozen(i, j)))) for o in outs])
    out_shape = ([jax.ShapeDtypeStruct((m_p, o.total_cols), o.dtype) for o in outs]
                 + [jax.ShapeDtypeStruct((m_s, o.total_cols), o.dtype) for o in outs])
    scratch_shapes = ([pltpu.VMEM((tile_m, c), dt) for c, dt in scratch]
                      + [pltpu.VMEM((m_s, c), dt) for c, dt in scratch])
    n_in, n_out, n_scr = len(operands), len(outs), len(scratch)
    row_pos = [k for k, op in enumerate(operands) if isinstance(op, _Rows)]
    n_rows, n_cast = len(row_pos), len(casts)
    cast_blocks = [w.shape[0] // rb for w, rb in casts]
    assert all(nb <= (m_p // tile_m) * inner for nb in cast_blocks)
    cast_specs = [pl.BlockSpec((rb, w.shape[1]), lambda i, j, nb=nb: (jnp.minimum(i * inner + j, nb - 1), 0))
                  for (w, rb), nb in zip(casts, cast_blocks)]

    def kernel(*refs):
        ins_p = list(refs[:n_in])
        ins_s = list(ins_p)
        for k, ref in zip(row_pos, refs[n_in:n_in + len(row_pos)]):
            ins_s[k] = ref
        r = n_in + n_rows
        cast_in = refs[r:r + n_cast]
        r += n_cast
        outs_p, outs_s = refs[r:r + n_out], refs[r + n_out:r + 2 * n_out]
        r += 2 * n_out
        cast_out = refs[r:r + n_cast]
        r += n_cast
        body(*ins_p, *outs_p, *refs[r:r + n_scr])

        @pl.when(pl.program_id(0) == last)
        def _():
            body(*ins_s, *outs_s, *refs[r + n_scr:])

        step = pl.program_id(0) * inner + pl.program_id(1)
        for src, dst, nb in zip(cast_in, cast_out, cast_blocks):
            @pl.when(step < nb)
            def _(src=src, dst=dst):
                dst[...] = src[...].astype(BF16)

    res = pl.pallas_call(
        kernel,
        grid=(m_p // tile_m, inner),
        in_specs=specs_p + specs_s + cast_specs,
        out_specs=out_specs + cast_specs,
        out_shape=out_shape + [jax.ShapeDtypeStruct(w.shape, BF16) for w, _ in casts],
        scratch_shapes=scratch_shapes,
        compiler_params=_params(("arbitrary", "arbitrary")),
        name=name,
    )(*args_p, *args_s, *[w for w, _ in casts])
    return list(zip(res[:n_out], res[n_out:2 * n_out])) + list(res[2 * n_out:])


def _proj_body(x_ref, g_ref, w_ref, o_ref, h_ref):
    @pl.when(pl.program_id(1) == 0)
    def _():
        h_ref[...] = _rms_rows(x_ref[...], g_ref[...]).astype(BF16)
    o_ref[...] = _dot(h_ref[...], w_ref[...].astype(BF16))


def _proj(x, g, w, later_weights):
    d, n = w.shape
    tn = TILE_N_PROJ
    return _dense_call(
        _proj_body, "proj", TILE_M_PROJ,
        [_Rows(*x, d, lambda j: 0, single=True),
         _Shared(g, (1, d), lambda j: (0, 0)),
         _Shared(w, (d, tn), lambda j: (0, j))],
        [_Out(n, tn, lambda j: j, F32)], [(d, BF16)], n // tn,
        casts=[(lw, CAST_ROWS_PROJ) for lw in later_weights])


def _merge_body(or_ref, oh_ref, wr_ref, wh_ref, ga_ref, gb_ref, o_ref):
    ya = _dot(or_ref[...], wr_ref[...])
    yb = _dot(oh_ref[...], wh_ref[...])
    o_ref[...] = (_sigmoid(ga_ref[...]) * ya + _sigmoid(gb_ref[...]) * yb).astype(BF16)


def _merge(o_r, o_h, w_r, w_h, proj):
    n = w_r.shape[1]
    return _dense_call(
        _merge_body, "merge", TILE_M_MERGE,
        [_Rows(*o_r, RET_V, lambda j: 0),
         _Rows(*o_h, HG_V, lambda j: 0),
         _Shared(w_r, (RET_V, n), lambda j: (0, 0), resident=True),
         _Shared(w_h, (HG_V, n), lambda j: (0, 0), resident=True),
         _Rows(*proj, n, lambda j: OFF_GA // n),
         _Rows(*proj, n, lambda j: OFF_GB // n)],
        [_Out(n, n, lambda j: 0, BF16)], [], 1)[0]


def _resid_body(x_ref, m_ref, w_ref, g_ref, o_ref, h_ref):
    x1 = x_ref[...] + _dot(m_ref[...], w_ref[...])
    o_ref[...] = x1
    h_ref[...] = _rms_rows(x1, g_ref[...]).astype(BF16)


def _resid(x, mrg, w, g):
    d = w.shape[1]
    return _dense_call(
        _resid_body, "resid", TILE_M_RESID,
        [_Rows(*x, d, lambda j: 0),
         _Rows(*mrg, d, lambda j: 0),
         _Shared(w, (d, d), lambda j: (0, 0), resident=True),
         _Shared(g, (1, d), lambda j: (0, 0))],
        [_Out(d, d, lambda j: 0, F32), _Out(d, d, lambda j: 0, BF16)], [], 1)


def _ffn_in_body(h_ref, wg_ref, wu_ref, o_ref):
    wg = wg_ref[...].astype(BF16)
    wu = wu_ref[...].astype(BF16)
    rows = h_ref.shape[0]
    step = min(rows, FFN_IN_ROW_CHUNK)
    for r in range(0, rows, step):
        h = h_ref[r:r + step, :]
        o_ref[r:r + step, :] = (_silu(_dot(h, wg)) * _dot(h, wu)).astype(BF16)


def _ffn_in(h, w, w_down):
    d = w.shape[0]
    tn = TILE_N_FFN_IN
    return _dense_call(
        _ffn_in_body, "ffn_in", TILE_M_FFN_IN,
        [_Rows(*h, d, lambda j: 0),
         _Shared(w, (d, tn), lambda j: (0, j)),
         _Shared(w, (d, tn), lambda j: (0, D_FF // tn + j))],
        [_Out(D_FF, tn, lambda j: j, BF16)], [], D_FF // tn, casts=[(w_down, CAST_ROWS_FFN)])


def _ffn_out_body(a_ref, w_ref, x_ref, g_ref, o_ref):
    o_ref[...] = _rms_rows(x_ref[...] + _dot(a_ref[...], w_ref[...]), g_ref[...])


def _ffn_out(act, w, x, g):
    k, d = w.shape
    return _dense_call(
        _ffn_out_body, "ffn_out", TILE_M_FFN_OUT,
        [_Rows(*act, k, lambda j: 0),
         _Shared(w, (k, d), lambda j: (0, 0), resident=True),
         _Rows(*x, d, lambda j: 0),
         _Shared(g, (1, d), lambda j: (0, 0))],
        [_Out(d, d, lambda j: 0, F32)], [], 1)[0]


def _rotary(x, cos, sin_signed):
    return x * cos + pltpu.roll(x, RET_DK // 2, 1) * sin_signed


def _ret_parts(p_ref, cos_ref, sin_ref, o_ref, so_ref, s_ref, dec_ref):
    c = CHUNK

    def init():
        s_ref[...] = jnp.zeros_like(s_ref)
        row = lax.broadcasted_iota(jnp.int32, (c, c), 0)
        col = lax.broadcasted_iota(jnp.int32, (c, c), 1)
        diff = (row - col).astype(F32)
        idx = lax.broadcasted_iota(jnp.int32, (c, RET_DK), 0).astype(F32)
        for h in range(RET_HEADS):
            lg = LOG_GAMMA[h]
            dec_ref[h, 0] = jnp.where(diff >= 0, jnp.exp(jnp.maximum(diff, 0.0) * lg), 0.0)
            dec_ref[h, 1] = jnp.exp((idx + 1.0) * lg)
            dec_ref[h, 2] = jnp.exp((c - 1.0 - idx) * lg)

    def step():
        cos = cos_ref[...]
        sin = sin_ref[...]
        for h in range(RET_HEADS):
            q = _rotary(p_ref[0, :, OFF_RQ + h * RET_DK:OFF_RQ + (h + 1) * RET_DK], cos, sin)
            k = _rotary(p_ref[0, :, OFF_RK + h * RET_DK:OFF_RK + (h + 1) * RET_DK], cos, sin) * (RET_DK ** -0.5)
            v = p_ref[0, :, OFF_RV + h * RET_DV:OFF_RV + (h + 1) * RET_DV].astype(BF16)
            chunk_decay = math.exp(c * LOG_GAMMA[h])
            s = s_ref[h]
            scores = _dot_nt(q.astype(BF16), k.astype(BF16)) * dec_ref[h, 0]
            o = _dot(scores.astype(BF16), v) + _dot((q * dec_ref[h, 1]).astype(BF16), s.astype(BF16))
            s_ref[h] = chunk_decay * s + _dot_tn((k * dec_ref[h, 2]).astype(BF16), v)
            o = o * lax.rsqrt(jnp.mean(o * o, axis=-1, keepdims=True) + EPS)
            gate = p_ref[0, :, OFF_RG + h * RET_DV:OFF_RG + (h + 1) * RET_DV]
            o_ref[0, :, h * RET_DV:(h + 1) * RET_DV] = (o * _silu(gate)).astype(BF16)

    def final():
        so_ref[0] = s_ref[...]

    return init, step, final


def _lower_bound(lbl):
    mx = jnp.max(lbl, axis=0, keepdims=True)
    e = jnp.exp(lbl - mx)
    return e[0:1, :] / jnp.sum(e, axis=0, keepdims=True)


def _hgrn_gates(z, lb):
    half = 0.5 * (1.0 - lb)
    ht = half * jnp.tanh(0.5 * z)
    return (lb + half) + ht, half - ht


def _cumsum_rows(tri, x):
    hi = x.astype(BF16)
    r1 = x - hi.astype(F32)
    mid = r1.astype(BF16)
    lo = (r1 - mid.astype(F32)).astype(BF16)
    return _dot(tri, hi) + _dot(tri, mid) + _dot(tri, lo)


def _bcast_rows(x, group, r):
    c, d = x.shape
    x3 = x.reshape(c // group, group, d)
    return jnp.broadcast_to(x3[:, r:r + 1, :], x3.shape).reshape(c, d)


def _hgrn_parts(qf_ref, ig_ref, lbl_ref, nrm_ref, o_ref, so_ref, st_ref):
    def init():
        st_ref[...] = jnp.zeros_like(st_ref)

    def final():
        for h in range(HG_HEADS):
            so_ref[0, h] = st_ref[h].T

    return init, functools.partial(_hgrn_step, qf_ref, ig_ref, lbl_ref, nrm_ref, o_ref, st_ref), final


def _hgrn_step(qf_ref, ig_ref, lbl_ref, nrm_ref, o_ref, st_ref):
    c = CHUNK
    row = lax.broadcasted_iota(jnp.int32, (c, c), 0)
    col = lax.broadcasted_iota(jnp.int32, (c, c), 1)
    rowd = lax.broadcasted_iota(jnp.int32, (c, HG_DK), 0)
    tri = jnp.where(row >= col, 1.0, 0.0).astype(BF16)
    r2, r4 = rowd & 1, rowd & 3
    block_masks = [(row >> s) == (col >> s) for s in range(c.bit_length())]
    nrm = nrm_ref[...]

    def up(x, k):
        return pltpu.roll(x, k, 0)

    def dn(x, k):
        return pltpu.roll(x, c - k, 0)

    def head(h):
        off = h * HG_DK
        lb = _lower_bound(lbl_ref[:, pl.ds(off, HG_DK)])
        f, kin = _hgrn_gates(qf_ref[0, :, pl.ds(HG_K + off, HG_DK)], lb)
        qh = _silu(qf_ref[0, :, pl.ds(off, HG_DK)])
        v = ig_ref[0, :, pl.ds(off, HG_DV)].astype(BF16)
        b = _cumsum_rows(tri, jnp.log2(f))

        f_up = f * up(f, 1)
        f_dn = f * dn(f, 1)
        pre2 = jnp.where(r2 == 1, f_up, f)
        pre4 = jnp.where(r4 >= 1, f_up, f)
        pre4 = jnp.where(r4 >= 2, pre4 * up(pre4, 2), pre4)
        suf4 = jnp.where(r4 <= 2, f_dn, f)
        suf4 = jnp.where(r4 <= 1, suf4 * dn(suf4, 2), suf4)

        def level(s):
            if s == 0:
                return qh, kin
            if s == 1:
                return qh * f, kin
            if s == 2:
                return qh * pre2, kin * jnp.where(r4 == 0, dn(f, 1), 1.0)
            if s == 3:
                return qh * pre4, kin * jnp.where(r4 == 3, 1.0, dn(suf4, 1))
            half = 1 << (s - 1)
            ref = _bcast_rows(b, 2 * half, half - 1)
            upper = (rowd & (2 * half - 1)) >= half
            return jnp.where(upper, qh * jnp.exp2(b - ref), 0.0), jnp.where(upper, 0.0, kin * jnp.exp2(ref - b))

        a = jnp.zeros((c, c), F32)
        for s in reversed(range(len(block_masks))):
            q_s, k_s = level(s)
            a = jnp.where(block_masks[s], _dot_nt(q_s.astype(BF16), k_s.astype(BF16)), a)
        a = jnp.where(col <= row, a, 0.0)

        st = st_ref[h]
        b_last = b[c - 1:c, :]
        o = _dot(a.astype(BF16), v) + _dot_nt((qh * jnp.exp2(b)).astype(BF16), st.astype(BF16))
        st_ref[h] = st * jnp.exp2(b_last) + _dot_tn(v, (kin * jnp.exp2(b_last - b)).astype(BF16))
        o = o * lax.rsqrt(jnp.mean(o * o, axis=-1, keepdims=True) + EPS) * nrm
        o_ref[0, :, pl.ds(off, HG_DV)] = (o * _silu(ig_ref[0, :, pl.ds(HG_V + off, HG_DV)])).astype(BF16)

    for h in range(HG_HEADS):
        head(h)


def _mixers_kernel(p_ref, cos_ref, sin_ref, qf_ref, ig_ref, lbl_ref, nrm_ref,
                   or_ref, sor_ref, oh_ref, soh_ref, s_ref, dec_ref, st_ref):
    n = pl.program_id(1)
    parts = (_ret_parts(p_ref, cos_ref, sin_ref, or_ref, sor_ref, s_ref, dec_ref),
             _hgrn_parts(qf_ref, ig_ref, lbl_ref, nrm_ref, oh_ref, soh_ref, st_ref))

    @pl.when(n == 0)
    def _():
        for init, _, _ in parts:
            init()

    for _, step, _ in parts:
        step()

    @pl.when(n == pl.num_programs(1) - 1)
    def _():
        for _, _, final in parts:
            final()


def _mixers_prompt(proj3, cos, sin, lbl, nrm):
    b, t, _ = proj3.shape
    nc = t // CHUNK
    return pl.pallas_call(
        _mixers_kernel,
        grid=(b, nc),
        in_specs=[pl.BlockSpec((1, CHUNK, OFF_HQ), lambda i, n: (i, n, 0)),
                  pl.BlockSpec((CHUNK, RET_DK), lambda i, n: (n, 0)),
                  pl.BlockSpec((CHUNK, RET_DK), lambda i, n: (n, 0)),
                  pl.BlockSpec((1, CHUNK, 2 * HG_K), lambda i, n: (i, n, OFF_HQ // (2 * HG_K))),
                  pl.BlockSpec((1, CHUNK, 2 * HG_V), lambda i, n: (i, n, OFF_HI // (2 * HG_V))),
                  pl.BlockSpec((DEPTH + 1, HG_K), lambda i, n: (0, 0)),
                  pl.BlockSpec((1, HG_DV), lambda i, n: (0, 0))],
        out_specs=[pl.BlockSpec((1, CHUNK, RET_V), lambda i, n: (i, n, 0)),
                   pl.BlockSpec((1, RET_HEADS, RET_DK, RET_DV), lambda i, n: (i, 0, 0, 0)),
                   pl.BlockSpec((1, CHUNK, HG_V), lambda i, n: (i, n, 0)),
                   pl.BlockSpec((1, HG_HEADS, HG_DK, HG_DV), lambda i, n: (i, 0, 0, 0))],
        out_shape=[jax.ShapeDtypeStruct((b, t, RET_V), BF16),
                   jax.ShapeDtypeStruct((b, RET_HEADS, RET_DK, RET_DV), F32),
                   jax.ShapeDtypeStruct((b, t, HG_V), BF16),
                   jax.ShapeDtypeStruct((b, HG_HEADS, HG_DK, HG_DV), F32)],
        scratch_shapes=[pltpu.VMEM((RET_HEADS, RET_DK, RET_DV), F32),
                        pltpu.VMEM((RET_HEADS, 3, CHUNK, RET_DK), F32),
                        pltpu.VMEM((HG_HEADS, HG_DV, HG_DK), F32)],
        compiler_params=_params(("parallel", "arbitrary")),
        name="mixers_prompt",
    )(proj3, cos, sin, proj3, proj3, lbl, nrm)


def _sample_prep_kernel(p_ref, cos_ref, sin_ref, lbl_ref, qr_ref, kr_ref, qh_ref, f_ref, kin_ref):
    cos = cos_ref[...]
    sin = sin_ref[...]
    for h in range(RET_HEADS):
        sl = slice(h * RET_DK, (h + 1) * RET_DK)
        qr_ref[:, sl] = _rotary(p_ref[:, OFF_RQ + h * RET_DK:OFF_RQ + (h + 1) * RET_DK], cos, sin)
        kr_ref[:, sl] = _rotary(p_ref[:, OFF_RK + h * RET_DK:OFF_RK + (h + 1) * RET_DK], cos, sin) * (RET_DK ** -0.5)
    f, kin = _hgrn_gates(p_ref[:, OFF_HF:OFF_HF + HG_K], _lower_bound(lbl_ref[...]))
    f_ref[...] = f
    kin_ref[...] = kin
    qh_ref[...] = _silu(p_ref[:, OFF_HQ:OFF_HQ + HG_K])


def _sample_prep(proj, cos, sin, lbl):
    m = proj.shape[0]
    shp = jax.ShapeDtypeStruct((m, RET_QK), F32)
    return pl.pallas_call(
        _sample_prep_kernel,
        grid=(1,),
        in_specs=[pl.BlockSpec((m, OFF_GA), lambda i: (0, 0)),
                  pl.BlockSpec((1, RET_DK), lambda i: (0, 0)),
                  pl.BlockSpec((1, RET_DK), lambda i: (0, 0)),
                  pl.BlockSpec((DEPTH + 1, HG_K), lambda i: (0, 0))],
        out_specs=[pl.BlockSpec((m, RET_QK), lambda i: (0, 0))] * 5,
        out_shape=[shp] * 5,
        compiler_params=_params(("arbitrary",)),
        name="sample_prep",
    )(proj, cos, sin, lbl)


def _sample_state_kernel(sr_ref, sh_ref, qr_ref, kr_ref, qh_ref, f_ref, kin_ref, vr_ref, rg_ref, vh_ref, hg_ref,
                         nrm_ref, sro_ref, sho_ref, or_ref, oh_ref):
    lanes = RET_DK
    for t in range(SAMPLE_SEQS_PER_STEP):
        b = pl.program_id(0) * SAMPLE_SEQS_PER_STEP + t
        base = pl.multiple_of((b // SUB) * SUB, SUB)

        def row(ref, sl, b=b, base=base):
            tile = ref[pl.ds(base, SUB), sl]
            sub = lax.broadcasted_iota(jnp.int32, tile.shape, 0)
            return jnp.sum(jnp.where(sub == b % SUB, tile, 0.0), axis=0, keepdims=True)

        def column(ref, h, row=row):
            r = row(ref, slice(h * RET_DK, (h + 1) * RET_DK))
            return jnp.broadcast_to(r, (RET_DK, RET_DK)).T

        for h in range(RET_HEADS):
            kc = column(kr_ref, h)
            qc = column(qr_ref, h)
            gamma = math.exp(LOG_GAMMA[h])
            outs = []
            for half in range(RET_DV // lanes):
                hs = slice(half * lanes, (half + 1) * lanes)
                s = gamma * sr_ref[t, h, :, hs] + kc * row(vr_ref, slice(h * RET_DV + hs.start, h * RET_DV + hs.stop))
                sro_ref[t, h, :, hs] = s
                outs.append(jnp.sum(qc * s, axis=0, keepdims=True))
            o = jnp.concatenate(outs, axis=1)
            o = o * lax.rsqrt(jnp.mean(o * o, axis=-1, keepdims=True) + EPS)
            sl = slice(h * RET_DV, (h + 1) * RET_DV)
            or_ref[t, :, sl] = o * _silu(row(rg_ref, sl))
        for h in range(HG_HEADS):
            sl = slice(h * HG_DV, (h + 1) * HG_DV)
            s = column(f_ref, h) * sh_ref[t, h] + column(kin_ref, h) * row(vh_ref, sl)
            sho_ref[t, h] = s
            o = jnp.sum(column(qh_ref, h) * s, axis=0, keepdims=True)
            o = o * lax.rsqrt(jnp.mean(o * o, axis=-1, keepdims=True) + EPS) * nrm_ref[...]
            oh_ref[t, :, sl] = o * _silu(row(hg_ref, sl))


def _sample_state(s_ret, s_hg, rows, proj, nrm):
    b = s_ret.shape[0]
    nb = SAMPLE_SEQS_PER_STEP
    row_spec = pl.BlockSpec((b, RET_QK), lambda i: (0, 0))
    return pl.pallas_call(
        _sample_state_kernel,
        grid=(b // nb,),
        in_specs=[pl.BlockSpec((nb, RET_HEADS, RET_DK, RET_DV), lambda i: (i, 0, 0, 0)),
                  pl.BlockSpec((nb, HG_HEADS, HG_DK, HG_DV), lambda i: (i, 0, 0, 0)),
                  row_spec, row_spec, row_spec, row_spec, row_spec,
                  pl.BlockSpec((b, RET_V), lambda i: (0, OFF_RV // RET_V)),
                  pl.BlockSpec((b, RET_V), lambda i: (0, OFF_RG // RET_V)),
                  pl.BlockSpec((b, HG_V), lambda i: (0, OFF_HI // HG_V)),
                  pl.BlockSpec((b, HG_V), lambda i: (0, OFF_HG // HG_V)),
                  pl.BlockSpec((1, HG_DV), lambda i: (0, 0))],
        out_specs=[pl.BlockSpec((nb, RET_HEADS, RET_DK, RET_DV), lambda i: (i, 0, 0, 0)),
                   pl.BlockSpec((nb, HG_HEADS, HG_DK, HG_DV), lambda i: (i, 0, 0, 0)),
                   pl.BlockSpec((nb, 1, RET_V), lambda i: (i, 0, 0)),
                   pl.BlockSpec((nb, 1, HG_V), lambda i: (i, 0, 0))],
        out_shape=[jax.ShapeDtypeStruct(s_ret.shape, F32),
                   jax.ShapeDtypeStruct(s_hg.shape, F32),
                   jax.ShapeDtypeStruct((b, 1, RET_V), F32),
                   jax.ShapeDtypeStruct((b, 1, HG_V), F32)],
        compiler_params=_params(("arbitrary",)),
        name="sample_state",
    )(s_ret, s_hg, *rows, proj, proj, proj, proj, nrm)


def _rope_tables(pos):
    half = RET_DK // 2
    inv_freq = ROPE_BASE ** (-jnp.arange(half, dtype=F32) / half)
    ang = pos.astype(F32)[:, None] * inv_freq[None, :]
    cos, sin = jnp.cos(ang), jnp.sin(ang)
    return jnp.concatenate([cos, cos], axis=-1), jnp.concatenate([-sin, sin], axis=-1)


def kernel(x_prompt, x_sample, state_ret, state_hgrn, w_in, w_ret_out, w_hgrn_out, w_out, norm_mix, norm_ffn,
           hgrn_norm, hgrn_lb_logits, w_ffn_in, w_ffn_out, norm_final):
    assert w_in.shape[0] == DEPTH == 1
    bp, t, d = x_prompt.shape
    bs = x_sample.shape[0]
    nrm = hgrn_norm[0][None, :]
    lbl = hgrn_lb_logits.astype(F32)
    x = (x_prompt.reshape(bp * t, d), x_sample.reshape(bs, d))

    proj, wb_ret, wb_hgrn, wb_out = _proj(x, norm_mix[0][None, :], w_in[0], [w_ret_out[0], w_hgrn_out[0], w_out[0]])

    proj_p3 = proj[0].reshape(bp, t, IN_TOTAL)
    cos_p, sin_p = _rope_tables(jnp.arange(t))
    o_r, s_ret_p, o_h, s_hg_p = _mixers_prompt(proj_p3, cos_p, sin_p, lbl, nrm)

    cos_s, sin_s = _rope_tables(PAST_LEN + jnp.arange(1))
    rows = _sample_prep(proj[1], cos_s, sin_s, lbl)
    s_ret_s, s_hg_s, o_rs, o_hs = _sample_state(state_ret[0], state_hgrn[0], rows, proj[1], nrm)

    o_r = (o_r.reshape(bp * t, RET_V), o_rs.reshape(bs, RET_V).astype(BF16))
    o_h = (o_h.reshape(bp * t, HG_V), o_hs.reshape(bs, HG_V).astype(BF16))
    mrg = _merge(o_r, o_h, wb_ret, wb_hgrn, proj)
    x1, h2 = _resid(x, mrg, wb_out, norm_ffn[0][None, :])
    act, wb_down = _ffn_in(h2, w_ffn_in[0], w_ffn_out[0])
    y_p, y_s = _ffn_out(act, wb_down, x1, norm_final[None, :])

    return (y_p.reshape(bp, t, d), y_s.reshape(bs, 1, d), s_ret_p[None], s_hg_p[None], s_ret_s[None], s_hg_s[None])
```

```python
import functools
import math
from typing import Any, Callable, NamedTuple

import numpy as np
import jax
import jax.numpy as jnp
from jax import lax
from jax.experimental import pallas as pl
from jax.experimental.pallas import tpu as pltpu

D_MODEL = 2048
DEPTH = 1
PAST_LEN = 16384
RET_HEADS = 8
RET_DK = 128
RET_DV = 256
RET_QK = RET_HEADS * RET_DK
RET_V = RET_HEADS * RET_DV
ROPE_BASE = 10000.0
HG_HEADS = 8
HG_DK = 128
HG_DV = 128
HG_K = HG_HEADS * HG_DK
HG_V = HG_HEADS * HG_DV
D_FF = 5632
EPS = 1e-6

OFF_RQ = 0
OFF_RK = OFF_RQ + RET_QK
OFF_RV = OFF_RK + RET_QK
OFF_RG = OFF_RV + RET_V
OFF_HQ = OFF_RG + RET_V
OFF_HF = OFF_HQ + HG_K
OFF_HI = OFF_HF + HG_K
OFF_HG = OFF_HI + HG_V
OFF_GA = OFF_HG + HG_V
OFF_GB = OFF_GA + D_MODEL
IN_TOTAL = OFF_GB + D_MODEL

CHUNK = 128
SUB = 8
SAMPLE_SEQS_PER_STEP = 4
V7X_VMEM_LIMIT = 56 * 1024 * 1024

F32 = jnp.float32
BF16 = jnp.bfloat16

LOG_GAMMA = [float(np.log(np.float32(1.0) - np.float32(2.0) ** np.float32(-5.0 - h))) for h in range(RET_HEADS)]


def _params(sem, vmem=V7X_VMEM_LIMIT):
    return pltpu.CompilerParams(dimension_semantics=sem, vmem_limit_bytes=vmem)


def _rms_rows(x, g):
    return (x * lax.rsqrt(jnp.mean(x * x, axis=-1, keepdims=True) + EPS)) * g


def _sigmoid(x):
    return 0.5 + 0.5 * jnp.tanh(0.5 * x)


def _silu(x):
    hx = 0.5 * x
    return hx + hx * jnp.tanh(hx)


def _dot(a, b):
    return jnp.dot(a, b, preferred_element_type=F32)


def _dot_nt(a, b):
    return lax.dot_general(a, b, (((1,), (1,)), ((), ())), preferred_element_type=F32)


def _dot_tn(a, b):
    return lax.dot_general(a, b, (((0,), (0,)), ((), ())), preferred_element_type=F32)


TILE_M_PROJ, TILE_N_PROJ = 2048, 512
TILE_M_MERGE = 512
TILE_M_RESID = 512
TILE_M_FFN_IN, TILE_N_FFN_IN = 2048, 512
TILE_M_FFN_OUT = 256
FFN_IN_ROW_CHUNK = 512
CAST_ROWS_FFN = 128


class _Rows(NamedTuple):
    prompt: Any
    sample: Any
    cols: int
    col: Callable
    single: bool = False


class _Shared(NamedTuple):
    array: Any
    block: tuple
    index: Callable
    resident: bool = False


class _Out(NamedTuple):
    total_cols: int
    cols: int
    col: Callable
    dtype: Any


def _dense_call(body, name, tile_m, operands, outs, scratch, inner, casts=(), joint_body=None):
    rows = [op for op in operands if isinstance(op, _Rows)]
    m_p, m_s = rows[0].prompt.shape[0], rows[0].sample.shape[0]
    last = m_p // tile_m - 1

    def frozen(i, j):
        return jnp.where(i == last, j, 0)

    specs_p, specs_s, args_p, args_s = [], [], [], []
    for op in operands:
        if isinstance(op, _Rows):
            mode = dict(pipeline_mode=pl.Buffered(1)) if op.single else {}
            specs_p.append(pl.BlockSpec((tile_m, op.cols), lambda i, j, op=op: (i, op.col(j)), **mode))
            specs_s.append(pl.BlockSpec((m_s, op.cols), lambda i, j, op=op: (0, op.col(frozen(i, j)))))
            args_p.append(op.prompt)
            args_s.append(op.sample)
        else:
            mode = dict(pipeline_mode=pl.Buffered(1)) if op.resident else {}
            specs_p.append(pl.BlockSpec(op.block, lambda i, j, op=op: op.index(j), **mode))
            args_p.append(op.array)
    out_specs = ([pl.BlockSpec((tile_m, o.cols), lambda i, j, o=o: (i, o.col(j))) for o in outs]
                 + [pl.BlockSpec((m_s, o.cols), lambda i, j, o=o: (0, o.col(frozen(i, j)))) for o in outs])
    out_shape = ([jax.ShapeDtypeStruct((m_p, o.total_cols), o.dtype) for o in outs]
                 + [jax.ShapeDtypeStruct((m_s, o.total_cols), o.dtype) for o in outs])
    if joint_body is None:
        scratch_shapes = ([pltpu.VMEM((tile_m, c), dt) for c, dt in scratch]
                          + [pltpu.VMEM((m_s, c), dt) for c, dt in scratch])
    else:
        scratch_shapes = [pltpu.VMEM((tile_m + m_s, c), dt) for c, dt in scratch]
    n_in, n_out, n_scr = len(operands), len(outs), len(scratch)
    row_pos = [k for k, op in enumerate(operands) if isinstance(op, _Rows)]
    n_rows, n_cast = len(row_pos), len(casts)
    cast_blocks = [w.shape[0] // rb for w, rb in casts]
    assert all(nb <= (m_p // tile_m) * inner for nb in cast_blocks)
    cast_specs = [pl.BlockSpec((rb, w.shape[1]), lambda i, j, nb=nb: (jnp.minimum(i * inner + j, nb - 1), 0))
                  for (w, rb), nb in zip(casts, cast_blocks)]

    def kernel(*refs):
        ins_p = list(refs[:n_in])
        ins_s = list(ins_p)
        for k, ref in zip(row_pos, refs[n_in:n_in + len(row_pos)]):
            ins_s[k] = ref
        r = n_in + n_rows
        cast_in = refs[r:r + n_cast]
        r += n_cast
        outs_p, outs_s = refs[r:r + n_out], refs[r + n_out:r + 2 * n_out]
        r += 2 * n_out
        cast_out = refs[r:r + n_cast]
        r += n_cast
        is_last = pl.program_id(0) == last
        if joint_body is None:
            body(*ins_p, *outs_p, *refs[r:r + n_scr])

            @pl.when(is_last)
            def _():
                body(*ins_s, *outs_s, *refs[r + n_scr:])
        else:
            scr = refs[r:r + n_scr]

            @pl.when(jnp.logical_not(is_last))
            def _():
                body(*ins_p, *outs_p, *[sc.at[pl.ds(0, tile_m)] for sc in scr])

            @pl.when(is_last)
            def _():
                joint_body(ins_p, refs[n_in:n_in + n_rows], outs_p, outs_s, scr)

        step = pl.program_id(0) * inner + pl.program_id(1)
        for src, dst, nb in zip(cast_in, cast_out, cast_blocks):
            @pl.when(step < nb)
            def _(src=src, dst=dst):
                dst[...] = src[...].astype(BF16)

    res = pl.pallas_call(
        kernel,
        grid=(m_p // tile_m, inner),
        in_specs=specs_p + specs_s + cast_specs,
        out_specs=out_specs + cast_specs,
        out_shape=out_shape + [jax.ShapeDtypeStruct(w.shape, BF16) for w, _ in casts],
        scratch_shapes=scratch_shapes,
        compiler_params=_params(("arbitrary", "arbitrary")),
        name=name,
    )(*args_p, *args_s, *[w for w, _ in casts])
    return list(zip(res[:n_out], res[n_out:2 * n_out])) + list(res[2 * n_out:])


def _proj_body(x_ref, g_ref, w_ref, o_ref, h_ref):
    @pl.when(pl.program_id(1) == 0)
    def _():
        h_ref[...] = _rms_rows(x_ref[...], g_ref[...]).astype(BF16)
    o_ref[...] = _dot(h_ref[...], w_ref[...].astype(BF16))


def _proj_joint_body(ins, sample_rows, outs_p, outs_s, scratch):
    (x_ref, g_ref, w_ref), (xs_ref,), (o_ref,), (os_ref,), (h_ref,) = ins, sample_rows, outs_p, outs_s, scratch
    tm = x_ref.shape[0]

    @pl.when(pl.program_id(1) == 0)
    def _():
        h_ref[:tm, :] = _rms_rows(x_ref[...], g_ref[...]).astype(BF16)
        h_ref[tm:, :] = _rms_rows(xs_ref[...], g_ref[...]).astype(BF16)
    res = _dot(h_ref[...], w_ref[...].astype(BF16))
    o_ref[...] = res[:tm]
    os_ref[...] = res[tm:]


def _proj(x, g, w):
    d, n = w.shape
    tn = TILE_N_PROJ
    return _dense_call(
        _proj_body, "proj", TILE_M_PROJ,
        [_Rows(*x, d, lambda j: 0, single=True),
         _Shared(g, (1, d), lambda j: (0, 0)),
         _Shared(w, (d, tn), lambda j: (0, j))],
        [_Out(n, tn, lambda j: j, F32)], [(d, BF16)], n // tn, joint_body=_proj_joint_body)[0]


def _merge_body(or_ref, oh_ref, wr_ref, wh_ref, ga_ref, gb_ref, o_ref):
    ya = _dot(or_ref[...], wr_ref[...])
    yb = _dot(oh_ref[...], wh_ref[...])
    o_ref[...] = (_sigmoid(ga_ref[...]) * ya + _sigmoid(gb_ref[...]) * yb).astype(BF16)


def _merge(o_r, o_h, w_r, w_h, proj):
    n = w_r.shape[1]
    return _dense_call(
        _merge_body, "merge", TILE_M_MERGE,
        [_Rows(*o_r, RET_V, lambda j: 0),
         _Rows(*o_h, HG_V, lambda j: 0),
         _Shared(w_r, (RET_V, n), lambda j: (0, 0), resident=True),
         _Shared(w_h, (HG_V, n), lambda j: (0, 0), resident=True),
         _Rows(*proj, n, lambda j: OFF_GA // n),
         _Rows(*proj, n, lambda j: OFF_GB // n)],
        [_Out(n, n, lambda j: 0, BF16)], [], 1)[0]


def _resid_body(x_ref, m_ref, w_ref, g_ref, o_ref, h_ref):
    x1 = x_ref[...] + _dot(m_ref[...], w_ref[...])
    o_ref[...] = x1
    h_ref[...] = _rms_rows(x1, g_ref[...]).astype(BF16)


def _resid(x, mrg, w, g):
    d = w.shape[1]
    return _dense_call(
        _resid_body, "resid", TILE_M_RESID,
        [_Rows(*x, d, lambda j: 0),
         _Rows(*mrg, d, lambda j: 0),
         _Shared(w, (d, d), lambda j: (0, 0), resident=True),
         _Shared(g, (1, d), lambda j: (0, 0))],
        [_Out(d, d, lambda j: 0, F32), _Out(d, d, lambda j: 0, BF16)], [], 1)


def _ffn_in_body(h_ref, wg_ref, wu_ref, o_ref):
    wg = wg_ref[...].astype(BF16)
    wu = wu_ref[...].astype(BF16)
    rows = h_ref.shape[0]
    step = min(rows, FFN_IN_ROW_CHUNK)
    for r in range(0, rows, step):
        h = h_ref[r:r + step, :]
        o_ref[r:r + step, :] = (_silu(_dot(h, wg)) * _dot(h, wu)).astype(BF16)


def _ffn_in(h, w, w_down):
    d = w.shape[0]
    tn = TILE_N_FFN_IN
    return _dense_call(
        _ffn_in_body, "ffn_in", TILE_M_FFN_IN,
        [_Rows(*h, d, lambda j: 0),
         _Shared(w, (d, tn), lambda j: (0, j)),
         _Shared(w, (d, tn), lambda j: (0, D_FF // tn + j))],
        [_Out(D_FF, tn, lambda j: j, BF16)], [], D_FF // tn, casts=[(w_down, CAST_ROWS_FFN)])


def _ffn_out_body(a_ref, w_ref, x_ref, g_ref, o_ref):
    o_ref[...] = _rms_rows(x_ref[...] + _dot(a_ref[...], w_ref[...]), g_ref[...])


def _ffn_out(act, w, x, g):
    k, d = w.shape
    return _dense_call(
        _ffn_out_body, "ffn_out", TILE_M_FFN_OUT,
        [_Rows(*act, k, lambda j: 0),
         _Shared(w, (k, d), lambda j: (0, 0), resident=True),
         _Rows(*x, d, lambda j: 0),
         _Shared(g, (1, d), lambda j: (0, 0))],
        [_Out(d, d, lambda j: 0, F32)], [], 1)[0]


def _rotary(x, cos, sin_signed):
    return x * cos + pltpu.roll(x, RET_DK // 2, 1) * sin_signed


def _ret_parts(p_ref, cos_ref, sin_ref, o_ref, so_ref, s_ref, dec_ref):
    c = CHUNK

    def init():
        s_ref[...] = jnp.zeros_like(s_ref)
        row = lax.broadcasted_iota(jnp.int32, (c, c), 0)
        col = lax.broadcasted_iota(jnp.int32, (c, c), 1)
        diff = (row - col).astype(F32)
        idx = lax.broadcasted_iota(jnp.int32, (c, RET_DK), 0).astype(F32)
        for h in range(RET_HEADS):
            lg = LOG_GAMMA[h]
            dec_ref[h, 0] = jnp.where(diff >= 0, jnp.exp(jnp.maximum(diff, 0.0) * lg), 0.0)
            dec_ref[h, 1] = jnp.exp((idx + 1.0) * lg)
            dec_ref[h, 2] = jnp.exp((c - 1.0 - idx) * lg)

    def step():
        cos = cos_ref[...]
        sin = sin_ref[...]
        for h in range(RET_HEADS):
            q = _rotary(p_ref[0, :, OFF_RQ + h * RET_DK:OFF_RQ + (h + 1) * RET_DK], cos, sin)
            k = _rotary(p_ref[0, :, OFF_RK + h * RET_DK:OFF_RK + (h + 1) * RET_DK], cos, sin) * (RET_DK ** -0.5)
            v = p_ref[0, :, OFF_RV + h * RET_DV:OFF_RV + (h + 1) * RET_DV].astype(BF16)
            chunk_decay = math.exp(c * LOG_GAMMA[h])
            s = s_ref[h]
            scores = _dot_nt(q.astype(BF16), k.astype(BF16)) * dec_ref[h, 0]
            o = _dot(scores.astype(BF16), v) + _dot((q * dec_ref[h, 1]).astype(BF16), s.astype(BF16))
            s_ref[h] = chunk_decay * s + _dot_tn((k * dec_ref[h, 2]).astype(BF16), v)
            o = o * lax.rsqrt(jnp.mean(o * o, axis=-1, keepdims=True) + EPS)
            gate = p_ref[0, :, OFF_RG + h * RET_DV:OFF_RG + (h + 1) * RET_DV]
            o_ref[0, :, h * RET_DV:(h + 1) * RET_DV] = (o * _silu(gate)).astype(BF16)

    def final():
        so_ref[0] = s_ref[...]

    return init, step, final


def _lower_bound(lbl):
    mx = jnp.max(lbl, axis=0, keepdims=True)
    e = jnp.exp(lbl - mx)
    return e[0:1, :] / jnp.sum(e, axis=0, keepdims=True)


def _hgrn_gates(z, lb):
    half = 0.5 * (1.0 - lb)
    ht = half * jnp.tanh(0.5 * z)
    return (lb + half) + ht, half - ht


def _cumsum_rows(tri, x):
    hi = x.astype(BF16)
    r1 = x - hi.astype(F32)
    mid = r1.astype(BF16)
    lo = (r1 - mid.astype(F32)).astype(BF16)
    return _dot(tri, hi) + _dot(tri, mid) + _dot(tri, lo)


def _bcast_rows(x, group, r):
    c, d = x.shape
    x3 = x.reshape(c // group, group, d)
    return jnp.broadcast_to(x3[:, r:r + 1, :], x3.shape).reshape(c, d)


def _hgrn_parts(qf_ref, ig_ref, lbl_ref, nrm_ref, o_ref, so_ref, st_ref):
    def init():
        st_ref[...] = jnp.zeros_like(st_ref)

    def final():
        for h in range(HG_HEADS):
            so_ref[0, h] = st_ref[h].T

    return init, functools.partial(_hgrn_step, qf_ref, ig_ref, lbl_ref, nrm_ref, o_ref, st_ref), final


def _hgrn_step(qf_ref, ig_ref, lbl_ref, nrm_ref, o_ref, st_ref):
    c = CHUNK
    row = lax.broadcasted_iota(jnp.int32, (c, c), 0)
    col = lax.broadcasted_iota(jnp.int32, (c, c), 1)
    rowd = lax.broadcasted_iota(jnp.int32, (c, HG_DK), 0)
    tri = jnp.where(row >= col, 1.0, 0.0).astype(BF16)
    r2, r4 = rowd & 1, rowd & 3
    block_masks = [(row >> s) == (col >> s) for s in range(c.bit_length())]
    nrm = nrm_ref[...]

    def up(x, k):
        return pltpu.roll(x, k, 0)

    def dn(x, k):
        return pltpu.roll(x, c - k, 0)

    def head(h):
        off = h * HG_DK
        lb = _lower_bound(lbl_ref[:, pl.ds(off, HG_DK)])
        f, kin = _hgrn_gates(qf_ref[0, :, pl.ds(HG_K + off, HG_DK)], lb)
        qh = _silu(qf_ref[0, :, pl.ds(off, HG_DK)])
        v = ig_ref[0, :, pl.ds(off, HG_DV)].astype(BF16)
        b = _cumsum_rows(tri, jnp.log2(f))

        f_up = f * up(f, 1)
        f_dn = f * dn(f, 1)
        pre2 = jnp.where(r2 == 1, f_up, f)
        pre4 = jnp.where(r4 >= 1, f_up, f)
        pre4 = jnp.where(r4 >= 2, pre4 * up(pre4, 2), pre4)
        suf4 = jnp.where(r4 <= 2, f_dn, f)
        suf4 = jnp.where(r4 <= 1, suf4 * dn(suf4, 2), suf4)

        def level(s):
            if s == 0:
                return qh, kin
            if s == 1:
                return qh * f, kin
            if s == 2:
                return qh * pre2, kin * jnp.where(r4 == 0, dn(f, 1), 1.0)
            if s == 3:
                return qh * pre4, kin * jnp.where(r4 == 3, 1.0, dn(suf4, 1))
            half = 1 << (s - 1)
            ref = _bcast_rows(b, 2 * half, half - 1)
            upper = (rowd & (2 * half - 1)) >= half
            return jnp.where(upper, qh * jnp.exp2(b - ref), 0.0), jnp.where(upper, 0.0, kin * jnp.exp2(ref - b))

        a = jnp.zeros((c, c), F32)
        for s in reversed(range(len(block_masks))):
            q_s, k_s = level(s)
            a = jnp.where(block_masks[s], _dot_nt(q_s.astype(BF16), k_s.astype(BF16)), a)
        a = jnp.where(col <= row, a, 0.0)

        st = st_ref[h]
        b_last = b[c - 1:c, :]
        o = _dot(a.astype(BF16), v) + _dot_nt((qh * jnp.exp2(b)).astype(BF16), st.astype(BF16))
        st_ref[h] = st * jnp.exp2(b_last) + _dot_tn(v, (kin * jnp.exp2(b_last - b)).astype(BF16))
        o = o * lax.rsqrt(jnp.mean(o * o, axis=-1, keepdims=True) + EPS) * nrm
        o_ref[0, :, pl.ds(off, HG_DV)] = (o * _silu(ig_ref[0, :, pl.ds(HG_V + off, HG_DV)])).astype(BF16)

    for h in range(HG_HEADS):
        head(h)


def _mixers_kernel(n_cast, p_ref, cos_ref, sin_ref, qf_ref, ig_ref, lbl_ref, nrm_ref, *refs):
    cast_in, (or_ref, sor_ref, oh_ref, soh_ref), refs = refs[:n_cast], refs[n_cast:n_cast + 4], refs[n_cast + 4:]
    cast_out, (s_ref, dec_ref, st_ref) = refs[:n_cast], refs[n_cast:]
    for src, dst in zip(cast_in, cast_out):
        dst[...] = src[...].astype(BF16)
    n = pl.program_id(1)
    parts = (_ret_parts(p_ref, cos_ref, sin_ref, or_ref, sor_ref, s_ref, dec_ref),
             _hgrn_parts(qf_ref, ig_ref, lbl_ref, nrm_ref, oh_ref, soh_ref, st_ref))

    @pl.when(n == 0)
    def _():
        for init, _, _ in parts:
            init()

    for _, step, _ in parts:
        step()

    @pl.when(n == pl.num_programs(1) - 1)
    def _():
        for _, _, final in parts:
            final()


def _mixers_prompt(proj3, cos, sin, lbl, nrm, later_weights):
    b, t, _ = proj3.shape
    nc = t // CHUNK
    cast_specs = [pl.BlockSpec((w.shape[0] // (b * nc), w.shape[1]), lambda i, n: (i * nc + n, 0))
                  for w in later_weights]
    return pl.pallas_call(
        functools.partial(_mixers_kernel, len(later_weights)),
        grid=(b, nc),
        in_specs=[pl.BlockSpec((1, CHUNK, OFF_HQ), lambda i, n: (i, n, 0)),
                  pl.BlockSpec((CHUNK, RET_DK), lambda i, n: (n, 0)),
                  pl.BlockSpec((CHUNK, RET_DK), lambda i, n: (n, 0)),
                  pl.BlockSpec((1, CHUNK, 2 * HG_K), lambda i, n: (i, n, OFF_HQ // (2 * HG_K))),
                  pl.BlockSpec((1, CHUNK, 2 * HG_V), lambda i, n: (i, n, OFF_HI // (2 * HG_V))),
                  pl.BlockSpec((DEPTH + 1, HG_K), lambda i, n: (0, 0)),
                  pl.BlockSpec((1, HG_DV), lambda i, n: (0, 0))] + cast_specs,
        out_specs=[pl.BlockSpec((1, CHUNK, RET_V), lambda i, n: (i, n, 0)),
                   pl.BlockSpec((1, RET_HEADS, RET_DK, RET_DV), lambda i, n: (i, 0, 0, 0)),
                   pl.BlockSpec((1, CHUNK, HG_V), lambda i, n: (i, n, 0)),
                   pl.BlockSpec((1, HG_HEADS, HG_DK, HG_DV), lambda i, n: (i, 0, 0, 0))] + cast_specs,
        out_shape=[jax.ShapeDtypeStruct((b, t, RET_V), BF16),
                   jax.ShapeDtypeStruct((b, RET_HEADS, RET_DK, RET_DV), F32),
                   jax.ShapeDtypeStruct((b, t, HG_V), BF16),
                   jax.ShapeDtypeStruct((b, HG_HEADS, HG_DK, HG_DV), F32)]
        + [jax.ShapeDtypeStruct(w.shape, BF16) for w in later_weights],
        scratch_shapes=[pltpu.VMEM((RET_HEADS, RET_DK, RET_DV), F32),
                        pltpu.VMEM((RET_HEADS, 3, CHUNK, RET_DK), F32),
                        pltpu.VMEM((HG_HEADS, HG_DV, HG_DK), F32)],
        compiler_params=_params(("arbitrary", "arbitrary")),
        name="mixers_prompt",
    )(proj3, cos, sin, proj3, proj3, lbl, nrm, *later_weights)


def _sample_prep_kernel(p_ref, cos_ref, sin_ref, lbl_ref, qr_ref, kr_ref, qh_ref, f_ref, kin_ref):
    cos = cos_ref[...]
    sin = sin_ref[...]
    for h in range(RET_HEADS):
        sl = slice(h * RET_DK, (h + 1) * RET_DK)
        qr_ref[:, sl] = _rotary(p_ref[:, OFF_RQ + h * RET_DK:OFF_RQ + (h + 1) * RET_DK], cos, sin)
        kr_ref[:, sl] = _rotary(p_ref[:, OFF_RK + h * RET_DK:OFF_RK + (h + 1) * RET_DK], cos, sin) * (RET_DK ** -0.5)
    f, kin = _hgrn_gates(p_ref[:, OFF_HF:OFF_HF + HG_K], _lower_bound(lbl_ref[...]))
    f_ref[...] = f
    kin_ref[...] = kin
    qh_ref[...] = _silu(p_ref[:, OFF_HQ:OFF_HQ + HG_K])


def _sample_prep(proj, cos, sin, lbl):
    m = proj.shape[0]
    shp = jax.ShapeDtypeStruct((m, RET_QK), F32)
    return pl.pallas_call(
        _sample_prep_kernel,
        grid=(1,),
        in_specs=[pl.BlockSpec((m, OFF_GA), lambda i: (0, 0)),
                  pl.BlockSpec((1, RET_DK), lambda i: (0, 0)),
                  pl.BlockSpec((1, RET_DK), lambda i: (0, 0)),
                  pl.BlockSpec((DEPTH + 1, HG_K), lambda i: (0, 0))],
        out_specs=[pl.BlockSpec((m, RET_QK), lambda i: (0, 0))] * 5,
        out_shape=[shp] * 5,
        compiler_params=_params(("arbitrary",)),
        name="sample_prep",
    )(proj, cos, sin, lbl)


def _sample_state_kernel(sr_ref, sh_ref, qr_ref, kr_ref, qh_ref, f_ref, kin_ref, vr_ref, rg_ref, vh_ref, hg_ref,
                         nrm_ref, sro_ref, sho_ref, or_ref, oh_ref):
    lanes = RET_DK
    for t in range(SAMPLE_SEQS_PER_STEP):
        b = pl.program_id(0) * SAMPLE_SEQS_PER_STEP + t
        base = pl.multiple_of((b // SUB) * SUB, SUB)

        def row(ref, sl, b=b, base=base):
            tile = ref[pl.ds(base, SUB), sl]
            sub = lax.broadcasted_iota(jnp.int32, tile.shape, 0)
            return jnp.sum(jnp.where(sub == b % SUB, tile, 0.0), axis=0, keepdims=True)

        def column(ref, h, row=row):
            r = row(ref, slice(h * RET_DK, (h + 1) * RET_DK))
            return jnp.broadcast_to(r, (RET_DK, RET_DK)).T

        for h in range(RET_HEADS):
            kc = column(kr_ref, h)
            qc = column(qr_ref, h)
            gamma = math.exp(LOG_GAMMA[h])
            outs = []
            for half in range(RET_DV // lanes):
                hs = slice(half * lanes, (half + 1) * lanes)
                s = gamma * sr_ref[t, h, :, hs] + kc * row(vr_ref, slice(h * RET_DV + hs.start, h * RET_DV + hs.stop))
                sro_ref[t, h, :, hs] = s
                outs.append(jnp.sum(qc * s, axis=0, keepdims=True))
            o = jnp.concatenate(outs, axis=1)
            o = o * lax.rsqrt(jnp.mean(o * o, axis=-1, keepdims=True) + EPS)
            sl = slice(h * RET_DV, (h + 1) * RET_DV)
            or_ref[t, :, sl] = o * _silu(row(rg_ref, sl))
        for h in range(HG_HEADS):
            sl = slice(h * HG_DV, (h + 1) * HG_DV)
            s = column(f_ref, h) * sh_ref[t, h] + column(kin_ref, h) * row(vh_ref, sl)
            sho_ref[t, h] = s
            o = jnp.sum(column(qh_ref, h) * s, axis=0, keepdims=True)
            o = o * lax.rsqrt(jnp.mean(o * o, axis=-1, keepdims=True) + EPS) * nrm_ref[...]
            oh_ref[t, :, sl] = o * _silu(row(hg_ref, sl))


def _sample_state(s_ret, s_hg, rows, proj, nrm):
    b = s_ret.shape[0]
    nb = SAMPLE_SEQS_PER_STEP
    row_spec = pl.BlockSpec((b, RET_QK), lambda i: (0, 0))
    return pl.pallas_call(
        _sample_state_kernel,
        grid=(b // nb,),
        in_specs=[pl.BlockSpec((nb, RET_HEADS, RET_DK, RET_DV), lambda i: (i, 0, 0, 0)),
                  pl.BlockSpec((nb, HG_HEADS, HG_DK, HG_DV), lambda i: (i, 0, 0, 0)),
                  row_spec, row_spec, row_spec, row_spec, row_spec,
                  pl.BlockSpec((b, RET_V), lambda i: (0, OFF_RV // RET_V)),
                  pl.BlockSpec((b, RET_V), lambda i: (0, OFF_RG // RET_V)),
                  pl.BlockSpec((b, HG_V), lambda i: (0, OFF_HI // HG_V)),
                  pl.BlockSpec((b, HG_V), lambda i: (0, OFF_HG // HG_V)),
                  pl.BlockSpec((1, HG_DV), lambda i: (0, 0))],
        out_specs=[pl.BlockSpec((nb, RET_HEADS, RET_DK, RET_DV), lambda i: (i, 0, 0, 0)),
                   pl.BlockSpec((nb, HG_HEADS, HG_DK, HG_DV), lambda i: (i, 0, 0, 0)),
                   pl.BlockSpec((nb, 1, RET_V), lambda i: (i, 0, 0)),
                   pl.BlockSpec((nb, 1, HG_V), lambda i: (i, 0, 0))],
        out_shape=[jax.ShapeDtypeStruct(s_ret.shape, F32),
                   jax.ShapeDtypeStruct(s_hg.shape, F32),
                   jax.ShapeDtypeStruct((b, 1, RET_V), F32),
                   jax.ShapeDtypeStruct((b, 1, HG_V), F32)],
        compiler_params=_params(("arbitrary",)),
        name="sample_state",
    )(s_ret, s_hg, *rows, proj, proj, proj, proj, nrm)


def _rope_tables(pos):
    half = RET_DK // 2
    inv_freq = ROPE_BASE ** (-jnp.arange(half, dtype=F32) / half)
    ang = pos.astype(F32)[:, None] * inv_freq[None, :]
    cos, sin = jnp.cos(ang), jnp.sin(ang)
    return jnp.concatenate([cos, cos], axis=-1), jnp.concatenate([-sin, sin], axis=-1)


def kernel(x_prompt, x_sample, state_ret, state_hgrn, w_in, w_ret_out, w_hgrn_out, w_out, norm_mix, norm_ffn,
           hgrn_norm, hgrn_lb_logits, w_ffn_in, w_ffn_out, norm_final):
    assert w_in.shape[0] == DEPTH == 1
    bp, t, d = x_prompt.shape
    bs = x_sample.shape[0]
    nrm = hgrn_norm[0][None, :]
    lbl = hgrn_lb_logits.astype(F32)
    x = (x_prompt.reshape(bp * t, d), x_sample.reshape(bs, d))

    proj = _proj(x, norm_mix[0][None, :], w_in[0])

    proj_p3 = proj[0].reshape(bp, t, IN_TOTAL)
    cos_p, sin_p = _rope_tables(jnp.arange(t))
    o_r, s_ret_p, o_h, s_hg_p, wb_ret, wb_hgrn, wb_out = _mixers_prompt(
        proj_p3, cos_p, sin_p, lbl, nrm, [w_ret_out[0], w_hgrn_out[0], w_out[0]])

    cos_s, sin_s = _rope_tables(PAST_LEN + jnp.arange(1))
    rows = _sample_prep(proj[1], cos_s, sin_s, lbl)
    s_ret_s, s_hg_s, o_rs, o_hs = _sample_state(state_ret[0], state_hgrn[0], rows, proj[1], nrm)

    o_r = (o_r.reshape(bp * t, RET_V), o_rs.reshape(bs, RET_V).astype(BF16))
    o_h = (o_h.reshape(bp * t, HG_V), o_hs.reshape(bs, HG_V).astype(BF16))
    mrg = _merge(o_r, o_h, wb_ret, wb_hgrn, proj)
    x1, h2 = _resid(x, mrg, wb_out, norm_ffn[0][None, :])
    act, wb_down = _ffn_in(h2, w_ffn_in[0], w_ffn_out[0])
    y_p, y_s = _ffn_out(act, wb_down, x1, norm_final[None, :])

    return (y_p.reshape(bp, t, d), y_s.reshape(bs, 1, d), s_ret_p[None], s_hg_p[None], s_ret_s[None], s_hg_s[None])
```

```python
import functools
import math
from typing import Any, Callable, NamedTuple

import numpy as np
import jax
import jax.numpy as jnp
from jax import lax
from jax.experimental import pallas as pl
from jax.experimental.pallas import tpu as pltpu

D_MODEL = 2048
DEPTH = 1
PAST_LEN = 16384
RET_HEADS = 8
RET_DK = 128
RET_DV = 256
RET_QK = RET_HEADS * RET_DK
RET_V = RET_HEADS * RET_DV
ROPE_BASE = 10000.0
HG_HEADS = 8
HG_DK = 128
HG_DV = 128
HG_K = HG_HEADS * HG_DK
HG_V = HG_HEADS * HG_DV
D_FF = 5632
EPS = 1e-6

OFF_RQ = 0
OFF_RK = OFF_RQ + RET_QK
OFF_RV = OFF_RK + RET_QK
OFF_RG = OFF_RV + RET_V
OFF_HQ = OFF_RG + RET_V
OFF_HF = OFF_HQ + HG_K
OFF_HI = OFF_HF + HG_K
OFF_HG = OFF_HI + HG_V
OFF_GA = OFF_HG + HG_V
OFF_GB = OFF_GA + D_MODEL
IN_TOTAL = OFF_GB + D_MODEL

CHUNK = 128
SUB = 8
SAMPLE_SEQS_PER_STEP = 4
V7X_VMEM_LIMIT = 58 * 1024 * 1024

F32 = jnp.float32
BF16 = jnp.bfloat16

LOG_GAMMA = [float(np.log(np.float32(1.0) - np.float32(2.0) ** np.float32(-5.0 - h))) for h in range(RET_HEADS)]


def _params(sem, vmem=V7X_VMEM_LIMIT):
    return pltpu.CompilerParams(dimension_semantics=sem, vmem_limit_bytes=vmem)


def _rms_rows(x, g):
    return (x * lax.rsqrt(jnp.mean(x * x, axis=-1, keepdims=True) + EPS)) * g


def _sigmoid(x):
    return 0.5 + 0.5 * jnp.tanh(0.5 * x)


def _silu(x):
    hx = 0.5 * x
    return hx + hx * jnp.tanh(hx)


def _dot(a, b):
    return jnp.dot(a, b, preferred_element_type=F32)


def _dot_nt(a, b):
    return lax.dot_general(a, b, (((1,), (1,)), ((), ())), preferred_element_type=F32)


def _dot_tn(a, b):
    return lax.dot_general(a, b, (((0,), (0,)), ((), ())), preferred_element_type=F32)


TILE_M_PROJ, TILE_N_PROJ = 2048, 512
TILE_M_MERGE = 512
TILE_M_RESID = 512
TILE_M_FFN_IN, TILE_N_FFN_IN = 2048, 512
TILE_M_FFN_OUT = 512
FFN_IN_ROW_CHUNK = 512
CAST_ROWS_FFN = 128


class _Rows(NamedTuple):
    prompt: Any
    sample: Any
    cols: int
    col: Callable
    single: bool = False


class _Shared(NamedTuple):
    array: Any
    block: tuple
    index: Callable
    resident: bool = False


class _Out(NamedTuple):
    total_cols: int
    cols: int
    col: Callable
    dtype: Any


def _dense_call(body, name, tile_m, operands, outs, scratch, inner, casts=(), joint_body=None):
    rows = [op for op in operands if isinstance(op, _Rows)]
    m_p, m_s = rows[0].prompt.shape[0], rows[0].sample.shape[0]
    last = m_p // tile_m - 1

    def frozen(i, j):
        return jnp.where(i == last, j, 0)

    specs_p, specs_s, args_p, args_s = [], [], [], []
    for op in operands:
        if isinstance(op, _Rows):
            mode = dict(pipeline_mode=pl.Buffered(1)) if op.single else {}
            specs_p.append(pl.BlockSpec((tile_m, op.cols), lambda i, j, op=op: (i, op.col(j)), **mode))
            specs_s.append(pl.BlockSpec((m_s, op.cols), lambda i, j, op=op: (0, op.col(frozen(i, j)))))
            args_p.append(op.prompt)
            args_s.append(op.sample)
        else:
            mode = dict(pipeline_mode=pl.Buffered(1)) if op.resident else {}
            specs_p.append(pl.BlockSpec(op.block, lambda i, j, op=op: op.index(j), **mode))
            args_p.append(op.array)
    out_specs = ([pl.BlockSpec((tile_m, o.cols), lambda i, j, o=o: (i, o.col(j))) for o in outs]
                 + [pl.BlockSpec((m_s, o.cols), lambda i, j, o=o: (0, o.col(frozen(i, j)))) for o in outs])
    out_shape = ([jax.ShapeDtypeStruct((m_p, o.total_cols), o.dtype) for o in outs]
                 + [jax.ShapeDtypeStruct((m_s, o.total_cols), o.dtype) for o in outs])
    if joint_body is None:
        scratch_shapes = ([pltpu.VMEM((tile_m, c), dt) for c, dt in scratch]
                          + [pltpu.VMEM((m_s, c), dt) for c, dt in scratch])
    else:
        scratch_shapes = [pltpu.VMEM((tile_m + m_s, c), dt) for c, dt in scratch]
    n_in, n_out, n_scr = len(operands), len(outs), len(scratch)
    row_pos = [k for k, op in enumerate(operands) if isinstance(op, _Rows)]
    n_rows, n_cast = len(row_pos), len(casts)
    cast_blocks = [w.shape[0] // rb for w, rb in casts]
    assert all(nb <= (m_p // tile_m) * inner for nb in cast_blocks)
    cast_specs = [pl.BlockSpec((rb, w.shape[1]), lambda i, j, nb=nb: (jnp.minimum(i * inner + j, nb - 1), 0))
                  for (w, rb), nb in zip(casts, cast_blocks)]

    def kernel(*refs):
        ins_p = list(refs[:n_in])
        ins_s = list(ins_p)
        for k, ref in zip(row_pos, refs[n_in:n_in + len(row_pos)]):
            ins_s[k] = ref
        r = n_in + n_rows
        cast_in = refs[r:r + n_cast]
        r += n_cast
        outs_p, outs_s = refs[r:r + n_out], refs[r + n_out:r + 2 * n_out]
        r += 2 * n_out
        cast_out = refs[r:r + n_cast]
        r += n_cast
        is_last = pl.program_id(0) == last
        if joint_body is None:
            body(*ins_p, *outs_p, *refs[r:r + n_scr])

            @pl.when(is_last)
            def _():
                body(*ins_s, *outs_s, *refs[r + n_scr:])
        else:
            scr = refs[r:r + n_scr]

            @pl.when(jnp.logical_not(is_last))
            def _():
                body(*ins_p, *outs_p, *[sc.at[pl.ds(0, tile_m)] for sc in scr])

            @pl.when(is_last)
            def _():
                joint_body(ins_p, refs[n_in:n_in + n_rows], outs_p, outs_s, scr)

        step = pl.program_id(0) * inner + pl.program_id(1)
        for src, dst, nb in zip(cast_in, cast_out, cast_blocks):
            @pl.when(step < nb)
            def _(src=src, dst=dst):
                dst[...] = src[...].astype(BF16)

    res = pl.pallas_call(
        kernel,
        grid=(m_p // tile_m, inner),
        in_specs=specs_p + specs_s + cast_specs,
        out_specs=out_specs + cast_specs,
        out_shape=out_shape + [jax.ShapeDtypeStruct(w.shape, BF16) for w, _ in casts],
        scratch_shapes=scratch_shapes,
        compiler_params=_params(("arbitrary", "arbitrary")),
        name=name,
    )(*args_p, *args_s, *[w for w, _ in casts])
    return list(zip(res[:n_out], res[n_out:2 * n_out])) + list(res[2 * n_out:])


def _proj_body(x_ref, g_ref, w_ref, o_ref, h_ref):
    @pl.when(pl.program_id(1) == 0)
    def _():
        h_ref[...] = _rms_rows(x_ref[...], g_ref[...]).astype(BF16)
    o_ref[...] = _dot(h_ref[...], w_ref[...].astype(BF16))


def _proj_joint_body(ins, sample_rows, outs_p, outs_s, scratch):
    (x_ref, g_ref, w_ref), (xs_ref,), (o_ref,), (os_ref,), (h_ref,) = ins, sample_rows, outs_p, outs_s, scratch
    tm = x_ref.shape[0]

    @pl.when(pl.program_id(1) == 0)
    def _():
        h_ref[:tm, :] = _rms_rows(x_ref[...], g_ref[...]).astype(BF16)
        h_ref[tm:, :] = _rms_rows(xs_ref[...], g_ref[...]).astype(BF16)
    res = _dot(h_ref[...], w_ref[...].astype(BF16))
    o_ref[...] = res[:tm]
    os_ref[...] = res[tm:]


def _proj(x, g, w):
    d, n = w.shape
    tn = TILE_N_PROJ
    return _dense_call(
        _proj_body, "proj", TILE_M_PROJ,
        [_Rows(*x, d, lambda j: 0, single=True),
         _Shared(g, (1, d), lambda j: (0, 0)),
         _Shared(w, (d, tn), lambda j: (0, j))],
        [_Out(n, tn, lambda j: j, F32)], [(d, BF16)], n // tn, joint_body=_proj_joint_body)[0]


def _merge_body(or_ref, oh_ref, wr_ref, wh_ref, ga_ref, gb_ref, o_ref):
    ya = _dot(or_ref[...], wr_ref[...])
    yb = _dot(oh_ref[...], wh_ref[...])
    o_ref[...] = (_sigmoid(ga_ref[...]) * ya + _sigmoid(gb_ref[...]) * yb).astype(BF16)


def _merge(o_r, o_h, w_r, w_h, proj):
    n = w_r.shape[1]
    return _dense_call(
        _merge_body, "merge", TILE_M_MERGE,
        [_Rows(*o_r, RET_V, lambda j: 0),
         _Rows(*o_h, HG_V, lambda j: 0),
         _Shared(w_r, (RET_V, n), lambda j: (0, 0), resident=True),
         _Shared(w_h, (HG_V, n), lambda j: (0, 0), resident=True),
         _Rows(*proj, n, lambda j: OFF_GA // n),
         _Rows(*proj, n, lambda j: OFF_GB // n)],
        [_Out(n, n, lambda j: 0, BF16)], [], 1)[0]


def _resid_body(x_ref, m_ref, w_ref, g_ref, o_ref, h_ref):
    x1 = x_ref[...] + _dot(m_ref[...], w_ref[...])
    o_ref[...] = x1
    h_ref[...] = _rms_rows(x1, g_ref[...]).astype(BF16)


def _resid(x, mrg, w, g):
    d = w.shape[1]
    return _dense_call(
        _resid_body, "resid", TILE_M_RESID,
        [_Rows(*x, d, lambda j: 0),
         _Rows(*mrg, d, lambda j: 0),
         _Shared(w, (d, d), lambda j: (0, 0), resident=True),
         _Shared(g, (1, d), lambda j: (0, 0))],
        [_Out(d, d, lambda j: 0, F32), _Out(d, d, lambda j: 0, BF16)], [], 1)


def _ffn_in_body(h_ref, wg_ref, wu_ref, o_ref):
    wg = wg_ref[...].astype(BF16)
    wu = wu_ref[...].astype(BF16)
    rows = h_ref.shape[0]
    step = min(rows, FFN_IN_ROW_CHUNK)
    for r in range(0, rows, step):
        h = h_ref[r:r + step, :]
        o_ref[r:r + step, :] = (_silu(_dot(h, wg)) * _dot(h, wu)).astype(BF16)


def _ffn_in(h, w, w_down):
    d = w.shape[0]
    tn = TILE_N_FFN_IN
    return _dense_call(
        _ffn_in_body, "ffn_in", TILE_M_FFN_IN,
        [_Rows(*h, d, lambda j: 0),
         _Shared(w, (d, tn), lambda j: (0, j)),
         _Shared(w, (d, tn), lambda j: (0, D_FF // tn + j))],
        [_Out(D_FF, tn, lambda j: j, BF16)], [], D_FF // tn, casts=[(w_down, CAST_ROWS_FFN)])


def _ffn_out_body(a_ref, w_ref, x_ref, g_ref, o_ref):
    o_ref[...] = _rms_rows(x_ref[...] + _dot(a_ref[...], w_ref[...]), g_ref[...])


def _ffn_out(act, w, x, g):
    k, d = w.shape
    return _dense_call(
        _ffn_out_body, "ffn_out", TILE_M_FFN_OUT,
        [_Rows(*act, k, lambda j: 0),
         _Shared(w, (k, d), lambda j: (0, 0), resident=True),
         _Rows(*x, d, lambda j: 0),
         _Shared(g, (1, d), lambda j: (0, 0))],
        [_Out(d, d, lambda j: 0, F32)], [], 1)[0]


def _rotary(x, cos, sin_signed):
    return x * cos + pltpu.roll(x, RET_DK // 2, 1) * sin_signed


def _ret_parts(p_ref, cos_ref, sin_ref, o_ref, so_ref, s_ref, dec_ref):
    c = CHUNK

    def init():
        s_ref[...] = jnp.zeros_like(s_ref)
        row = lax.broadcasted_iota(jnp.int32, (c, c), 0)
        col = lax.broadcasted_iota(jnp.int32, (c, c), 1)
        diff = (row - col).astype(F32)
        idx = lax.broadcasted_iota(jnp.int32, (c, RET_DK), 0).astype(F32)
        for h in range(RET_HEADS):
            lg = LOG_GAMMA[h]
            dec_ref[h, 0] = jnp.where(diff >= 0, jnp.exp(jnp.maximum(diff, 0.0) * lg), 0.0)
            dec_ref[h, 1] = jnp.exp((idx + 1.0) * lg)
            dec_ref[h, 2] = jnp.exp((c - 1.0 - idx) * lg)

    def step():
        cos = cos_ref[...]
        sin = sin_ref[...]
        for h in range(RET_HEADS):
            q = _rotary(p_ref[0, :, OFF_RQ + h * RET_DK:OFF_RQ + (h + 1) * RET_DK], cos, sin)
            k = _rotary(p_ref[0, :, OFF_RK + h * RET_DK:OFF_RK + (h + 1) * RET_DK], cos, sin) * (RET_DK ** -0.5)
            v = p_ref[0, :, OFF_RV + h * RET_DV:OFF_RV + (h + 1) * RET_DV].astype(BF16)
            chunk_decay = math.exp(c * LOG_GAMMA[h])
            s = s_ref[h]
            scores = _dot_nt(q.astype(BF16), k.astype(BF16)) * dec_ref[h, 0]
            o = _dot(scores.astype(BF16), v) + _dot((q * dec_ref[h, 1]).astype(BF16), s.astype(BF16))
            s_ref[h] = chunk_decay * s + _dot_tn((k * dec_ref[h, 2]).astype(BF16), v)
            o = o * lax.rsqrt(jnp.mean(o * o, axis=-1, keepdims=True) + EPS)
            gate = p_ref[0, :, OFF_RG + h * RET_DV:OFF_RG + (h + 1) * RET_DV]
            o_ref[0, :, h * RET_DV:(h + 1) * RET_DV] = (o * _silu(gate)).astype(BF16)

    def final():
        so_ref[0] = s_ref[...]

    return init, step, final


def _lower_bound(lbl):
    mx = jnp.max(lbl, axis=0, keepdims=True)
    e = jnp.exp(lbl - mx)
    return e[0:1, :] / jnp.sum(e, axis=0, keepdims=True)


def _hgrn_gates(z, lb):
    half = 0.5 * (1.0 - lb)
    ht = half * jnp.tanh(0.5 * z)
    return (lb + half) + ht, half - ht


def _cumsum_rows(tri, x):
    hi = x.astype(BF16)
    r1 = x - hi.astype(F32)
    mid = r1.astype(BF16)
    lo = (r1 - mid.astype(F32)).astype(BF16)
    return _dot(tri, hi) + _dot(tri, mid) + _dot(tri, lo)


def _bcast_rows(x, group, r):
    c, d = x.shape
    x3 = x.reshape(c // group, group, d)
    return jnp.broadcast_to(x3[:, r:r + 1, :], x3.shape).reshape(c, d)


def _hgrn_parts(qf_ref, ig_ref, lbl_ref, nrm_ref, o_ref, so_ref, st_ref):
    def init():
        st_ref[...] = jnp.zeros_like(st_ref)

    def final():
        for h in range(HG_HEADS):
            so_ref[0, h] = st_ref[h].T

    return init, functools.partial(_hgrn_step, qf_ref, ig_ref, lbl_ref, nrm_ref, o_ref, st_ref), final


def _hgrn_step(qf_ref, ig_ref, lbl_ref, nrm_ref, o_ref, st_ref):
    c = CHUNK
    row = lax.broadcasted_iota(jnp.int32, (c, c), 0)
    col = lax.broadcasted_iota(jnp.int32, (c, c), 1)
    rowd = lax.broadcasted_iota(jnp.int32, (c, HG_DK), 0)
    tri = jnp.where(row >= col, 1.0, 0.0).astype(BF16)
    r2, r4 = rowd & 1, rowd & 3
    block_masks = [(row >> s) == (col >> s) for s in range(c.bit_length())]
    nrm = nrm_ref[...]

    def up(x, k):
        return pltpu.roll(x, k, 0)

    def dn(x, k):
        return pltpu.roll(x, c - k, 0)

    def head(h):
        off = h * HG_DK
        lb = _lower_bound(lbl_ref[:, pl.ds(off, HG_DK)])
        f, kin = _hgrn_gates(qf_ref[0, :, pl.ds(HG_K + off, HG_DK)], lb)
        qh = _silu(qf_ref[0, :, pl.ds(off, HG_DK)])
        v = ig_ref[0, :, pl.ds(off, HG_DV)].astype(BF16)
        b = _cumsum_rows(tri, jnp.log2(f))

        f_up = f * up(f, 1)
        f_dn = f * dn(f, 1)
        pre2 = jnp.where(r2 == 1, f_up, f)
        pre4 = jnp.where(r4 >= 1, f_up, f)
        pre4 = jnp.where(r4 >= 2, pre4 * up(pre4, 2), pre4)
        suf4 = jnp.where(r4 <= 2, f_dn, f)
        suf4 = jnp.where(r4 <= 1, suf4 * dn(suf4, 2), suf4)

        def level(s):
            if s == 0:
                return qh, kin
            if s == 1:
                return qh * f, kin
            if s == 2:
                return qh * pre2, kin * jnp.where(r4 == 0, dn(f, 1), 1.0)
            if s == 3:
                return qh * pre4, kin * jnp.where(r4 == 3, 1.0, dn(suf4, 1))
            half = 1 << (s - 1)
            ref = _bcast_rows(b, 2 * half, half - 1)
            upper = (rowd & (2 * half - 1)) >= half
            return jnp.where(upper, qh * jnp.exp2(b - ref), 0.0), jnp.where(upper, 0.0, kin * jnp.exp2(ref - b))

        a = jnp.zeros((c, c), F32)
        for s in reversed(range(len(block_masks))):
            q_s, k_s = level(s)
            a = jnp.where(block_masks[s], _dot_nt(q_s.astype(BF16), k_s.astype(BF16)), a)
        a = jnp.where(col <= row, a, 0.0)

        st = st_ref[h]
        b_last = b[c - 1:c, :]
        o = _dot(a.astype(BF16), v) + _dot_nt((qh * jnp.exp2(b)).astype(BF16), st.astype(BF16))
        st_ref[h] = st * jnp.exp2(b_last) + _dot_tn(v, (kin * jnp.exp2(b_last - b)).astype(BF16))
        o = o * lax.rsqrt(jnp.mean(o * o, axis=-1, keepdims=True) + EPS) * nrm
        o_ref[0, :, pl.ds(off, HG_DV)] = (o * _silu(ig_ref[0, :, pl.ds(HG_V + off, HG_DV)])).astype(BF16)

    for h in range(HG_HEADS):
        head(h)


def _mixers_kernel(n_cast, p_ref, cos_ref, sin_ref, qf_ref, ig_ref, lbl_ref, nrm_ref, *refs):
    cast_in, (or_ref, sor_ref, oh_ref, soh_ref), refs = refs[:n_cast], refs[n_cast:n_cast + 4], refs[n_cast + 4:]
    cast_out, (s_ref, dec_ref, st_ref) = refs[:n_cast], refs[n_cast:]
    for src, dst in zip(cast_in, cast_out):
        dst[...] = src[...].astype(BF16)
    n = pl.program_id(1)
    parts = (_ret_parts(p_ref, cos_ref, sin_ref, or_ref, sor_ref, s_ref, dec_ref),
             _hgrn_parts(qf_ref, ig_ref, lbl_ref, nrm_ref, oh_ref, soh_ref, st_ref))

    @pl.when(n == 0)
    def _():
        for init, _, _ in parts:
            init()

    for _, step, _ in parts:
        step()

    @pl.when(n == pl.num_programs(1) - 1)
    def _():
        for _, _, final in parts:
            final()


def _mixers_prompt(proj3, cos, sin, lbl, nrm, later_weights):
    b, t, _ = proj3.shape
    nc = t // CHUNK
    cast_specs = [pl.BlockSpec((w.shape[0] // (b * nc), w.shape[1]), lambda i, n: (i * nc + n, 0))
                  for w in later_weights]
    return pl.pallas_call(
        functools.partial(_mixers_kernel, len(later_weights)),
        grid=(b, nc),
        in_specs=[pl.BlockSpec((1, CHUNK, OFF_HQ), lambda i, n: (i, n, 0)),
                  pl.BlockSpec((CHUNK, RET_DK), lambda i, n: (n, 0)),
                  pl.BlockSpec((CHUNK, RET_DK), lambda i, n: (n, 0)),
                  pl.BlockSpec((1, CHUNK, 2 * HG_K), lambda i, n: (i, n, OFF_HQ // (2 * HG_K))),
                  pl.BlockSpec((1, CHUNK, 2 * HG_V), lambda i, n: (i, n, OFF_HI // (2 * HG_V))),
                  pl.BlockSpec((DEPTH + 1, HG_K), lambda i, n: (0, 0)),
                  pl.BlockSpec((1, HG_DV), lambda i, n: (0, 0))] + cast_specs,
        out_specs=[pl.BlockSpec((1, CHUNK, RET_V), lambda i, n: (i, n, 0)),
                   pl.BlockSpec((1, RET_HEADS, RET_DK, RET_DV), lambda i, n: (i, 0, 0, 0)),
                   pl.BlockSpec((1, CHUNK, HG_V), lambda i, n: (i, n, 0)),
                   pl.BlockSpec((1, HG_HEADS, HG_DK, HG_DV), lambda i, n: (i, 0, 0, 0))] + cast_specs,
        out_shape=[jax.ShapeDtypeStruct((b, t, RET_V), BF16),
                   jax.ShapeDtypeStruct((b, RET_HEADS, RET_DK, RET_DV), F32),
                   jax.ShapeDtypeStruct((b, t, HG_V), BF16),
                   jax.ShapeDtypeStruct((b, HG_HEADS, HG_DK, HG_DV), F32)]
        + [jax.ShapeDtypeStruct(w.shape, BF16) for w in later_weights],
        scratch_shapes=[pltpu.VMEM((RET_HEADS, RET_DK, RET_DV), F32),
                        pltpu.VMEM((RET_HEADS, 3, CHUNK, RET_DK), F32),
                        pltpu.VMEM((HG_HEADS, HG_DV, HG_DK), F32)],
        compiler_params=_params(("arbitrary", "arbitrary")),
        name="mixers_prompt",
    )(proj3, cos, sin, proj3, proj3, lbl, nrm, *later_weights)


def _sample_prep_kernel(p_ref, cos_ref, sin_ref, lbl_ref, qr_ref, kr_ref, qh_ref, f_ref, kin_ref):
    cos = cos_ref[...]
    sin = sin_ref[...]
    for h in range(RET_HEADS):
        sl = slice(h * RET_DK, (h + 1) * RET_DK)
        qr_ref[:, sl] = _rotary(p_ref[:, OFF_RQ + h * RET_DK:OFF_RQ + (h + 1) * RET_DK], cos, sin)
        kr_ref[:, sl] = _rotary(p_ref[:, OFF_RK + h * RET_DK:OFF_RK + (h + 1) * RET_DK], cos, sin) * (RET_DK ** -0.5)
    f, kin = _hgrn_gates(p_ref[:, OFF_HF:OFF_HF + HG_K], _lower_bound(lbl_ref[...]))
    f_ref[...] = f
    kin_ref[...] = kin
    qh_ref[...] = _silu(p_ref[:, OFF_HQ:OFF_HQ + HG_K])


def _sample_prep(proj, cos, sin, lbl):
    m = proj.shape[0]
    shp = jax.ShapeDtypeStruct((m, RET_QK), F32)
    return pl.pallas_call(
        _sample_prep_kernel,
        grid=(1,),
        in_specs=[pl.BlockSpec((m, OFF_GA), lambda i: (0, 0)),
                  pl.BlockSpec((1, RET_DK), lambda i: (0, 0)),
                  pl.BlockSpec((1, RET_DK), lambda i: (0, 0)),
                  pl.BlockSpec((DEPTH + 1, HG_K), lambda i: (0, 0))],
        out_specs=[pl.BlockSpec((m, RET_QK), lambda i: (0, 0))] * 5,
        out_shape=[shp] * 5,
        compiler_params=_params(("arbitrary",)),
        name="sample_prep",
    )(proj, cos, sin, lbl)


def _sample_state_kernel(sr_ref, sh_ref, qr_ref, kr_ref, qh_ref, f_ref, kin_ref, vr_ref, rg_ref, vh_ref, hg_ref,
                         nrm_ref, sro_ref, sho_ref, or_ref, oh_ref):
    lanes = RET_DK
    for t in range(SAMPLE_SEQS_PER_STEP):
        b = pl.program_id(0) * SAMPLE_SEQS_PER_STEP + t
        base = pl.multiple_of((b // SUB) * SUB, SUB)

        def row(ref, sl, b=b, base=base):
            tile = ref[pl.ds(base, SUB), sl]
            sub = lax.broadcasted_iota(jnp.int32, tile.shape, 0)
            return jnp.sum(jnp.where(sub == b % SUB, tile, 0.0), axis=0, keepdims=True)

        def column(ref, h, row=row):
            r = row(ref, slice(h * RET_DK, (h + 1) * RET_DK))
            return jnp.broadcast_to(r, (RET_DK, RET_DK)).T

        for h in range(RET_HEADS):
            kc = column(kr_ref, h)
            qc = column(qr_ref, h)
            gamma = math.exp(LOG_GAMMA[h])
            outs = []
            for half in range(RET_DV // lanes):
                hs = slice(half * lanes, (half + 1) * lanes)
                s = gamma * sr_ref[t, h, :, hs] + kc * row(vr_ref, slice(h * RET_DV + hs.start, h * RET_DV + hs.stop))
                sro_ref[t, h, :, hs] = s
                outs.append(jnp.sum(qc * s, axis=0, keepdims=True))
            o = jnp.concatenate(outs, axis=1)
            o = o * lax.rsqrt(jnp.mean(o * o, axis=-1, keepdims=True) + EPS)
            sl = slice(h * RET_DV, (h + 1) * RET_DV)
            or_ref[t, :, sl] = o * _silu(row(rg_ref, sl))
        for h in range(HG_HEADS):
            sl = slice(h * HG_DV, (h + 1) * HG_DV)
            s = column(f_ref, h) * sh_ref[t, h] + column(kin_ref, h) * row(vh_ref, sl)
            sho_ref[t, h] = s
            o = jnp.sum(column(qh_ref, h) * s, axis=0, keepdims=True)
            o = o * lax.rsqrt(jnp.mean(o * o, axis=-1, keepdims=True) + EPS) * nrm_ref[...]
            oh_ref[t, :, sl] = o * _silu(row(hg_ref, sl))


def _sample_state(s_ret, s_hg, rows, proj, nrm):
    b = s_ret.shape[0]
    nb = SAMPLE_SEQS_PER_STEP
    row_spec = pl.BlockSpec((b, RET_QK), lambda i: (0, 0))
    return pl.pallas_call(
        _sample_state_kernel,
        grid=(b // nb,),
        in_specs=[pl.BlockSpec((nb, RET_HEADS, RET_DK, RET_DV), lambda i: (i, 0, 0, 0)),
                  pl.BlockSpec((nb, HG_HEADS, HG_DK, HG_DV), lambda i: (i, 0, 0, 0)),
                  row_spec, row_spec, row_spec, row_spec, row_spec,
                  pl.BlockSpec((b, RET_V), lambda i: (0, OFF_RV // RET_V)),
                  pl.BlockSpec((b, RET_V), lambda i: (0, OFF_RG // RET_V)),
                  pl.BlockSpec((b, HG_V), lambda i: (0, OFF_HI // HG_V)),
                  pl.BlockSpec((b, HG_V), lambda i: (0, OFF_HG // HG_V)),
                  pl.BlockSpec((1, HG_DV), lambda i: (0, 0))],
        out_specs=[pl.BlockSpec((nb, RET_HEADS, RET_DK, RET_DV), lambda i: (i, 0, 0, 0)),
                   pl.BlockSpec((nb, HG_HEADS, HG_DK, HG_DV), lambda i: (i, 0, 0, 0)),
                   pl.BlockSpec((nb, 1, RET_V), lambda i: (i, 0, 0)),
                   pl.BlockSpec((nb, 1, HG_V), lambda i: (i, 0, 0))],
        out_shape=[jax.ShapeDtypeStruct(s_ret.shape, F32),
                   jax.ShapeDtypeStruct(s_hg.shape, F32),
                   jax.ShapeDtypeStruct((b, 1, RET_V), F32),
                   jax.ShapeDtypeStruct((b, 1, HG_V), F32)],
        compiler_params=_params(("arbitrary",)),
        name="sample_state",
    )(s_ret, s_hg, *rows, proj, proj, proj, proj, nrm)


def _rope_tables(first, count):
    half = RET_DK // 2
    inv_freq = ROPE_BASE ** (-np.arange(half, dtype=np.float64) / half)
    ang = np.arange(first, first + count, dtype=np.float64)[:, None] * inv_freq[None, :]
    cos, sin = np.cos(ang), np.sin(ang)
    return (jnp.asarray(np.concatenate([cos, cos], axis=-1), F32),
            jnp.asarray(np.concatenate([-sin, sin], axis=-1), F32))


def kernel(x_prompt, x_sample, state_ret, state_hgrn, w_in, w_ret_out, w_hgrn_out, w_out, norm_mix, norm_ffn,
           hgrn_norm, hgrn_lb_logits, w_ffn_in, w_ffn_out, norm_final):
    assert w_in.shape[0] == DEPTH == 1 and x_sample.shape[1] == 1
    bp, t, d = x_prompt.shape
    bs = x_sample.shape[0]
    nrm = hgrn_norm[0][None, :]
    lbl = hgrn_lb_logits.astype(F32)
    x = (x_prompt.reshape(bp * t, d), x_sample.reshape(bs, d))

    proj = _proj(x, norm_mix[0][None, :], w_in[0])

    proj_p3 = proj[0].reshape(bp, t, IN_TOTAL)
    cos_p, sin_p = _rope_tables(0, t)
    o_r, s_ret_p, o_h, s_hg_p, wb_ret, wb_hgrn, wb_out = _mixers_prompt(
        proj_p3, cos_p, sin_p, lbl, nrm, [w_ret_out[0], w_hgrn_out[0], w_out[0]])

    cos_s, sin_s = _rope_tables(PAST_LEN, x_sample.shape[1])
    rows = _sample_prep(proj[1], cos_s, sin_s, lbl)
    s_ret_s, s_hg_s, o_rs, o_hs = _sample_state(state_ret[0], state_hgrn[0], rows, proj[1], nrm)

    o_r = (o_r.reshape(bp * t, RET_V), o_rs.reshape(bs, RET_V).astype(BF16))
    o_h = (o_h.reshape(bp * t, HG_V), o_hs.reshape(bs, HG_V).astype(BF16))
    mrg = _merge(o_r, o_h, wb_ret, wb_hgrn, proj)
    x1, h2 = _resid(x, mrg, wb_out, norm_ffn[0][None, :])
    act, wb_down = _ffn_in(h2, w_ffn_in[0], w_ffn_out[0])
    y_p, y_s = _ffn_out(act, wb_down, x1, norm_final[None, :])

    return (y_p.reshape(bp, t, d), y_s.reshape(bs, 1, d), s_ret_p[None], s_hg_p[None], s_ret_s[None], s_hg_s[None])
```

```python
import functools
import math
from typing import Any, Callable, NamedTuple

import numpy as np
import jax
import jax.numpy as jnp
from jax import lax
from jax.experimental import pallas as pl
from jax.experimental.pallas import tpu as pltpu

D_MODEL = 2048
DEPTH = 1
PAST_LEN = 16384
RET_HEADS = 8
RET_DK = 128
RET_DV = 256
RET_QK = RET_HEADS * RET_DK
RET_V = RET_HEADS * RET_DV
ROPE_BASE = 10000.0
HG_HEADS = 8
HG_DK = 128
HG_DV = 128
HG_K = HG_HEADS * HG_DK
HG_V = HG_HEADS * HG_DV
D_FF = 5632
EPS = 1e-6

OFF_RQ = 0
OFF_RK = OFF_RQ + RET_QK
OFF_RV = OFF_RK + RET_QK
OFF_RG = OFF_RV + RET_V
OFF_HQ = OFF_RG + RET_V
OFF_HF = OFF_HQ + HG_K
OFF_HI = OFF_HF + HG_K
OFF_HG = OFF_HI + HG_V
OFF_GA = OFF_HG + HG_V
OFF_GB = OFF_GA + D_MODEL
IN_TOTAL = OFF_GB + D_MODEL

CHUNK = 128
SUB = 8
SAMPLE_SEQS_PER_STEP = 4
V7X_VMEM_LIMIT = 58 * 1024 * 1024

F32 = jnp.float32
BF16 = jnp.bfloat16

LOG_GAMMA = [float(np.log(np.float32(1.0) - np.float32(2.0) ** np.float32(-5.0 - h))) for h in range(RET_HEADS)]


def _params(sem, vmem=V7X_VMEM_LIMIT):
    return pltpu.CompilerParams(dimension_semantics=sem, vmem_limit_bytes=vmem)


def _rms_rows(x, g):
    return (x * lax.rsqrt(jnp.mean(x * x, axis=-1, keepdims=True) + EPS)) * g


def _sigmoid(x):
    return 0.5 + 0.5 * jnp.tanh(0.5 * x)


def _silu(x):
    hx = 0.5 * x
    return hx + hx * jnp.tanh(hx)


def _dot(a, b):
    return jnp.dot(a, b, preferred_element_type=F32)


def _dot_nt(a, b):
    return lax.dot_general(a, b, (((1,), (1,)), ((), ())), preferred_element_type=F32)


def _dot_tn(a, b):
    return lax.dot_general(a, b, (((0,), (0,)), ((), ())), preferred_element_type=F32)


TILE_M_PROJ, TILE_N_PROJ = 2048, 512
TILE_M_MERGE = 512
TILE_M_RESID = 512
TILE_M_FFN_IN, TILE_N_FFN_IN = 2048, 512
TILE_M_FFN_OUT = 512
FFN_IN_ROW_CHUNK = 512
CAST_ROWS_FFN = 128


class _Rows(NamedTuple):
    prompt: Any
    sample: Any
    cols: int
    col: Callable
    single: bool = False


class _Shared(NamedTuple):
    array: Any
    block: tuple
    index: Callable
    resident: bool = False


class _Out(NamedTuple):
    total_cols: int
    cols: int
    col: Callable
    dtype: Any


def _dense_call(body, name, tile_m, operands, outs, scratch, inner, casts=(), joint_body=None):
    rows = [op for op in operands if isinstance(op, _Rows)]
    m_p, m_s = rows[0].prompt.shape[0], rows[0].sample.shape[0]
    last = m_p // tile_m - 1

    def frozen(i, j):
        return jnp.where(i == last, j, 0)

    specs_p, specs_s, args_p, args_s = [], [], [], []
    for op in operands:
        if isinstance(op, _Rows):
            mode = dict(pipeline_mode=pl.Buffered(1)) if op.single else {}
            specs_p.append(pl.BlockSpec((tile_m, op.cols), lambda i, j, op=op: (i, op.col(j)), **mode))
            specs_s.append(pl.BlockSpec((m_s, op.cols), lambda i, j, op=op: (0, op.col(frozen(i, j)))))
            args_p.append(op.prompt)
            args_s.append(op.sample)
        else:
            mode = dict(pipeline_mode=pl.Buffered(1)) if op.resident else {}
            specs_p.append(pl.BlockSpec(op.block, lambda i, j, op=op: op.index(j), **mode))
            args_p.append(op.array)
    out_specs = ([pl.BlockSpec((tile_m, o.cols), lambda i, j, o=o: (i, o.col(j))) for o in outs]
                 + [pl.BlockSpec((m_s, o.cols), lambda i, j, o=o: (0, o.col(frozen(i, j)))) for o in outs])
    out_shape = ([jax.ShapeDtypeStruct((m_p, o.total_cols), o.dtype) for o in outs]
                 + [jax.ShapeDtypeStruct((m_s, o.total_cols), o.dtype) for o in outs])
    if joint_body is None:
        scratch_shapes = ([pltpu.VMEM((tile_m, c), dt) for c, dt in scratch]
                          + [pltpu.VMEM((m_s, c), dt) for c, dt in scratch])
    else:
        scratch_shapes = [pltpu.VMEM((tile_m + m_s, c), dt) for c, dt in scratch]
    n_in, n_out, n_scr = len(operands), len(outs), len(scratch)
    row_pos = [k for k, op in enumerate(operands) if isinstance(op, _Rows)]
    n_rows, n_cast = len(row_pos), len(casts)
    cast_blocks = [w.shape[0] // rb for w, rb in casts]
    assert all(nb <= (m_p // tile_m) * inner for nb in cast_blocks)
    cast_specs = [pl.BlockSpec((rb, w.shape[1]), lambda i, j, nb=nb: (jnp.minimum(i * inner + j, nb - 1), 0))
                  for (w, rb), nb in zip(casts, cast_blocks)]

    def kernel(*refs):
        ins_p = list(refs[:n_in])
        ins_s = list(ins_p)
        for k, ref in zip(row_pos, refs[n_in:n_in + len(row_pos)]):
            ins_s[k] = ref
        r = n_in + n_rows
        cast_in = refs[r:r + n_cast]
        r += n_cast
        outs_p, outs_s = refs[r:r + n_out], refs[r + n_out:r + 2 * n_out]
        r += 2 * n_out
        cast_out = refs[r:r + n_cast]
        r += n_cast
        is_last = pl.program_id(0) == last
        if joint_body is None:
            body(*ins_p, *outs_p, *refs[r:r + n_scr])

            @pl.when(is_last)
            def _():
                body(*ins_s, *outs_s, *refs[r + n_scr:])
        else:
            scr = refs[r:r + n_scr]

            @pl.when(jnp.logical_not(is_last))
            def _():
                body(*ins_p, *outs_p, *[sc.at[pl.ds(0, tile_m)] for sc in scr])

            @pl.when(is_last)
            def _():
                joint_body(ins_p, refs[n_in:n_in + n_rows], outs_p, outs_s, scr)

        step = pl.program_id(0) * inner + pl.program_id(1)
        for src, dst, nb in zip(cast_in, cast_out, cast_blocks):
            @pl.when(step < nb)
            def _(src=src, dst=dst):
                dst[...] = src[...].astype(BF16)

    res = pl.pallas_call(
        kernel,
        grid=(m_p // tile_m, inner),
        in_specs=specs_p + specs_s + cast_specs,
        out_specs=out_specs + cast_specs,
        out_shape=out_shape + [jax.ShapeDtypeStruct(w.shape, BF16) for w, _ in casts],
        scratch_shapes=scratch_shapes,
        compiler_params=_params(("arbitrary", "arbitrary")),
        name=name,
    )(*args_p, *args_s, *[w for w, _ in casts])
    return list(zip(res[:n_out], res[n_out:2 * n_out])) + list(res[2 * n_out:])


def _proj_body(x_ref, g_ref, w_ref, o_ref, h_ref):
    @pl.when(pl.program_id(1) == 0)
    def _():
        h_ref[...] = _rms_rows(x_ref[...], g_ref[...]).astype(BF16)
    o_ref[...] = _dot(h_ref[...], w_ref[...].astype(BF16))


def _proj_joint_body(ins, sample_rows, outs_p, outs_s, scratch):
    (x_ref, g_ref, w_ref), (xs_ref,), (o_ref,), (os_ref,), (h_ref,) = ins, sample_rows, outs_p, outs_s, scratch
    tm = x_ref.shape[0]

    @pl.when(pl.program_id(1) == 0)
    def _():
        h_ref[:tm, :] = _rms_rows(x_ref[...], g_ref[...]).astype(BF16)
        h_ref[tm:, :] = _rms_rows(xs_ref[...], g_ref[...]).astype(BF16)
    res = _dot(h_ref[...], w_ref[...].astype(BF16))
    o_ref[...] = res[:tm]
    os_ref[...] = res[tm:]


def _proj(x, g, w):
    d, n = w.shape
    tn = TILE_N_PROJ
    return _dense_call(
        _proj_body, "proj", TILE_M_PROJ,
        [_Rows(*x, d, lambda j: 0, single=True),
         _Shared(g, (1, d), lambda j: (0, 0)),
         _Shared(w, (d, tn), lambda j: (0, j))],
        [_Out(n, tn, lambda j: j, F32)], [(d, BF16)], n // tn, joint_body=_proj_joint_body)[0]


def _merge_body(or_ref, oh_ref, wr_ref, wh_ref, ga_ref, gb_ref, o_ref):
    ya = _dot(or_ref[...], wr_ref[...])
    yb = _dot(oh_ref[...], wh_ref[...])
    o_ref[...] = (_sigmoid(ga_ref[...]) * ya + _sigmoid(gb_ref[...]) * yb).astype(BF16)


def _merge(o_r, o_h, w_r, w_h, proj):
    n = w_r.shape[1]
    return _dense_call(
        _merge_body, "merge", TILE_M_MERGE,
        [_Rows(*o_r, RET_V, lambda j: 0),
         _Rows(*o_h, HG_V, lambda j: 0),
         _Shared(w_r, (RET_V, n), lambda j: (0, 0), resident=True),
         _Shared(w_h, (HG_V, n), lambda j: (0, 0), resident=True),
         _Rows(*proj, n, lambda j: OFF_GA // n),
         _Rows(*proj, n, lambda j: OFF_GB // n)],
        [_Out(n, n, lambda j: 0, BF16)], [], 1)[0]


def _resid_body(x_ref, m_ref, w_ref, g_ref, o_ref, h_ref):
    x1 = x_ref[...] + _dot(m_ref[...], w_ref[...])
    o_ref[...] = x1
    h_ref[...] = _rms_rows(x1, g_ref[...]).astype(BF16)


def _resid(x, mrg, w, g):
    d = w.shape[1]
    return _dense_call(
        _resid_body, "resid", TILE_M_RESID,
        [_Rows(*x, d, lambda j: 0),
         _Rows(*mrg, d, lambda j: 0),
         _Shared(w, (d, d), lambda j: (0, 0), resident=True),
         _Shared(g, (1, d), lambda j: (0, 0))],
        [_Out(d, d, lambda j: 0, F32), _Out(d, d, lambda j: 0, BF16)], [], 1)


def _ffn_in_body(h_ref, wg_ref, wu_ref, o_ref):
    wg = wg_ref[...].astype(BF16)
    wu = wu_ref[...].astype(BF16)
    rows = h_ref.shape[0]
    step = min(rows, FFN_IN_ROW_CHUNK)
    for r in range(0, rows, step):
        h = h_ref[r:r + step, :]
        o_ref[r:r + step, :] = (_silu(_dot(h, wg)) * _dot(h, wu)).astype(BF16)


def _ffn_in(h, w, w_down):
    d = w.shape[0]
    tn = TILE_N_FFN_IN
    return _dense_call(
        _ffn_in_body, "ffn_in", TILE_M_FFN_IN,
        [_Rows(*h, d, lambda j: 0),
         _Shared(w, (d, tn), lambda j: (0, j)),
         _Shared(w, (d, tn), lambda j: (0, D_FF // tn + j))],
        [_Out(D_FF, tn, lambda j: j, BF16)], [], D_FF // tn, casts=[(w_down, CAST_ROWS_FFN)])


def _ffn_out_body(a_ref, w_ref, x_ref, g_ref, o_ref):
    o_ref[...] = _rms_rows(x_ref[...] + _dot(a_ref[...], w_ref[...]), g_ref[...])


def _ffn_out(act, w, x, g):
    k, d = w.shape
    return _dense_call(
        _ffn_out_body, "ffn_out", TILE_M_FFN_OUT,
        [_Rows(*act, k, lambda j: 0),
         _Shared(w, (k, d), lambda j: (0, 0), resident=True),
         _Rows(*x, d, lambda j: 0),
         _Shared(g, (1, d), lambda j: (0, 0))],
        [_Out(d, d, lambda j: 0, F32)], [], 1)[0]


def _rotary(x, cos, sin_signed):
    return x * cos + pltpu.roll(x, RET_DK // 2, 1) * sin_signed


def _ret_parts(p_ref, cos_ref, sin_ref, o_ref, so_ref, s_ref, dec_ref):
    c = CHUNK

    def init():
        s_ref[...] = jnp.zeros_like(s_ref)
        row = lax.broadcasted_iota(jnp.int32, (c, c), 0)
        col = lax.broadcasted_iota(jnp.int32, (c, c), 1)
        diff = (row - col).astype(F32)
        idx = lax.broadcasted_iota(jnp.int32, (c, RET_DK), 0).astype(F32)
        for h in range(RET_HEADS):
            lg = LOG_GAMMA[h]
            dec_ref[h, 0] = jnp.where(diff >= 0, jnp.exp(jnp.maximum(diff, 0.0) * lg), 0.0)
            dec_ref[h, 1] = jnp.exp((idx + 1.0) * lg)
            dec_ref[h, 2] = jnp.exp((c - 1.0 - idx) * lg)

    def step(heads):
        cos = cos_ref[...]
        sin = sin_ref[...]
        for h in heads:
            q = _rotary(p_ref[0, :, OFF_RQ + h * RET_DK:OFF_RQ + (h + 1) * RET_DK], cos, sin)
            k = _rotary(p_ref[0, :, OFF_RK + h * RET_DK:OFF_RK + (h + 1) * RET_DK], cos, sin) * (RET_DK ** -0.5)
            v = p_ref[0, :, OFF_RV + h * RET_DV:OFF_RV + (h + 1) * RET_DV].astype(BF16)
            chunk_decay = math.exp(c * LOG_GAMMA[h])
            s = s_ref[h]
            scores = _dot_nt(q.astype(BF16), k.astype(BF16)) * dec_ref[h, 0]
            o = _dot(scores.astype(BF16), v) + _dot((q * dec_ref[h, 1]).astype(BF16), s.astype(BF16))
            s_ref[h] = chunk_decay * s + _dot_tn((k * dec_ref[h, 2]).astype(BF16), v)
            o = o * lax.rsqrt(jnp.mean(o * o, axis=-1, keepdims=True) + EPS)
            gate = p_ref[0, :, OFF_RG + h * RET_DV:OFF_RG + (h + 1) * RET_DV]
            o_ref[0, :, h * RET_DV:(h + 1) * RET_DV] = (o * _silu(gate)).astype(BF16)

    def final():
        so_ref[0] = s_ref[...]

    return init, step, final


def _lower_bound(lbl):
    mx = jnp.max(lbl, axis=0, keepdims=True)
    e = jnp.exp(lbl - mx)
    return e[0:1, :] / jnp.sum(e, axis=0, keepdims=True)


def _hgrn_gates(z, lb):
    half = 0.5 * (1.0 - lb)
    ht = half * jnp.tanh(0.5 * z)
    return (lb + half) + ht, half - ht


def _cumsum_rows(tri, x):
    hi = x.astype(BF16)
    r1 = x - hi.astype(F32)
    mid = r1.astype(BF16)
    lo = (r1 - mid.astype(F32)).astype(BF16)
    return _dot(tri, hi) + _dot(tri, mid) + _dot(tri, lo)


def _bcast_rows(x, group, r):
    c, d = x.shape
    x3 = x.reshape(c // group, group, d)
    return jnp.broadcast_to(x3[:, r:r + 1, :], x3.shape).reshape(c, d)


def _hgrn_parts(qf_ref, ig_ref, lbl_ref, nrm_ref, o_ref, so_ref, st_ref):
    def init():
        st_ref[...] = jnp.zeros_like(st_ref)

    def final():
        for h in range(HG_HEADS):
            so_ref[0, h] = st_ref[h].T

    return init, functools.partial(_hgrn_step, qf_ref, ig_ref, lbl_ref, nrm_ref, o_ref, st_ref), final


def _hgrn_step(qf_ref, ig_ref, lbl_ref, nrm_ref, o_ref, st_ref, heads):
    c = CHUNK
    row = lax.broadcasted_iota(jnp.int32, (c, c), 0)
    col = lax.broadcasted_iota(jnp.int32, (c, c), 1)
    rowd = lax.broadcasted_iota(jnp.int32, (c, HG_DK), 0)
    tri = jnp.where(row >= col, 1.0, 0.0).astype(BF16)
    r2, r4 = rowd & 1, rowd & 3
    block_masks = [(row >> s) == (col >> s) for s in range(c.bit_length())]
    nrm = nrm_ref[...]

    def up(x, k):
        return pltpu.roll(x, k, 0)

    def dn(x, k):
        return pltpu.roll(x, c - k, 0)

    def head(h):
        off = h * HG_DK
        lb = _lower_bound(lbl_ref[:, pl.ds(off, HG_DK)])
        f, kin = _hgrn_gates(qf_ref[0, :, pl.ds(HG_K + off, HG_DK)], lb)
        qh = _silu(qf_ref[0, :, pl.ds(off, HG_DK)])
        v = ig_ref[0, :, pl.ds(off, HG_DV)].astype(BF16)
        b = _cumsum_rows(tri, jnp.log2(f))

        f_up = f * up(f, 1)
        f_dn = f * dn(f, 1)
        pre2 = jnp.where(r2 == 1, f_up, f)
        pre4 = jnp.where(r4 >= 1, f_up, f)
        pre4 = jnp.where(r4 >= 2, pre4 * up(pre4, 2), pre4)
        suf4 = jnp.where(r4 <= 2, f_dn, f)
        suf4 = jnp.where(r4 <= 1, suf4 * dn(suf4, 2), suf4)

        def level(s):
            if s == 0:
                return qh, kin
            if s == 1:
                return qh * f, kin
            if s == 2:
                return qh * pre2, kin * jnp.where(r4 == 0, dn(f, 1), 1.0)
            if s == 3:
                return qh * pre4, kin * jnp.where(r4 == 3, 1.0, dn(suf4, 1))
            half = 1 << (s - 1)
            ref = _bcast_rows(b, 2 * half, half - 1)
            upper = (rowd & (2 * half - 1)) >= half
            return jnp.where(upper, qh * jnp.exp2(b - ref), 0.0), jnp.where(upper, 0.0, kin * jnp.exp2(ref - b))

        a = jnp.zeros((c, c), F32)
        for s in reversed(range(len(block_masks))):
            q_s, k_s = level(s)
            a = jnp.where(block_masks[s], _dot_nt(q_s.astype(BF16), k_s.astype(BF16)), a)
        a = jnp.where(col <= row, a, 0.0)

        st = st_ref[h]
        b_last = b[c - 1:c, :]
        o = _dot(a.astype(BF16), v) + _dot_nt((qh * jnp.exp2(b)).astype(BF16), st.astype(BF16))
        st_ref[h] = st * jnp.exp2(b_last) + _dot_tn(v, (kin * jnp.exp2(b_last - b)).astype(BF16))
        o = o * lax.rsqrt(jnp.mean(o * o, axis=-1, keepdims=True) + EPS) * nrm
        o_ref[0, :, pl.ds(off, HG_DV)] = (o * _silu(ig_ref[0, :, pl.ds(HG_V + off, HG_DV)])).astype(BF16)

    for h in heads:
        head(h)


def _mixers_kernel(n_cast, p_ref, cos_ref, sin_ref, qf_ref, ig_ref, lbl_ref, nrm_ref, *refs):
    cast_in, (or_ref, sor_ref, oh_ref, soh_ref), refs = refs[:n_cast], refs[n_cast:n_cast + 4], refs[n_cast + 4:]
    cast_out, (s_ref, dec_ref, st_ref) = refs[:n_cast], refs[n_cast:]
    for src, dst in zip(cast_in, cast_out):
        dst[...] = src[...].astype(BF16)
    n = pl.program_id(1)
    parts = (_ret_parts(p_ref, cos_ref, sin_ref, or_ref, sor_ref, s_ref, dec_ref),
             _hgrn_parts(qf_ref, ig_ref, lbl_ref, nrm_ref, oh_ref, soh_ref, st_ref))

    @pl.when(n == 0)
    def _():
        for init, _, _ in parts:
            init()

    for h in range(max(RET_HEADS, HG_HEADS)):
        for (_, step, _), n_heads in zip(parts, (RET_HEADS, HG_HEADS)):
            if h < n_heads:
                step([h])

    @pl.when(n == pl.num_programs(1) - 1)
    def _():
        for _, _, final in parts:
            final()


def _mixers_prompt(proj3, cos, sin, lbl, nrm, later_weights):
    b, t, _ = proj3.shape
    nc = t // CHUNK
    cast_specs = [pl.BlockSpec((w.shape[0] // (b * nc), w.shape[1]), lambda i, n: (i * nc + n, 0))
                  for w in later_weights]
    return pl.pallas_call(
        functools.partial(_mixers_kernel, len(later_weights)),
        grid=(b, nc),
        in_specs=[pl.BlockSpec((1, CHUNK, OFF_HQ), lambda i, n: (i, n, 0)),
                  pl.BlockSpec((CHUNK, RET_DK), lambda i, n: (n, 0)),
                  pl.BlockSpec((CHUNK, RET_DK), lambda i, n: (n, 0)),
                  pl.BlockSpec((1, CHUNK, 2 * HG_K), lambda i, n: (i, n, OFF_HQ // (2 * HG_K))),
                  pl.BlockSpec((1, CHUNK, 2 * HG_V), lambda i, n: (i, n, OFF_HI // (2 * HG_V))),
                  pl.BlockSpec((DEPTH + 1, HG_K), lambda i, n: (0, 0)),
                  pl.BlockSpec((1, HG_DV), lambda i, n: (0, 0))] + cast_specs,
        out_specs=[pl.BlockSpec((1, CHUNK, RET_V), lambda i, n: (i, n, 0)),
                   pl.BlockSpec((1, RET_HEADS, RET_DK, RET_DV), lambda i, n: (i, 0, 0, 0)),
                   pl.BlockSpec((1, CHUNK, HG_V), lambda i, n: (i, n, 0)),
                   pl.BlockSpec((1, HG_HEADS, HG_DK, HG_DV), lambda i, n: (i, 0, 0, 0))] + cast_specs,
        out_shape=[jax.ShapeDtypeStruct((b, t, RET_V), BF16),
                   jax.ShapeDtypeStruct((b, RET_HEADS, RET_DK, RET_DV), F32),
                   jax.ShapeDtypeStruct((b, t, HG_V), BF16),
                   jax.ShapeDtypeStruct((b, HG_HEADS, HG_DK, HG_DV), F32)]
        + [jax.ShapeDtypeStruct(w.shape, BF16) for w in later_weights],
        scratch_shapes=[pltpu.VMEM((RET_HEADS, RET_DK, RET_DV), F32),
                        pltpu.VMEM((RET_HEADS, 3, CHUNK, RET_DK), F32),
                        pltpu.VMEM((HG_HEADS, HG_DV, HG_DK), F32)],
        compiler_params=_params(("arbitrary", "arbitrary")),
        name="mixers_prompt",
    )(proj3, cos, sin, proj3, proj3, lbl, nrm, *later_weights)


def _sample_prep_kernel(p_ref, cos_ref, sin_ref, lbl_ref, qr_ref, kr_ref, qh_ref, f_ref, kin_ref):
    cos = cos_ref[...]
    sin = sin_ref[...]
    for h in range(RET_HEADS):
        sl = slice(h * RET_DK, (h + 1) * RET_DK)
        qr_ref[:, sl] = _rotary(p_ref[:, OFF_RQ + h * RET_DK:OFF_RQ + (h + 1) * RET_DK], cos, sin)
        kr_ref[:, sl] = _rotary(p_ref[:, OFF_RK + h * RET_DK:OFF_RK + (h + 1) * RET_DK], cos, sin) * (RET_DK ** -0.5)
    f, kin = _hgrn_gates(p_ref[:, OFF_HF:OFF_HF + HG_K], _lower_bound(lbl_ref[...]))
    f_ref[...] = f
    kin_ref[...] = kin
    qh_ref[...] = _silu(p_ref[:, OFF_HQ:OFF_HQ + HG_K])


def _sample_prep(proj, cos, sin, lbl):
    m = proj.shape[0]
    shp = jax.ShapeDtypeStruct((m, RET_QK), F32)
    return pl.pallas_call(
        _sample_prep_kernel,
        grid=(1,),
        in_specs=[pl.BlockSpec((m, OFF_GA), lambda i: (0, 0)),
                  pl.BlockSpec((1, RET_DK), lambda i: (0, 0)),
                  pl.BlockSpec((1, RET_DK), lambda i: (0, 0)),
                  pl.BlockSpec((DEPTH + 1, HG_K), lambda i: (0, 0))],
        out_specs=[pl.BlockSpec((m, RET_QK), lambda i: (0, 0))] * 5,
        out_shape=[shp] * 5,
        compiler_params=_params(("arbitrary",)),
        name="sample_prep",
    )(proj, cos, sin, lbl)


def _sample_state_kernel(sr_ref, sh_ref, qr_ref, kr_ref, qh_ref, f_ref, kin_ref, vr_ref, rg_ref, vh_ref, hg_ref,
                         nrm_ref, sro_ref, sho_ref, or_ref, oh_ref):
    lanes = RET_DK
    for t in range(SAMPLE_SEQS_PER_STEP):
        b = pl.program_id(0) * SAMPLE_SEQS_PER_STEP + t
        base = pl.multiple_of((b // SUB) * SUB, SUB)

        def row(ref, sl, b=b, base=base):
            tile = ref[pl.ds(base, SUB), sl]
            sub = lax.broadcasted_iota(jnp.int32, tile.shape, 0)
            return jnp.sum(jnp.where(sub == b % SUB, tile, 0.0), axis=0, keepdims=True)

        def column(ref, h, row=row):
            r = row(ref, slice(h * RET_DK, (h + 1) * RET_DK))
            return jnp.broadcast_to(r, (RET_DK, RET_DK)).T

        for h in range(RET_HEADS):
            kc = column(kr_ref, h)
            qc = column(qr_ref, h)
            gamma = math.exp(LOG_GAMMA[h])
            outs = []
            for half in range(RET_DV // lanes):
                hs = slice(half * lanes, (half + 1) * lanes)
                s = gamma * sr_ref[t, h, :, hs] + kc * row(vr_ref, slice(h * RET_DV + hs.start, h * RET_DV + hs.stop))
                sro_ref[t, h, :, hs] = s
                outs.append(jnp.sum(qc * s, axis=0, keepdims=True))
            o = jnp.concatenate(outs, axis=1)
            o = o * lax.rsqrt(jnp.mean(o * o, axis=-1, keepdims=True) + EPS)
            sl = slice(h * RET_DV, (h + 1) * RET_DV)
            or_ref[t, :, sl] = o * _silu(row(rg_ref, sl))
        for h in range(HG_HEADS):
            sl = slice(h * HG_DV, (h + 1) * HG_DV)
            s = column(f_ref, h) * sh_ref[t, h] + column(kin_ref, h) * row(vh_ref, sl)
            sho_ref[t, h] = s
            o = jnp.sum(column(qh_ref, h) * s, axis=0, keepdims=True)
            o = o * lax.rsqrt(jnp.mean(o * o, axis=-1, keepdims=True) + EPS) * nrm_ref[...]
            oh_ref[t, :, sl] = o * _silu(row(hg_ref, sl))


def _sample_state(s_ret, s_hg, rows, proj, nrm):
    b = s_ret.shape[0]
    nb = SAMPLE_SEQS_PER_STEP
    row_spec = pl.BlockSpec((b, RET_QK), lambda i: (0, 0))
    return pl.pallas_call(
        _sample_state_kernel,
        grid=(b // nb,),
        in_specs=[pl.BlockSpec((nb, RET_HEADS, RET_DK, RET_DV), lambda i: (i, 0, 0, 0)),
                  pl.BlockSpec((nb, HG_HEADS, HG_DK, HG_DV), lambda i: (i, 0, 0, 0)),
                  row_spec, row_spec, row_spec, row_spec, row_spec,
                  pl.BlockSpec((b, RET_V), lambda i: (0, OFF_RV // RET_V)),
                  pl.BlockSpec((b, RET_V), lambda i: (0, OFF_RG // RET_V)),
                  pl.BlockSpec((b, HG_V), lambda i: (0, OFF_HI // HG_V)),
                  pl.BlockSpec((b, HG_V), lambda i: (0, OFF_HG // HG_V)),
                  pl.BlockSpec((1, HG_DV), lambda i: (0, 0))],
        out_specs=[pl.BlockSpec((nb, RET_HEADS, RET_DK, RET_DV), lambda i: (i, 0, 0, 0)),
                   pl.BlockSpec((nb, HG_HEADS, HG_DK, HG_DV), lambda i: (i, 0, 0, 0)),
                   pl.BlockSpec((nb, 1, RET_V), lambda i: (i, 0, 0)),
                   pl.BlockSpec((nb, 1, HG_V), lambda i: (i, 0, 0))],
        out_shape=[jax.ShapeDtypeStruct(s_ret.shape, F32),
                   jax.ShapeDtypeStruct(s_hg.shape, F32),
                   jax.ShapeDtypeStruct((b, 1, RET_V), F32),
                   jax.ShapeDtypeStruct((b, 1, HG_V), F32)],
        compiler_params=_params(("arbitrary",)),
        name="sample_state",
    )(s_ret, s_hg, *rows, proj, proj, proj, proj, nrm)


def _rope_tables(first, count):
    half = RET_DK // 2
    inv_freq = ROPE_BASE ** (-np.arange(half, dtype=np.float64) / half)
    ang = np.arange(first, first + count, dtype=np.float64)[:, None] * inv_freq[None, :]
    cos, sin = np.cos(ang), np.sin(ang)
    return (jnp.asarray(np.concatenate([cos, cos], axis=-1), F32),
            jnp.asarray(np.concatenate([-sin, sin], axis=-1), F32))


def kernel(x_prompt, x_sample, state_ret, state_hgrn, w_in, w_ret_out, w_hgrn_out, w_out, norm_mix, norm_ffn,
           hgrn_norm, hgrn_lb_logits, w_ffn_in, w_ffn_out, norm_final):
    assert w_in.shape[0] == DEPTH == 1 and x_sample.shape[1] == 1
    bp, t, d = x_prompt.shape
    bs = x_sample.shape[0]
    nrm = hgrn_norm[0][None, :]
    lbl = hgrn_lb_logits.astype(F32)
    x = (x_prompt.reshape(bp * t, d), x_sample.reshape(bs, d))

    proj = _proj(x, norm_mix[0][None, :], w_in[0])

    proj_p3 = proj[0].reshape(bp, t, IN_TOTAL)
    cos_p, sin_p = _rope_tables(0, t)
    o_r, s_ret_p, o_h, s_hg_p, wb_ret, wb_hgrn, wb_out = _mixers_prompt(
        proj_p3, cos_p, sin_p, lbl, nrm, [w_ret_out[0], w_hgrn_out[0], w_out[0]])

    cos_s, sin_s = _rope_tables(PAST_LEN, x_sample.shape[1])
    rows = _sample_prep(proj[1], cos_s, sin_s, lbl)
    s_ret_s, s_hg_s, o_rs, o_hs = _sample_state(state_ret[0], state_hgrn[0], rows, proj[1], nrm)

    o_r = (o_r.reshape(bp * t, RET_V), o_rs.reshape(bs, RET_V).astype(BF16))
    o_h = (o_h.reshape(bp * t, HG_V), o_hs.reshape(bs, HG_V).astype(BF16))
    mrg = _merge(o_r, o_h, wb_ret, wb_hgrn, proj)
    x1, h2 = _resid(x, mrg, wb_out, norm_ffn[0][None, :])
    act, wb_down = _ffn_in(h2, w_ffn_in[0], w_ffn_out[0])
    y_p, y_s = _ffn_out(act, wb_down, x1, norm_final[None, :])

    return (y_p.reshape(bp, t, d), y_s.reshape(bs, 1, d), s_ret_p[None], s_hg_p[None], s_ret_s[None], s_hg_s[None])
```

```python
import functools
import math
from typing import Any, Callable, NamedTuple

import numpy as np
import jax
import jax.numpy as jnp
from jax import lax
from jax.experimental import pallas as pl
from jax.experimental.pallas import tpu as pltpu

D_MODEL = 2048
DEPTH = 1
PAST_LEN = 16384
RET_HEADS = 8
RET_DK = 128
RET_DV = 256
RET_QK = RET_HEADS * RET_DK
RET_V = RET_HEADS * RET_DV
ROPE_BASE = 10000.0
HG_HEADS = 8
HG_DK = 128
HG_DV = 128
HG_K = HG_HEADS * HG_DK
HG_V = HG_HEADS * HG_DV
D_FF = 5632
EPS = 1e-6

OFF_RQ = 0
OFF_RK = OFF_RQ + RET_QK
OFF_RV = OFF_RK + RET_QK
OFF_RG = OFF_RV + RET_V
OFF_HQ = OFF_RG + RET_V
OFF_HF = OFF_HQ + HG_K
OFF_HI = OFF_HF + HG_K
OFF_HG = OFF_HI + HG_V
OFF_GA = OFF_HG + HG_V
OFF_GB = OFF_GA + D_MODEL
IN_TOTAL = OFF_GB + D_MODEL

CHUNK = 128
SUB = 8
MIXER_HEAD_GROUP = 8
SAMPLE_SEQS_PER_STEP = 4
V7X_VMEM_LIMIT = 58 * 1024 * 1024

F32 = jnp.float32
BF16 = jnp.bfloat16

LOG_GAMMA = [float(np.log(np.float32(1.0) - np.float32(2.0) ** np.float32(-5.0 - h))) for h in range(RET_HEADS)]


def _params(sem, vmem=V7X_VMEM_LIMIT):
    return pltpu.CompilerParams(dimension_semantics=sem, vmem_limit_bytes=vmem)


def _rms_rows(x, g):
    return (x * lax.rsqrt(jnp.mean(x * x, axis=-1, keepdims=True) + EPS)) * g


def _sigmoid(x):
    return 0.5 + 0.5 * jnp.tanh(0.5 * x)


def _silu(x):
    hx = 0.5 * x
    return hx + hx * jnp.tanh(hx)


def _dot(a, b):
    return jnp.dot(a, b, preferred_element_type=F32)


def _dot_nt(a, b):
    return lax.dot_general(a, b, (((1,), (1,)), ((), ())), preferred_element_type=F32)


def _dot_tn(a, b):
    return lax.dot_general(a, b, (((0,), (0,)), ((), ())), preferred_element_type=F32)


TILE_M_PROJ, TILE_N_PROJ = 2048, 512
TILE_M_MERGE = 512
TILE_M_RESID = 512
TILE_M_FFN_IN, TILE_N_FFN_IN = 2048, 512
TILE_M_FFN_OUT = 512
FFN_IN_ROW_CHUNK = 512
CAST_ROWS_FFN = 128


class _Rows(NamedTuple):
    prompt: Any
    sample: Any
    cols: int
    col: Callable
    single: bool = False


class _Shared(NamedTuple):
    array: Any
    block: tuple
    index: Callable
    resident: bool = False


class _Out(NamedTuple):
    total_cols: int
    cols: int
    col: Callable
    dtype: Any


def _dense_call(body, name, tile_m, operands, outs, scratch, inner, casts=(), joint_body=None):
    rows = [op for op in operands if isinstance(op, _Rows)]
    m_p, m_s = rows[0].prompt.shape[0], rows[0].sample.shape[0]
    last = m_p // tile_m - 1

    def frozen(i, j):
        return jnp.where(i == last, j, 0)

    specs_p, specs_s, args_p, args_s = [], [], [], []
    for op in operands:
        if isinstance(op, _Rows):
            mode = dict(pipeline_mode=pl.Buffered(1)) if op.single else {}
            specs_p.append(pl.BlockSpec((tile_m, op.cols), lambda i, j, op=op: (i, op.col(j)), **mode))
            specs_s.append(pl.BlockSpec((m_s, op.cols), lambda i, j, op=op: (0, op.col(frozen(i, j)))))
            args_p.append(op.prompt)
            args_s.append(op.sample)
        else:
            mode = dict(pipeline_mode=pl.Buffered(1)) if op.resident else {}
            specs_p.append(pl.BlockSpec(op.block, lambda i, j, op=op: op.index(j), **mode))
            args_p.append(op.array)
    out_specs = ([pl.BlockSpec((tile_m, o.cols), lambda i, j, o=o: (i, o.col(j))) for o in outs]
                 + [pl.BlockSpec((m_s, o.cols), lambda i, j, o=o: (0, o.col(frozen(i, j)))) for o in outs])
    out_shape = ([jax.ShapeDtypeStruct((m_p, o.total_cols), o.dtype) for o in outs]
                 + [jax.ShapeDtypeStruct((m_s, o.total_cols), o.dtype) for o in outs])
    if joint_body is None:
        scratch_shapes = ([pltpu.VMEM((tile_m, c), dt) for c, dt in scratch]
                          + [pltpu.VMEM((m_s, c), dt) for c, dt in scratch])
    else:
        scratch_shapes = [pltpu.VMEM((tile_m + m_s, c), dt) for c, dt in scratch]
    n_in, n_out, n_scr = len(operands), len(outs), len(scratch)
    row_pos = [k for k, op in enumerate(operands) if isinstance(op, _Rows)]
    n_rows, n_cast = len(row_pos), len(casts)
    cast_blocks = [w.shape[0] // rb for w, rb in casts]
    assert all(nb <= (m_p // tile_m) * inner for nb in cast_blocks)
    cast_specs = [pl.BlockSpec((rb, w.shape[1]), lambda i, j, nb=nb: (jnp.minimum(i * inner + j, nb - 1), 0))
                  for (w, rb), nb in zip(casts, cast_blocks)]

    def kernel(*refs):
        ins_p = list(refs[:n_in])
        ins_s = list(ins_p)
        for k, ref in zip(row_pos, refs[n_in:n_in + len(row_pos)]):
            ins_s[k] = ref
        r = n_in + n_rows
        cast_in = refs[r:r + n_cast]
        r += n_cast
        outs_p, outs_s = refs[r:r + n_out], refs[r + n_out:r + 2 * n_out]
        r += 2 * n_out
        cast_out = refs[r:r + n_cast]
        r += n_cast
        is_last = pl.program_id(0) == last
        if joint_body is None:
            body(*ins_p, *outs_p, *refs[r:r + n_scr])

            @pl.when(is_last)
            def _():
                body(*ins_s, *outs_s, *refs[r + n_scr:])
        else:
            scr = refs[r:r + n_scr]

            @pl.when(jnp.logical_not(is_last))
            def _():
                body(*ins_p, *outs_p, *[sc.at[pl.ds(0, tile_m)] for sc in scr])

            @pl.when(is_last)
            def _():
                joint_body(ins_p, refs[n_in:n_in + n_rows], outs_p, outs_s, scr)

        step = pl.program_id(0) * inner + pl.program_id(1)
        for src, dst, nb in zip(cast_in, cast_out, cast_blocks):
            @pl.when(step < nb)
            def _(src=src, dst=dst):
                dst[...] = src[...].astype(BF16)

    res = pl.pallas_call(
        kernel,
        grid=(m_p // tile_m, inner),
        in_specs=specs_p + specs_s + cast_specs,
        out_specs=out_specs + cast_specs,
        out_shape=out_shape + [jax.ShapeDtypeStruct(w.shape, BF16) for w, _ in casts],
        scratch_shapes=scratch_shapes,
        compiler_params=_params(("arbitrary", "arbitrary")),
        name=name,
    )(*args_p, *args_s, *[w for w, _ in casts])
    return list(zip(res[:n_out], res[n_out:2 * n_out])) + list(res[2 * n_out:])


def _proj_body(x_ref, g_ref, w_ref, o_ref, h_ref):
    @pl.when(pl.program_id(1) == 0)
    def _():
        h_ref[...] = _rms_rows(x_ref[...], g_ref[...]).astype(BF16)
    o_ref[...] = _dot(h_ref[...], w_ref[...].astype(BF16))


def _proj_joint_body(ins, sample_rows, outs_p, outs_s, scratch):
    (x_ref, g_ref, w_ref), (xs_ref,), (o_ref,), (os_ref,), (h_ref,) = ins, sample_rows, outs_p, outs_s, scratch
    tm = x_ref.shape[0]

    @pl.when(pl.program_id(1) == 0)
    def _():
        h_ref[:tm, :] = _rms_rows(x_ref[...], g_ref[...]).astype(BF16)
        h_ref[tm:, :] = _rms_rows(xs_ref[...], g_ref[...]).astype(BF16)
    res = _dot(h_ref[...], w_ref[...].astype(BF16))
    o_ref[...] = res[:tm]
    os_ref[...] = res[tm:]


def _proj(x, g, w):
    d, n = w.shape
    tn = TILE_N_PROJ
    return _dense_call(
        _proj_body, "proj", TILE_M_PROJ,
        [_Rows(*x, d, lambda j: 0, single=True),
         _Shared(g, (1, d), lambda j: (0, 0)),
         _Shared(w, (d, tn), lambda j: (0, j))],
        [_Out(n, tn, lambda j: j, F32)], [(d, BF16)], n // tn, joint_body=_proj_joint_body)[0]


def _merge_body(or_ref, oh_ref, wr_ref, wh_ref, ga_ref, gb_ref, o_ref):
    ya = _dot(or_ref[...], wr_ref[...])
    yb = _dot(oh_ref[...], wh_ref[...])
    o_ref[...] = (_sigmoid(ga_ref[...]) * ya + _sigmoid(gb_ref[...]) * yb).astype(BF16)


def _merge(o_r, o_h, w_r, w_h, proj):
    n = w_r.shape[1]
    return _dense_call(
        _merge_body, "merge", TILE_M_MERGE,
        [_Rows(*o_r, RET_V, lambda j: 0),
         _Rows(*o_h, HG_V, lambda j: 0),
         _Shared(w_r, (RET_V, n), lambda j: (0, 0), resident=True),
         _Shared(w_h, (HG_V, n), lambda j: (0, 0), resident=True),
         _Rows(*proj, n, lambda j: OFF_GA // n),
         _Rows(*proj, n, lambda j: OFF_GB // n)],
        [_Out(n, n, lambda j: 0, BF16)], [], 1)[0]


def _resid_body(x_ref, m_ref, w_ref, g_ref, o_ref, h_ref):
    x1 = x_ref[...] + _dot(m_ref[...], w_ref[...])
    o_ref[...] = x1
    h_ref[...] = _rms_rows(x1, g_ref[...]).astype(BF16)


def _resid(x, mrg, w, g):
    d = w.shape[1]
    return _dense_call(
        _resid_body, "resid", TILE_M_RESID,
        [_Rows(*x, d, lambda j: 0),
         _Rows(*mrg, d, lambda j: 0),
         _Shared(w, (d, d), lambda j: (0, 0), resident=True),
         _Shared(g, (1, d), lambda j: (0, 0))],
        [_Out(d, d, lambda j: 0, F32), _Out(d, d, lambda j: 0, BF16)], [], 1)


def _ffn_in_body(h_ref, wg_ref, wu_ref, o_ref):
    wg = wg_ref[...].astype(BF16)
    wu = wu_ref[...].astype(BF16)
    rows = h_ref.shape[0]
    step = min(rows, FFN_IN_ROW_CHUNK)
    for r in range(0, rows, step):
        h = h_ref[r:r + step, :]
        o_ref[r:r + step, :] = (_silu(_dot(h, wg)) * _dot(h, wu)).astype(BF16)


def _ffn_in(h, w, w_down):
    d = w.shape[0]
    tn = TILE_N_FFN_IN
    return _dense_call(
        _ffn_in_body, "ffn_in", TILE_M_FFN_IN,
        [_Rows(*h, d, lambda j: 0),
         _Shared(w, (d, tn), lambda j: (0, j)),
         _Shared(w, (d, tn), lambda j: (0, D_FF // tn + j))],
        [_Out(D_FF, tn, lambda j: j, BF16)], [], D_FF // tn, casts=[(w_down, CAST_ROWS_FFN)])


def _ffn_out_body(a_ref, w_ref, x_ref, g_ref, o_ref):
    o_ref[...] = _rms_rows(x_ref[...] + _dot(a_ref[...], w_ref[...]), g_ref[...])


def _ffn_out(act, w, x, g):
    k, d = w.shape
    return _dense_call(
        _ffn_out_body, "ffn_out", TILE_M_FFN_OUT,
        [_Rows(*act, k, lambda j: 0),
         _Shared(w, (k, d), lambda j: (0, 0), resident=True),
         _Rows(*x, d, lambda j: 0),
         _Shared(g, (1, d), lambda j: (0, 0))],
        [_Out(d, d, lambda j: 0, F32)], [], 1)[0]


def _rotary(x, cos, sin_signed):
    return x * cos + pltpu.roll(x, RET_DK // 2, 1) * sin_signed


def _ret_parts(p_ref, cos_ref, sin_ref, o_ref, so_ref, s_ref, dec_ref):
    c = CHUNK

    def init():
        s_ref[...] = jnp.zeros_like(s_ref)
        row = lax.broadcasted_iota(jnp.int32, (c, c), 0)
        col = lax.broadcasted_iota(jnp.int32, (c, c), 1)
        diff = (row - col).astype(F32)
        idx = lax.broadcasted_iota(jnp.int32, (c, RET_DK), 0).astype(F32)
        for h in range(RET_HEADS):
            lg = LOG_GAMMA[h]
            dec_ref[h, 0] = jnp.where(diff >= 0, jnp.exp(jnp.maximum(diff, 0.0) * lg), 0.0)
            dec_ref[h, 1] = jnp.exp((idx + 1.0) * lg)
            dec_ref[h, 2] = jnp.exp((c - 1.0 - idx) * lg)

    def step(heads):
        cos = cos_ref[...]
        sin = sin_ref[...]
        def rotate(h):
            q = _rotary(p_ref[0, :, OFF_RQ + h * RET_DK:OFF_RQ + (h + 1) * RET_DK], cos, sin)
            k = _rotary(p_ref[0, :, OFF_RK + h * RET_DK:OFF_RK + (h + 1) * RET_DK], cos, sin) * (RET_DK ** -0.5)
            return q, k

        def attend(h, q, k):
            v = p_ref[0, :, OFF_RV + h * RET_DV:OFF_RV + (h + 1) * RET_DV].astype(BF16)
            chunk_decay = math.exp(c * LOG_GAMMA[h])
            s = s_ref[h]
            scores = _dot_nt(q.astype(BF16), k.astype(BF16)) * dec_ref[h, 0]
            o = _dot(scores.astype(BF16), v) + _dot((q * dec_ref[h, 1]).astype(BF16), s.astype(BF16))
            s_ref[h] = chunk_decay * s + _dot_tn((k * dec_ref[h, 2]).astype(BF16), v)
            return o

        def emit(h, o):
            o = o * lax.rsqrt(jnp.mean(o * o, axis=-1, keepdims=True) + EPS)
            gate = p_ref[0, :, OFF_RG + h * RET_DV:OFF_RG + (h + 1) * RET_DV]
            o_ref[0, :, h * RET_DV:(h + 1) * RET_DV] = (o * _silu(gate)).astype(BF16)

        qk = [rotate(h) for h in heads]
        outs = [attend(h, q, k) for h, (q, k) in zip(heads, qk)]
        for h, o in zip(heads, outs):
            emit(h, o)

    def final():
        so_ref[0] = s_ref[...]

    return init, step, final


def _lower_bound(lbl):
    mx = jnp.max(lbl, axis=0, keepdims=True)
    e = jnp.exp(lbl - mx)
    return e[0:1, :] / jnp.sum(e, axis=0, keepdims=True)


def _hgrn_gates(z, lb):
    half = 0.5 * (1.0 - lb)
    ht = half * jnp.tanh(0.5 * z)
    return (lb + half) + ht, half - ht


def _cumsum_rows(tri, x):
    hi = x.astype(BF16)
    r1 = x - hi.astype(F32)
    mid = r1.astype(BF16)
    lo = (r1 - mid.astype(F32)).astype(BF16)
    return _dot(tri, hi) + _dot(tri, mid) + _dot(tri, lo)


def _bcast_rows(x, group, r):
    c, d = x.shape
    x3 = x.reshape(c // group, group, d)
    return jnp.broadcast_to(x3[:, r:r + 1, :], x3.shape).reshape(c, d)


def _hgrn_parts(qf_ref, ig_ref, lbl_ref, nrm_ref, o_ref, so_ref, st_ref):
    def init():
        st_ref[...] = jnp.zeros_like(st_ref)

    def final():
        for h in range(HG_HEADS):
            so_ref[0, h] = st_ref[h].T

    return init, functools.partial(_hgrn_step, qf_ref, ig_ref, lbl_ref, nrm_ref, o_ref, st_ref), final


def _hgrn_step(qf_ref, ig_ref, lbl_ref, nrm_ref, o_ref, st_ref, heads):
    c = CHUNK
    row = lax.broadcasted_iota(jnp.int32, (c, c), 0)
    col = lax.broadcasted_iota(jnp.int32, (c, c), 1)
    rowd = lax.broadcasted_iota(jnp.int32, (c, HG_DK), 0)
    tri = jnp.where(row >= col, 1.0, 0.0).astype(BF16)
    r2, r4 = rowd & 1, rowd & 3
    block_masks = [(row >> s) == (col >> s) for s in range(c.bit_length())]
    nrm = nrm_ref[...]

    def up(x, k):
        return pltpu.roll(x, k, 0)

    def dn(x, k):
        return pltpu.roll(x, c - k, 0)

    def gates(h):
        off = h * HG_DK
        lb = _lower_bound(lbl_ref[:, pl.ds(off, HG_DK)])
        f, kin = _hgrn_gates(qf_ref[0, :, pl.ds(HG_K + off, HG_DK)], lb)
        qh = _silu(qf_ref[0, :, pl.ds(off, HG_DK)])
        v = ig_ref[0, :, pl.ds(off, HG_DV)].astype(BF16)
        b = _cumsum_rows(tri, jnp.log2(f))
        return f, kin, qh, v, b

    def intra(f, kin, qh, b):
        f_up = f * up(f, 1)
        f_dn = f * dn(f, 1)
        pre2 = jnp.where(r2 == 1, f_up, f)
        pre4 = jnp.where(r4 >= 1, f_up, f)
        pre4 = jnp.where(r4 >= 2, pre4 * up(pre4, 2), pre4)
        suf4 = jnp.where(r4 <= 2, f_dn, f)
        suf4 = jnp.where(r4 <= 1, suf4 * dn(suf4, 2), suf4)

        def level(s):
            if s == 0:
                return qh, kin
            if s == 1:
                return qh * f, kin
            if s == 2:
                return qh * pre2, kin * jnp.where(r4 == 0, dn(f, 1), 1.0)
            if s == 3:
                return qh * pre4, kin * jnp.where(r4 == 3, 1.0, dn(suf4, 1))
            half = 1 << (s - 1)
            ref = _bcast_rows(b, 2 * half, half - 1)
            upper = (rowd & (2 * half - 1)) >= half
            return jnp.where(upper, qh * jnp.exp2(b - ref), 0.0), jnp.where(upper, 0.0, kin * jnp.exp2(ref - b))

        a = jnp.zeros((c, c), F32)
        for s in reversed(range(len(block_masks))):
            q_s, k_s = level(s)
            a = jnp.where(block_masks[s], _dot_nt(q_s.astype(BF16), k_s.astype(BF16)), a)
        return jnp.where(col <= row, a, 0.0)

    def finish(h, kin, qh, v, b, a):
        off = h * HG_DK
        st = st_ref[h]
        b_last = b[c - 1:c, :]
        o = _dot(a.astype(BF16), v) + _dot_nt((qh * jnp.exp2(b)).astype(BF16), st.astype(BF16))
        st_ref[h] = st * jnp.exp2(b_last) + _dot_tn(v, (kin * jnp.exp2(b_last - b)).astype(BF16))
        o = o * lax.rsqrt(jnp.mean(o * o, axis=-1, keepdims=True) + EPS) * nrm
        o_ref[0, :, pl.ds(off, HG_DV)] = (o * _silu(ig_ref[0, :, pl.ds(HG_V + off, HG_DV)])).astype(BF16)

    g = [gates(h) for h in heads]
    a = [intra(f, kin, qh, b) for f, kin, qh, v, b in g]
    for h, (f, kin, qh, v, b), a_h in zip(heads, g, a):
        finish(h, kin, qh, v, b, a_h)


def _mixers_kernel(n_cast, p_ref, cos_ref, sin_ref, qf_ref, ig_ref, lbl_ref, nrm_ref, *refs):
    cast_in, (or_ref, sor_ref, oh_ref, soh_ref), refs = refs[:n_cast], refs[n_cast:n_cast + 4], refs[n_cast + 4:]
    cast_out, (s_ref, dec_ref, st_ref) = refs[:n_cast], refs[n_cast:]
    for src, dst in zip(cast_in, cast_out):
        dst[...] = src[...].astype(BF16)
    n = pl.program_id(1)
    parts = (_ret_parts(p_ref, cos_ref, sin_ref, or_ref, sor_ref, s_ref, dec_ref),
             _hgrn_parts(qf_ref, ig_ref, lbl_ref, nrm_ref, oh_ref, soh_ref, st_ref))

    @pl.when(n == 0)
    def _():
        for init, _, _ in parts:
            init()

    for h in range(0, max(RET_HEADS, HG_HEADS), MIXER_HEAD_GROUP):
        for (_, step, _), n_heads in zip(parts, (RET_HEADS, HG_HEADS)):
            step(range(h, min(h + MIXER_HEAD_GROUP, n_heads)))

    @pl.when(n == pl.num_programs(1) - 1)
    def _():
        for _, _, final in parts:
            final()


def _mixers_prompt(proj3, cos, sin, lbl, nrm, later_weights):
    b, t, _ = proj3.shape
    nc = t // CHUNK
    cast_specs = [pl.BlockSpec((w.shape[0] // (b * nc), w.shape[1]), lambda i, n: (i * nc + n, 0))
                  for w in later_weights]
    return pl.pallas_call(
        functools.partial(_mixers_kernel, len(later_weights)),
        grid=(b, nc),
        in_specs=[pl.BlockSpec((1, CHUNK, OFF_HQ), lambda i, n: (i, n, 0)),
                  pl.BlockSpec((CHUNK, RET_DK), lambda i, n: (n, 0)),
                  pl.BlockSpec((CHUNK, RET_DK), lambda i, n: (n, 0)),
                  pl.BlockSpec((1, CHUNK, 2 * HG_K), lambda i, n: (i, n, OFF_HQ // (2 * HG_K))),
                  pl.BlockSpec((1, CHUNK, 2 * HG_V), lambda i, n: (i, n, OFF_HI // (2 * HG_V))),
                  pl.BlockSpec((DEPTH + 1, HG_K), lambda i, n: (0, 0)),
                  pl.BlockSpec((1, HG_DV), lambda i, n: (0, 0))] + cast_specs,
        out_specs=[pl.BlockSpec((1, CHUNK, RET_V), lambda i, n: (i, n, 0)),
                   pl.BlockSpec((1, RET_HEADS, RET_DK, RET_DV), lambda i, n: (i, 0, 0, 0)),
                   pl.BlockSpec((1, CHUNK, HG_V), lambda i, n: (i, n, 0)),
                   pl.BlockSpec((1, HG_HEADS, HG_DK, HG_DV), lambda i, n: (i, 0, 0, 0))] + cast_specs,
        out_shape=[jax.ShapeDtypeStruct((b, t, RET_V), BF16),
                   jax.ShapeDtypeStruct((b, RET_HEADS, RET_DK, RET_DV), F32),
                   jax.ShapeDtypeStruct((b, t, HG_V), BF16),
                   jax.ShapeDtypeStruct((b, HG_HEADS, HG_DK, HG_DV), F32)]
        + [jax.ShapeDtypeStruct(w.shape, BF16) for w in later_weights],
        scratch_shapes=[pltpu.VMEM((RET_HEADS, RET_DK, RET_DV), F32),
                        pltpu.VMEM((RET_HEADS, 3, CHUNK, RET_DK), F32),
                        pltpu.VMEM((HG_HEADS, HG_DV, HG_DK), F32)],
        compiler_params=_params(("arbitrary", "arbitrary")),
        name="mixers_prompt",
    )(proj3, cos, sin, proj3, proj3, lbl, nrm, *later_weights)


def _sample_prep_kernel(p_ref, cos_ref, sin_ref, lbl_ref, qr_ref, kr_ref, qh_ref, f_ref, kin_ref):
    cos = cos_ref[...]
    sin = sin_ref[...]
    for h in range(RET_HEADS):
        sl = slice(h * RET_DK, (h + 1) * RET_DK)
        qr_ref[:, sl] = _rotary(p_ref[:, OFF_RQ + h * RET_DK:OFF_RQ + (h + 1) * RET_DK], cos, sin)
        kr_ref[:, sl] = _rotary(p_ref[:, OFF_RK + h * RET_DK:OFF_RK + (h + 1) * RET_DK], cos, sin) * (RET_DK ** -0.5)
    f, kin = _hgrn_gates(p_ref[:, OFF_HF:OFF_HF + HG_K], _lower_bound(lbl_ref[...]))
    f_ref[...] = f
    kin_ref[...] = kin
    qh_ref[...] = _silu(p_ref[:, OFF_HQ:OFF_HQ + HG_K])


def _sample_prep(proj, cos, sin, lbl):
    m = proj.shape[0]
    shp = jax.ShapeDtypeStruct((m, RET_QK), F32)
    return pl.pallas_call(
        _sample_prep_kernel,
        grid=(1,),
        in_specs=[pl.BlockSpec((m, OFF_GA), lambda i: (0, 0)),
                  pl.BlockSpec((1, RET_DK), lambda i: (0, 0)),
                  pl.BlockSpec((1, RET_DK), lambda i: (0, 0)),
                  pl.BlockSpec((DEPTH + 1, HG_K), lambda i: (0, 0))],
        out_specs=[pl.BlockSpec((m, RET_QK), lambda i: (0, 0))] * 5,
        out_shape=[shp] * 5,
        compiler_params=_params(("arbitrary",)),
        name="sample_prep",
    )(proj, cos, sin, lbl)


def _sample_state_kernel(sr_ref, sh_ref, qr_ref, kr_ref, qh_ref, f_ref, kin_ref, vr_ref, rg_ref, vh_ref, hg_ref,
                         nrm_ref, sro_ref, sho_ref, or_ref, oh_ref):
    lanes = RET_DK
    for t in range(SAMPLE_SEQS_PER_STEP):
        b = pl.program_id(0) * SAMPLE_SEQS_PER_STEP + t
        base = pl.multiple_of((b // SUB) * SUB, SUB)

        def row(ref, sl, b=b, base=base):
            tile = ref[pl.ds(base, SUB), sl]
            sub = lax.broadcasted_iota(jnp.int32, tile.shape, 0)
            return jnp.sum(jnp.where(sub == b % SUB, tile, 0.0), axis=0, keepdims=True)

        def column(ref, h, row=row):
            r = row(ref, slice(h * RET_DK, (h + 1) * RET_DK))
            return jnp.broadcast_to(r, (RET_DK, RET_DK)).T

        for h in range(RET_HEADS):
            kc = column(kr_ref, h)
            qc = column(qr_ref, h)
            gamma = math.exp(LOG_GAMMA[h])
            outs = []
            for half in range(RET_DV // lanes):
                hs = slice(half * lanes, (half + 1) * lanes)
                s = gamma * sr_ref[t, h, :, hs] + kc * row(vr_ref, slice(h * RET_DV + hs.start, h * RET_DV + hs.stop))
                sro_ref[t, h, :, hs] = s
                outs.append(jnp.sum(qc * s, axis=0, keepdims=True))
            o = jnp.concatenate(outs, axis=1)
            o = o * lax.rsqrt(jnp.mean(o * o, axis=-1, keepdims=True) + EPS)
            sl = slice(h * RET_DV, (h + 1) * RET_DV)
            or_ref[t, :, sl] = o * _silu(row(rg_ref, sl))
        for h in range(HG_HEADS):
            sl = slice(h * HG_DV, (h + 1) * HG_DV)
            s = column(f_ref, h) * sh_ref[t, h] + column(kin_ref, h) * row(vh_ref, sl)
            sho_ref[t, h] = s
            o = jnp.sum(column(qh_ref, h) * s, axis=0, keepdims=True)
            o = o * lax.rsqrt(jnp.mean(o * o, axis=-1, keepdims=True) + EPS) * nrm_ref[...]
            oh_ref[t, :, sl] = o * _silu(row(hg_ref, sl))


def _sample_state(s_ret, s_hg, rows, proj, nrm):
    b = s_ret.shape[0]
    nb = SAMPLE_SEQS_PER_STEP
    row_spec = pl.BlockSpec((b, RET_QK), lambda i: (0, 0))
    return pl.pallas_call(
        _sample_state_kernel,
        grid=(b // nb,),
        in_specs=[pl.BlockSpec((nb, RET_HEADS, RET_DK, RET_DV), lambda i: (i, 0, 0, 0)),
                  pl.BlockSpec((nb, HG_HEADS, HG_DK, HG_DV), lambda i: (i, 0, 0, 0)),
                  row_spec, row_spec, row_spec, row_spec, row_spec,
                  pl.BlockSpec((b, RET_V), lambda i: (0, OFF_RV // RET_V)),
                  pl.BlockSpec((b, RET_V), lambda i: (0, OFF_RG // RET_V)),
                  pl.BlockSpec((b, HG_V), lambda i: (0, OFF_HI // HG_V)),
                  pl.BlockSpec((b, HG_V), lambda i: (0, OFF_HG // HG_V)),
                  pl.BlockSpec((1, HG_DV), lambda i: (0, 0))],
        out_specs=[pl.BlockSpec((nb, RET_HEADS, RET_DK, RET_DV), lambda i: (i, 0, 0, 0)),
                   pl.BlockSpec((nb, HG_HEADS, HG_DK, HG_DV), lambda i: (i, 0, 0, 0)),
                   pl.BlockSpec((nb, 1, RET_V), lambda i: (i, 0, 0)),
                   pl.BlockSpec((nb, 1, HG_V), lambda i: (i, 0, 0))],
        out_shape=[jax.ShapeDtypeStruct(s_ret.shape, F32),
                   jax.ShapeDtypeStruct(s_hg.shape, F32),
                   jax.ShapeDtypeStruct((b, 1, RET_V), F32),
                   jax.ShapeDtypeStruct((b, 1, HG_V), F32)],
        compiler_params=_params(("arbitrary",)),
        name="sample_state",
    )(s_ret, s_hg, *rows, proj, proj, proj, proj, nrm)


def _rope_tables(first, count):
    half = RET_DK // 2
    inv_freq = ROPE_BASE ** (-np.arange(half, dtype=np.float64) / half)
    ang = np.arange(first, first + count, dtype=np.float64)[:, None] * inv_freq[None, :]
    cos, sin = np.cos(ang), np.sin(ang)
    return (jnp.asarray(np.concatenate([cos, cos], axis=-1), F32),
            jnp.asarray(np.concatenate([-sin, sin], axis=-1), F32))


def kernel(x_prompt, x_sample, state_ret, state_hgrn, w_in, w_ret_out, w_hgrn_out, w_out, norm_mix, norm_ffn,
           hgrn_norm, hgrn_lb_logits, w_ffn_in, w_ffn_out, norm_final):
    assert w_in.shape[0] == DEPTH == 1 and x_sample.shape[1] == 1
    bp, t, d = x_prompt.shape
    bs = x_sample.shape[0]
    nrm = hgrn_norm[0][None, :]
    lbl = hgrn_lb_logits.astype(F32)
    x = (x_prompt.reshape(bp * t, d), x_sample.reshape(bs, d))

    proj = _proj(x, norm_mix[0][None, :], w_in[0])

    proj_p3 = proj[0].reshape(bp, t, IN_TOTAL)
    cos_p, sin_p = _rope_tables(0, t)
    o_r, s_ret_p, o_h, s_hg_p, wb_ret, wb_hgrn, wb_out = _mixers_prompt(
        proj_p3, cos_p, sin_p, lbl, nrm, [w_ret_out[0], w_hgrn_out[0], w_out[0]])

    cos_s, sin_s = _rope_tables(PAST_LEN, x_sample.shape[1])
    rows = _sample_prep(proj[1], cos_s, sin_s, lbl)
    s_ret_s, s_hg_s, o_rs, o_hs = _sample_state(state_ret[0], state_hgrn[0], rows, proj[1], nrm)

    o_r = (o_r.reshape(bp * t, RET_V), o_rs.reshape(bs, RET_V).astype(BF16))
    o_h = (o_h.reshape(bp * t, HG_V), o_hs.reshape(bs, HG_V).astype(BF16))
    mrg = _merge(o_r, o_h, wb_ret, wb_hgrn, proj)
    x1, h2 = _resid(x, mrg, wb_out, norm_ffn[0][None, :])
    act, wb_down = _ffn_in(h2, w_ffn_in[0], w_ffn_out[0])
    y_p, y_s = _ffn_out(act, wb_down, x1, norm_final[None, :])

    return (y_p.reshape(bp, t, d), y_s.reshape(bs, 1, d), s_ret_p[None], s_hg_p[None], s_ret_s[None], s_hg_s[None])
```

```python
import functools
import math
from typing import Any, Callable, NamedTuple

import numpy as np
import jax
import jax.numpy as jnp
from jax import lax
from jax.experimental import pallas as pl
from jax.experimental.pallas import tpu as pltpu

D_MODEL = 2048
DEPTH = 1
PAST_LEN = 16384
RET_HEADS = 8
RET_DK = 128
RET_DV = 256
RET_QK = RET_HEADS * RET_DK
RET_V = RET_HEADS * RET_DV
ROPE_BASE = 10000.0
HG_HEADS = 8
HG_DK = 128
HG_DV = 128
HG_K = HG_HEADS * HG_DK
HG_V = HG_HEADS * HG_DV
D_FF = 5632
EPS = 1e-6

OFF_RQ = 0
OFF_RK = OFF_RQ + RET_QK
OFF_RV = OFF_RK + RET_QK
OFF_RG = OFF_RV + RET_V
OFF_HQ = OFF_RG + RET_V
OFF_HF = OFF_HQ + HG_K
OFF_HI = OFF_HF + HG_K
OFF_HG = OFF_HI + HG_V
OFF_GA = OFF_HG + HG_V
OFF_GB = OFF_GA + D_MODEL
IN_TOTAL = OFF_GB + D_MODEL

CHUNK = 128
SUB = 8
MIXER_HEAD_GROUP = 8
SAMPLE_SEQS_PER_STEP = 4
V7X_VMEM_LIMIT = 58 * 1024 * 1024

F32 = jnp.float32
BF16 = jnp.bfloat16

LOG_GAMMA = [float(np.log(np.float32(1.0) - np.float32(2.0) ** np.float32(-5.0 - h))) for h in range(RET_HEADS)]


def _params(sem, vmem=V7X_VMEM_LIMIT):
    return pltpu.CompilerParams(dimension_semantics=sem, vmem_limit_bytes=vmem)


def _rms_rows(x, g):
    return (x * lax.rsqrt(jnp.mean(x * x, axis=-1, keepdims=True) + EPS)) * g


def _sigmoid(x):
    return 0.5 + 0.5 * jnp.tanh(0.5 * x)


def _silu(x):
    hx = 0.5 * x
    return hx + hx * jnp.tanh(hx)


def _dot(a, b):
    return jnp.dot(a, b, preferred_element_type=F32)


def _dot_nt(a, b):
    return lax.dot_general(a, b, (((1,), (1,)), ((), ())), preferred_element_type=F32)


def _dot_tn(a, b):
    return lax.dot_general(a, b, (((0,), (0,)), ((), ())), preferred_element_type=F32)


TILE_M_PROJ, TILE_N_PROJ = 2048, 512
TILE_M_MERGE = 512
TILE_M_RESID = 512
TILE_M_FFN_IN, TILE_N_FFN_IN = 2048, 512
TILE_M_FFN_OUT = 512
FFN_IN_ROW_CHUNK = 512
CAST_ROWS_FFN = 128


class _Rows(NamedTuple):
    prompt: Any
    sample: Any
    cols: int
    col: Callable
    single: bool = False


class _Shared(NamedTuple):
    array: Any
    block: tuple
    index: Callable
    resident: bool = False


class _Out(NamedTuple):
    total_cols: int
    cols: int
    col: Callable
    dtype: Any


def _dense_call(body, name, tile_m, operands, outs, scratch, inner, casts=(), joint_body=None):
    rows = [op for op in operands if isinstance(op, _Rows)]
    m_p, m_s = rows[0].prompt.shape[0], rows[0].sample.shape[0]
    last = m_p // tile_m - 1

    def frozen(i, j):
        return jnp.where(i == last, j, 0)

    specs_p, specs_s, args_p, args_s = [], [], [], []
    for op in operands:
        if isinstance(op, _Rows):
            mode = dict(pipeline_mode=pl.Buffered(1)) if op.single else {}
            specs_p.append(pl.BlockSpec((tile_m, op.cols), lambda i, j, op=op: (i, op.col(j)), **mode))
            specs_s.append(pl.BlockSpec((m_s, op.cols), lambda i, j, op=op: (0, op.col(frozen(i, j)))))
            args_p.append(op.prompt)
            args_s.append(op.sample)
        else:
            mode = dict(pipeline_mode=pl.Buffered(1)) if op.resident else {}
            specs_p.append(pl.BlockSpec(op.block, lambda i, j, op=op: op.index(j), **mode))
            args_p.append(op.array)
    out_specs = ([pl.BlockSpec((tile_m, o.cols), lambda i, j, o=o: (i, o.col(j))) for o in outs]
                 + [pl.BlockSpec((m_s, o.cols), lambda i, j, o=o: (0, o.col(frozen(i, j)))) for o in outs])
    out_shape = ([jax.ShapeDtypeStruct((m_p, o.total_cols), o.dtype) for o in outs]
                 + [jax.ShapeDtypeStruct((m_s, o.total_cols), o.dtype) for o in outs])
    if joint_body is None:
        scratch_shapes = ([pltpu.VMEM((tile_m, c), dt) for c, dt in scratch]
                          + [pltpu.VMEM((m_s, c), dt) for c, dt in scratch])
    else:
        scratch_shapes = [pltpu.VMEM((tile_m + m_s, c), dt) for c, dt in scratch]
    n_in, n_out, n_scr = len(operands), len(outs), len(scratch)
    row_pos = [k for k, op in enumerate(operands) if isinstance(op, _Rows)]
    n_rows, n_cast = len(row_pos), len(casts)
    cast_blocks = [w.shape[0] // rb for w, rb in casts]
    assert all(nb <= (m_p // tile_m) * inner for nb in cast_blocks)
    cast_specs = [pl.BlockSpec((rb, w.shape[1]), lambda i, j, nb=nb: (jnp.minimum(i * inner + j, nb - 1), 0))
                  for (w, rb), nb in zip(casts, cast_blocks)]

    def kernel(*refs):
        ins_p = list(refs[:n_in])
        ins_s = list(ins_p)
        for k, ref in zip(row_pos, refs[n_in:n_in + len(row_pos)]):
            ins_s[k] = ref
        r = n_in + n_rows
        cast_in = refs[r:r + n_cast]
        r += n_cast
        outs_p, outs_s = refs[r:r + n_out], refs[r + n_out:r + 2 * n_out]
        r += 2 * n_out
        cast_out = refs[r:r + n_cast]
        r += n_cast
        is_last = pl.program_id(0) == last
        if joint_body is None:
            body(*ins_p, *outs_p, *refs[r:r + n_scr])

            @pl.when(is_last)
            def _():
                body(*ins_s, *outs_s, *refs[r + n_scr:])
        else:
            scr = refs[r:r + n_scr]

            @pl.when(jnp.logical_not(is_last))
            def _():
                body(*ins_p, *outs_p, *[sc.at[pl.ds(0, tile_m)] for sc in scr])

            @pl.when(is_last)
            def _():
                joint_body(ins_p, refs[n_in:n_in + n_rows], outs_p, outs_s, scr)

        step = pl.program_id(0) * inner + pl.program_id(1)
        for src, dst, nb in zip(cast_in, cast_out, cast_blocks):
            @pl.when(step < nb)
            def _(src=src, dst=dst):
                dst[...] = src[...].astype(BF16)

    res = pl.pallas_call(
        kernel,
        grid=(m_p // tile_m, inner),
        in_specs=specs_p + specs_s + cast_specs,
        out_specs=out_specs + cast_specs,
        out_shape=out_shape + [jax.ShapeDtypeStruct(w.shape, BF16) for w, _ in casts],
        scratch_shapes=scratch_shapes,
        compiler_params=_params(("arbitrary", "arbitrary")),
        name=name,
    )(*args_p, *args_s, *[w for w, _ in casts])
    return list(zip(res[:n_out], res[n_out:2 * n_out])) + list(res[2 * n_out:])


def _proj_body(x_ref, g_ref, w_ref, o_ref, h_ref):
    @pl.when(pl.program_id(1) == 0)
    def _():
        h_ref[...] = _rms_rows(x_ref[...], g_ref[...]).astype(BF16)
    o_ref[...] = _dot(h_ref[...], w_ref[...].astype(BF16))


def _proj_joint_body(ins, sample_rows, outs_p, outs_s, scratch):
    (x_ref, g_ref, w_ref), (xs_ref,), (o_ref,), (os_ref,), (h_ref,) = ins, sample_rows, outs_p, outs_s, scratch
    tm = x_ref.shape[0]

    @pl.when(pl.program_id(1) == 0)
    def _():
        h_ref[:tm, :] = _rms_rows(x_ref[...], g_ref[...]).astype(BF16)
        h_ref[tm:, :] = _rms_rows(xs_ref[...], g_ref[...]).astype(BF16)
    res = _dot(h_ref[...], w_ref[...].astype(BF16))
    o_ref[...] = res[:tm]
    os_ref[...] = res[tm:]


def _proj(x, g, w):
    d, n = w.shape
    tn = TILE_N_PROJ
    return _dense_call(
        _proj_body, "proj", TILE_M_PROJ,
        [_Rows(*x, d, lambda j: 0, single=True),
         _Shared(g, (1, d), lambda j: (0, 0)),
         _Shared(w, (d, tn), lambda j: (0, j))],
        [_Out(n, tn, lambda j: j, F32)], [(d, BF16)], n // tn, joint_body=_proj_joint_body)[0]


def _merge_body(or_ref, oh_ref, wr_ref, wh_ref, ga_ref, gb_ref, o_ref):
    ya = _dot(or_ref[...], wr_ref[...])
    yb = _dot(oh_ref[...], wh_ref[...])
    o_ref[...] = (_sigmoid(ga_ref[...]) * ya + _sigmoid(gb_ref[...]) * yb).astype(BF16)


def _merge(o_r, o_h, w_r, w_h, proj):
    n = w_r.shape[1]
    return _dense_call(
        _merge_body, "merge", TILE_M_MERGE,
        [_Rows(*o_r, RET_V, lambda j: 0),
         _Rows(*o_h, HG_V, lambda j: 0),
         _Shared(w_r, (RET_V, n), lambda j: (0, 0), resident=True),
         _Shared(w_h, (HG_V, n), lambda j: (0, 0), resident=True),
         _Rows(*proj, n, lambda j: OFF_GA // n),
         _Rows(*proj, n, lambda j: OFF_GB // n)],
        [_Out(n, n, lambda j: 0, BF16)], [], 1)[0]


def _resid_body(x_ref, m_ref, w_ref, g_ref, o_ref, h_ref):
    x1 = x_ref[...] + _dot(m_ref[...], w_ref[...])
    o_ref[...] = x1
    h_ref[...] = _rms_rows(x1, g_ref[...]).astype(BF16)


def _resid(x, mrg, w, g):
    d = w.shape[1]
    return _dense_call(
        _resid_body, "resid", TILE_M_RESID,
        [_Rows(*x, d, lambda j: 0),
         _Rows(*mrg, d, lambda j: 0),
         _Shared(w, (d, d), lambda j: (0, 0), resident=True),
         _Shared(g, (1, d), lambda j: (0, 0))],
        [_Out(d, d, lambda j: 0, F32), _Out(d, d, lambda j: 0, BF16)], [], 1)


def _ffn_in_body(h_ref, wg_ref, wu_ref, o_ref):
    wg = wg_ref[...].astype(BF16)
    wu = wu_ref[...].astype(BF16)
    rows = h_ref.shape[0]
    step = min(rows, FFN_IN_ROW_CHUNK)
    for r in range(0, rows, step):
        h = h_ref[r:r + step, :]
        o_ref[r:r + step, :] = (_silu(_dot(h, wg)) * _dot(h, wu)).astype(BF16)


def _ffn_in(h, w, w_down):
    d = w.shape[0]
    tn = TILE_N_FFN_IN
    return _dense_call(
        _ffn_in_body, "ffn_in", TILE_M_FFN_IN,
        [_Rows(*h, d, lambda j: 0),
         _Shared(w, (d, tn), lambda j: (0, j)),
         _Shared(w, (d, tn), lambda j: (0, D_FF // tn + j))],
        [_Out(D_FF, tn, lambda j: j, BF16)], [], D_FF // tn, casts=[(w_down, CAST_ROWS_FFN)])


def _ffn_out_body(a_ref, w_ref, x_ref, g_ref, o_ref):
    o_ref[...] = _rms_rows(x_ref[...] + _dot(a_ref[...], w_ref[...]), g_ref[...])


def _ffn_out(act, w, x, g):
    k, d = w.shape
    return _dense_call(
        _ffn_out_body, "ffn_out", TILE_M_FFN_OUT,
        [_Rows(*act, k, lambda j: 0),
         _Shared(w, (k, d), lambda j: (0, 0), resident=True),
         _Rows(*x, d, lambda j: 0),
         _Shared(g, (1, d), lambda j: (0, 0))],
        [_Out(d, d, lambda j: 0, F32)], [], 1)[0]


def _rotary(x, cos, sin_signed):
    return x * cos + pltpu.roll(x, RET_DK // 2, 1) * sin_signed


def _ret_parts(p_ref, cos_ref, sin_ref, o_ref, so_ref, s_ref, dec_ref):
    c = CHUNK

    def init():
        s_ref[...] = jnp.zeros_like(s_ref)
        row = lax.broadcasted_iota(jnp.int32, (c, c), 0)
        col = lax.broadcasted_iota(jnp.int32, (c, c), 1)
        diff = (row - col).astype(F32)
        idx = lax.broadcasted_iota(jnp.int32, (c, RET_DK), 0).astype(F32)
        for h in range(RET_HEADS):
            lg = LOG_GAMMA[h]
            dec_ref[h, 0] = jnp.where(diff >= 0, jnp.exp(jnp.maximum(diff, 0.0) * lg), 0.0)
            dec_ref[h, 1] = jnp.exp((idx + 1.0) * lg)
            dec_ref[h, 2] = jnp.exp((c - 1.0 - idx) * lg)

    def step(heads):
        cos = cos_ref[...]
        sin = sin_ref[...]
        def rotate(h):
            q = _rotary(p_ref[0, :, OFF_RQ + h * RET_DK:OFF_RQ + (h + 1) * RET_DK], cos, sin)
            k = _rotary(p_ref[0, :, OFF_RK + h * RET_DK:OFF_RK + (h + 1) * RET_DK], cos, sin) * (RET_DK ** -0.5)
            return q, k

        def attend(h, q, k):
            v = p_ref[0, :, OFF_RV + h * RET_DV:OFF_RV + (h + 1) * RET_DV].astype(BF16)
            chunk_decay = math.exp(c * LOG_GAMMA[h])
            s = s_ref[h]
            scores = _dot_nt(q.astype(BF16), k.astype(BF16)) * dec_ref[h, 0]
            o = _dot(scores.astype(BF16), v) + _dot((q * dec_ref[h, 1]).astype(BF16), s.astype(BF16))
            s_ref[h] = chunk_decay * s + _dot_tn((k * dec_ref[h, 2]).astype(BF16), v)
            return o

        def emit(h, o):
            o = o * lax.rsqrt(jnp.mean(o * o, axis=-1, keepdims=True) + EPS)
            gate = p_ref[0, :, OFF_RG + h * RET_DV:OFF_RG + (h + 1) * RET_DV]
            o_ref[0, :, h * RET_DV:(h + 1) * RET_DV] = (o * _silu(gate)).astype(BF16)

        qk = [rotate(h) for h in heads]
        yield
        outs = [attend(h, q, k) for h, (q, k) in zip(heads, qk)]
        yield
        for h, o in zip(heads, outs):
            emit(h, o)

    def final():
        so_ref[0] = s_ref[...]

    return init, step, final


def _lower_bound(lbl):
    mx = jnp.max(lbl, axis=0, keepdims=True)
    e = jnp.exp(lbl - mx)
    return e[0:1, :] / jnp.sum(e, axis=0, keepdims=True)


def _hgrn_gates(z, lb):
    half = 0.5 * (1.0 - lb)
    ht = half * jnp.tanh(0.5 * z)
    return (lb + half) + ht, half - ht


def _cumsum_rows(tri, x):
    hi = x.astype(BF16)
    r1 = x - hi.astype(F32)
    mid = r1.astype(BF16)
    lo = (r1 - mid.astype(F32)).astype(BF16)
    return _dot(tri, hi) + _dot(tri, mid) + _dot(tri, lo)


def _bcast_rows(x, group, r):
    c, d = x.shape
    x3 = x.reshape(c // group, group, d)
    return jnp.broadcast_to(x3[:, r:r + 1, :], x3.shape).reshape(c, d)


def _hgrn_parts(qf_ref, ig_ref, lbl_ref, nrm_ref, o_ref, so_ref, st_ref):
    def init():
        st_ref[...] = jnp.zeros_like(st_ref)

    def final():
        for h in range(HG_HEADS):
            so_ref[0, h] = st_ref[h].T

    return init, functools.partial(_hgrn_step, qf_ref, ig_ref, lbl_ref, nrm_ref, o_ref, st_ref), final


def _hgrn_step(qf_ref, ig_ref, lbl_ref, nrm_ref, o_ref, st_ref, heads):
    c = CHUNK
    row = lax.broadcasted_iota(jnp.int32, (c, c), 0)
    col = lax.broadcasted_iota(jnp.int32, (c, c), 1)
    rowd = lax.broadcasted_iota(jnp.int32, (c, HG_DK), 0)
    tri = jnp.where(row >= col, 1.0, 0.0).astype(BF16)
    r2, r4 = rowd & 1, rowd & 3
    block_masks = [(row >> s) == (col >> s) for s in range(c.bit_length())]
    nrm = nrm_ref[...]

    def up(x, k):
        return pltpu.roll(x, k, 0)

    def dn(x, k):
        return pltpu.roll(x, c - k, 0)

    def gates(h):
        off = h * HG_DK
        lb = _lower_bound(lbl_ref[:, pl.ds(off, HG_DK)])
        f, kin = _hgrn_gates(qf_ref[0, :, pl.ds(HG_K + off, HG_DK)], lb)
        qh = _silu(qf_ref[0, :, pl.ds(off, HG_DK)])
        v = ig_ref[0, :, pl.ds(off, HG_DV)].astype(BF16)
        b = _cumsum_rows(tri, jnp.log2(f))
        return f, kin, qh, v, b

    def intra(f, kin, qh, b):
        f_up = f * up(f, 1)
        f_dn = f * dn(f, 1)
        pre2 = jnp.where(r2 == 1, f_up, f)
        pre4 = jnp.where(r4 >= 1, f_up, f)
        pre4 = jnp.where(r4 >= 2, pre4 * up(pre4, 2), pre4)
        suf4 = jnp.where(r4 <= 2, f_dn, f)
        suf4 = jnp.where(r4 <= 1, suf4 * dn(suf4, 2), suf4)

        def level(s):
            if s == 0:
                return qh, kin
            if s == 1:
                return qh * f, kin
            if s == 2:
                return qh * pre2, kin * jnp.where(r4 == 0, dn(f, 1), 1.0)
            if s == 3:
                return qh * pre4, kin * jnp.where(r4 == 3, 1.0, dn(suf4, 1))
            half = 1 << (s - 1)
            ref = _bcast_rows(b, 2 * half, half - 1)
            upper = (rowd & (2 * half - 1)) >= half
            return jnp.where(upper, qh * jnp.exp2(b - ref), 0.0), jnp.where(upper, 0.0, kin * jnp.exp2(ref - b))

        a = jnp.zeros((c, c), F32)
        for s in reversed(range(len(block_masks))):
            q_s, k_s = level(s)
            a = jnp.where(block_masks[s], _dot_nt(q_s.astype(BF16), k_s.astype(BF16)), a)
        return jnp.where(col <= row, a, 0.0)

    def attend(h, kin, qh, v, b, a):
        st = st_ref[h]
        b_last = b[c - 1:c, :]
        o = _dot(a.astype(BF16), v) + _dot_nt((qh * jnp.exp2(b)).astype(BF16), st.astype(BF16))
        st_ref[h] = st * jnp.exp2(b_last) + _dot_tn(v, (kin * jnp.exp2(b_last - b)).astype(BF16))
        return o

    def emit(h, o):
        off = h * HG_DK
        o = o * lax.rsqrt(jnp.mean(o * o, axis=-1, keepdims=True) + EPS) * nrm
        o_ref[0, :, pl.ds(off, HG_DV)] = (o * _silu(ig_ref[0, :, pl.ds(HG_V + off, HG_DV)])).astype(BF16)

    g = [gates(h) for h in heads]
    yield
    a = [intra(f, kin, qh, b) for f, kin, qh, v, b in g]
    yield
    outs = [attend(h, kin, qh, v, b, a_h) for h, (f, kin, qh, v, b), a_h in zip(heads, g, a)]
    for h, o in zip(heads, outs):
        emit(h, o)


def _mixers_kernel(n_cast, p_ref, cos_ref, sin_ref, qf_ref, ig_ref, lbl_ref, nrm_ref, *refs):
    cast_in, (or_ref, sor_ref, oh_ref, soh_ref), refs = refs[:n_cast], refs[n_cast:n_cast + 4], refs[n_cast + 4:]
    cast_out, (s_ref, dec_ref, st_ref) = refs[:n_cast], refs[n_cast:]
    for src, dst in zip(cast_in, cast_out):
        dst[...] = src[...].astype(BF16)
    n = pl.program_id(1)
    parts = (_ret_parts(p_ref, cos_ref, sin_ref, or_ref, sor_ref, s_ref, dec_ref),
             _hgrn_parts(qf_ref, ig_ref, lbl_ref, nrm_ref, oh_ref, soh_ref, st_ref))

    @pl.when(n == 0)
    def _():
        for init, _, _ in parts:
            init()

    for h in range(0, max(RET_HEADS, HG_HEADS), MIXER_HEAD_GROUP):
        phases = [step(range(h, min(h + MIXER_HEAD_GROUP, n_heads)))
                  for (_, step, _), n_heads in zip(parts, (RET_HEADS, HG_HEADS))]
        while phases:
            phases = [ph for ph in phases if next(ph, True) is None]

    @pl.when(n == pl.num_programs(1) - 1)
    def _():
        for _, _, final in parts:
            final()


def _mixers_prompt(proj3, cos, sin, lbl, nrm, later_weights):
    b, t, _ = proj3.shape
    nc = t // CHUNK
    cast_specs = [pl.BlockSpec((w.shape[0] // (b * nc), w.shape[1]), lambda i, n: (i * nc + n, 0))
                  for w in later_weights]
    return pl.pallas_call(
        functools.partial(_mixers_kernel, len(later_weights)),
        grid=(b, nc),
        in_specs=[pl.BlockSpec((1, CHUNK, OFF_HQ), lambda i, n: (i, n, 0)),
                  pl.BlockSpec((CHUNK, RET_DK), lambda i, n: (n, 0)),
                  pl.BlockSpec((CHUNK, RET_DK), lambda i, n: (n, 0)),
                  pl.BlockSpec((1, CHUNK, 2 * HG_K), lambda i, n: (i, n, OFF_HQ // (2 * HG_K))),
                  pl.BlockSpec((1, CHUNK, 2 * HG_V), lambda i, n: (i, n, OFF_HI // (2 * HG_V))),
                  pl.BlockSpec((DEPTH + 1, HG_K), lambda i, n: (0, 0)),
                  pl.BlockSpec((1, HG_DV), lambda i, n: (0, 0))] + cast_specs,
        out_specs=[pl.BlockSpec((1, CHUNK, RET_V), lambda i, n: (i, n, 0)),
                   pl.BlockSpec((1, RET_HEADS, RET_DK, RET_DV), lambda i, n: (i, 0, 0, 0)),
                   pl.BlockSpec((1, CHUNK, HG_V), lambda i, n: (i, n, 0)),
                   pl.BlockSpec((1, HG_HEADS, HG_DK, HG_DV), lambda i, n: (i, 0, 0, 0))] + cast_specs,
        out_shape=[jax.ShapeDtypeStruct((b, t, RET_V), BF16),
                   jax.ShapeDtypeStruct((b, RET_HEADS, RET_DK, RET_DV), F32),
                   jax.ShapeDtypeStruct((b, t, HG_V), BF16),
                   jax.ShapeDtypeStruct((b, HG_HEADS, HG_DK, HG_DV), F32)]
        + [jax.ShapeDtypeStruct(w.shape, BF16) for w in later_weights],
        scratch_shapes=[pltpu.VMEM((RET_HEADS, RET_DK, RET_DV), F32),
                        pltpu.VMEM((RET_HEADS, 3, CHUNK, RET_DK), F32),
                        pltpu.VMEM((HG_HEADS, HG_DV, HG_DK), F32)],
        compiler_params=_params(("arbitrary", "arbitrary")),
        name="mixers_prompt",
    )(proj3, cos, sin, proj3, proj3, lbl, nrm, *later_weights)


def _sample_prep_kernel(p_ref, cos_ref, sin_ref, lbl_ref, qr_ref, kr_ref, qh_ref, f_ref, kin_ref):
    cos = cos_ref[...]
    sin = sin_ref[...]
    for h in range(RET_HEADS):
        sl = slice(h * RET_DK, (h + 1) * RET_DK)
        qr_ref[:, sl] = _rotary(p_ref[:, OFF_RQ + h * RET_DK:OFF_RQ + (h + 1) * RET_DK], cos, sin)
        kr_ref[:, sl] = _rotary(p_ref[:, OFF_RK + h * RET_DK:OFF_RK + (h + 1) * RET_DK], cos, sin) * (RET_DK ** -0.5)
    f, kin = _hgrn_gates(p_ref[:, OFF_HF:OFF_HF + HG_K], _lower_bound(lbl_ref[...]))
    f_ref[...] = f
    kin_ref[...] = kin
    qh_ref[...] = _silu(p_ref[:, OFF_HQ:OFF_HQ + HG_K])


def _sample_prep(proj, cos, sin, lbl):
    m = proj.shape[0]
    shp = jax.ShapeDtypeStruct((m, RET_QK), F32)
    return pl.pallas_call(
        _sample_prep_kernel,
        grid=(1,),
        in_specs=[pl.BlockSpec((m, OFF_GA), lambda i: (0, 0)),
                  pl.BlockSpec((1, RET_DK), lambda i: (0, 0)),
                  pl.BlockSpec((1, RET_DK), lambda i: (0, 0)),
                  pl.BlockSpec((DEPTH + 1, HG_K), lambda i: (0, 0))],
        out_specs=[pl.BlockSpec((m, RET_QK), lambda i: (0, 0))] * 5,
        out_shape=[shp] * 5,
        compiler_params=_params(("arbitrary",)),
        name="sample_prep",
    )(proj, cos, sin, lbl)


def _sample_state_kernel(sr_ref, sh_ref, qr_ref, kr_ref, qh_ref, f_ref, kin_ref, vr_ref, rg_ref, vh_ref, hg_ref,
                         nrm_ref, sro_ref, sho_ref, or_ref, oh_ref):
    lanes = RET_DK
    for t in range(SAMPLE_SEQS_PER_STEP):
        b = pl.program_id(0) * SAMPLE_SEQS_PER_STEP + t
        base = pl.multiple_of((b // SUB) * SUB, SUB)

        def row(ref, sl, b=b, base=base):
            tile = ref[pl.ds(base, SUB), sl]
            sub = lax.broadcasted_iota(jnp.int32, tile.shape, 0)
            return jnp.sum(jnp.where(sub == b % SUB, tile, 0.0), axis=0, keepdims=True)

        def column(ref, h, row=row):
            r = row(ref, slice(h * RET_DK, (h + 1) * RET_DK))
            return jnp.broadcast_to(r, (RET_DK, RET_DK)).T

        for h in range(RET_HEADS):
            kc = column(kr_ref, h)
            qc = column(qr_ref, h)
            gamma = math.exp(LOG_GAMMA[h])
            outs = []
            for half in range(RET_DV // lanes):
                hs = slice(half * lanes, (half + 1) * lanes)
                s = gamma * sr_ref[t, h, :, hs] + kc * row(vr_ref, slice(h * RET_DV + hs.start, h * RET_DV + hs.stop))
                sro_ref[t, h, :, hs] = s
                outs.append(jnp.sum(qc * s, axis=0, keepdims=True))
            o = jnp.concatenate(outs, axis=1)
            o = o * lax.rsqrt(jnp.mean(o * o, axis=-1, keepdims=True) + EPS)
            sl = slice(h * RET_DV, (h + 1) * RET_DV)
            or_ref[t, :, sl] = o * _silu(row(rg_ref, sl))
        for h in range(HG_HEADS):
            sl = slice(h * HG_DV, (h + 1) * HG_DV)
            s = column(f_ref, h) * sh_ref[t, h] + column(kin_ref, h) * row(vh_ref, sl)
            sho_ref[t, h] = s
            o = jnp.sum(column(qh_ref, h) * s, axis=0, keepdims=True)
            o = o * lax.rsqrt(jnp.mean(o * o, axis=-1, keepdims=True) + EPS) * nrm_ref[...]
            oh_ref[t, :, sl] = o * _silu(row(hg_ref, sl))


def _sample_state(s_ret, s_hg, rows, proj, nrm):
    b = s_ret.shape[0]
    nb = SAMPLE_SEQS_PER_STEP
    row_spec = pl.BlockSpec((b, RET_QK), lambda i: (0, 0))
    return pl.pallas_call(
        _sample_state_kernel,
        grid=(b // nb,),
        in_specs=[pl.BlockSpec((nb, RET_HEADS, RET_DK, RET_DV), lambda i: (i, 0, 0, 0)),
                  pl.BlockSpec((nb, HG_HEADS, HG_DK, HG_DV), lambda i: (i, 0, 0, 0)),
                  row_spec, row_spec, row_spec, row_spec, row_spec,
                  pl.BlockSpec((b, RET_V), lambda i: (0, OFF_RV // RET_V)),
                  pl.BlockSpec((b, RET_V), lambda i: (0, OFF_RG // RET_V)),
                  pl.BlockSpec((b, HG_V), lambda i: (0, OFF_HI // HG_V)),
                  pl.BlockSpec((b, HG_V), lambda i: (0, OFF_HG // HG_V)),
                  pl.BlockSpec((1, HG_DV), lambda i: (0, 0))],
        out_specs=[pl.BlockSpec((nb, RET_HEADS, RET_DK, RET_DV), lambda i: (i, 0, 0, 0)),
                   pl.BlockSpec((nb, HG_HEADS, HG_DK, HG_DV), lambda i: (i, 0, 0, 0)),
                   pl.BlockSpec((nb, 1, RET_V), lambda i: (i, 0, 0)),
                   pl.BlockSpec((nb, 1, HG_V), lambda i: (i, 0, 0))],
        out_shape=[jax.ShapeDtypeStruct(s_ret.shape, F32),
                   jax.ShapeDtypeStruct(s_hg.shape, F32),
                   jax.ShapeDtypeStruct((b, 1, RET_V), F32),
                   jax.ShapeDtypeStruct((b, 1, HG_V), F32)],
        compiler_params=_params(("arbitrary",)),
        name="sample_state",
    )(s_ret, s_hg, *rows, proj, proj, proj, proj, nrm)


def _rope_tables(first, count):
    half = RET_DK // 2
    inv_freq = ROPE_BASE ** (-np.arange(half, dtype=np.float64) / half)
    ang = np.arange(first, first + count, dtype=np.float64)[:, None] * inv_freq[None, :]
    cos, sin = np.cos(ang), np.sin(ang)
    return (jnp.asarray(np.concatenate([cos, cos], axis=-1), F32),
            jnp.asarray(np.concatenate([-sin, sin], axis=-1), F32))


def kernel(x_prompt, x_sample, state_ret, state_hgrn, w_in, w_ret_out, w_hgrn_out, w_out, norm_mix, norm_ffn,
           hgrn_norm, hgrn_lb_logits, w_ffn_in, w_ffn_out, norm_final):
    assert w_in.shape[0] == DEPTH == 1 and x_sample.shape[1] == 1
    bp, t, d = x_prompt.shape
    bs = x_sample.shape[0]
    nrm = hgrn_norm[0][None, :]
    lbl = hgrn_lb_logits.astype(F32)
    x = (x_prompt.reshape(bp * t, d), x_sample.reshape(bs, d))

    proj = _proj(x, norm_mix[0][None, :], w_in[0])

    proj_p3 = proj[0].reshape(bp, t, IN_TOTAL)
    cos_p, sin_p = _rope_tables(0, t)
    o_r, s_ret_p, o_h, s_hg_p, wb_ret, wb_hgrn, wb_out = _mixers_prompt(
        proj_p3, cos_p, sin_p, lbl, nrm, [w_ret_out[0], w_hgrn_out[0], w_out[0]])

    cos_s, sin_s = _rope_tables(PAST_LEN, x_sample.shape[1])
    rows = _sample_prep(proj[1], cos_s, sin_s, lbl)
    s_ret_s, s_hg_s, o_rs, o_hs = _sample_state(state_ret[0], state_hgrn[0], rows, proj[1], nrm)

    o_r = (o_r.reshape(bp * t, RET_V), o_rs.reshape(bs, RET_V).astype(BF16))
    o_h = (o_h.reshape(bp * t, HG_V), o_hs.reshape(bs, HG_V).astype(BF16))
    mrg = _merge(o_r, o_h, wb_ret, wb_hgrn, proj)
    x1, h2 = _resid(x, mrg, wb_out, norm_ffn[0][None, :])
    act, wb_down = _ffn_in(h2, w_ffn_in[0], w_ffn_out[0])
    y_p, y_s = _ffn_out(act, wb_down, x1, norm_final[None, :])

    return (y_p.reshape(bp, t, d), y_s.reshape(bs, 1, d), s_ret_p[None], s_hg_p[None], s_ret_s[None], s_hg_s[None])
```

```python
import functools
import math
from typing import Any, Callable, NamedTuple

import numpy as np
import jax
import jax.numpy as jnp
from jax import lax
from jax.experimental import pallas as pl
from jax.experimental.pallas import tpu as pltpu

D_MODEL = 2048
DEPTH = 1
PAST_LEN = 16384
RET_HEADS = 8
RET_DK = 128
RET_DV = 256
RET_QK = RET_HEADS * RET_DK
RET_V = RET_HEADS * RET_DV
ROPE_BASE = 10000.0
HG_HEADS = 8
HG_DK = 128
HG_DV = 128
HG_K = HG_HEADS * HG_DK
HG_V = HG_HEADS * HG_DV
D_FF = 5632
EPS = 1e-6

OFF_RQ = 0
OFF_RK = OFF_RQ + RET_QK
OFF_RV = OFF_RK + RET_QK
OFF_RG = OFF_RV + RET_V
OFF_HQ = OFF_RG + RET_V
OFF_HF = OFF_HQ + HG_K
OFF_HI = OFF_HF + HG_K
OFF_HG = OFF_HI + HG_V
OFF_GA = OFF_HG + HG_V
OFF_GB = OFF_GA + D_MODEL
IN_TOTAL = OFF_GB + D_MODEL

CHUNK = 128
SUB = 8
MIXER_HEAD_GROUP = 8
SAMPLE_SEQS_PER_STEP = 4
V7X_VMEM_LIMIT = 58 * 1024 * 1024

F32 = jnp.float32
BF16 = jnp.bfloat16

LOG_GAMMA = [float(np.log(np.float32(1.0) - np.float32(2.0) ** np.float32(-5.0 - h))) for h in range(RET_HEADS)]


def _params(sem, vmem=V7X_VMEM_LIMIT):
    return pltpu.CompilerParams(dimension_semantics=sem, vmem_limit_bytes=vmem)


def _rms_rows(x, g):
    return (x * lax.rsqrt(jnp.mean(x * x, axis=-1, keepdims=True) + EPS)) * g


def _sigmoid(x):
    return 0.5 + 0.5 * jnp.tanh(0.5 * x)


def _silu(x):
    hx = 0.5 * x
    return hx + hx * jnp.tanh(hx)


def _dot(a, b):
    return jnp.dot(a, b, preferred_element_type=F32)


def _dot_nt(a, b):
    return lax.dot_general(a, b, (((1,), (1,)), ((), ())), preferred_element_type=F32)


def _dot_tn(a, b):
    return lax.dot_general(a, b, (((0,), (0,)), ((), ())), preferred_element_type=F32)


TILE_M_PROJ, TILE_N_PROJ = 2048, 512
TILE_M_MERGE = 512
TILE_M_RESID = 512
TILE_M_FFN_IN, TILE_N_FFN_IN = 2048, 512
TILE_M_FFN_OUT = 512
FFN_IN_ROW_CHUNK = 512
CAST_ROWS_FFN = 128


class _Rows(NamedTuple):
    prompt: Any
    sample: Any
    cols: int
    col: Callable
    single: bool = False


class _Shared(NamedTuple):
    array: Any
    block: tuple
    index: Callable
    resident: bool = False


class _Out(NamedTuple):
    total_cols: int
    cols: int
    col: Callable
    dtype: Any


def _dense_call(body, name, tile_m, operands, outs, scratch, inner, casts=(), joint_body=None):
    rows = [op for op in operands if isinstance(op, _Rows)]
    m_p, m_s = rows[0].prompt.shape[0], rows[0].sample.shape[0]
    last = m_p // tile_m - 1

    def frozen(i, j):
        return jnp.where(i == last, j, 0)

    specs_p, specs_s, args_p, args_s = [], [], [], []
    for op in operands:
        if isinstance(op, _Rows):
            mode = dict(pipeline_mode=pl.Buffered(1)) if op.single else {}
            specs_p.append(pl.BlockSpec((tile_m, op.cols), lambda i, j, op=op: (i, op.col(j)), **mode))
            specs_s.append(pl.BlockSpec((m_s, op.cols), lambda i, j, op=op: (0, op.col(frozen(i, j)))))
            args_p.append(op.prompt)
            args_s.append(op.sample)
        else:
            mode = dict(pipeline_mode=pl.Buffered(1)) if op.resident else {}
            specs_p.append(pl.BlockSpec(op.block, lambda i, j, op=op: op.index(j), **mode))
            args_p.append(op.array)
    out_specs = ([pl.BlockSpec((tile_m, o.cols), lambda i, j, o=o: (i, o.col(j))) for o in outs]
                 + [pl.BlockSpec((m_s, o.cols), lambda i, j, o=o: (0, o.col(frozen(i, j)))) for o in outs])
    out_shape = ([jax.ShapeDtypeStruct((m_p, o.total_cols), o.dtype) for o in outs]
                 + [jax.ShapeDtypeStruct((m_s, o.total_cols), o.dtype) for o in outs])
    if joint_body is None:
        scratch_shapes = ([pltpu.VMEM((tile_m, c), dt) for c, dt in scratch]
                          + [pltpu.VMEM((m_s, c), dt) for c, dt in scratch])
    else:
        scratch_shapes = [pltpu.VMEM((tile_m + m_s, c), dt) for c, dt in scratch]
    n_in, n_out, n_scr = len(operands), len(outs), len(scratch)
    row_pos = [k for k, op in enumerate(operands) if isinstance(op, _Rows)]
    n_rows, n_cast = len(row_pos), len(casts)
    cast_blocks = [w.shape[0] // rb for w, rb in casts]
    assert all(nb <= (m_p // tile_m) * inner for nb in cast_blocks)
    cast_specs = [pl.BlockSpec((rb, w.shape[1]), lambda i, j, nb=nb: (jnp.minimum(i * inner + j, nb - 1), 0))
                  for (w, rb), nb in zip(casts, cast_blocks)]

    def kernel(*refs):
        ins_p = list(refs[:n_in])
        ins_s = list(ins_p)
        for k, ref in zip(row_pos, refs[n_in:n_in + len(row_pos)]):
            ins_s[k] = ref
        r = n_in + n_rows
        cast_in = refs[r:r + n_cast]
        r += n_cast
        outs_p, outs_s = refs[r:r + n_out], refs[r + n_out:r + 2 * n_out]
        r += 2 * n_out
        cast_out = refs[r:r + n_cast]
        r += n_cast
        is_last = pl.program_id(0) == last
        if joint_body is None:
            body(*ins_p, *outs_p, *refs[r:r + n_scr])

            @pl.when(is_last)
            def _():
                body(*ins_s, *outs_s, *refs[r + n_scr:])
        else:
            scr = refs[r:r + n_scr]

            @pl.when(jnp.logical_not(is_last))
            def _():
                body(*ins_p, *outs_p, *[sc.at[pl.ds(0, tile_m)] for sc in scr])

            @pl.when(is_last)
            def _():
                joint_body(ins_p, refs[n_in:n_in + n_rows], outs_p, outs_s, scr)

        step = pl.program_id(0) * inner + pl.program_id(1)
        for src, dst, nb in zip(cast_in, cast_out, cast_blocks):
            @pl.when(step < nb)
            def _(src=src, dst=dst):
                dst[...] = src[...].astype(BF16)

    res = pl.pallas_call(
        kernel,
        grid=(m_p // tile_m, inner),
        in_specs=specs_p + specs_s + cast_specs,
        out_specs=out_specs + cast_specs,
        out_shape=out_shape + [jax.ShapeDtypeStruct(w.shape, BF16) for w, _ in casts],
        scratch_shapes=scratch_shapes,
        compiler_params=_params(("arbitrary", "arbitrary")),
        name=name,
    )(*args_p, *args_s, *[w for w, _ in casts])
    return list(zip(res[:n_out], res[n_out:2 * n_out])) + list(res[2 * n_out:])


def _proj_body(x_ref, g_ref, w_ref, o_ref, h_ref):
    @pl.when(pl.program_id(1) == 0)
    def _():
        h_ref[...] = _rms_rows(x_ref[...], g_ref[...]).astype(BF16)
    o_ref[...] = _dot(h_ref[...], w_ref[...].astype(BF16))


def _proj_joint_body(ins, sample_rows, outs_p, outs_s, scratch):
    (x_ref, g_ref, w_ref), (xs_ref,), (o_ref,), (os_ref,), (h_ref,) = ins, sample_rows, outs_p, outs_s, scratch
    tm = x_ref.shape[0]

    @pl.when(pl.program_id(1) == 0)
    def _():
        h_ref[:tm, :] = _rms_rows(x_ref[...], g_ref[...]).astype(BF16)
        h_ref[tm:, :] = _rms_rows(xs_ref[...], g_ref[...]).astype(BF16)
    res = _dot(h_ref[...], w_ref[...].astype(BF16))
    o_ref[...] = res[:tm]
    os_ref[...] = res[tm:]


def _proj(x, g, w):
    d, n = w.shape
    tn = TILE_N_PROJ
    return _dense_call(
        _proj_body, "proj", TILE_M_PROJ,
        [_Rows(*x, d, lambda j: 0, single=True),
         _Shared(g, (1, d), lambda j: (0, 0)),
         _Shared(w, (d, tn), lambda j: (0, j))],
        [_Out(n, tn, lambda j: j, F32)], [(d, BF16)], n // tn, joint_body=_proj_joint_body)[0]


def _merge_body(or_ref, oh_ref, wr_ref, wh_ref, ga_ref, gb_ref, o_ref):
    ya = _dot(or_ref[...], wr_ref[...])
    yb = _dot(oh_ref[...], wh_ref[...])
    o_ref[...] = (_sigmoid(ga_ref[...]) * ya + _sigmoid(gb_ref[...]) * yb).astype(BF16)


def _merge(o_r, o_h, w_r, w_h, proj):
    n = w_r.shape[1]
    return _dense_call(
        _merge_body, "merge", TILE_M_MERGE,
        [_Rows(*o_r, RET_V, lambda j: 0),
         _Rows(*o_h, HG_V, lambda j: 0),
         _Shared(w_r, (RET_V, n), lambda j: (0, 0), resident=True),
         _Shared(w_h, (HG_V, n), lambda j: (0, 0), resident=True),
         _Rows(*proj, n, lambda j: OFF_GA // n),
         _Rows(*proj, n, lambda j: OFF_GB // n)],
        [_Out(n, n, lambda j: 0, BF16)], [], 1)[0]


def _resid_body(x_ref, m_ref, w_ref, g_ref, o_ref, h_ref):
    x1 = x_ref[...] + _dot(m_ref[...], w_ref[...])
    o_ref[...] = x1
    h_ref[...] = _rms_rows(x1, g_ref[...]).astype(BF16)


def _resid(x, mrg, w, g):
    d = w.shape[1]
    return _dense_call(
        _resid_body, "resid", TILE_M_RESID,
        [_Rows(*x, d, lambda j: 0),
         _Rows(*mrg, d, lambda j: 0),
         _Shared(w, (d, d), lambda j: (0, 0), resident=True),
         _Shared(g, (1, d), lambda j: (0, 0))],
        [_Out(d, d, lambda j: 0, F32), _Out(d, d, lambda j: 0, BF16)], [], 1)


def _ffn_in_body(h_ref, wg_ref, wu_ref, o_ref):
    wg = wg_ref[...].astype(BF16)
    wu = wu_ref[...].astype(BF16)
    rows = h_ref.shape[0]
    step = min(rows, FFN_IN_ROW_CHUNK)
    for r in range(0, rows, step):
        h = h_ref[r:r + step, :]
        o_ref[r:r + step, :] = (_silu(_dot(h, wg)) * _dot(h, wu)).astype(BF16)


def _ffn_in(h, w, w_down):
    d = w.shape[0]
    tn = TILE_N_FFN_IN
    return _dense_call(
        _ffn_in_body, "ffn_in", TILE_M_FFN_IN,
        [_Rows(*h, d, lambda j: 0),
         _Shared(w, (d, tn), lambda j: (0, j)),
         _Shared(w, (d, tn), lambda j: (0, D_FF // tn + j))],
        [_Out(D_FF, tn, lambda j: j, BF16)], [], D_FF // tn, casts=[(w_down, CAST_ROWS_FFN)])


def _ffn_out_body(a_ref, w_ref, x_ref, g_ref, o_ref):
    o_ref[...] = _rms_rows(x_ref[...] + _dot(a_ref[...], w_ref[...]), g_ref[...])


def _ffn_out(act, w, x, g):
    k, d = w.shape
    return _dense_call(
        _ffn_out_body, "ffn_out", TILE_M_FFN_OUT,
        [_Rows(*act, k, lambda j: 0),
         _Shared(w, (k, d), lambda j: (0, 0), resident=True),
         _Rows(*x, d, lambda j: 0),
         _Shared(g, (1, d), lambda j: (0, 0))],
        [_Out(d, d, lambda j: 0, F32)], [], 1)[0]


def _rotary(x, cos, sin_signed):
    return x * cos + pltpu.roll(x, RET_DK // 2, 1) * sin_signed


def _ret_parts(p_ref, cos_ref, sin_ref, o_ref, so_ref, s_ref, dec_ref):
    c = CHUNK

    def init():
        s_ref[...] = jnp.zeros_like(s_ref)
        row = lax.broadcasted_iota(jnp.int32, (c, c), 0)
        col = lax.broadcasted_iota(jnp.int32, (c, c), 1)
        diff = (row - col).astype(F32)
        idx = lax.broadcasted_iota(jnp.int32, (c, RET_DK), 0).astype(F32)
        for h in range(RET_HEADS):
            lg = LOG_GAMMA[h]
            dec_ref[h, 0] = jnp.where(diff >= 0, jnp.exp(jnp.maximum(diff, 0.0) * lg), 0.0)
            dec_ref[h, 1] = jnp.exp((idx + 1.0) * lg)
            dec_ref[h, 2] = jnp.exp((c - 1.0 - idx) * lg)

    def step(heads):
        cos = cos_ref[...]
        sin = sin_ref[...]
        def rotate(h):
            q = _rotary(p_ref[0, :, OFF_RQ + h * RET_DK:OFF_RQ + (h + 1) * RET_DK], cos, sin)
            k = _rotary(p_ref[0, :, OFF_RK + h * RET_DK:OFF_RK + (h + 1) * RET_DK], cos, sin) * (RET_DK ** -0.5)
            return q, k

        def attend(h, q, k):
            v = p_ref[0, :, OFF_RV + h * RET_DV:OFF_RV + (h + 1) * RET_DV].astype(BF16)
            chunk_decay = math.exp(c * LOG_GAMMA[h])
            s = s_ref[h]
            scores = _dot_nt(q.astype(BF16), k.astype(BF16)) * dec_ref[h, 0]
            o = _dot(scores.astype(BF16), v) + _dot((q * dec_ref[h, 1]).astype(BF16), s.astype(BF16))
            s_ref[h] = chunk_decay * s + _dot_tn((k * dec_ref[h, 2]).astype(BF16), v)
            return o

        def emit(h, o):
            o = o * lax.rsqrt(jnp.mean(o * o, axis=-1, keepdims=True) + EPS)
            gate = p_ref[0, :, OFF_RG + h * RET_DV:OFF_RG + (h + 1) * RET_DV]
            o_ref[0, :, h * RET_DV:(h + 1) * RET_DV] = (o * _silu(gate)).astype(BF16)

        qk = [rotate(h) for h in heads]
        outs = [attend(h, q, k) for h, (q, k) in zip(heads, qk)]
        for h, o in zip(heads, outs):
            emit(h, o)

    def final():
        so_ref[0] = s_ref[...]

    return init, step, final


def _lower_bound(lbl):
    mx = jnp.max(lbl, axis=0, keepdims=True)
    e = jnp.exp(lbl - mx)
    return e[0:1, :] / jnp.sum(e, axis=0, keepdims=True)


def _hgrn_gates(z, lb):
    half = 0.5 * (1.0 - lb)
    ht = half * jnp.tanh(0.5 * z)
    return (lb + half) + ht, half - ht


def _cumsum_rows(tri, x):
    hi = x.astype(BF16)
    r1 = x - hi.astype(F32)
    mid = r1.astype(BF16)
    lo = (r1 - mid.astype(F32)).astype(BF16)
    return _dot(tri, hi) + _dot(tri, mid) + _dot(tri, lo)


def _bcast_rows(x, group, r):
    c, d = x.shape
    x3 = x.reshape(c // group, group, d)
    return jnp.broadcast_to(x3[:, r:r + 1, :], x3.shape).reshape(c, d)


def _hgrn_parts(qf_ref, ig_ref, lbl_ref, nrm_ref, o_ref, so_ref, st_ref):
    def init():
        st_ref[...] = jnp.zeros_like(st_ref)

    def final():
        for h in range(HG_HEADS):
            so_ref[0, h] = st_ref[h].T

    return init, functools.partial(_hgrn_step, qf_ref, ig_ref, lbl_ref, nrm_ref, o_ref, st_ref), final


def _hgrn_step(qf_ref, ig_ref, lbl_ref, nrm_ref, o_ref, st_ref, heads):
    c = CHUNK
    row = lax.broadcasted_iota(jnp.int32, (c, c), 0)
    col = lax.broadcasted_iota(jnp.int32, (c, c), 1)
    rowd = lax.broadcasted_iota(jnp.int32, (c, HG_DK), 0)
    tri = jnp.where(row >= col, 1.0, 0.0).astype(BF16)
    r2, r4 = rowd & 1, rowd & 3
    block_masks = [(row >> s) == (col >> s) for s in range(c.bit_length())]
    nrm = nrm_ref[...]

    def up(x, k):
        return pltpu.roll(x, k, 0)

    def dn(x, k):
        return pltpu.roll(x, c - k, 0)

    def gates(h):
        off = h * HG_DK
        lb = _lower_bound(lbl_ref[:, pl.ds(off, HG_DK)])
        f, kin = _hgrn_gates(qf_ref[0, :, pl.ds(HG_K + off, HG_DK)], lb)
        qh = _silu(qf_ref[0, :, pl.ds(off, HG_DK)])
        v = ig_ref[0, :, pl.ds(off, HG_DV)].astype(BF16)
        b = _cumsum_rows(tri, jnp.log2(f))
        return f, kin, qh, v, b

    def intra(f, kin, qh, b):
        f_up = f * up(f, 1)
        f_dn = f * dn(f, 1)
        pre2 = jnp.where(r2 == 1, f_up, f)
        pre4 = jnp.where(r4 >= 1, f_up, f)
        pre4 = jnp.where(r4 >= 2, pre4 * up(pre4, 2), pre4)
        suf4 = jnp.where(r4 <= 2, f_dn, f)
        suf4 = jnp.where(r4 <= 1, suf4 * dn(suf4, 2), suf4)

        def level(s):
            if s == 0:
                return qh, kin
            if s == 1:
                return qh * f, kin
            if s == 2:
                return qh * pre2, kin * jnp.where(r4 == 0, dn(f, 1), 1.0)
            if s == 3:
                return qh * pre4, kin * jnp.where(r4 == 3, 1.0, dn(suf4, 1))
            half = 1 << (s - 1)
            ref = _bcast_rows(b, 2 * half, half - 1)
            upper = (rowd & (2 * half - 1)) >= half
            return jnp.where(upper, qh * jnp.exp2(b - ref), 0.0), jnp.where(upper, 0.0, kin * jnp.exp2(ref - b))

        a = jnp.zeros((c, c), F32)
        for s in reversed(range(len(block_masks))):
            q_s, k_s = level(s)
            a = jnp.where(block_masks[s], _dot_nt(q_s.astype(BF16), k_s.astype(BF16)), a)
        return jnp.where(col <= row, a, 0.0)

    def finish(h, kin, qh, v, b, a):
        off = h * HG_DK
        st = st_ref[h]
        b_last = b[c - 1:c, :]
        o = _dot(a.astype(BF16), v) + _dot_nt((qh * jnp.exp2(b)).astype(BF16), st.astype(BF16))
        st_ref[h] = st * jnp.exp2(b_last) + _dot_tn(v, (kin * jnp.exp2(b_last - b)).astype(BF16))
        o = o * lax.rsqrt(jnp.mean(o * o, axis=-1, keepdims=True) + EPS) * nrm
        o_ref[0, :, pl.ds(off, HG_DV)] = (o * _silu(ig_ref[0, :, pl.ds(HG_V + off, HG_DV)])).astype(BF16)

    g = [gates(h) for h in heads]
    a = [intra(f, kin, qh, b) for f, kin, qh, v, b in g]
    for h, (f, kin, qh, v, b), a_h in zip(heads, g, a):
        finish(h, kin, qh, v, b, a_h)


def _mixers_kernel(n_cast, p_ref, cos_ref, sin_ref, qf_ref, ig_ref, lbl_ref, nrm_ref, *refs):
    cast_in, (or_ref, sor_ref, oh_ref, soh_ref), refs = refs[:n_cast], refs[n_cast:n_cast + 4], refs[n_cast + 4:]
    cast_out, (s_ref, dec_ref, st_ref) = refs[:n_cast], refs[n_cast:]
    for src, dst in zip(cast_in, cast_out):
        dst[...] = src[...].astype(BF16)
    n = pl.program_id(1)
    parts = (_ret_parts(p_ref, cos_ref, sin_ref, or_ref, sor_ref, s_ref, dec_ref),
             _hgrn_parts(qf_ref, ig_ref, lbl_ref, nrm_ref, oh_ref, soh_ref, st_ref))

    @pl.when(n == 0)
    def _():
        for init, _, _ in parts:
            init()

    for h in range(0, max(RET_HEADS, HG_HEADS), MIXER_HEAD_GROUP):
        for (_, step, _), n_heads in zip(parts, (RET_HEADS, HG_HEADS)):
            step(range(h, min(h + MIXER_HEAD_GROUP, n_heads)))

    @pl.when(n == pl.num_programs(1) - 1)
    def _():
        for _, _, final in parts:
            final()


def _mixers_prompt(proj3, cos, sin, lbl, nrm, later_weights):
    b, t, _ = proj3.shape
    nc = t // CHUNK
    cast_specs = [pl.BlockSpec((w.shape[0] // (b * nc), w.shape[1]), lambda i, n: (i * nc + n, 0))
                  for w in later_weights]
    return pl.pallas_call(
        functools.partial(_mixers_kernel, len(later_weights)),
        grid=(b, nc),
        in_specs=[pl.BlockSpec((1, CHUNK, OFF_HQ), lambda i, n: (i, n, 0)),
                  pl.BlockSpec((CHUNK, RET_DK), lambda i, n: (n, 0)),
                  pl.BlockSpec((CHUNK, RET_DK), lambda i, n: (n, 0)),
                  pl.BlockSpec((1, CHUNK, 2 * HG_K), lambda i, n: (i, n, OFF_HQ // (2 * HG_K))),
                  pl.BlockSpec((1, CHUNK, 2 * HG_V), lambda i, n: (i, n, OFF_HI // (2 * HG_V))),
                  pl.BlockSpec((DEPTH + 1, HG_K), lambda i, n: (0, 0)),
                  pl.BlockSpec((1, HG_DV), lambda i, n: (0, 0))] + cast_specs,
        out_specs=[pl.BlockSpec((1, CHUNK, RET_V), lambda i, n: (i, n, 0)),
                   pl.BlockSpec((1, RET_HEADS, RET_DK, RET_DV), lambda i, n: (i, 0, 0, 0)),
                   pl.BlockSpec((1, CHUNK, HG_V), lambda i, n: (i, n, 0)),
                   pl.BlockSpec((1, HG_HEADS, HG_DK, HG_DV), lambda i, n: (i, 0, 0, 0))] + cast_specs,
        out_shape=[jax.ShapeDtypeStruct((b, t, RET_V), BF16),
                   jax.ShapeDtypeStruct((b, RET_HEADS, RET_DK, RET_DV), F32),
                   jax.ShapeDtypeStruct((b, t, HG_V), BF16),
                   jax.ShapeDtypeStruct((b, HG_HEADS, HG_DK, HG_DV), F32)]
        + [jax.ShapeDtypeStruct(w.shape, BF16) for w in later_weights],
        scratch_shapes=[pltpu.VMEM((RET_HEADS, RET_DK, RET_DV), F32),
                        pltpu.VMEM((RET_HEADS, 3, CHUNK, RET_DK), F32),
                        pltpu.VMEM((HG_HEADS, HG_DV, HG_DK), F32)],
        compiler_params=_params(("arbitrary", "arbitrary")),
        name="mixers_prompt",
    )(proj3, cos, sin, proj3, proj3, lbl, nrm, *later_weights)


def _sample_prep_kernel(p_ref, cos_ref, sin_ref, lbl_ref, qr_ref, kr_ref, qh_ref, f_ref, kin_ref):
    cos = cos_ref[...]
    sin = sin_ref[...]
    for h in range(RET_HEADS):
        sl = slice(h * RET_DK, (h + 1) * RET_DK)
        qr_ref[:, sl] = _rotary(p_ref[:, OFF_RQ + h * RET_DK:OFF_RQ + (h + 1) * RET_DK], cos, sin)
        kr_ref[:, sl] = _rotary(p_ref[:, OFF_RK + h * RET_DK:OFF_RK + (h + 1) * RET_DK], cos, sin) * (RET_DK ** -0.5)
    f, kin = _hgrn_gates(p_ref[:, OFF_HF:OFF_HF + HG_K], _lower_bound(lbl_ref[...]))
    f_ref[...] = f
    kin_ref[...] = kin
    qh_ref[...] = _silu(p_ref[:, OFF_HQ:OFF_HQ + HG_K])


def _sample_prep(proj, cos, sin, lbl):
    m = proj.shape[0]
    shp = jax.ShapeDtypeStruct((m, RET_QK), F32)
    return pl.pallas_call(
        _sample_prep_kernel,
        grid=(1,),
        in_specs=[pl.BlockSpec((m, OFF_GA), lambda i: (0, 0)),
                  pl.BlockSpec((1, RET_DK), lambda i: (0, 0)),
                  pl.BlockSpec((1, RET_DK), lambda i: (0, 0)),
                  pl.BlockSpec((DEPTH + 1, HG_K), lambda i: (0, 0))],
        out_specs=[pl.BlockSpec((m, RET_QK), lambda i: (0, 0))] * 5,
        out_shape=[shp] * 5,
        compiler_params=_params(("arbitrary",)),
        name="sample_prep",
    )(proj, cos, sin, lbl)


def _sample_state_kernel(sr_ref, sh_ref, qr_ref, kr_ref, qh_ref, f_ref, kin_ref, vr_ref, rg_ref, vh_ref, hg_ref,
                         nrm_ref, sro_ref, sho_ref, or_ref, oh_ref):
    lanes = RET_DK
    for t in range(SAMPLE_SEQS_PER_STEP):
        b = pl.program_id(0) * SAMPLE_SEQS_PER_STEP + t
        base = pl.multiple_of((b // SUB) * SUB, SUB)

        def row(ref, sl, b=b, base=base):
            tile = ref[pl.ds(base, SUB), sl]
            sub = lax.broadcasted_iota(jnp.int32, tile.shape, 0)
            return jnp.sum(jnp.where(sub == b % SUB, tile, 0.0), axis=0, keepdims=True)

        def column(ref, h, row=row):
            r = row(ref, slice(h * RET_DK, (h + 1) * RET_DK))
            return jnp.broadcast_to(r, (RET_DK, RET_DK)).T

        for h in range(RET_HEADS):
            kc = column(kr_ref, h)
            qc = column(qr_ref, h)
            gamma = math.exp(LOG_GAMMA[h])
            outs = []
            for half in range(RET_DV // lanes):
                hs = slice(half * lanes, (half + 1) * lanes)
                s = gamma * sr_ref[t, h, :, hs] + kc * row(vr_ref, slice(h * RET_DV + hs.start, h * RET_DV + hs.stop))
                sro_ref[t, h, :, hs] = s
                outs.append(jnp.sum(qc * s, axis=0, keepdims=True))
            o = jnp.concatenate(outs, axis=1)
            o = o * lax.rsqrt(jnp.mean(o * o, axis=-1, keepdims=True) + EPS)
            sl = slice(h * RET_DV, (h + 1) * RET_DV)
            or_ref[t, :, sl] = o * _silu(row(rg_ref, sl))
        for h in range(HG_HEADS):
            sl = slice(h * HG_DV, (h + 1) * HG_DV)
            s = column(f_ref, h) * sh_ref[t, h] + column(kin_ref, h) * row(vh_ref, sl)
            sho_ref[t, h] = s
            o = jnp.sum(column(qh_ref, h) * s, axis=0, keepdims=True)
            o = o * lax.rsqrt(jnp.mean(o * o, axis=-1, keepdims=True) + EPS) * nrm_ref[...]
            oh_ref[t, :, sl] = o * _silu(row(hg_ref, sl))


def _sample_state(s_ret, s_hg, rows, proj, nrm):
    b = s_ret.shape[0]
    nb = SAMPLE_SEQS_PER_STEP
    row_spec = pl.BlockSpec((b, RET_QK), lambda i: (0, 0))
    return pl.pallas_call(
        _sample_state_kernel,
        grid=(b // nb,),
        in_specs=[pl.BlockSpec((nb, RET_HEADS, RET_DK, RET_DV), lambda i: (i, 0, 0, 0)),
                  pl.BlockSpec((nb, HG_HEADS, HG_DK, HG_DV), lambda i: (i, 0, 0, 0)),
                  row_spec, row_spec, row_spec, row_spec, row_spec,
                  pl.BlockSpec((b, RET_V), lambda i: (0, OFF_RV // RET_V)),
                  pl.BlockSpec((b, RET_V), lambda i: (0, OFF_RG // RET_V)),
                  pl.BlockSpec((b, HG_V), lambda i: (0, OFF_HI // HG_V)),
                  pl.BlockSpec((b, HG_V), lambda i: (0, OFF_HG // HG_V)),
                  pl.BlockSpec((1, HG_DV), lambda i: (0, 0))],
        out_specs=[pl.BlockSpec((nb, RET_HEADS, RET_DK, RET_DV), lambda i: (i, 0, 0, 0)),
                   pl.BlockSpec((nb, HG_HEADS, HG_DK, HG_DV), lambda i: (i, 0, 0, 0)),
                   pl.BlockSpec((nb, 1, RET_V), lambda i: (i, 0, 0)),
                   pl.BlockSpec((nb, 1, HG_V), lambda i: (i, 0, 0))],
        out_shape=[jax.ShapeDtypeStruct(s_ret.shape, F32),
                   jax.ShapeDtypeStruct(s_hg.shape, F32),
                   jax.ShapeDtypeStruct((b, 1, RET_V), F32),
                   jax.ShapeDtypeStruct((b, 1, HG_V), F32)],
        compiler_params=_params(("arbitrary",)),
        name="sample_state",
    )(s_ret, s_hg, *rows, proj, proj, proj, proj, nrm)


def _rope_tables(first, count):
    half = RET_DK // 2
    inv_freq = ROPE_BASE ** (-np.arange(half, dtype=np.float64) / half)
    ang = np.arange(first, first + count, dtype=np.float64)[:, None] * inv_freq[None, :]
    cos, sin = np.cos(ang), np.sin(ang)
    return (jnp.asarray(np.concatenate([cos, cos], axis=-1), F32),
            jnp.asarray(np.concatenate([-sin, sin], axis=-1), F32))


def kernel(x_prompt, x_sample, state_ret, state_hgrn, w_in, w_ret_out, w_hgrn_out, w_out, norm_mix, norm_ffn,
           hgrn_norm, hgrn_lb_logits, w_ffn_in, w_ffn_out, norm_final):
    assert w_in.shape[0] == DEPTH == 1 and x_sample.shape[1] == 1
    bp, t, d = x_prompt.shape
    bs = x_sample.shape[0]
    nrm = hgrn_norm[0][None, :]
    lbl = hgrn_lb_logits.astype(F32)
    x = (x_prompt.reshape(bp * t, d), x_sample.reshape(bs, d))

    proj = _proj(x, norm_mix[0][None, :], w_in[0])

    proj_p3 = proj[0].reshape(bp, t, IN_TOTAL)
    cos_p, sin_p = _rope_tables(0, t)
    o_r, s_ret_p, o_h, s_hg_p, wb_ret, wb_hgrn, wb_out = _mixers_prompt(
        proj_p3, cos_p, sin_p, lbl, nrm, [w_ret_out[0], w_hgrn_out[0], w_out[0]])

    cos_s, sin_s = _rope_tables(PAST_LEN, x_sample.shape[1])
    rows = _sample_prep(proj[1], cos_s, sin_s, lbl)
    s_ret_s, s_hg_s, o_rs, o_hs = _sample_state(state_ret[0], state_hgrn[0], rows, proj[1], nrm)

    o_r = (o_r.reshape(bp * t, RET_V), o_rs.reshape(bs, RET_V).astype(BF16))
    o_h = (o_h.reshape(bp * t, HG_V), o_hs.reshape(bs, HG_V).astype(BF16))
    mrg = _merge(o_r, o_h, wb_ret, wb_hgrn, proj)
    x1, h2 = _resid(x, mrg, wb_out, norm_ffn[0][None, :])
    act, wb_down = _ffn_in(h2, w_ffn_in[0], w_ffn_out[0])
    y_p, y_s = _ffn_out(act, wb_down, x1, norm_final[None, :])

    return (y_p.reshape(bp, t, d), y_s.reshape(bs, 1, d), s_ret_p[None], s_hg_p[None], s_ret_s[None], s_hg_s[None])
```

```python
import functools
import math
from typing import Any, Callable, NamedTuple

import numpy as np
import jax
import jax.numpy as jnp
from jax import lax
from jax.experimental import pallas as pl
from jax.experimental.pallas import tpu as pltpu

D_MODEL = 2048
DEPTH = 1
PAST_LEN = 16384
RET_HEADS = 8
RET_DK = 128
RET_DV = 256
RET_QK = RET_HEADS * RET_DK
RET_V = RET_HEADS * RET_DV
ROPE_BASE = 10000.0
HG_HEADS = 8
HG_DK = 128
HG_DV = 128
HG_K = HG_HEADS * HG_DK
HG_V = HG_HEADS * HG_DV
D_FF = 5632
EPS = 1e-6

OFF_RQ = 0
OFF_RK = OFF_RQ + RET_QK
OFF_RV = OFF_RK + RET_QK
OFF_RG = OFF_RV + RET_V
OFF_HQ = OFF_RG + RET_V
OFF_HF = OFF_HQ + HG_K
OFF_HI = OFF_HF + HG_K
OFF_HG = OFF_HI + HG_V
OFF_GA = OFF_HG + HG_V
OFF_GB = OFF_GA + D_MODEL
IN_TOTAL = OFF_GB + D_MODEL

CHUNK = 128
SUB = 8
MIXER_HEAD_GROUP = 8
SAMPLE_SEQS_PER_STEP = 4
V7X_VMEM_LIMIT = 58 * 1024 * 1024

F32 = jnp.float32
BF16 = jnp.bfloat16

LOG_GAMMA = [float(np.log(np.float32(1.0) - np.float32(2.0) ** np.float32(-5.0 - h))) for h in range(RET_HEADS)]


def _params(sem, vmem=V7X_VMEM_LIMIT):
    return pltpu.CompilerParams(dimension_semantics=sem, vmem_limit_bytes=vmem)


def _rms_rows(x, g):
    return (x * lax.rsqrt(jnp.mean(x * x, axis=-1, keepdims=True) + EPS)) * g


def _sigmoid(x):
    return 0.5 + 0.5 * jnp.tanh(0.5 * x)


def _silu(x):
    hx = 0.5 * x
    return hx + hx * jnp.tanh(hx)


def _dot(a, b):
    return jnp.dot(a, b, preferred_element_type=F32)


def _dot_nt(a, b):
    return lax.dot_general(a, b, (((1,), (1,)), ((), ())), preferred_element_type=F32)


def _dot_tn(a, b):
    return lax.dot_general(a, b, (((0,), (0,)), ((), ())), preferred_element_type=F32)


TILE_M_PROJ, TILE_N_PROJ = 2048, 512
TILE_M_MERGE = 512
TILE_M_RESID = 512
TILE_M_FFN_IN, TILE_N_FFN_IN = 2048, 512
TILE_M_FFN_OUT = 512
FFN_IN_ROW_CHUNK = 512
CAST_ROWS_FFN = 128


class _Rows(NamedTuple):
    prompt: Any
    sample: Any
    cols: int
    col: Callable
    single: bool = False


class _Shared(NamedTuple):
    array: Any
    block: tuple
    index: Callable
    resident: bool = False


class _Out(NamedTuple):
    total_cols: int
    cols: int
    col: Callable
    dtype: Any


def _dense_call(body, name, tile_m, operands, outs, scratch, inner, casts=(), joint_body=None):
    rows = [op for op in operands if isinstance(op, _Rows)]
    m_p, m_s = rows[0].prompt.shape[0], rows[0].sample.shape[0]
    last = m_p // tile_m - 1

    def frozen(i, j):
        return jnp.where(i == last, j, 0)

    specs_p, specs_s, args_p, args_s = [], [], [], []
    for op in operands:
        if isinstance(op, _Rows):
            mode = dict(pipeline_mode=pl.Buffered(1)) if op.single else {}
            specs_p.append(pl.BlockSpec((tile_m, op.cols), lambda i, j, op=op: (i, op.col(j)), **mode))
            specs_s.append(pl.BlockSpec((m_s, op.cols), lambda i, j, op=op: (0, op.col(frozen(i, j)))))
            args_p.append(op.prompt)
            args_s.append(op.sample)
        else:
            mode = dict(pipeline_mode=pl.Buffered(1)) if op.resident else {}
            specs_p.append(pl.BlockSpec(op.block, lambda i, j, op=op: op.index(j), **mode))
            args_p.append(op.array)
    out_specs = ([pl.BlockSpec((tile_m, o.cols), lambda i, j, o=o: (i, o.col(j))) for o in outs]
                 + [pl.BlockSpec((m_s, o.cols), lambda i, j, o=o: (0, o.col(frozen(i, j)))) for o in outs])
    out_shape = ([jax.ShapeDtypeStruct((m_p, o.total_cols), o.dtype) for o in outs]
                 + [jax.ShapeDtypeStruct((m_s, o.total_cols), o.dtype) for o in outs])
    if joint_body is None:
        scratch_shapes = ([pltpu.VMEM((tile_m, c), dt) for c, dt in scratch]
                          + [pltpu.VMEM((m_s, c), dt) for c, dt in scratch])
    else:
        scratch_shapes = [pltpu.VMEM((tile_m + m_s, c), dt) for c, dt in scratch]
    n_in, n_out, n_scr = len(operands), len(outs), len(scratch)
    row_pos = [k for k, op in enumerate(operands) if isinstance(op, _Rows)]
    n_rows, n_cast = len(row_pos), len(casts)
    cast_blocks = [w.shape[0] // rb for w, rb in casts]
    assert all(nb <= (m_p // tile_m) * inner for nb in cast_blocks)
    cast_specs = [pl.BlockSpec((rb, w.shape[1]), lambda i, j, nb=nb: (jnp.minimum(i * inner + j, nb - 1), 0))
                  for (w, rb), nb in zip(casts, cast_blocks)]

    def kernel(*refs):
        ins_p = list(refs[:n_in])
        ins_s = list(ins_p)
        for k, ref in zip(row_pos, refs[n_in:n_in + len(row_pos)]):
            ins_s[k] = ref
        r = n_in + n_rows
        cast_in = refs[r:r + n_cast]
        r += n_cast
        outs_p, outs_s = refs[r:r + n_out], refs[r + n_out:r + 2 * n_out]
        r += 2 * n_out
        cast_out = refs[r:r + n_cast]
        r += n_cast
        is_last = pl.program_id(0) == last
        if joint_body is None:
            body(*ins_p, *outs_p, *refs[r:r + n_scr])

            @pl.when(is_last)
            def _():
                body(*ins_s, *outs_s, *refs[r + n_scr:])
        else:
            scr = refs[r:r + n_scr]

            @pl.when(jnp.logical_not(is_last))
            def _():
                body(*ins_p, *outs_p, *[sc.at[pl.ds(0, tile_m)] for sc in scr])

            @pl.when(is_last)
            def _():
                joint_body(ins_p, refs[n_in:n_in + n_rows], outs_p, outs_s, scr)

        step = pl.program_id(0) * inner + pl.program_id(1)
        for src, dst, nb in zip(cast_in, cast_out, cast_blocks):
            @pl.when(step < nb)
            def _(src=src, dst=dst):
                dst[...] = src[...].astype(BF16)

    res = pl.pallas_call(
        kernel,
        grid=(m_p // tile_m, inner),
        in_specs=specs_p + specs_s + cast_specs,
        out_specs=out_specs + cast_specs,
        out_shape=out_shape + [jax.ShapeDtypeStruct(w.shape, BF16) for w, _ in casts],
        scratch_shapes=scratch_shapes,
        compiler_params=_params(("arbitrary", "arbitrary")),
        name=name,
    )(*args_p, *args_s, *[w for w, _ in casts])
    return list(zip(res[:n_out], res[n_out:2 * n_out])) + list(res[2 * n_out:])


def _proj_body(x_ref, g_ref, w_ref, o_ref, h_ref):
    @pl.when(pl.program_id(1) == 0)
    def _():
        h_ref[...] = _rms_rows(x_ref[...], g_ref[...]).astype(BF16)
    o_ref[...] = _dot(h_ref[...], w_ref[...].astype(BF16))


def _proj_joint_body(ins, sample_rows, outs_p, outs_s, scratch):
    (x_ref, g_ref, w_ref), (xs_ref,), (o_ref,), (os_ref,), (h_ref,) = ins, sample_rows, outs_p, outs_s, scratch
    tm = x_ref.shape[0]

    @pl.when(pl.program_id(1) == 0)
    def _():
        h_ref[:tm, :] = _rms_rows(x_ref[...], g_ref[...]).astype(BF16)
        h_ref[tm:, :] = _rms_rows(xs_ref[...], g_ref[...]).astype(BF16)
    res = _dot(h_ref[...], w_ref[...].astype(BF16))
    o_ref[...] = res[:tm]
    os_ref[...] = res[tm:]


def _proj(x, g, w):
    d, n = w.shape
    tn = TILE_N_PROJ
    return _dense_call(
        _proj_body, "proj", TILE_M_PROJ,
        [_Rows(*x, d, lambda j: 0, single=True),
         _Shared(g, (1, d), lambda j: (0, 0)),
         _Shared(w, (d, tn), lambda j: (0, j))],
        [_Out(n, tn, lambda j: j, F32)], [(d, BF16)], n // tn, joint_body=_proj_joint_body)[0]


def _merge_body(or_ref, oh_ref, wr_ref, wh_ref, ga_ref, gb_ref, o_ref):
    ya = _dot(or_ref[...], wr_ref[...])
    yb = _dot(oh_ref[...], wh_ref[...])
    o_ref[...] = (_sigmoid(ga_ref[...]) * ya + _sigmoid(gb_ref[...]) * yb).astype(BF16)


def _merge(o_r, o_h, w_r, w_h, proj):
    n = w_r.shape[1]
    return _dense_call(
        _merge_body, "merge", TILE_M_MERGE,
        [_Rows(*o_r, RET_V, lambda j: 0),
         _Rows(*o_h, HG_V, lambda j: 0),
         _Shared(w_r, (RET_V, n), lambda j: (0, 0), resident=True),
         _Shared(w_h, (HG_V, n), lambda j: (0, 0), resident=True),
         _Rows(*proj, n, lambda j: OFF_GA // n),
         _Rows(*proj, n, lambda j: OFF_GB // n)],
        [_Out(n, n, lambda j: 0, BF16)], [], 1)[0]


def _resid_body(x_ref, m_ref, w_ref, g_ref, o_ref, h_ref):
    x1 = x_ref[...] + _dot(m_ref[...], w_ref[...])
    o_ref[...] = x1
    h_ref[...] = _rms_rows(x1, g_ref[...]).astype(BF16)


def _resid(x, mrg, w, g):
    d = w.shape[1]
    return _dense_call(
        _resid_body, "resid", TILE_M_RESID,
        [_Rows(*x, d, lambda j: 0),
         _Rows(*mrg, d, lambda j: 0),
         _Shared(w, (d, d), lambda j: (0, 0), resident=True),
         _Shared(g, (1, d), lambda j: (0, 0))],
        [_Out(d, d, lambda j: 0, F32), _Out(d, d, lambda j: 0, BF16)], [], 1)


def _ffn_in_body(h_ref, wg_ref, wu_ref, o_ref):
    wg = wg_ref[...].astype(BF16)
    wu = wu_ref[...].astype(BF16)
    rows = h_ref.shape[0]
    step = min(rows, FFN_IN_ROW_CHUNK)
    for r in range(0, rows, step):
        h = h_ref[r:r + step, :]
        o_ref[r:r + step, :] = (_silu(_dot(h, wg)) * _dot(h, wu)).astype(BF16)


def _ffn_in(h, w, w_down):
    d = w.shape[0]
    tn = TILE_N_FFN_IN
    return _dense_call(
        _ffn_in_body, "ffn_in", TILE_M_FFN_IN,
        [_Rows(*h, d, lambda j: 0),
         _Shared(w, (d, tn), lambda j: (0, j)),
         _Shared(w, (d, tn), lambda j: (0, D_FF // tn + j))],
        [_Out(D_FF, tn, lambda j: j, BF16)], [], D_FF // tn, casts=[(w_down, CAST_ROWS_FFN)])


def _ffn_out_body(a_ref, w_ref, x_ref, g_ref, o_ref):
    o_ref[...] = _rms_rows(x_ref[...] + _dot(a_ref[...], w_ref[...]), g_ref[...])


def _ffn_out(act, w, x, g):
    k, d = w.shape
    return _dense_call(
        _ffn_out_body, "ffn_out", TILE_M_FFN_OUT,
        [_Rows(*act, k, lambda j: 0),
         _Shared(w, (k, d), lambda j: (0, 0), resident=True),
         _Rows(*x, d, lambda j: 0),
         _Shared(g, (1, d), lambda j: (0, 0))],
        [_Out(d, d, lambda j: 0, F32)], [], 1)[0]


def _rotary(x, cos, sin_signed):
    return x * cos + pltpu.roll(x, RET_DK // 2, 1) * sin_signed


def _ret_parts(p_ref, cos_ref, sin_ref, o_ref, so_ref, s_ref, dec_ref):
    c = CHUNK

    def init():
        s_ref[...] = jnp.zeros_like(s_ref)
        row = lax.broadcasted_iota(jnp.int32, (c, c), 0)
        col = lax.broadcasted_iota(jnp.int32, (c, c), 1)
        diff = (row - col).astype(F32)
        idx = lax.broadcasted_iota(jnp.int32, (c, RET_DK), 0).astype(F32)
        for h in range(RET_HEADS):
            lg = LOG_GAMMA[h]
            dec_ref[h, 0] = jnp.where(diff >= 0, jnp.exp(jnp.maximum(diff, 0.0) * lg), 0.0)
            dec_ref[h, 1] = jnp.exp((idx + 1.0) * lg)
            dec_ref[h, 2] = jnp.exp((c - 1.0 - idx) * lg)

    def step(heads):
        cos = cos_ref[...]
        sin = sin_ref[...]
        def rotate(h):
            q = _rotary(p_ref[0, :, OFF_RQ + h * RET_DK:OFF_RQ + (h + 1) * RET_DK], cos, sin)
            k = _rotary(p_ref[0, :, OFF_RK + h * RET_DK:OFF_RK + (h + 1) * RET_DK], cos, sin) * (RET_DK ** -0.5)
            return q, k

        def attend(h, q, k):
            v = p_ref[0, :, OFF_RV + h * RET_DV:OFF_RV + (h + 1) * RET_DV].astype(BF16)
            chunk_decay = math.exp(c * LOG_GAMMA[h])
            s = s_ref[h]
            scores = _dot_nt(q.astype(BF16), k.astype(BF16)) * dec_ref[h, 0]
            o = _dot(scores.astype(BF16), v) + _dot((q * dec_ref[h, 1]).astype(BF16), s.astype(BF16))
            s_ref[h] = chunk_decay * s + _dot_tn((k * dec_ref[h, 2]).astype(BF16), v)
            return o

        def emit(h, o):
            o = o * lax.rsqrt(jnp.mean(o * o, axis=-1, keepdims=True) + EPS)
            gate = p_ref[0, :, OFF_RG + h * RET_DV:OFF_RG + (h + 1) * RET_DV]
            o_ref[0, :, h * RET_DV:(h + 1) * RET_DV] = (o * _silu(gate)).astype(BF16)

        qk = [rotate(h) for h in heads]
        outs = [attend(h, q, k) for h, (q, k) in zip(heads, qk)]
        for h, o in zip(heads, outs):
            emit(h, o)

    def final():
        so_ref[0] = s_ref[...]

    return init, step, final


def _lower_bound(lbl):
    mx = jnp.max(lbl, axis=0, keepdims=True)
    e = jnp.exp(lbl - mx)
    return e[0:1, :] / jnp.sum(e, axis=0, keepdims=True)


def _hgrn_gates(z, lb):
    half = 0.5 * (1.0 - lb)
    ht = half * jnp.tanh(0.5 * z)
    return (lb + half) + ht, half - ht


def _cumsum_rows(tri, x):
    hi = x.astype(BF16)
    r1 = x - hi.astype(F32)
    mid = r1.astype(BF16)
    lo = (r1 - mid.astype(F32)).astype(BF16)
    return _dot(tri, hi) + _dot(tri, mid) + _dot(tri, lo)


def _bcast_rows(x, group, r):
    c, d = x.shape
    x3 = x.reshape(c // group, group, d)
    return jnp.broadcast_to(x3[:, r:r + 1, :], x3.shape).reshape(c, d)


def _hgrn_parts(qf_ref, ig_ref, lbl_ref, nrm_ref, o_ref, so_ref, st_ref):
    def init():
        st_ref[...] = jnp.zeros_like(st_ref)

    def final():
        for h in range(HG_HEADS):
            so_ref[0, h] = st_ref[h].T

    return init, functools.partial(_hgrn_step, qf_ref, ig_ref, lbl_ref, nrm_ref, o_ref, st_ref), final


def _hgrn_step(qf_ref, ig_ref, lbl_ref, nrm_ref, o_ref, st_ref, heads):
    c = CHUNK
    row = lax.broadcasted_iota(jnp.int32, (c, c), 0)
    col = lax.broadcasted_iota(jnp.int32, (c, c), 1)
    rowd = lax.broadcasted_iota(jnp.int32, (c, HG_DK), 0)
    tri = jnp.where(row >= col, 1.0, 0.0).astype(BF16)
    r2, r4 = rowd & 1, rowd & 3
    block_masks = [(row >> s) == (col >> s) for s in range(c.bit_length())]
    nrm = nrm_ref[...]

    def up(x, k):
        return pltpu.roll(x, k, 0)

    def dn(x, k):
        return pltpu.roll(x, c - k, 0)

    def gates(h):
        off = h * HG_DK
        lb = _lower_bound(lbl_ref[:, pl.ds(off, HG_DK)])
        f, kin = _hgrn_gates(qf_ref[0, :, pl.ds(HG_K + off, HG_DK)], lb)
        qh = _silu(qf_ref[0, :, pl.ds(off, HG_DK)])
        v = ig_ref[0, :, pl.ds(off, HG_DV)].astype(BF16)
        b = _cumsum_rows(tri, jnp.log2(f))
        return f, kin, qh, v, b

    def intra(f, kin, qh, b):
        f_up = f * up(f, 1)
        f_dn = f * dn(f, 1)
        pre2 = jnp.where(r2 == 1, f_up, f)
        pre4 = jnp.where(r4 >= 1, f_up, f)
        pre4 = jnp.where(r4 >= 2, pre4 * up(pre4, 2), pre4)
        suf4 = jnp.where(r4 <= 2, f_dn, f)
        suf4 = jnp.where(r4 <= 1, suf4 * dn(suf4, 2), suf4)

        def level(s):
            if s == 0:
                return qh, kin
            if s == 1:
                return qh * f, kin
            if s == 2:
                return qh * pre2, kin * jnp.where(r4 == 0, dn(f, 1), 1.0)
            if s == 3:
                return qh * pre4, kin * jnp.where(r4 == 3, 1.0, dn(suf4, 1))
            half = 1 << (s - 1)
            ref = _bcast_rows(b, 2 * half, half - 1)
            upper = (rowd & (2 * half - 1)) >= half
            return jnp.where(upper, qh * jnp.exp2(b - ref), 0.0), jnp.where(upper, 0.0, kin * jnp.exp2(ref - b))

        a = jnp.zeros((c, c), F32)
        for s in reversed(range(len(block_masks))):
            q_s, k_s = level(s)
            a = jnp.where(block_masks[s], _dot_nt(q_s.astype(BF16), k_s.astype(BF16)), a)
        return jnp.where(col <= row, a, 0.0)

    def finish(h, kin, qh, v, b, a):
        off = h * HG_DK
        st = st_ref[h]
        b_last = b[c - 1:c, :]
        o = _dot(a.astype(BF16), v) + _dot_nt((qh * jnp.exp2(b)).astype(BF16), st.astype(BF16))
        st_ref[h] = st * jnp.exp2(b_last) + _dot_tn(v, (kin * jnp.exp2(b_last - b)).astype(BF16))
        o = o * lax.rsqrt(jnp.mean(o * o, axis=-1, keepdims=True) + EPS) * nrm
        o_ref[0, :, pl.ds(off, HG_DV)] = (o * _silu(ig_ref[0, :, pl.ds(HG_V + off, HG_DV)])).astype(BF16)

    g = [gates(h) for h in heads]
    a = [intra(f, kin, qh, b) for f, kin, qh, v, b in g]
    for h, (f, kin, qh, v, b), a_h in zip(heads, g, a):
        finish(h, kin, qh, v, b, a_h)


def _mixers_kernel(n_cast, p_ref, cos_ref, sin_ref, qf_ref, ig_ref, lbl_ref, nrm_ref, *refs):
    cast_in, (or_ref, sor_ref, oh_ref, soh_ref), refs = refs[:n_cast], refs[n_cast:n_cast + 4], refs[n_cast + 4:]
    cast_out, (s_ref, dec_ref, st_ref) = refs[:n_cast], refs[n_cast:]
    for src, dst in zip(cast_in, cast_out):
        dst[...] = src[...].astype(BF16)
    n = pl.program_id(1)
    parts = (_ret_parts(p_ref, cos_ref, sin_ref, or_ref, sor_ref, s_ref, dec_ref),
             _hgrn_parts(qf_ref, ig_ref, lbl_ref, nrm_ref, oh_ref, soh_ref, st_ref))

    @pl.when(n == 0)
    def _():
        for init, _, _ in parts:
            init()

    for h in range(0, max(RET_HEADS, HG_HEADS), MIXER_HEAD_GROUP):
        for (_, step, _), n_heads in zip(parts, (RET_HEADS, HG_HEADS)):
            step(range(h, min(h + MIXER_HEAD_GROUP, n_heads)))

    @pl.when(n == pl.num_programs(1) - 1)
    def _():
        for _, _, final in parts:
            final()


def _mixers_prompt(proj3, cos, sin, lbl, nrm, later_weights):
    b, t, _ = proj3.shape
    nc = t // CHUNK
    cast_specs = [pl.BlockSpec((w.shape[0] // (b * nc), w.shape[1]), lambda i, n: (i * nc + n, 0))
                  for w in later_weights]
    return pl.pallas_call(
        functools.partial(_mixers_kernel, len(later_weights)),
        grid=(b, nc),
        in_specs=[pl.BlockSpec((1, CHUNK, OFF_HQ), lambda i, n: (i, n, 0)),
                  pl.BlockSpec((CHUNK, RET_DK), lambda i, n: (n, 0)),
                  pl.BlockSpec((CHUNK, RET_DK), lambda i, n: (n, 0)),
                  pl.BlockSpec((1, CHUNK, 2 * HG_K), lambda i, n: (i, n, OFF_HQ // (2 * HG_K))),
                  pl.BlockSpec((1, CHUNK, 2 * HG_V), lambda i, n: (i, n, OFF_HI // (2 * HG_V))),
                  pl.BlockSpec((DEPTH + 1, HG_K), lambda i, n: (0, 0)),
                  pl.BlockSpec((1, HG_DV), lambda i, n: (0, 0))] + cast_specs,
        out_specs=[pl.BlockSpec((1, CHUNK, RET_V), lambda i, n: (i, n, 0)),
                   pl.BlockSpec((1, RET_HEADS, RET_DK, RET_DV), lambda i, n: (i, 0, 0, 0)),
                   pl.BlockSpec((1, CHUNK, HG_V), lambda i, n: (i, n, 0)),
                   pl.BlockSpec((1, HG_HEADS, HG_DK, HG_DV), lambda i, n: (i, 0, 0, 0))] + cast_specs,
        out_shape=[jax.ShapeDtypeStruct((b, t, RET_V), BF16),
                   jax.ShapeDtypeStruct((b, RET_HEADS, RET_DK, RET_DV), F32),
                   jax.ShapeDtypeStruct((b, t, HG_V), BF16),
                   jax.ShapeDtypeStruct((b, HG_HEADS, HG_DK, HG_DV), F32)]
        + [jax.ShapeDtypeStruct(w.shape, BF16) for w in later_weights],
        scratch_shapes=[pltpu.VMEM((RET_HEADS, RET_DK, RET_DV), F32),
                        pltpu.VMEM((RET_HEADS, 3, CHUNK, RET_DK), F32),
                        pltpu.VMEM((HG_HEADS, HG_DV, HG_DK), F32)],
        compiler_params=_params(("arbitrary", "arbitrary")),
        name="mixers_prompt",
    )(proj3, cos, sin, proj3, proj3, lbl, nrm, *later_weights)


def _sample_prep_kernel(p_ref, cos_ref, sin_ref, lbl_ref, qr_ref, kr_ref, qh_ref, f_ref, kin_ref):
    cos = cos_ref[...]
    sin = sin_ref[...]
    for h in range(RET_HEADS):
        sl = slice(h * RET_DK, (h + 1) * RET_DK)
        qr_ref[:, sl] = _rotary(p_ref[:, OFF_RQ + h * RET_DK:OFF_RQ + (h + 1) * RET_DK], cos, sin)
        kr_ref[:, sl] = _rotary(p_ref[:, OFF_RK + h * RET_DK:OFF_RK + (h + 1) * RET_DK], cos, sin) * (RET_DK ** -0.5)
    f, kin = _hgrn_gates(p_ref[:, OFF_HF:OFF_HF + HG_K], _lower_bound(lbl_ref[...]))
    f_ref[...] = f
    kin_ref[...] = kin
    qh_ref[...] = _silu(p_ref[:, OFF_HQ:OFF_HQ + HG_K])


def _sample_prep(proj, cos, sin, lbl):
    m = proj.shape[0]
    shp = jax.ShapeDtypeStruct((m, RET_QK), F32)
    return pl.pallas_call(
        _sample_prep_kernel,
        grid=(1,),
        in_specs=[pl.BlockSpec((m, OFF_GA), lambda i: (0, 0)),
                  pl.BlockSpec((1, RET_DK), lambda i: (0, 0)),
                  pl.BlockSpec((1, RET_DK), lambda i: (0, 0)),
                  pl.BlockSpec((DEPTH + 1, HG_K), lambda i: (0, 0))],
        out_specs=[pl.BlockSpec((m, RET_QK), lambda i: (0, 0))] * 5,
        out_shape=[shp] * 5,
        compiler_params=_params(("arbitrary",)),
        name="sample_prep",
    )(proj, cos, sin, lbl)


def _sample_state_kernel(sr_ref, sh_ref, qr_ref, kr_ref, qh_ref, f_ref, kin_ref, vr_ref, rg_ref, vh_ref, hg_ref,
                         nrm_ref, sel_ref, sro_ref, sho_ref, or_ref, oh_ref):
    lanes = RET_DK
    vec_refs = (qr_ref, kr_ref, qh_ref, f_ref, kin_ref)
    sub8 = lax.broadcasted_iota(jnp.int32, (SUB, lanes), 0)
    for t in range(SAMPLE_SEQS_PER_STEP):
        b = pl.program_id(0) * SAMPLE_SEQS_PER_STEP + t
        base = pl.multiple_of((b // SUB) * SUB, SUB)

        def row(ref, sl, b=b, base=base):
            tile = ref[pl.ds(base, SUB), sl]
            sub = lax.broadcasted_iota(jnp.int32, tile.shape, 0)
            return jnp.sum(jnp.where(sub == b % SUB, tile, 0.0), axis=0, keepdims=True)

        def columns(h, row=row):
            x = jnp.zeros((SUB, lanes), F32)
            for j, ref in enumerate(vec_refs):
                x = jnp.where(sub8 == j, row(ref, slice(h * RET_DK, (h + 1) * RET_DK)), x)
            hi = x.astype(BF16).astype(F32)
            mid = (x - hi).astype(BF16).astype(F32)
            lo = x - hi - mid
            terms = jnp.concatenate([hi, mid, lo, jnp.zeros_like(x)], axis=0).astype(BF16)
            cols = _dot_tn(terms, sel_ref[...])
            return [cols[:, j * lanes:(j + 1) * lanes] for j in range(len(vec_refs))]

        all_cols = [columns(h) for h in range(RET_HEADS)]
        for h in range(RET_HEADS):
            qc, kc = all_cols[h][0], all_cols[h][1]
            gamma = math.exp(LOG_GAMMA[h])
            outs = []
            for half in range(RET_DV // lanes):
                hs = slice(half * lanes, (half + 1) * lanes)
                s = gamma * sr_ref[t, h, :, hs] + kc * row(vr_ref, slice(h * RET_DV + hs.start, h * RET_DV + hs.stop))
                sro_ref[t, h, :, hs] = s
                outs.append(jnp.sum(qc * s, axis=0, keepdims=True))
            o = jnp.concatenate(outs, axis=1)
            o = o * lax.rsqrt(jnp.mean(o * o, axis=-1, keepdims=True) + EPS)
            sl = slice(h * RET_DV, (h + 1) * RET_DV)
            or_ref[t, :, sl] = o * _silu(row(rg_ref, sl))
        for h in range(HG_HEADS):
            sl = slice(h * HG_DV, (h + 1) * HG_DV)
            _, _, qc, fc, kc = all_cols[h]
            s = fc * sh_ref[t, h] + kc * row(vh_ref, sl)
            sho_ref[t, h] = s
            o = jnp.sum(qc * s, axis=0, keepdims=True)
            o = o * lax.rsqrt(jnp.mean(o * o, axis=-1, keepdims=True) + EPS) * nrm_ref[...]
            oh_ref[t, :, sl] = o * _silu(row(hg_ref, sl))


def _sample_state(s_ret, s_hg, rows, proj, nrm):
    b = s_ret.shape[0]
    nb = SAMPLE_SEQS_PER_STEP
    n_vec, n_terms = 5, 3
    sel = np.zeros(((n_terms + 1) * SUB, n_vec * RET_DK), np.float32)
    for p in range(n_terms):
        for j in range(n_vec):
            sel[SUB * p + j, j * RET_DK:(j + 1) * RET_DK] = 1.0
    sel = jnp.asarray(sel, BF16)
    row_spec = pl.BlockSpec((b, RET_QK), lambda i: (0, 0))
    return pl.pallas_call(
        _sample_state_kernel,
        grid=(b // nb,),
        in_specs=[pl.BlockSpec((nb, RET_HEADS, RET_DK, RET_DV), lambda i: (i, 0, 0, 0)),
                  pl.BlockSpec((nb, HG_HEADS, HG_DK, HG_DV), lambda i: (i, 0, 0, 0)),
                  row_spec, row_spec, row_spec, row_spec, row_spec,
                  pl.BlockSpec((b, RET_V), lambda i: (0, OFF_RV // RET_V)),
                  pl.BlockSpec((b, RET_V), lambda i: (0, OFF_RG // RET_V)),
                  pl.BlockSpec((b, HG_V), lambda i: (0, OFF_HI // HG_V)),
                  pl.BlockSpec((b, HG_V), lambda i: (0, OFF_HG // HG_V)),
                  pl.BlockSpec((1, HG_DV), lambda i: (0, 0)),
                  pl.BlockSpec(sel.shape, lambda i: (0, 0))],
        out_specs=[pl.BlockSpec((nb, RET_HEADS, RET_DK, RET_DV), lambda i: (i, 0, 0, 0)),
                   pl.BlockSpec((nb, HG_HEADS, HG_DK, HG_DV), lambda i: (i, 0, 0, 0)),
                   pl.BlockSpec((nb, 1, RET_V), lambda i: (i, 0, 0)),
                   pl.BlockSpec((nb, 1, HG_V), lambda i: (i, 0, 0))],
        out_shape=[jax.ShapeDtypeStruct(s_ret.shape, F32),
                   jax.ShapeDtypeStruct(s_hg.shape, F32),
                   jax.ShapeDtypeStruct((b, 1, RET_V), F32),
                   jax.ShapeDtypeStruct((b, 1, HG_V), F32)],
        compiler_params=_params(("arbitrary",)),
        name="sample_state",
    )(s_ret, s_hg, *rows, proj, proj, proj, proj, nrm, sel)


def _rope_tables(first, count):
    half = RET_DK // 2
    inv_freq = ROPE_BASE ** (-np.arange(half, dtype=np.float64) / half)
    ang = np.arange(first, first + count, dtype=np.float64)[:, None] * inv_freq[None, :]
    cos, sin = np.cos(ang), np.sin(ang)
    return (jnp.asarray(np.concatenate([cos, cos], axis=-1), F32),
            jnp.asarray(np.concatenate([-sin, sin], axis=-1), F32))


def kernel(x_prompt, x_sample, state_ret, state_hgrn, w_in, w_ret_out, w_hgrn_out, w_out, norm_mix, norm_ffn,
           hgrn_norm, hgrn_lb_logits, w_ffn_in, w_ffn_out, norm_final):
    assert w_in.shape[0] == DEPTH == 1 and x_sample.shape[1] == 1
    bp, t, d = x_prompt.shape
    bs = x_sample.shape[0]
    nrm = hgrn_norm[0][None, :]
    lbl = hgrn_lb_logits.astype(F32)
    x = (x_prompt.reshape(bp * t, d), x_sample.reshape(bs, d))

    proj = _proj(x, norm_mix[0][None, :], w_in[0])

    proj_p3 = proj[0].reshape(bp, t, IN_TOTAL)
    cos_p, sin_p = _rope_tables(0, t)
    o_r, s_ret_p, o_h, s_hg_p, wb_ret, wb_hgrn, wb_out = _mixers_prompt(
        proj_p3, cos_p, sin_p, lbl, nrm, [w_ret_out[0], w_hgrn_out[0], w_out[0]])

    cos_s, sin_s = _rope_tables(PAST_LEN, x_sample.shape[1])
    rows = _sample_prep(proj[1], cos_s, sin_s, lbl)
    s_ret_s, s_hg_s, o_rs, o_hs = _sample_state(state_ret[0], state_hgrn[0], rows, proj[1], nrm)

    o_r = (o_r.reshape(bp * t, RET_V), o_rs.reshape(bs, RET_V).astype(BF16))
    o_h = (o_h.reshape(bp * t, HG_V), o_hs.reshape(bs, HG_V).astype(BF16))
    mrg = _merge(o_r, o_h, wb_ret, wb_hgrn, proj)
    x1, h2 = _resid(x, mrg, wb_out, norm_ffn[0][None, :])
    act, wb_down = _ffn_in(h2, w_ffn_in[0], w_ffn_out[0])
    y_p, y_s = _ffn_out(act, wb_down, x1, norm_final[None, :])

    return (y_p.reshape(bp, t, d), y_s.reshape(bs, 1, d), s_ret_p[None], s_hg_p[None], s_ret_s[None], s_hg_s[None])
```

```python
import functools
import math
from typing import Any, Callable, NamedTuple

import numpy as np
import jax
import jax.numpy as jnp
from jax import lax
from jax.experimental import pallas as pl
from jax.experimental.pallas import tpu as pltpu

D_MODEL = 2048
DEPTH = 1
PAST_LEN = 16384
RET_HEADS = 8
RET_DK = 128
RET_DV = 256
RET_QK = RET_HEADS * RET_DK
RET_V = RET_HEADS * RET_DV
ROPE_BASE = 10000.0
HG_HEADS = 8
HG_DK = 128
HG_DV = 128
HG_K = HG_HEADS * HG_DK
HG_V = HG_HEADS * HG_DV
D_FF = 5632
EPS = 1e-6

OFF_RQ = 0
OFF_RK = OFF_RQ + RET_QK
OFF_RV = OFF_RK + RET_QK
OFF_RG = OFF_RV + RET_V
OFF_HQ = OFF_RG + RET_V
OFF_HF = OFF_HQ + HG_K
OFF_HI = OFF_HF + HG_K
OFF_HG = OFF_HI + HG_V
OFF_GA = OFF_HG + HG_V
OFF_GB = OFF_GA + D_MODEL
IN_TOTAL = OFF_GB + D_MODEL

CHUNK = 128
SUB = 8
MIXER_HEAD_GROUP = 8
SAMPLE_SEQS_PER_STEP = 4
STATE_RING_SLOTS = 3
V7X_VMEM_LIMIT = 58 * 1024 * 1024

F32 = jnp.float32
BF16 = jnp.bfloat16

LOG_GAMMA = [float(np.log(np.float32(1.0) - np.float32(2.0) ** np.float32(-5.0 - h))) for h in range(RET_HEADS)]


def _params(sem, vmem=V7X_VMEM_LIMIT):
    return pltpu.CompilerParams(dimension_semantics=sem, vmem_limit_bytes=vmem)


def _rms_rows(x, g):
    return (x * lax.rsqrt(jnp.mean(x * x, axis=-1, keepdims=True) + EPS)) * g


def _sigmoid(x):
    return 0.5 + 0.5 * jnp.tanh(0.5 * x)


def _silu(x):
    hx = 0.5 * x
    return hx + hx * jnp.tanh(hx)


def _dot(a, b):
    return jnp.dot(a, b, preferred_element_type=F32)


def _dot_nt(a, b):
    return lax.dot_general(a, b, (((1,), (1,)), ((), ())), preferred_element_type=F32)


def _dot_tn(a, b):
    return lax.dot_general(a, b, (((0,), (0,)), ((), ())), preferred_element_type=F32)


TILE_M_PROJ, TILE_N_PROJ = 2048, 512
TILE_M_MERGE = 512
TILE_M_RESID = 512
TILE_M_FFN_IN, TILE_N_FFN_IN = 2048, 512
TILE_M_FFN_OUT = 512
FFN_IN_ROW_CHUNK = 512
CAST_ROWS_FFN = 128


class _Rows(NamedTuple):
    prompt: Any
    sample: Any
    cols: int
    col: Callable
    single: bool = False


class _Shared(NamedTuple):
    array: Any
    block: tuple
    index: Callable
    resident: bool = False


class _Out(NamedTuple):
    total_cols: int
    cols: int
    col: Callable
    dtype: Any


def _dense_call(body, name, tile_m, operands, outs, scratch, inner, casts=(), joint_body=None):
    rows = [op for op in operands if isinstance(op, _Rows)]
    m_p, m_s = rows[0].prompt.shape[0], rows[0].sample.shape[0]
    last = m_p // tile_m - 1

    def frozen(i, j):
        return jnp.where(i == last, j, 0)

    specs_p, specs_s, args_p, args_s = [], [], [], []
    for op in operands:
        if isinstance(op, _Rows):
            mode = dict(pipeline_mode=pl.Buffered(1)) if op.single else {}
            specs_p.append(pl.BlockSpec((tile_m, op.cols), lambda i, j, op=op: (i, op.col(j)), **mode))
            specs_s.append(pl.BlockSpec((m_s, op.cols), lambda i, j, op=op: (0, op.col(frozen(i, j)))))
            args_p.append(op.prompt)
            args_s.append(op.sample)
        else:
            mode = dict(pipeline_mode=pl.Buffered(1)) if op.resident else {}
            specs_p.append(pl.BlockSpec(op.block, lambda i, j, op=op: op.index(j), **mode))
            args_p.append(op.array)
    out_specs = ([pl.BlockSpec((tile_m, o.cols), lambda i, j, o=o: (i, o.col(j))) for o in outs]
                 + [pl.BlockSpec((m_s, o.cols), lambda i, j, o=o: (0, o.col(frozen(i, j)))) for o in outs])
    out_shape = ([jax.ShapeDtypeStruct((m_p, o.total_cols), o.dtype) for o in outs]
                 + [jax.ShapeDtypeStruct((m_s, o.total_cols), o.dtype) for o in outs])
    if joint_body is None:
        scratch_shapes = ([pltpu.VMEM((tile_m, c), dt) for c, dt in scratch]
                          + [pltpu.VMEM((m_s, c), dt) for c, dt in scratch])
    else:
        scratch_shapes = [pltpu.VMEM((tile_m + m_s, c), dt) for c, dt in scratch]
    n_in, n_out, n_scr = len(operands), len(outs), len(scratch)
    row_pos = [k for k, op in enumerate(operands) if isinstance(op, _Rows)]
    n_rows, n_cast = len(row_pos), len(casts)
    cast_blocks = [w.shape[0] // rb for w, rb in casts]
    assert all(nb <= (m_p // tile_m) * inner for nb in cast_blocks)
    cast_specs = [pl.BlockSpec((rb, w.shape[1]), lambda i, j, nb=nb: (jnp.minimum(i * inner + j, nb - 1), 0))
                  for (w, rb), nb in zip(casts, cast_blocks)]

    def kernel(*refs):
        ins_p = list(refs[:n_in])
        ins_s = list(ins_p)
        for k, ref in zip(row_pos, refs[n_in:n_in + len(row_pos)]):
            ins_s[k] = ref
        r = n_in + n_rows
        cast_in = refs[r:r + n_cast]
        r += n_cast
        outs_p, outs_s = refs[r:r + n_out], refs[r + n_out:r + 2 * n_out]
        r += 2 * n_out
        cast_out = refs[r:r + n_cast]
        r += n_cast
        is_last = pl.program_id(0) == last
        if joint_body is None:
            body(*ins_p, *outs_p, *refs[r:r + n_scr])

            @pl.when(is_last)
            def _():
                body(*ins_s, *outs_s, *refs[r + n_scr:])
        else:
            scr = refs[r:r + n_scr]

            @pl.when(jnp.logical_not(is_last))
            def _():
                body(*ins_p, *outs_p, *[sc.at[pl.ds(0, tile_m)] for sc in scr])

            @pl.when(is_last)
            def _():
                joint_body(ins_p, refs[n_in:n_in + n_rows], outs_p, outs_s, scr)

        step = pl.program_id(0) * inner + pl.program_id(1)
        for src, dst, nb in zip(cast_in, cast_out, cast_blocks):
            @pl.when(step < nb)
            def _(src=src, dst=dst):
                dst[...] = src[...].astype(BF16)

    res = pl.pallas_call(
        kernel,
        grid=(m_p // tile_m, inner),
        in_specs=specs_p + specs_s + cast_specs,
        out_specs=out_specs + cast_specs,
        out_shape=out_shape + [jax.ShapeDtypeStruct(w.shape, BF16) for w, _ in casts],
        scratch_shapes=scratch_shapes,
        compiler_params=_params(("arbitrary", "arbitrary")),
        name=name,
    )(*args_p, *args_s, *[w for w, _ in casts])
    return list(zip(res[:n_out], res[n_out:2 * n_out])) + list(res[2 * n_out:])


def _proj_body(x_ref, g_ref, w_ref, o_ref, h_ref):
    @pl.when(pl.program_id(1) == 0)
    def _():
        h_ref[...] = _rms_rows(x_ref[...], g_ref[...]).astype(BF16)
    o_ref[...] = _dot(h_ref[...], w_ref[...].astype(BF16))


def _proj_joint_body(ins, sample_rows, outs_p, outs_s, scratch):
    (x_ref, g_ref, w_ref), (xs_ref,), (o_ref,), (os_ref,), (h_ref,) = ins, sample_rows, outs_p, outs_s, scratch
    tm = x_ref.shape[0]

    @pl.when(pl.program_id(1) == 0)
    def _():
        h_ref[:tm, :] = _rms_rows(x_ref[...], g_ref[...]).astype(BF16)
        h_ref[tm:, :] = _rms_rows(xs_ref[...], g_ref[...]).astype(BF16)
    res = _dot(h_ref[...], w_ref[...].astype(BF16))
    o_ref[...] = res[:tm]
    os_ref[...] = res[tm:]


def _proj(x, g, w):
    d, n = w.shape
    tn = TILE_N_PROJ
    return _dense_call(
        _proj_body, "proj", TILE_M_PROJ,
        [_Rows(*x, d, lambda j: 0, single=True),
         _Shared(g, (1, d), lambda j: (0, 0)),
         _Shared(w, (d, tn), lambda j: (0, j))],
        [_Out(n, tn, lambda j: j, F32)], [(d, BF16)], n // tn, joint_body=_proj_joint_body)[0]


def _merge_body(or_ref, oh_ref, wr_ref, wh_ref, ga_ref, gb_ref, o_ref):
    ya = _dot(or_ref[...], wr_ref[...])
    yb = _dot(oh_ref[...], wh_ref[...])
    o_ref[...] = (_sigmoid(ga_ref[...]) * ya + _sigmoid(gb_ref[...]) * yb).astype(BF16)


def _merge(o_r, o_h, w_r, w_h, proj):
    n = w_r.shape[1]
    return _dense_call(
        _merge_body, "merge", TILE_M_MERGE,
        [_Rows(*o_r, RET_V, lambda j: 0),
         _Rows(*o_h, HG_V, lambda j: 0),
         _Shared(w_r, (RET_V, n), lambda j: (0, 0), resident=True),
         _Shared(w_h, (HG_V, n), lambda j: (0, 0), resident=True),
         _Rows(*proj, n, lambda j: OFF_GA // n),
         _Rows(*proj, n, lambda j: OFF_GB // n)],
        [_Out(n, n, lambda j: 0, BF16)], [], 1)[0]


def _resid_body(x_ref, m_ref, w_ref, g_ref, o_ref, h_ref):
    x1 = x_ref[...] + _dot(m_ref[...], w_ref[...])
    o_ref[...] = x1
    h_ref[...] = _rms_rows(x1, g_ref[...]).astype(BF16)


def _resid(x, mrg, w, g):
    d = w.shape[1]
    return _dense_call(
        _resid_body, "resid", TILE_M_RESID,
        [_Rows(*x, d, lambda j: 0),
         _Rows(*mrg, d, lambda j: 0),
         _Shared(w, (d, d), lambda j: (0, 0), resident=True),
         _Shared(g, (1, d), lambda j: (0, 0))],
        [_Out(d, d, lambda j: 0, F32), _Out(d, d, lambda j: 0, BF16)], [], 1)


def _ffn_in_body(h_ref, wg_ref, wu_ref, o_ref):
    wg = wg_ref[...].astype(BF16)
    wu = wu_ref[...].astype(BF16)
    rows = h_ref.shape[0]
    step = min(rows, FFN_IN_ROW_CHUNK)
    for r in range(0, rows, step):
        h = h_ref[r:r + step, :]
        o_ref[r:r + step, :] = (_silu(_dot(h, wg)) * _dot(h, wu)).astype(BF16)


def _ffn_in(h, w, w_down):
    d = w.shape[0]
    tn = TILE_N_FFN_IN
    return _dense_call(
        _ffn_in_body, "ffn_in", TILE_M_FFN_IN,
        [_Rows(*h, d, lambda j: 0),
         _Shared(w, (d, tn), lambda j: (0, j)),
         _Shared(w, (d, tn), lambda j: (0, D_FF // tn + j))],
        [_Out(D_FF, tn, lambda j: j, BF16)], [], D_FF // tn, casts=[(w_down, CAST_ROWS_FFN)])


def _ffn_out_body(a_ref, w_ref, x_ref, g_ref, o_ref):
    o_ref[...] = _rms_rows(x_ref[...] + _dot(a_ref[...], w_ref[...]), g_ref[...])


def _ffn_out(act, w, x, g):
    k, d = w.shape
    return _dense_call(
        _ffn_out_body, "ffn_out", TILE_M_FFN_OUT,
        [_Rows(*act, k, lambda j: 0),
         _Shared(w, (k, d), lambda j: (0, 0), resident=True),
         _Rows(*x, d, lambda j: 0),
         _Shared(g, (1, d), lambda j: (0, 0))],
        [_Out(d, d, lambda j: 0, F32)], [], 1)[0]


def _rotary(x, cos, sin_signed):
    return x * cos + pltpu.roll(x, RET_DK // 2, 1) * sin_signed


def _ret_parts(p_ref, cos_ref, sin_ref, o_ref, so_ref, s_ref, dec_ref):
    c = CHUNK

    def init():
        s_ref[...] = jnp.zeros_like(s_ref)
        row = lax.broadcasted_iota(jnp.int32, (c, c), 0)
        col = lax.broadcasted_iota(jnp.int32, (c, c), 1)
        diff = (row - col).astype(F32)
        idx = lax.broadcasted_iota(jnp.int32, (c, RET_DK), 0).astype(F32)
        for h in range(RET_HEADS):
            lg = LOG_GAMMA[h]
            dec_ref[h, 0] = jnp.where(diff >= 0, jnp.exp(jnp.maximum(diff, 0.0) * lg), 0.0)
            dec_ref[h, 1] = jnp.exp((idx + 1.0) * lg)
            dec_ref[h, 2] = jnp.exp((c - 1.0 - idx) * lg)

    def step(heads):
        cos = cos_ref[...]
        sin = sin_ref[...]
        def rotate(h):
            q = _rotary(p_ref[0, :, OFF_RQ + h * RET_DK:OFF_RQ + (h + 1) * RET_DK], cos, sin)
            k = _rotary(p_ref[0, :, OFF_RK + h * RET_DK:OFF_RK + (h + 1) * RET_DK], cos, sin) * (RET_DK ** -0.5)
            return q, k

        def attend(h, q, k):
            v = p_ref[0, :, OFF_RV + h * RET_DV:OFF_RV + (h + 1) * RET_DV].astype(BF16)
            chunk_decay = math.exp(c * LOG_GAMMA[h])
            s = s_ref[h]
            scores = _dot_nt(q.astype(BF16), k.astype(BF16)) * dec_ref[h, 0]
            o = _dot(scores.astype(BF16), v) + _dot((q * dec_ref[h, 1]).astype(BF16), s.astype(BF16))
            s_ref[h] = chunk_decay * s + _dot_tn((k * dec_ref[h, 2]).astype(BF16), v)
            return o

        def emit(h, o):
            o = o * lax.rsqrt(jnp.mean(o * o, axis=-1, keepdims=True) + EPS)
            gate = p_ref[0, :, OFF_RG + h * RET_DV:OFF_RG + (h + 1) * RET_DV]
            o_ref[0, :, h * RET_DV:(h + 1) * RET_DV] = (o * _silu(gate)).astype(BF16)

        qk = [rotate(h) for h in heads]
        outs = [attend(h, q, k) for h, (q, k) in zip(heads, qk)]
        for h, o in zip(heads, outs):
            emit(h, o)

    def final():
        so_ref[0] = s_ref[...]

    return init, step, final


def _lower_bound(lbl):
    mx = jnp.max(lbl, axis=0, keepdims=True)
    e = jnp.exp(lbl - mx)
    return e[0:1, :] / jnp.sum(e, axis=0, keepdims=True)


def _hgrn_gates(z, lb):
    half = 0.5 * (1.0 - lb)
    ht = half * jnp.tanh(0.5 * z)
    return (lb + half) + ht, half - ht


def _cumsum_rows(tri, x):
    hi = x.astype(BF16)
    r1 = x - hi.astype(F32)
    mid = r1.astype(BF16)
    lo = (r1 - mid.astype(F32)).astype(BF16)
    return _dot(tri, hi) + _dot(tri, mid) + _dot(tri, lo)


def _bcast_rows(x, group, r):
    c, d = x.shape
    x3 = x.reshape(c // group, group, d)
    return jnp.broadcast_to(x3[:, r:r + 1, :], x3.shape).reshape(c, d)


def _hgrn_parts(qf_ref, ig_ref, lbl_ref, nrm_ref, o_ref, so_ref, st_ref):
    def init():
        st_ref[...] = jnp.zeros_like(st_ref)

    def final():
        for h in range(HG_HEADS):
            so_ref[0, h] = st_ref[h].T

    return init, functools.partial(_hgrn_step, qf_ref, ig_ref, lbl_ref, nrm_ref, o_ref, st_ref), final


def _hgrn_step(qf_ref, ig_ref, lbl_ref, nrm_ref, o_ref, st_ref, heads):
    c = CHUNK
    row = lax.broadcasted_iota(jnp.int32, (c, c), 0)
    col = lax.broadcasted_iota(jnp.int32, (c, c), 1)
    rowd = lax.broadcasted_iota(jnp.int32, (c, HG_DK), 0)
    tri = jnp.where(row >= col, 1.0, 0.0).astype(BF16)
    r2, r4 = rowd & 1, rowd & 3
    block_masks = [(row >> s) == (col >> s) for s in range(c.bit_length())]
    nrm = nrm_ref[...]

    def up(x, k):
        return pltpu.roll(x, k, 0)

    def dn(x, k):
        return pltpu.roll(x, c - k, 0)

    def gates(h):
        off = h * HG_DK
        lb = _lower_bound(lbl_ref[:, pl.ds(off, HG_DK)])
        f, kin = _hgrn_gates(qf_ref[0, :, pl.ds(HG_K + off, HG_DK)], lb)
        qh = _silu(qf_ref[0, :, pl.ds(off, HG_DK)])
        v = ig_ref[0, :, pl.ds(off, HG_DV)].astype(BF16)
        b = _cumsum_rows(tri, jnp.log2(f))
        return f, kin, qh, v, b

    def intra(f, kin, qh, b):
        f_up = f * up(f, 1)
        f_dn = f * dn(f, 1)
        pre2 = jnp.where(r2 == 1, f_up, f)
        pre4 = jnp.where(r4 >= 1, f_up, f)
        pre4 = jnp.where(r4 >= 2, pre4 * up(pre4, 2), pre4)
        suf4 = jnp.where(r4 <= 2, f_dn, f)
        suf4 = jnp.where(r4 <= 1, suf4 * dn(suf4, 2), suf4)

        def level(s):
            if s == 0:
                return qh, kin
            if s == 1:
                return qh * f, kin
            if s == 2:
                return qh * pre2, kin * jnp.where(r4 == 0, dn(f, 1), 1.0)
            if s == 3:
                return qh * pre4, kin * jnp.where(r4 == 3, 1.0, dn(suf4, 1))
            half = 1 << (s - 1)
            ref = _bcast_rows(b, 2 * half, half - 1)
            upper = (rowd & (2 * half - 1)) >= half
            return jnp.where(upper, qh * jnp.exp2(b - ref), 0.0), jnp.where(upper, 0.0, kin * jnp.exp2(ref - b))

        a = jnp.zeros((c, c), F32)
        for s in reversed(range(len(block_masks))):
            q_s, k_s = level(s)
            a = jnp.where(block_masks[s], _dot_nt(q_s.astype(BF16), k_s.astype(BF16)), a)
        return jnp.where(col <= row, a, 0.0)

    def finish(h, kin, qh, v, b, a):
        off = h * HG_DK
        st = st_ref[h]
        b_last = b[c - 1:c, :]
        o = _dot(a.astype(BF16), v) + _dot_nt((qh * jnp.exp2(b)).astype(BF16), st.astype(BF16))
        st_ref[h] = st * jnp.exp2(b_last) + _dot_tn(v, (kin * jnp.exp2(b_last - b)).astype(BF16))
        o = o * lax.rsqrt(jnp.mean(o * o, axis=-1, keepdims=True) + EPS) * nrm
        o_ref[0, :, pl.ds(off, HG_DV)] = (o * _silu(ig_ref[0, :, pl.ds(HG_V + off, HG_DV)])).astype(BF16)

    g = [gates(h) for h in heads]
    a = [intra(f, kin, qh, b) for f, kin, qh, v, b in g]
    for h, (f, kin, qh, v, b), a_h in zip(heads, g, a):
        finish(h, kin, qh, v, b, a_h)


def _mixers_kernel(n_cast, p_ref, cos_ref, sin_ref, qf_ref, ig_ref, lbl_ref, nrm_ref, *refs):
    cast_in, (or_ref, sor_ref, oh_ref, soh_ref), refs = refs[:n_cast], refs[n_cast:n_cast + 4], refs[n_cast + 4:]
    cast_out, (s_ref, dec_ref, st_ref) = refs[:n_cast], refs[n_cast:]
    for src, dst in zip(cast_in, cast_out):
        dst[...] = src[...].astype(BF16)
    n = pl.program_id(1)
    parts = (_ret_parts(p_ref, cos_ref, sin_ref, or_ref, sor_ref, s_ref, dec_ref),
             _hgrn_parts(qf_ref, ig_ref, lbl_ref, nrm_ref, oh_ref, soh_ref, st_ref))

    @pl.when(n == 0)
    def _():
        for init, _, _ in parts:
            init()

    for h in range(0, max(RET_HEADS, HG_HEADS), MIXER_HEAD_GROUP):
        for (_, step, _), n_heads in zip(parts, (RET_HEADS, HG_HEADS)):
            step(range(h, min(h + MIXER_HEAD_GROUP, n_heads)))

    @pl.when(n == pl.num_programs(1) - 1)
    def _():
        for _, _, final in parts:
            final()


def _mixers_prompt(proj3, cos, sin, lbl, nrm, later_weights):
    b, t, _ = proj3.shape
    nc = t // CHUNK
    cast_specs = [pl.BlockSpec((w.shape[0] // (b * nc), w.shape[1]), lambda i, n: (i * nc + n, 0))
                  for w in later_weights]
    return pl.pallas_call(
        functools.partial(_mixers_kernel, len(later_weights)),
        grid=(b, nc),
        in_specs=[pl.BlockSpec((1, CHUNK, OFF_HQ), lambda i, n: (i, n, 0)),
                  pl.BlockSpec((CHUNK, RET_DK), lambda i, n: (n, 0)),
                  pl.BlockSpec((CHUNK, RET_DK), lambda i, n: (n, 0)),
                  pl.BlockSpec((1, CHUNK, 2 * HG_K), lambda i, n: (i, n, OFF_HQ // (2 * HG_K))),
                  pl.BlockSpec((1, CHUNK, 2 * HG_V), lambda i, n: (i, n, OFF_HI // (2 * HG_V))),
                  pl.BlockSpec((DEPTH + 1, HG_K), lambda i, n: (0, 0)),
                  pl.BlockSpec((1, HG_DV), lambda i, n: (0, 0))] + cast_specs,
        out_specs=[pl.BlockSpec((1, CHUNK, RET_V), lambda i, n: (i, n, 0)),
                   pl.BlockSpec((1, RET_HEADS, RET_DK, RET_DV), lambda i, n: (i, 0, 0, 0)),
                   pl.BlockSpec((1, CHUNK, HG_V), lambda i, n: (i, n, 0)),
                   pl.BlockSpec((1, HG_HEADS, HG_DK, HG_DV), lambda i, n: (i, 0, 0, 0))] + cast_specs,
        out_shape=[jax.ShapeDtypeStruct((b, t, RET_V), BF16),
                   jax.ShapeDtypeStruct((b, RET_HEADS, RET_DK, RET_DV), F32),
                   jax.ShapeDtypeStruct((b, t, HG_V), BF16),
                   jax.ShapeDtypeStruct((b, HG_HEADS, HG_DK, HG_DV), F32)]
        + [jax.ShapeDtypeStruct(w.shape, BF16) for w in later_weights],
        scratch_shapes=[pltpu.VMEM((RET_HEADS, RET_DK, RET_DV), F32),
                        pltpu.VMEM((RET_HEADS, 3, CHUNK, RET_DK), F32),
                        pltpu.VMEM((HG_HEADS, HG_DV, HG_DK), F32)],
        compiler_params=_params(("arbitrary", "arbitrary")),
        name="mixers_prompt",
    )(proj3, cos, sin, proj3, proj3, lbl, nrm, *later_weights)


def _sample_prep_kernel(p_ref, cos_ref, sin_ref, lbl_ref, qr_ref, kr_ref, qh_ref, f_ref, kin_ref):
    cos = cos_ref[...]
    sin = sin_ref[...]
    for h in range(RET_HEADS):
        sl = slice(h * RET_DK, (h + 1) * RET_DK)
        qr_ref[:, sl] = _rotary(p_ref[:, OFF_RQ + h * RET_DK:OFF_RQ + (h + 1) * RET_DK], cos, sin)
        kr_ref[:, sl] = _rotary(p_ref[:, OFF_RK + h * RET_DK:OFF_RK + (h + 1) * RET_DK], cos, sin) * (RET_DK ** -0.5)
    f, kin = _hgrn_gates(p_ref[:, OFF_HF:OFF_HF + HG_K], _lower_bound(lbl_ref[...]))
    f_ref[...] = f
    kin_ref[...] = kin
    qh_ref[...] = _silu(p_ref[:, OFF_HQ:OFF_HQ + HG_K])


def _sample_prep(proj, cos, sin, lbl):
    m = proj.shape[0]
    shp = jax.ShapeDtypeStruct((m, RET_QK), F32)
    return pl.pallas_call(
        _sample_prep_kernel,
        grid=(1,),
        in_specs=[pl.BlockSpec((m, OFF_GA), lambda i: (0, 0)),
                  pl.BlockSpec((1, RET_DK), lambda i: (0, 0)),
                  pl.BlockSpec((1, RET_DK), lambda i: (0, 0)),
                  pl.BlockSpec((DEPTH + 1, HG_K), lambda i: (0, 0))],
        out_specs=[pl.BlockSpec((m, RET_QK), lambda i: (0, 0))] * 5,
        out_shape=[shp] * 5,
        compiler_params=_params(("arbitrary",)),
        name="sample_prep",
    )(proj, cos, sin, lbl)


def _sample_state_kernel(sr_hbm, sh_hbm, qr_ref, kr_ref, qh_ref, f_ref, kin_ref, vr_ref, rg_ref, vh_ref, hg_ref,
                         nrm_ref, sro_ref, sho_ref, or_ref, oh_ref, ring_r, ring_h, sem):
    nb = SAMPLE_SEQS_PER_STEP
    step_id, n_steps = pl.program_id(0), pl.num_programs(0)

    def state_copies(step, slot):
        rows = pl.ds(step * nb, nb)
        return (pltpu.make_async_copy(sr_hbm.at[rows], ring_r.at[slot], sem.at[0, slot]),
                pltpu.make_async_copy(sh_hbm.at[rows], ring_h.at[slot], sem.at[1, slot]))

    @pl.when(step_id == 0)
    def _():
        for first in range(STATE_RING_SLOTS - 1):
            for cp in state_copies(first, first):
                cp.start()

    ahead = step_id + STATE_RING_SLOTS - 1

    @pl.when(ahead < n_steps)
    def _():
        for cp in state_copies(ahead, ahead % STATE_RING_SLOTS):
            cp.start()

    slot = step_id % STATE_RING_SLOTS
    for cp in state_copies(step_id, slot):
        cp.wait()
    sr_ref, sh_ref = ring_r.at[slot], ring_h.at[slot]
    lanes = RET_DK
    for t in range(SAMPLE_SEQS_PER_STEP):
        b = pl.program_id(0) * SAMPLE_SEQS_PER_STEP + t
        base = pl.multiple_of((b // SUB) * SUB, SUB)

        def row(ref, sl, b=b, base=base):
            tile = ref[pl.ds(base, SUB), sl]
            sub = lax.broadcasted_iota(jnp.int32, tile.shape, 0)
            return jnp.sum(jnp.where(sub == b % SUB, tile, 0.0), axis=0, keepdims=True)

        def column(ref, h, row=row):
            r = row(ref, slice(h * RET_DK, (h + 1) * RET_DK))
            return jnp.broadcast_to(r, (RET_DK, RET_DK)).T

        for h in range(RET_HEADS):
            kc = column(kr_ref, h)
            qc = column(qr_ref, h)
            gamma = math.exp(LOG_GAMMA[h])
            outs = []
            for half in range(RET_DV // lanes):
                hs = slice(half * lanes, (half + 1) * lanes)
                s = gamma * sr_ref[t, h, :, hs] + kc * row(vr_ref, slice(h * RET_DV + hs.start, h * RET_DV + hs.stop))
                sro_ref[t, h, :, hs] = s
                outs.append(jnp.sum(qc * s, axis=0, keepdims=True))
            o = jnp.concatenate(outs, axis=1)
            o = o * lax.rsqrt(jnp.mean(o * o, axis=-1, keepdims=True) + EPS)
            sl = slice(h * RET_DV, (h + 1) * RET_DV)
            or_ref[t, :, sl] = o * _silu(row(rg_ref, sl))
        for h in range(HG_HEADS):
            sl = slice(h * HG_DV, (h + 1) * HG_DV)
            s = column(f_ref, h) * sh_ref[t, h] + column(kin_ref, h) * row(vh_ref, sl)
            sho_ref[t, h] = s
            o = jnp.sum(column(qh_ref, h) * s, axis=0, keepdims=True)
            o = o * lax.rsqrt(jnp.mean(o * o, axis=-1, keepdims=True) + EPS) * nrm_ref[...]
            oh_ref[t, :, sl] = o * _silu(row(hg_ref, sl))


def _sample_state(s_ret, s_hg, rows, proj, nrm):
    b = s_ret.shape[0]
    nb = SAMPLE_SEQS_PER_STEP
    row_spec = pl.BlockSpec((b, RET_QK), lambda i: (0, 0))
    return pl.pallas_call(
        _sample_state_kernel,
        grid=(b // nb,),
        in_specs=[pl.BlockSpec(memory_space=pl.ANY), pl.BlockSpec(memory_space=pl.ANY),
                  row_spec, row_spec, row_spec, row_spec, row_spec,
                  pl.BlockSpec((b, RET_V), lambda i: (0, OFF_RV // RET_V)),
                  pl.BlockSpec((b, RET_V), lambda i: (0, OFF_RG // RET_V)),
                  pl.BlockSpec((b, HG_V), lambda i: (0, OFF_HI // HG_V)),
                  pl.BlockSpec((b, HG_V), lambda i: (0, OFF_HG // HG_V)),
                  pl.BlockSpec((1, HG_DV), lambda i: (0, 0))],
        out_specs=[pl.BlockSpec((nb, RET_HEADS, RET_DK, RET_DV), lambda i: (i, 0, 0, 0)),
                   pl.BlockSpec((nb, HG_HEADS, HG_DK, HG_DV), lambda i: (i, 0, 0, 0)),
                   pl.BlockSpec((nb, 1, RET_V), lambda i: (i, 0, 0)),
                   pl.BlockSpec((nb, 1, HG_V), lambda i: (i, 0, 0))],
        out_shape=[jax.ShapeDtypeStruct(s_ret.shape, F32),
                   jax.ShapeDtypeStruct(s_hg.shape, F32),
                   jax.ShapeDtypeStruct((b, 1, RET_V), F32),
                   jax.ShapeDtypeStruct((b, 1, HG_V), F32)],
        scratch_shapes=[pltpu.VMEM((STATE_RING_SLOTS, nb, RET_HEADS, RET_DK, RET_DV), F32),
                        pltpu.VMEM((STATE_RING_SLOTS, nb, HG_HEADS, HG_DK, HG_DV), F32),
                        pltpu.SemaphoreType.DMA((2, STATE_RING_SLOTS))],
        compiler_params=_params(("arbitrary",)),
        name="sample_state",
    )(s_ret, s_hg, *rows, proj, proj, proj, proj, nrm)


def _rope_tables(first, count):
    half = RET_DK // 2
    inv_freq = ROPE_BASE ** (-np.arange(half, dtype=np.float64) / half)
    ang = np.arange(first, first + count, dtype=np.float64)[:, None] * inv_freq[None, :]
    cos, sin = np.cos(ang), np.sin(ang)
    return (jnp.asarray(np.concatenate([cos, cos], axis=-1), F32),
            jnp.asarray(np.concatenate([-sin, sin], axis=-1), F32))


def kernel(x_prompt, x_sample, state_ret, state_hgrn, w_in, w_ret_out, w_hgrn_out, w_out, norm_mix, norm_ffn,
           hgrn_norm, hgrn_lb_logits, w_ffn_in, w_ffn_out, norm_final):
    assert w_in.shape[0] == DEPTH == 1 and x_sample.shape[1] == 1
    bp, t, d = x_prompt.shape
    bs = x_sample.shape[0]
    nrm = hgrn_norm[0][None, :]
    lbl = hgrn_lb_logits.astype(F32)
    x = (x_prompt.reshape(bp * t, d), x_sample.reshape(bs, d))

    proj = _proj(x, norm_mix[0][None, :], w_in[0])

    proj_p3 = proj[0].reshape(bp, t, IN_TOTAL)
    cos_p, sin_p = _rope_tables(0, t)
    o_r, s_ret_p, o_h, s_hg_p, wb_ret, wb_hgrn, wb_out = _mixers_prompt(
        proj_p3, cos_p, sin_p, lbl, nrm, [w_ret_out[0], w_hgrn_out[0], w_out[0]])

    cos_s, sin_s = _rope_tables(PAST_LEN, x_sample.shape[1])
    rows = _sample_prep(proj[1], cos_s, sin_s, lbl)
    s_ret_s, s_hg_s, o_rs, o_hs = _sample_state(state_ret[0], state_hgrn[0], rows, proj[1], nrm)

    o_r = (o_r.reshape(bp * t, RET_V), o_rs.reshape(bs, RET_V).astype(BF16))
    o_h = (o_h.reshape(bp * t, HG_V), o_hs.reshape(bs, HG_V).astype(BF16))
    mrg = _merge(o_r, o_h, wb_ret, wb_hgrn, proj)
    x1, h2 = _resid(x, mrg, wb_out, norm_ffn[0][None, :])
    act, wb_down = _ffn_in(h2, w_ffn_in[0], w_ffn_out[0])
    y_p, y_s = _ffn_out(act, wb_down, x1, norm_final[None, :])

    return (y_p.reshape(bp, t, d), y_s.reshape(bs, 1, d), s_ret_p[None], s_hg_p[None], s_ret_s[None], s_hg_s[None])
```

```python
import functools
import math
from typing import Any, Callable, NamedTuple

import numpy as np
import jax
import jax.numpy as jnp
from jax import lax
from jax.experimental import pallas as pl
from jax.experimental.pallas import tpu as pltpu

D_MODEL = 2048
DEPTH = 1
PAST_LEN = 16384
RET_HEADS = 8
RET_DK = 128
RET_DV = 256
RET_QK = RET_HEADS * RET_DK
RET_V = RET_HEADS * RET_DV
ROPE_BASE = 10000.0
HG_HEADS = 8
HG_DK = 128
HG_DV = 128
HG_K = HG_HEADS * HG_DK
HG_V = HG_HEADS * HG_DV
D_FF = 5632
EPS = 1e-6

OFF_RQ = 0
OFF_RK = OFF_RQ + RET_QK
OFF_RV = OFF_RK + RET_QK
OFF_RG = OFF_RV + RET_V
OFF_HQ = OFF_RG + RET_V
OFF_HF = OFF_HQ + HG_K
OFF_HI = OFF_HF + HG_K
OFF_HG = OFF_HI + HG_V
OFF_GA = OFF_HG + HG_V
OFF_GB = OFF_GA + D_MODEL
IN_TOTAL = OFF_GB + D_MODEL

CHUNK = 128
SUB = 8
MIXER_HEAD_GROUP = 8
SAMPLE_SEQS_PER_STEP = 4
STATE_RING_SLOTS = 3
V7X_VMEM_LIMIT = 58 * 1024 * 1024

F32 = jnp.float32
BF16 = jnp.bfloat16

LOG_GAMMA = [float(np.log(np.float32(1.0) - np.float32(2.0) ** np.float32(-5.0 - h))) for h in range(RET_HEADS)]


def _params(sem, vmem=V7X_VMEM_LIMIT):
    return pltpu.CompilerParams(dimension_semantics=sem, vmem_limit_bytes=vmem)


def _rms_rows(x, g):
    return (x * lax.rsqrt(jnp.mean(x * x, axis=-1, keepdims=True) + EPS)) * g


def _sigmoid(x):
    return 0.5 + 0.5 * jnp.tanh(0.5 * x)


def _silu(x):
    hx = 0.5 * x
    return hx + hx * jnp.tanh(hx)


def _dot(a, b):
    return jnp.dot(a, b, preferred_element_type=F32)


def _dot_nt(a, b):
    return lax.dot_general(a, b, (((1,), (1,)), ((), ())), preferred_element_type=F32)


def _dot_tn(a, b):
    return lax.dot_general(a, b, (((0,), (0,)), ((), ())), preferred_element_type=F32)


TILE_M_PROJ, TILE_N_PROJ = 2048, 512
TILE_M_MERGE = 512
TILE_M_RESID = 512
TILE_M_FFN_IN, TILE_N_FFN_IN = 2048, 512
TILE_M_FFN_OUT = 512
FFN_IN_ROW_CHUNK = 512
CAST_ROWS_FFN = 128


class _Rows(NamedTuple):
    prompt: Any
    sample: Any
    cols: int
    col: Callable
    single: bool = False


class _Shared(NamedTuple):
    array: Any
    block: tuple
    index: Callable
    resident: bool = False


class _Out(NamedTuple):
    total_cols: int
    cols: int
    col: Callable
    dtype: Any


def _dense_call(body, name, tile_m, operands, outs, scratch, inner, casts=(), joint_body=None):
    rows = [op for op in operands if isinstance(op, _Rows)]
    m_p, m_s = rows[0].prompt.shape[0], rows[0].sample.shape[0]
    last = m_p // tile_m - 1

    def frozen(i, j):
        return jnp.where(i == last, j, 0)

    specs_p, specs_s, args_p, args_s = [], [], [], []
    for op in operands:
        if isinstance(op, _Rows):
            mode = dict(pipeline_mode=pl.Buffered(1)) if op.single else {}
            specs_p.append(pl.BlockSpec((tile_m, op.cols), lambda i, j, op=op: (i, op.col(j)), **mode))
            specs_s.append(pl.BlockSpec((m_s, op.cols), lambda i, j, op=op: (0, op.col(frozen(i, j)))))
            args_p.append(op.prompt)
            args_s.append(op.sample)
        else:
            mode = dict(pipeline_mode=pl.Buffered(1)) if op.resident else {}
            specs_p.append(pl.BlockSpec(op.block, lambda i, j, op=op: op.index(j), **mode))
            args_p.append(op.array)
    out_specs = ([pl.BlockSpec((tile_m, o.cols), lambda i, j, o=o: (i, o.col(j))) for o in outs]
                 + [pl.BlockSpec((m_s, o.cols), lambda i, j, o=o: (0, o.col(frozen(i, j)))) for o in outs])
    out_shape = ([jax.ShapeDtypeStruct((m_p, o.total_cols), o.dtype) for o in outs]
                 + [jax.ShapeDtypeStruct((m_s, o.total_cols), o.dtype) for o in outs])
    if joint_body is None:
        scratch_shapes = ([pltpu.VMEM((tile_m, c), dt) for c, dt in scratch]
                          + [pltpu.VMEM((m_s, c), dt) for c, dt in scratch])
    else:
        scratch_shapes = [pltpu.VMEM((tile_m + m_s, c), dt) for c, dt in scratch]
    n_in, n_out, n_scr = len(operands), len(outs), len(scratch)
    row_pos = [k for k, op in enumerate(operands) if isinstance(op, _Rows)]
    n_rows, n_cast = len(row_pos), len(casts)
    cast_blocks = [w.shape[0] // rb for w, rb in casts]
    assert all(nb <= (m_p // tile_m) * inner for nb in cast_blocks)
    cast_specs = [pl.BlockSpec((rb, w.shape[1]), lambda i, j, nb=nb: (jnp.minimum(i * inner + j, nb - 1), 0))
                  for (w, rb), nb in zip(casts, cast_blocks)]

    def kernel(*refs):
        ins_p = list(refs[:n_in])
        ins_s = list(ins_p)
        for k, ref in zip(row_pos, refs[n_in:n_in + len(row_pos)]):
            ins_s[k] = ref
        r = n_in + n_rows
        cast_in = refs[r:r + n_cast]
        r += n_cast
        outs_p, outs_s = refs[r:r + n_out], refs[r + n_out:r + 2 * n_out]
        r += 2 * n_out
        cast_out = refs[r:r + n_cast]
        r += n_cast
        is_last = pl.program_id(0) == last
        if joint_body is None:
            body(*ins_p, *outs_p, *refs[r:r + n_scr])

            @pl.when(is_last)
            def _():
                body(*ins_s, *outs_s, *refs[r + n_scr:])
        else:
            scr = refs[r:r + n_scr]

            @pl.when(jnp.logical_not(is_last))
            def _():
                body(*ins_p, *outs_p, *[sc.at[pl.ds(0, tile_m)] for sc in scr])

            @pl.when(is_last)
            def _():
                joint_body(ins_p, refs[n_in:n_in + n_rows], outs_p, outs_s, scr)

        step = pl.program_id(0) * inner + pl.program_id(1)
        for src, dst, nb in zip(cast_in, cast_out, cast_blocks):
            @pl.when(step < nb)
            def _(src=src, dst=dst):
                dst[...] = src[...].astype(BF16)

    res = pl.pallas_call(
        kernel,
        grid=(m_p // tile_m, inner),
        in_specs=specs_p + specs_s + cast_specs,
        out_specs=out_specs + cast_specs,
        out_shape=out_shape + [jax.ShapeDtypeStruct(w.shape, BF16) for w, _ in casts],
        scratch_shapes=scratch_shapes,
        compiler_params=_params(("arbitrary", "arbitrary")),
        name=name,
    )(*args_p, *args_s, *[w for w, _ in casts])
    return list(zip(res[:n_out], res[n_out:2 * n_out])) + list(res[2 * n_out:])


def _proj_body(x_ref, g_ref, w_ref, o_ref, h_ref):
    @pl.when(pl.program_id(1) == 0)
    def _():
        h_ref[...] = _rms_rows(x_ref[...], g_ref[...]).astype(BF16)
    o_ref[...] = _dot(h_ref[...], w_ref[...].astype(BF16))


def _proj_joint_body(ins, sample_rows, outs_p, outs_s, scratch):
    (x_ref, g_ref, w_ref), (xs_ref,), (o_ref,), (os_ref,), (h_ref,) = ins, sample_rows, outs_p, outs_s, scratch
    tm = x_ref.shape[0]

    @pl.when(pl.program_id(1) == 0)
    def _():
        h_ref[:tm, :] = _rms_rows(x_ref[...], g_ref[...]).astype(BF16)
        h_ref[tm:, :] = _rms_rows(xs_ref[...], g_ref[...]).astype(BF16)
    res = _dot(h_ref[...], w_ref[...].astype(BF16))
    o_ref[...] = res[:tm]
    os_ref[...] = res[tm:]


def _proj(x, g, w):
    d, n = w.shape
    tn = TILE_N_PROJ
    return _dense_call(
        _proj_body, "proj", TILE_M_PROJ,
        [_Rows(*x, d, lambda j: 0, single=True),
         _Shared(g, (1, d), lambda j: (0, 0)),
         _Shared(w, (d, tn), lambda j: (0, j))],
        [_Out(n, tn, lambda j: j, F32)], [(d, BF16)], n // tn, joint_body=_proj_joint_body)[0]


def _merge_body(or_ref, oh_ref, wr_ref, wh_ref, ga_ref, gb_ref, o_ref):
    ya = _dot(or_ref[...], wr_ref[...])
    yb = _dot(oh_ref[...], wh_ref[...])
    o_ref[...] = (_sigmoid(ga_ref[...]) * ya + _sigmoid(gb_ref[...]) * yb).astype(BF16)


def _merge(o_r, o_h, w_r, w_h, proj):
    n = w_r.shape[1]
    return _dense_call(
        _merge_body, "merge", TILE_M_MERGE,
        [_Rows(*o_r, RET_V, lambda j: 0),
         _Rows(*o_h, HG_V, lambda j: 0),
         _Shared(w_r, (RET_V, n), lambda j: (0, 0), resident=True),
         _Shared(w_h, (HG_V, n), lambda j: (0, 0), resident=True),
         _Rows(*proj, n, lambda j: OFF_GA // n),
         _Rows(*proj, n, lambda j: OFF_GB // n)],
        [_Out(n, n, lambda j: 0, BF16)], [], 1)[0]


def _resid_body(x_ref, m_ref, w_ref, g_ref, o_ref, h_ref):
    x1 = x_ref[...] + _dot(m_ref[...], w_ref[...])
    o_ref[...] = x1
    h_ref[...] = _rms_rows(x1, g_ref[...]).astype(BF16)


def _resid(x, mrg, w, g):
    d = w.shape[1]
    return _dense_call(
        _resid_body, "resid", TILE_M_RESID,
        [_Rows(*x, d, lambda j: 0),
         _Rows(*mrg, d, lambda j: 0),
         _Shared(w, (d, d), lambda j: (0, 0), resident=True),
         _Shared(g, (1, d), lambda j: (0, 0))],
        [_Out(d, d, lambda j: 0, F32), _Out(d, d, lambda j: 0, BF16)], [], 1)


def _ffn_in_body(h_ref, wg_ref, wu_ref, o_ref):
    wg = wg_ref[...].astype(BF16)
    wu = wu_ref[...].astype(BF16)
    rows = h_ref.shape[0]
    step = min(rows, FFN_IN_ROW_CHUNK)
    for r in range(0, rows, step):
        h = h_ref[r:r + step, :]
        o_ref[r:r + step, :] = (_silu(_dot(h, wg)) * _dot(h, wu)).astype(BF16)


def _ffn_in(h, w, w_down):
    d = w.shape[0]
    tn = TILE_N_FFN_IN
    return _dense_call(
        _ffn_in_body, "ffn_in", TILE_M_FFN_IN,
        [_Rows(*h, d, lambda j: 0),
         _Shared(w, (d, tn), lambda j: (0, j)),
         _Shared(w, (d, tn), lambda j: (0, D_FF // tn + j))],
        [_Out(D_FF, tn, lambda j: j, BF16)], [], D_FF // tn, casts=[(w_down, CAST_ROWS_FFN)])


def _ffn_out_body(a_ref, w_ref, x_ref, g_ref, o_ref):
    o_ref[...] = _rms_rows(x_ref[...] + _dot(a_ref[...], w_ref[...]), g_ref[...])


def _ffn_out(act, w, x, g):
    k, d = w.shape
    return _dense_call(
        _ffn_out_body, "ffn_out", TILE_M_FFN_OUT,
        [_Rows(*act, k, lambda j: 0),
         _Shared(w, (k, d), lambda j: (0, 0), resident=True),
         _Rows(*x, d, lambda j: 0),
         _Shared(g, (1, d), lambda j: (0, 0))],
        [_Out(d, d, lambda j: 0, F32)], [], 1)[0]


def _rotary(x, cos, sin_signed):
    return x * cos + pltpu.roll(x, RET_DK // 2, 1) * sin_signed


def _ret_parts(p_ref, cos_ref, sin_ref, o_ref, so_ref, s_ref, dec_ref):
    c = CHUNK

    def init():
        s_ref[...] = jnp.zeros_like(s_ref)
        row = lax.broadcasted_iota(jnp.int32, (c, c), 0)
        col = lax.broadcasted_iota(jnp.int32, (c, c), 1)
        diff = (row - col).astype(F32)
        idx = lax.broadcasted_iota(jnp.int32, (c, RET_DK), 0).astype(F32)
        for h in range(RET_HEADS):
            lg = LOG_GAMMA[h]
            dec_ref[h, 0] = jnp.where(diff >= 0, jnp.exp(jnp.maximum(diff, 0.0) * lg), 0.0)
            dec_ref[h, 1] = jnp.exp((idx + 1.0) * lg)
            dec_ref[h, 2] = jnp.exp((c - 1.0 - idx) * lg)

    def step(heads):
        cos = cos_ref[...]
        sin = sin_ref[...]
        def rotate(h):
            q = _rotary(p_ref[0, :, OFF_RQ + h * RET_DK:OFF_RQ + (h + 1) * RET_DK], cos, sin)
            k = _rotary(p_ref[0, :, OFF_RK + h * RET_DK:OFF_RK + (h + 1) * RET_DK], cos, sin) * (RET_DK ** -0.5)
            return q, k

        def attend(h, q, k):
            v = p_ref[0, :, OFF_RV + h * RET_DV:OFF_RV + (h + 1) * RET_DV].astype(BF16)
            chunk_decay = math.exp(c * LOG_GAMMA[h])
            s = s_ref[h]
            scores = _dot_nt(q.astype(BF16), k.astype(BF16)) * dec_ref[h, 0]
            o = _dot(scores.astype(BF16), v) + _dot((q * dec_ref[h, 1]).astype(BF16), s.astype(BF16))
            s_ref[h] = chunk_decay * s + _dot_tn((k * dec_ref[h, 2]).astype(BF16), v)
            return o

        def emit(h, o):
            o = o * lax.rsqrt(jnp.mean(o * o, axis=-1, keepdims=True) + EPS)
            gate = p_ref[0, :, OFF_RG + h * RET_DV:OFF_RG + (h + 1) * RET_DV]
            o_ref[0, :, h * RET_DV:(h + 1) * RET_DV] = (o * _silu(gate)).astype(BF16)

        qk = [rotate(h) for h in heads]
        outs = [attend(h, q, k) for h, (q, k) in zip(heads, qk)]
        for h, o in zip(heads, outs):
            emit(h, o)

    def final():
        so_ref[0] = s_ref[...]

    return init, step, final


def _lower_bound(lbl):
    mx = jnp.max(lbl, axis=0, keepdims=True)
    e = jnp.exp(lbl - mx)
    return e[0:1, :] / jnp.sum(e, axis=0, keepdims=True)


def _hgrn_gates(z, lb):
    half = 0.5 * (1.0 - lb)
    ht = half * jnp.tanh(0.5 * z)
    return (lb + half) + ht, half - ht


def _cumsum_rows(tri, x):
    hi = x.astype(BF16)
    r1 = x - hi.astype(F32)
    mid = r1.astype(BF16)
    lo = (r1 - mid.astype(F32)).astype(BF16)
    return _dot(tri, hi) + _dot(tri, mid) + _dot(tri, lo)


def _bcast_rows(x, group, r):
    c, d = x.shape
    x3 = x.reshape(c // group, group, d)
    return jnp.broadcast_to(x3[:, r:r + 1, :], x3.shape).reshape(c, d)


def _hgrn_parts(qf_ref, ig_ref, lbl_ref, nrm_ref, o_ref, so_ref, st_ref):
    def init():
        st_ref[...] = jnp.zeros_like(st_ref)

    def final():
        for h in range(HG_HEADS):
            so_ref[0, h] = st_ref[h].T

    return init, functools.partial(_hgrn_step, qf_ref, ig_ref, lbl_ref, nrm_ref, o_ref, st_ref), final


def _hgrn_step(qf_ref, ig_ref, lbl_ref, nrm_ref, o_ref, st_ref, heads):
    c = CHUNK
    row = lax.broadcasted_iota(jnp.int32, (c, c), 0)
    col = lax.broadcasted_iota(jnp.int32, (c, c), 1)
    rowd = lax.broadcasted_iota(jnp.int32, (c, HG_DK), 0)
    tri = jnp.where(row >= col, 1.0, 0.0).astype(BF16)
    r2, r4 = rowd & 1, rowd & 3
    block_masks = [(row >> s) == (col >> s) for s in range(c.bit_length())]
    nrm = nrm_ref[...]

    def up(x, k):
        return pltpu.roll(x, k, 0)

    def dn(x, k):
        return pltpu.roll(x, c - k, 0)

    def gates(h):
        off = h * HG_DK
        lb = _lower_bound(lbl_ref[:, pl.ds(off, HG_DK)])
        f, kin = _hgrn_gates(qf_ref[0, :, pl.ds(HG_K + off, HG_DK)], lb)
        qh = _silu(qf_ref[0, :, pl.ds(off, HG_DK)])
        v = ig_ref[0, :, pl.ds(off, HG_DV)].astype(BF16)
        b = _cumsum_rows(tri, jnp.log2(f))
        return f, kin, qh, v, b

    def intra(f, kin, qh, b):
        f_up = f * up(f, 1)
        f_dn = f * dn(f, 1)
        pre2 = jnp.where(r2 == 1, f_up, f)
        pre4 = jnp.where(r4 >= 1, f_up, f)
        pre4 = jnp.where(r4 >= 2, pre4 * up(pre4, 2), pre4)
        suf4 = jnp.where(r4 <= 2, f_dn, f)
        suf4 = jnp.where(r4 <= 1, suf4 * dn(suf4, 2), suf4)

        def level(s):
            if s == 0:
                return qh, kin
            if s == 1:
                return qh * f, kin
            if s == 2:
                return qh * pre2, kin * jnp.where(r4 == 0, dn(f, 1), 1.0)
            if s == 3:
                return qh * pre4, kin * jnp.where(r4 == 3, 1.0, dn(suf4, 1))
            half = 1 << (s - 1)
            ref = _bcast_rows(b, 2 * half, half - 1)
            upper = (rowd & (2 * half - 1)) >= half
            return jnp.where(upper, qh * jnp.exp2(b - ref), 0.0), jnp.where(upper, 0.0, kin * jnp.exp2(ref - b))

        a = jnp.zeros((c, c), F32)
        for s in reversed(range(len(block_masks))):
            q_s, k_s = level(s)
            a = jnp.where(block_masks[s], _dot_nt(q_s.astype(BF16), k_s.astype(BF16)), a)
        return jnp.where(col <= row, a, 0.0)

    def finish(h, kin, qh, v, b, a):
        off = h * HG_DK
        st = st_ref[h]
        b_last = b[c - 1:c, :]
        o = _dot(a.astype(BF16), v) + _dot_nt((qh * jnp.exp2(b)).astype(BF16), st.astype(BF16))
        st_ref[h] = st * jnp.exp2(b_last) + _dot_tn(v, (kin * jnp.exp2(b_last - b)).astype(BF16))
        o = o * lax.rsqrt(jnp.mean(o * o, axis=-1, keepdims=True) + EPS) * nrm
        o_ref[0, :, pl.ds(off, HG_DV)] = (o * _silu(ig_ref[0, :, pl.ds(HG_V + off, HG_DV)])).astype(BF16)

    g = [gates(h) for h in heads]
    a = [intra(f, kin, qh, b) for f, kin, qh, v, b in g]
    for h, (f, kin, qh, v, b), a_h in zip(heads, g, a):
        finish(h, kin, qh, v, b, a_h)


def _mixers_kernel(n_cast, p_ref, cos_ref, sin_ref, qf_ref, ig_ref, lbl_ref, nrm_ref, *refs):
    cast_in, (or_ref, sor_ref, oh_ref, soh_ref), refs = refs[:n_cast], refs[n_cast:n_cast + 4], refs[n_cast + 4:]
    cast_out, (s_ref, dec_ref, st_ref) = refs[:n_cast], refs[n_cast:]
    for src, dst in zip(cast_in, cast_out):
        dst[...] = src[...].astype(BF16)
    n = pl.program_id(1)
    parts = (_ret_parts(p_ref, cos_ref, sin_ref, or_ref, sor_ref, s_ref, dec_ref),
             _hgrn_parts(qf_ref, ig_ref, lbl_ref, nrm_ref, oh_ref, soh_ref, st_ref))

    @pl.when(n == 0)
    def _():
        for init, _, _ in parts:
            init()

    for h in range(0, max(RET_HEADS, HG_HEADS), MIXER_HEAD_GROUP):
        for (_, step, _), n_heads in zip(parts, (RET_HEADS, HG_HEADS)):
            step(range(h, min(h + MIXER_HEAD_GROUP, n_heads)))

    @pl.when(n == pl.num_programs(1) - 1)
    def _():
        for _, _, final in parts:
            final()


def _mixers_prompt(proj3, cos, sin, lbl, nrm, later_weights):
    b, t, _ = proj3.shape
    nc = t // CHUNK
    cast_specs = [pl.BlockSpec((w.shape[0] // (b * nc), w.shape[1]), lambda i, n: (i * nc + n, 0))
                  for w in later_weights]
    return pl.pallas_call(
        functools.partial(_mixers_kernel, len(later_weights)),
        grid=(b, nc),
        in_specs=[pl.BlockSpec((1, CHUNK, OFF_HQ), lambda i, n: (i, n, 0)),
                  pl.BlockSpec((CHUNK, RET_DK), lambda i, n: (n, 0)),
                  pl.BlockSpec((CHUNK, RET_DK), lambda i, n: (n, 0)),
                  pl.BlockSpec((1, CHUNK, 2 * HG_K), lambda i, n: (i, n, OFF_HQ // (2 * HG_K))),
                  pl.BlockSpec((1, CHUNK, 2 * HG_V), lambda i, n: (i, n, OFF_HI // (2 * HG_V))),
                  pl.BlockSpec((DEPTH + 1, HG_K), lambda i, n: (0, 0)),
                  pl.BlockSpec((1, HG_DV), lambda i, n: (0, 0))] + cast_specs,
        out_specs=[pl.BlockSpec((1, CHUNK, RET_V), lambda i, n: (i, n, 0)),
                   pl.BlockSpec((1, RET_HEADS, RET_DK, RET_DV), lambda i, n: (i, 0, 0, 0)),
                   pl.BlockSpec((1, CHUNK, HG_V), lambda i, n: (i, n, 0)),
                   pl.BlockSpec((1, HG_HEADS, HG_DK, HG_DV), lambda i, n: (i, 0, 0, 0))] + cast_specs,
        out_shape=[jax.ShapeDtypeStruct((b, t, RET_V), BF16),
                   jax.ShapeDtypeStruct((b, RET_HEADS, RET_DK, RET_DV), F32),
                   jax.ShapeDtypeStruct((b, t, HG_V), BF16),
                   jax.ShapeDtypeStruct((b, HG_HEADS, HG_DK, HG_DV), F32)]
        + [jax.ShapeDtypeStruct(w.shape, BF16) for w in later_weights],
        scratch_shapes=[pltpu.VMEM((RET_HEADS, RET_DK, RET_DV), F32),
                        pltpu.VMEM((RET_HEADS, 3, CHUNK, RET_DK), F32),
                        pltpu.VMEM((HG_HEADS, HG_DV, HG_DK), F32)],
        compiler_params=_params(("arbitrary", "arbitrary")),
        name="mixers_prompt",
    )(proj3, cos, sin, proj3, proj3, lbl, nrm, *later_weights)


def _sample_prep_kernel(p_ref, cos_ref, sin_ref, lbl_ref, qr_ref, kr_ref, qh_ref, f_ref, kin_ref):
    cos = cos_ref[...]
    sin = sin_ref[...]
    for h in range(RET_HEADS):
        sl = slice(h * RET_DK, (h + 1) * RET_DK)
        qr_ref[:, sl] = _rotary(p_ref[:, OFF_RQ + h * RET_DK:OFF_RQ + (h + 1) * RET_DK], cos, sin)
        kr_ref[:, sl] = _rotary(p_ref[:, OFF_RK + h * RET_DK:OFF_RK + (h + 1) * RET_DK], cos, sin) * (RET_DK ** -0.5)
    f, kin = _hgrn_gates(p_ref[:, OFF_HF:OFF_HF + HG_K], _lower_bound(lbl_ref[...]))
    f_ref[...] = f
    kin_ref[...] = kin
    qh_ref[...] = _silu(p_ref[:, OFF_HQ:OFF_HQ + HG_K])


def _sample_prep(proj, cos, sin, lbl):
    m = proj.shape[0]
    shp = jax.ShapeDtypeStruct((m, RET_QK), F32)
    return pl.pallas_call(
        _sample_prep_kernel,
        grid=(1,),
        in_specs=[pl.BlockSpec((m, OFF_GA), lambda i: (0, 0)),
                  pl.BlockSpec((1, RET_DK), lambda i: (0, 0)),
                  pl.BlockSpec((1, RET_DK), lambda i: (0, 0)),
                  pl.BlockSpec((DEPTH + 1, HG_K), lambda i: (0, 0))],
        out_specs=[pl.BlockSpec((m, RET_QK), lambda i: (0, 0))] * 5,
        out_shape=[shp] * 5,
        compiler_params=_params(("arbitrary",)),
        name="sample_prep",
    )(proj, cos, sin, lbl)


def _sample_state_kernel(sr_hbm, sh_hbm, qr_ref, kr_ref, qh_ref, f_ref, kin_ref, vr_ref, rg_ref, vh_ref, hg_ref,
                         nrm_ref, sel_ref, sro_ref, sho_ref, or_ref, oh_ref, ring_r, ring_h, sem):
    nb = SAMPLE_SEQS_PER_STEP
    step_id, n_steps = pl.program_id(0), pl.num_programs(0)

    def state_copies(step, slot):
        rows = pl.ds(step * nb, nb)
        return (pltpu.make_async_copy(sr_hbm.at[rows], ring_r.at[slot], sem.at[0, slot]),
                pltpu.make_async_copy(sh_hbm.at[rows], ring_h.at[slot], sem.at[1, slot]))

    @pl.when(step_id == 0)
    def _():
        for first in range(STATE_RING_SLOTS - 1):
            for cp in state_copies(first, first):
                cp.start()

    ahead = step_id + STATE_RING_SLOTS - 1

    @pl.when(ahead < n_steps)
    def _():
        for cp in state_copies(ahead, ahead % STATE_RING_SLOTS):
            cp.start()

    slot = step_id % STATE_RING_SLOTS
    for cp in state_copies(step_id, slot):
        cp.wait()
    sr_ref, sh_ref = ring_r.at[slot], ring_h.at[slot]
    lanes = RET_DK
    vec_refs = (qr_ref, kr_ref, qh_ref, f_ref, kin_ref)
    sub8 = lax.broadcasted_iota(jnp.int32, (SUB, lanes), 0)
    for t in range(SAMPLE_SEQS_PER_STEP):
        b = pl.program_id(0) * SAMPLE_SEQS_PER_STEP + t
        base = pl.multiple_of((b // SUB) * SUB, SUB)

        def row(ref, sl, b=b, base=base):
            tile = ref[pl.ds(base, SUB), sl]
            sub = lax.broadcasted_iota(jnp.int32, tile.shape, 0)
            return jnp.sum(jnp.where(sub == b % SUB, tile, 0.0), axis=0, keepdims=True)

        def columns(h, row=row):
            x = jnp.zeros((SUB, lanes), F32)
            for j, ref in enumerate(vec_refs):
                x = jnp.where(sub8 == j, row(ref, slice(h * RET_DK, (h + 1) * RET_DK)), x)
            hi = x.astype(BF16).astype(F32)
            mid = (x - hi).astype(BF16).astype(F32)
            lo = x - hi - mid
            terms = jnp.concatenate([hi, mid, lo, jnp.zeros_like(x)], axis=0).astype(BF16)
            cols = _dot_tn(terms, sel_ref[...])
            return [cols[:, j * lanes:(j + 1) * lanes] for j in range(len(vec_refs))]

        all_cols = [columns(h) for h in range(RET_HEADS)]
        for h in range(RET_HEADS):
            qc, kc = all_cols[h][0], all_cols[h][1]
            gamma = math.exp(LOG_GAMMA[h])
            outs = []
            for half in range(RET_DV // lanes):
                hs = slice(half * lanes, (half + 1) * lanes)
                s = gamma * sr_ref[t, h, :, hs] + kc * row(vr_ref, slice(h * RET_DV + hs.start, h * RET_DV + hs.stop))
                sro_ref[t, h, :, hs] = s
                outs.append(jnp.sum(qc * s, axis=0, keepdims=True))
            o = jnp.concatenate(outs, axis=1)
            o = o * lax.rsqrt(jnp.mean(o * o, axis=-1, keepdims=True) + EPS)
            sl = slice(h * RET_DV, (h + 1) * RET_DV)
            or_ref[t, :, sl] = o * _silu(row(rg_ref, sl))
        for h in range(HG_HEADS):
            sl = slice(h * HG_DV, (h + 1) * HG_DV)
            _, _, qc, fc, kc = all_cols[h]
            s = fc * sh_ref[t, h] + kc * row(vh_ref, sl)
            sho_ref[t, h] = s
            o = jnp.sum(qc * s, axis=0, keepdims=True)
            o = o * lax.rsqrt(jnp.mean(o * o, axis=-1, keepdims=True) + EPS) * nrm_ref[...]
            oh_ref[t, :, sl] = o * _silu(row(hg_ref, sl))


def _sample_state(s_ret, s_hg, rows, proj, nrm):
    b = s_ret.shape[0]
    nb = SAMPLE_SEQS_PER_STEP
    n_vec, n_terms = 5, 3
    sel = np.zeros(((n_terms + 1) * SUB, n_vec * RET_DK), np.float32)
    for p in range(n_terms):
        for j in range(n_vec):
            sel[SUB * p + j, j * RET_DK:(j + 1) * RET_DK] = 1.0
    sel = jnp.asarray(sel, BF16)
    row_spec = pl.BlockSpec((b, RET_QK), lambda i: (0, 0))
    return pl.pallas_call(
        _sample_state_kernel,
        grid=(b // nb,),
        in_specs=[pl.BlockSpec(memory_space=pl.ANY), pl.BlockSpec(memory_space=pl.ANY),
                  row_spec, row_spec, row_spec, row_spec, row_spec,
                  pl.BlockSpec((b, RET_V), lambda i: (0, OFF_RV // RET_V)),
                  pl.BlockSpec((b, RET_V), lambda i: (0, OFF_RG // RET_V)),
                  pl.BlockSpec((b, HG_V), lambda i: (0, OFF_HI // HG_V)),
                  pl.BlockSpec((b, HG_V), lambda i: (0, OFF_HG // HG_V)),
                  pl.BlockSpec((1, HG_DV), lambda i: (0, 0)),
                  pl.BlockSpec(sel.shape, lambda i: (0, 0))],
        out_specs=[pl.BlockSpec((nb, RET_HEADS, RET_DK, RET_DV), lambda i: (i, 0, 0, 0)),
                   pl.BlockSpec((nb, HG_HEADS, HG_DK, HG_DV), lambda i: (i, 0, 0, 0)),
                   pl.BlockSpec((nb, 1, RET_V), lambda i: (i, 0, 0)),
                   pl.BlockSpec((nb, 1, HG_V), lambda i: (i, 0, 0))],
        out_shape=[jax.ShapeDtypeStruct(s_ret.shape, F32),
                   jax.ShapeDtypeStruct(s_hg.shape, F32),
                   jax.ShapeDtypeStruct((b, 1, RET_V), F32),
                   jax.ShapeDtypeStruct((b, 1, HG_V), F32)],
        scratch_shapes=[pltpu.VMEM((STATE_RING_SLOTS, nb, RET_HEADS, RET_DK, RET_DV), F32),
                        pltpu.VMEM((STATE_RING_SLOTS, nb, HG_HEADS, HG_DK, HG_DV), F32),
                        pltpu.SemaphoreType.DMA((2, STATE_RING_SLOTS))],
        compiler_params=_params(("arbitrary",)),
        name="sample_state",
    )(s_ret, s_hg, *rows, proj, proj, proj, proj, nrm, sel)


def _rope_tables(first, count):
    half = RET_DK // 2
    inv_freq = ROPE_BASE ** (-np.arange(half, dtype=np.float64) / half)
    ang = np.arange(first, first + count, dtype=np.float64)[:, None] * inv_freq[None, :]
    cos, sin = np.cos(ang), np.sin(ang)
    return (jnp.asarray(np.concatenate([cos, cos], axis=-1), F32),
            jnp.asarray(np.concatenate([-sin, sin], axis=-1), F32))


def kernel(x_prompt, x_sample, state_ret, state_hgrn, w_in, w_ret_out, w_hgrn_out, w_out, norm_mix, norm_ffn,
           hgrn_norm, hgrn_lb_logits, w_ffn_in, w_ffn_out, norm_final):
    assert w_in.shape[0] == DEPTH == 1 and x_sample.shape[1] == 1
    bp, t, d = x_prompt.shape
    bs = x_sample.shape[0]
    nrm = hgrn_norm[0][None, :]
    lbl = hgrn_lb_logits.astype(F32)
    x = (x_prompt.reshape(bp * t, d), x_sample.reshape(bs, d))

    proj = _proj(x, norm_mix[0][None, :], w_in[0])

    proj_p3 = proj[0].reshape(bp, t, IN_TOTAL)
    cos_p, sin_p = _rope_tables(0, t)
    o_r, s_ret_p, o_h, s_hg_p, wb_ret, wb_hgrn, wb_out = _mixers_prompt(
        proj_p3, cos_p, sin_p, lbl, nrm, [w_ret_out[0], w_hgrn_out[0], w_out[0]])

    cos_s, sin_s = _rope_tables(PAST_LEN, x_sample.shape[1])
    rows = _sample_prep(proj[1], cos_s, sin_s, lbl)
    s_ret_s, s_hg_s, o_rs, o_hs = _sample_state(state_ret[0], state_hgrn[0], rows, proj[1], nrm)

    o_r = (o_r.reshape(bp * t, RET_V), o_rs.reshape(bs, RET_V).astype(BF16))
    o_h = (o_h.reshape(bp * t, HG_V), o_hs.reshape(bs, HG_V).astype(BF16))
    mrg = _merge(o_r, o_h, wb_ret, wb_hgrn, proj)
    x1, h2 = _resid(x, mrg, wb_out, norm_ffn[0][None, :])
    act, wb_down = _ffn_in(h2, w_ffn_in[0], w_ffn_out[0])
    y_p, y_s = _ffn_out(act, wb_down, x1, norm_final[None, :])

    return (y_p.reshape(bp, t, d), y_s.reshape(bs, 1, d), s_ret_p[None], s_hg_p[None], s_ret_s[None], s_hg_s[None])
```
